```python
import math
import jax, jax.numpy as jnp
from jax import lax
import numpy as np

D_MODEL = 1024
BATCH = 8
SEQ = 2048
DEPTH = 1
DEC_BATCH = 128
DEC_SEQ = 4
PAST_LEN = 16384
PAGE_SIZE = 128

D_MIX = D_MODEL
D_POOL = D_MIX // 4
POOL_WINDOWS = (2, 4, 8, 16)
N_POOL_GROUPS = len(POOL_WINDOWS)
POOL_GW = D_POOL // N_POOL_GROUPS
POOL_BUF = max(POOL_WINDOWS) - 1
D_SSM = D_MIX - D_POOL
SSM_HEADDIM = 64
SSM_HEADS = D_SSM // SSM_HEADDIM
SSM_GROUPS = 2
SSM_HPG = SSM_HEADS // SSM_GROUPS
SSM_STATE = 128
CONV_W = 4
CONV_DIM = D_SSM + 2 * SSM_GROUPS * SSM_STATE
CHUNK = 128
D_IN = D_POOL + D_SSM + CONV_DIM + SSM_HEADS
N_MEM = 256
MEM_HEADS = 4
MEM_HD = D_MODEL // MEM_HEADS
D_FF = 2816
EPS = 1e-6

kernel_name = 'hybrid_pool_ssd_macaron_decode_step'


def rmsnorm(x, g):
    xf = x.astype(jnp.float32)
    y = xf * lax.rsqrt(jnp.mean(xf * xf, axis=-1, keepdims=True) + EPS)
    return (y * g.astype(jnp.float32)).astype(x.dtype)


def swiglu(h, w_gate, w_up, w_down):
    return (jax.nn.silu(h @ w_gate) * (h @ w_up)) @ w_down


def pool_mix(u, buf, start_pos, w_pool, pool_scale):
    b, t, _ = u.shape
    ext = jnp.concatenate([buf, u], axis=1)
    cs = jnp.cumsum(ext.astype(jnp.float32), axis=1)
    cs = jnp.concatenate([jnp.zeros_like(cs[:, :1]), cs], axis=1)
    pos = start_pos + jnp.arange(t)
    means = []
    for g, w in enumerate(POOL_WINDOWS):
        sl = slice(g * POOL_GW, (g + 1) * POOL_GW)
        hi = cs[:, POOL_BUF + 1:POOL_BUF + 1 + t, sl]
        lo = cs[:, POOL_BUF + 1 - w:POOL_BUF + 1 - w + t, sl]
        cnt = jnp.minimum(pos + 1, w).astype(jnp.float32)
        means.append((hi - lo) / cnt[None, :, None])
    mean = jnp.concatenate(means, axis=-1).astype(u.dtype)
    p = (mean - u).reshape(b, t, N_POOL_GROUPS, POOL_GW)
    out = jnp.einsum('btgc,gcd->btgd', p, w_pool).reshape(b, t, D_POOL) * pool_scale
    return out, ext[:, -POOL_BUF:]


def causal_conv(xbc, conv_buf, conv_w, conv_b):
    t = xbc.shape[1]
    ext = jnp.concatenate([conv_buf, xbc], axis=1)
    out = sum(ext[:, k:k + t] * conv_w[k] for k in range(CONV_W)) + conv_b
    return jax.nn.silu(out), ext[:, -(CONV_W - 1):]


def ssd_chunked(x, dt, a, bm, cm, h0, chunk):
    b, t = x.shape[:2]
    nc = t // chunk
    r = lambda z: z.reshape((b, nc, chunk) + z.shape[2:])
    xc, dtc, bc, cc = r(x), r(dt), r(bm), r(cm)
    acs = jnp.cumsum(dtc * a, axis=2)
    seg = acs[:, :, :, None] - acs[:, :, None, :]
    causal = (jnp.arange(chunk)[:, None] >= jnp.arange(chunk)[None, :])[None, None, :, :, None, None]
    decay = jnp.exp(jnp.where(causal, seg, -jnp.inf))
    scores = jnp.einsum('bclgn,bcsgn->bclsg', cc, bc)
    wgt = scores[..., None] * decay * dtc[:, :, None]
    y_diag = jnp.einsum('bclsgk,bcsgkp->bclgkp', wgt, xc)
    decay_end = jnp.exp(acs[:, :, -1:] - acs) * dtc
    states = jnp.einsum('bclgn,bclgk,bclgkp->bcgkpn', bc, decay_end, xc)
    chunk_decay = jnp.exp(acs[:, :, -1])

    def step(h, inp):
        st, dec = inp
        return h * dec[..., None, None] + st, h

    h_last, h_prev = lax.scan(step, h0, (jnp.moveaxis(states, 1, 0), jnp.moveaxis(chunk_decay, 1, 0)))
    h_prev = jnp.moveaxis(h_prev, 0, 1)
    y_off = jnp.einsum('bclgn,bcgkpn,bclgk->bclgkp', cc, h_prev, jnp.exp(acs))
    y = (y_diag + y_off).reshape(b, t, SSM_GROUPS, SSM_HPG, SSM_HEADDIM)
    return y, h_last


def token_mix(h, pool_buf, conv_buf, ssm_h0, start_pos, w_in, conv_w, conv_b, dt_bias, a_log,
              d_skip, ssm_norm, w_pool, pool_scale, w_out):
    b, t, _ = h.shape
    proj = h @ w_in
    u, z, xbc, dt_raw = jnp.split(proj, [D_POOL, D_POOL + D_SSM, D_POOL + D_SSM + CONV_DIM], axis=-1)
    pool_out, new_pool = pool_mix(u, pool_buf, start_pos, w_pool, pool_scale)
    xbc, new_conv = causal_conv(xbc, conv_buf, conv_w, conv_b)
    xs, bm, cm = jnp.split(xbc, [D_SSM, D_SSM + SSM_GROUPS * SSM_STATE], axis=-1)
    xs = xs.reshape(b, t, SSM_GROUPS, SSM_HPG, SSM_HEADDIM)
    bm = bm.reshape(b, t, SSM_GROUPS, SSM_STATE)
    cm = cm.reshape(b, t, SSM_GROUPS, SSM_STATE)
    dt = jax.nn.softplus(dt_raw.astype(jnp.float32) + dt_bias.astype(jnp.float32))
    dt = dt.reshape(b, t, SSM_GROUPS, SSM_HPG)
    a = -jnp.exp(a_log.astype(jnp.float32)).reshape(SSM_GROUPS, SSM_HPG)
    h0 = ssm_h0.astype(jnp.float32).reshape(b, SSM_GROUPS, SSM_HPG, SSM_HEADDIM, SSM_STATE)
    chunk = CHUNK if t % CHUNK == 0 else t
    y, h_last = ssd_chunked(xs, dt, a, bm, cm, h0, chunk)
    y = y + xs * d_skip.reshape(SSM_GROUPS, SSM_HPG, 1)
    y = y.reshape(b, t, D_SSM).astype(h.dtype)
    y = rmsnorm(y * jax.nn.silu(z), ssm_norm)
    out = jnp.concatenate([pool_out, y], axis=-1) @ w_out
    new_ssm = h_last.reshape(b, SSM_HEADS, SSM_HEADDIM, SSM_STATE)
    return out, new_pool, new_conv, new_ssm


def mem_kv(mem, g_mem, w_mem_k, w_mem_v):
    b = mem.shape[0]
    m = rmsnorm(mem, g_mem)
    k = (m @ w_mem_k).reshape(b, N_MEM, MEM_HEADS, MEM_HD)
    v = (m @ w_mem_v).reshape(b, N_MEM, MEM_HEADS, MEM_HD)
    return k, v


def cross_attn(h, k, v, w_xq, w_xo):
    b, t, _ = h.shape
    q = (h @ w_xq).reshape(b, t, MEM_HEADS, MEM_HD)
    s = jnp.einsum('bthd,bmhd->bhtm', q, k).astype(jnp.float32) * (MEM_HD ** -0.5)
    p = jax.nn.softmax(s, axis=-1).astype(v.dtype)
    o = jnp.einsum('bhtm,bmhd->bthd', p, v).reshape(b, t, D_MODEL)
    return o @ w_xo


def layer(x, k_mem, v_mem, pool_buf, conv_buf, ssm_h0, start_pos, lw):
    (g_ffn1, w1_gate, w1_up, w1_down, g_mix, w_in, conv_w, conv_b, dt_bias, a_log, d_skip,
     ssm_norm, w_pool, pool_scale, w_out, g_xq, w_xq, w_xo, g_ffn2, w2_gate, w2_up, w2_down) = lw
    x = x + 0.5 * swiglu(rmsnorm(x, g_ffn1), w1_gate, w1_up, w1_down)
    mix, new_pool, new_conv, new_ssm = token_mix(rmsnorm(x, g_mix), pool_buf, conv_buf, ssm_h0, start_pos,
                                                 w_in, conv_w, conv_b, dt_bias, a_log, d_skip, ssm_norm,
                                                 w_pool, pool_scale, w_out)
    x = x + mix
    x = x + cross_attn(rmsnorm(x, g_xq), k_mem, v_mem, w_xq, w_xo)
    x = x + 0.5 * swiglu(rmsnorm(x, g_ffn2), w2_gate, w2_up, w2_down)
    return x, new_pool, new_conv, new_ssm


def setup_inputs(seed: int = 0) -> dict:
    key = jax.random.key(seed)
    keys = iter(jax.random.split(key, 48))
    f32 = jnp.float32
    L = DEPTH

    def nrm(shape, scale):
        return jax.random.normal(next(keys), shape, f32) * scale

    def gain(shape):
        return 1.0 + 0.05 * jax.random.normal(next(keys), shape, f32)

    d = {}
    d['x_prompt'] = nrm((BATCH, SEQ, D_MODEL), 1.0)
    d['x_sample'] = nrm((DEC_BATCH, DEC_SEQ, D_MODEL), 1.0)
    d['mem_prompt'] = nrm((BATCH, N_MEM, D_MODEL), 1.0)
    d['cache_mem_k'] = nrm((L, DEC_BATCH, N_MEM, MEM_HEADS, MEM_HD), 1.0)
    d['cache_mem_v'] = nrm((L, DEC_BATCH, N_MEM, MEM_HEADS, MEM_HD), 1.0)
    d['state_pool'] = nrm((L, DEC_BATCH, POOL_BUF, D_POOL), 1.0)
    d['state_conv'] = nrm((L, DEC_BATCH, CONV_W - 1, CONV_DIM), 1.0)
    d['state_ssm'] = nrm((L, DEC_BATCH, SSM_HEADS, SSM_HEADDIM, SSM_STATE), 0.5)
    d['g_ffn1'] = gain((L, D_MODEL))
    d['w1_gate'] = nrm((L, D_MODEL, D_FF), D_MODEL ** -0.5)
    d['w1_up'] = nrm((L, D_MODEL, D_FF), D_MODEL ** -0.5)
    d['w1_down'] = nrm((L, D_FF, D_MODEL), D_FF ** -0.5)
    d['g_mix'] = gain((L, D_MODEL))
    d['w_in'] = nrm((L, D_MODEL, D_IN), D_MODEL ** -0.5)
    d['conv_w'] = nrm((L, CONV_W, CONV_DIM), CONV_W ** -0.5)
    d['conv_b'] = nrm((L, CONV_DIM), 0.02)
    dt0 = jnp.exp(jax.random.uniform(next(keys), (L, SSM_HEADS), f32, math.log(1e-3), math.log(1e-1)))
    d['dt_bias'] = dt0 + jnp.log(-jnp.expm1(-dt0))
    d['a_log'] = jnp.log(jax.random.uniform(next(keys), (L, SSM_HEADS), f32, 1.0, 16.0))
    d['d_skip'] = gain((L, SSM_HEADS))
    d['ssm_norm'] = gain((L, D_SSM))
    d['w_pool'] = nrm((L, N_POOL_GROUPS, POOL_GW, POOL_GW), POOL_GW ** -0.5)
    d['pool_scale'] = 0.5 + 0.1 * jax.random.normal(next(keys), (L, D_POOL), f32)
    d['w_out'] = nrm((L, D_MIX, D_MODEL), D_MIX ** -0.5)
    d['g_mem'] = gain((L, D_MODEL))
    d['w_mem_k'] = nrm((L, D_MODEL, D_MODEL), D_MODEL ** -0.5)
    d['w_mem_v'] = nrm((L, D_MODEL, D_MODEL), D_MODEL ** -0.5)
    d['g_xq'] = gain((L, D_MODEL))
    d['w_xq'] = nrm((L, D_MODEL, D_MODEL), D_MODEL ** -0.5)
    d['w_xo'] = nrm((L, D_MODEL, D_MODEL), D_MODEL ** -0.5)
    d['g_ffn2'] = gain((L, D_MODEL))
    d['w2_gate'] = nrm((L, D_MODEL, D_FF), D_MODEL ** -0.5)
    d['w2_up'] = nrm((L, D_MODEL, D_FF), D_MODEL ** -0.5)
    d['w2_down'] = nrm((L, D_FF, D_MODEL), D_FF ** -0.5)
    d['g_final'] = gain((D_MODEL,))
    return d


def reference(x_prompt, x_sample, mem_prompt, cache_mem_k, cache_mem_v, state_pool, state_conv, state_ssm,
              g_ffn1, w1_gate, w1_up, w1_down, g_mix, w_in, conv_w, conv_b, dt_bias, a_log, d_skip,
              ssm_norm, w_pool, pool_scale, w_out, g_mem, w_mem_k, w_mem_v, g_xq, w_xq, w_xo,
              g_ffn2, w2_gate, w2_up, w2_down, g_final):
    bp = x_prompt.shape[0]
    hp, hs = x_prompt, x_sample
    mk_p, mv_p, pool_p, conv_p, ssm_p = [], [], [], [], []
    pool_s, conv_s, ssm_s = [], [], []
    for l in range(DEPTH):
        lw = (g_ffn1[l], w1_gate[l], w1_up[l], w1_down[l], g_mix[l], w_in[l], conv_w[l], conv_b[l],
              dt_bias[l], a_log[l], d_skip[l], ssm_norm[l], w_pool[l], pool_scale[l], w_out[l],
              g_xq[l], w_xq[l], w_xo[l], g_ffn2[l], w2_gate[l], w2_up[l], w2_down[l])
        k_p, v_p = mem_kv(mem_prompt, g_mem[l], w_mem_k[l], w_mem_v[l])
        zero_pool = jnp.zeros((bp, POOL_BUF, D_POOL), x_prompt.dtype)
        zero_conv = jnp.zeros((bp, CONV_W - 1, CONV_DIM), x_prompt.dtype)
        zero_ssm = jnp.zeros((bp, SSM_HEADS, SSM_HEADDIM, SSM_STATE), jnp.float32)
        hp, pb, cb, sb = layer(hp, k_p, v_p, zero_pool, zero_conv, zero_ssm, 0, lw)
        mk_p.append(k_p)
        mv_p.append(v_p)
        pool_p.append(pb)
        conv_p.append(cb)
        ssm_p.append(sb)
        hs, pb2, cb2, sb2 = layer(hs, cache_mem_k[l], cache_mem_v[l], state_pool[l], state_conv[l],
                                  state_ssm[l], PAST_LEN, lw)
        pool_s.append(pb2)
        conv_s.append(cb2)
        ssm_s.append(sb2)
    y_prompt = rmsnorm(hp, g_final)
    y_sample = rmsnorm(hs, g_final)
    return (y_prompt, y_sample, jnp.stack(mk_p), jnp.stack(mv_p), jnp.stack(pool_p), jnp.stack(conv_p),
            jnp.stack(ssm_p), jnp.stack(pool_s), jnp.stack(conv_s), jnp.stack(ssm_s))
```

```python
import functools

import jax
import jax.numpy as jnp
from jax import lax
from jax.experimental import pallas as pl
from jax.experimental.pallas import tpu as pltpu

F32 = jnp.float32
BF16 = jnp.bfloat16

D_MODEL = 1024
D_POOL = 256
POOL_WINDOWS = (2, 4, 8, 16)
POOL_GW = 64
POOL_BUF = 15
D_SSM = 768
SSM_HEADDIM = 64
SSM_HEADS = 12
SSM_GROUPS = 2
SSM_HPG = 6
SSM_STATE = 128
CONV_W = 4
CONV_DIM = 1280
N_MEM = 256
MEM_HEADS = 4
MEM_HD = 256
D_FF = 2816
EPS = 1e-6
LANES = 128
CHUNK = 128
POOL_HIST = 32
CONV_HIST = 8
VMEM_LIMIT = 56 * 1024 * 1024


def _cparams(sem):
    return pltpu.CompilerParams(dimension_semantics=sem, vmem_limit_bytes=VMEM_LIMIT)


def _const_spec(shape):
    nd = len(shape)
    return pl.BlockSpec(shape, lambda *_: (0,) * nd, pipeline_mode=pl.Buffered(1))


def _rms(x, g):
    ms = jnp.mean(x * x, axis=-1, keepdims=True)
    return x * lax.rsqrt(ms + EPS) * g


def _silu(x):
    return x * jax.nn.sigmoid(x)


def _dot(a, b):
    return jnp.dot(a, b, preferred_element_type=F32)


def _dot_nt(a, b):
    return lax.dot_general(a, b, (((1,), (1,)), ((), ())), preferred_element_type=F32)


def _dot_tn(a, b):
    return lax.dot_general(a, b, (((0,), (0,)), ((), ())), preferred_element_type=F32)


def _ffn_kernel(x_ref, g_ref, wg_ref, wu_ref, wd_ref, gf_ref, o_ref, *, final):
    x = x_ref[...]
    xn = _rms(x, g_ref[...]).astype(BF16)
    gate = _dot(xn, wg_ref[...])
    up = _dot(xn, wu_ref[...])
    h = (_silu(gate) * up).astype(BF16)
    out = x + 0.5 * _dot(h, wd_ref[...])
    if final:
        out = _rms(out, gf_ref[...])
    o_ref[...] = out


def _ffn(x, g, wg, wu, wd, gf, *, final, tm):
    t, d = x.shape
    assert t % tm == 0
    return pl.pallas_call(
        functools.partial(_ffn_kernel, final=final),
        grid=(t // tm,),
        in_specs=[
            pl.BlockSpec((tm, d), lambda i: (i, 0)),
            _const_spec(g.shape), _const_spec(wg.shape), _const_spec(wu.shape),
            _const_spec(wd.shape), _const_spec(gf.shape),
        ],
        out_specs=pl.BlockSpec((tm, d), lambda i: (i, 0)),
        out_shape=jax.ShapeDtypeStruct((t, d), F32),
        compiler_params=_cparams(("parallel",)),
        name="ffn_final" if final else "ffn",
    )(x, g, wg, wu, wd, gf)


def _memkv_kernel(m_ref, g_ref, wk_ref, wv_ref, k_ref, v_ref):
    mn = _rms(m_ref[...], g_ref[...]).astype(BF16)
    k_ref[...] = _dot(mn, wk_ref[...])
    v_ref[...] = _dot(mn, wv_ref[...])


def _memkv(mem, g, wk, wv, *, tm):
    t, d = mem.shape
    return pl.pallas_call(
        _memkv_kernel,
        grid=(t // tm,),
        in_specs=[pl.BlockSpec((tm, d), lambda i: (i, 0)), _const_spec(g.shape),
                  _const_spec(wk.shape), _const_spec(wv.shape)],
        out_specs=[pl.BlockSpec((tm, d), lambda i: (i, 0))] * 2,
        out_shape=[jax.ShapeDtypeStruct((t, d), F32)] * 2,
        compiler_params=_cparams(("parallel",)),
        name="memkv",
    )(mem, g, wk, wv)


def _lane_group_select(vals, width):
    lane = lax.broadcasted_iota(jnp.int32, vals[0].shape, 1)
    out = vals[-1]
    for g in range(len(vals) - 2, -1, -1):
        out = jnp.where(lane < (g + 1) * width, vals[g], out)
    return out


def _pair_select(a, b):
    lane = lax.broadcasted_iota(jnp.int32, a.shape, 1)
    return jnp.where(lane < SSM_HEADDIM, a, b)


def _col_bcast(m, k, width):
    return jnp.broadcast_to(m[:, k:k + 1], (m.shape[0], width))


def _gate_norm_out(y, z, pool_out, x_res, ssm_norm, w_out):
    yn = _rms(y * _silu(z), ssm_norm)
    cat = jnp.concatenate([pool_out, yn], axis=-1).astype(BF16)
    return x_res + _dot(cat, w_out)


def _mix_prompt_kernel(x_ref, g_ref, wu_ref, wz_ref, wx_ref, wdt_ref, cw_ref, cb_ref,
                       dtb_ref, alog_ref, dsk_ref, sn_ref, wp_ref, ps_ref, wo_ref,
                       o_ref, opool_ref, oconv_ref, ossm_ref,
                       ubuf, cbuf, ht_ref, ybuf, *, tc):
    c = pl.program_id(1)
    nc = pl.num_programs(1)

    @pl.when(c == 0)
    def _():
        ubuf[0:POOL_HIST, :] = jnp.zeros((POOL_HIST, D_POOL), F32)
        cbuf[0:CONV_HIST, :] = jnp.zeros((CONV_HIST, CONV_DIM), F32)
        ht_ref[...] = jnp.zeros_like(ht_ref)

    x = x_ref[...]
    xn = _rms(x, g_ref[...]).astype(BF16)

    u = _dot(xn, wu_ref[...])
    ubuf[POOL_HIST:POOL_HIST + tc, :] = u
    e = ubuf[...]
    s2 = e + pltpu.roll(e, 1, 0)
    s4 = s2 + pltpu.roll(s2, 2, 0)
    s8 = s4 + pltpu.roll(s4, 4, 0)
    s16 = s8 + pltpu.roll(s8, 8, 0)
    sums = _lane_group_select([s[POOL_HIST:] for s in (s2, s4, s8, s16)], POOL_GW)
    pos = c * tc + lax.broadcasted_iota(jnp.int32, (tc, D_POOL), 0)
    win = _lane_group_select(
        [jnp.full((tc, D_POOL), w, jnp.int32) for w in POOL_WINDOWS], POOL_GW)
    cnt = jnp.minimum(pos + 1, win).astype(F32)
    p = (sums / cnt - u).astype(BF16)
    pool_out = _dot(p, wp_ref[...]) * ps_ref[...]

    xbc_raw = _dot(xn, wx_ref[...])
    cbuf[CONV_HIST:CONV_HIST + tc, :] = xbc_raw
    ce = cbuf[...]
    acc = ce * cw_ref[CONV_W - 1:CONV_W, :]
    for j in range(1, CONV_W):
        acc = acc + pltpu.roll(ce, j, 0) * cw_ref[CONV_W - 1 - j:CONV_W - j, :]
    xbc = _silu(acc[CONV_HIST:] + cb_ref[...])
    xs = xbc[:, :D_SSM]
    bm = xbc[:, D_SSM:D_SSM + SSM_GROUPS * SSM_STATE]
    cm = xbc[:, D_SSM + SSM_GROUPS * SSM_STATE:]

    dt = jax.nn.softplus(_dot(xn, wdt_ref[...]) + dtb_ref[...])
    a_neg = -jnp.exp(alog_ref[...])
    row = lax.broadcasted_iota(jnp.int32, (CHUNK, CHUNK), 0)
    col = lax.broadcasted_iota(jnp.int32, (CHUNK, CHUNK), 1)
    causal = row >= col
    lane1 = lax.broadcasted_iota(jnp.int32, (CHUNK, LANES), 1)
    lo_half = lane1 < SSM_HEADDIM
    for j in range(tc // CHUNK):
        sl = slice(j * CHUNK, (j + 1) * CHUNK)
        dt_c = dt[sl]
        acs = dt_c * a_neg
        sh = 1
        while sh < CHUNK:
            acs = acs + jnp.where(row >= sh, pltpu.roll(acs, sh, 0), 0.0)
            sh *= 2
        acs_last = acs[CHUNK - 1:CHUNK, :]
        fdec = jnp.exp(acs_last - acs) * dt_c
        acs_t = acs.T
        dt_t = dt_c.T
        xs_c = xs[sl]
        ht = ht_ref[...]
        cdec = []
        xw = []
        for g in range(SSM_GROUPS):
            c_g = cm[sl, g * SSM_STATE:(g + 1) * SSM_STATE]
            b_g = bm[sl, g * SSM_STATE:(g + 1) * SSM_STATE].astype(BF16)
            scores = _dot_nt(c_g.astype(BF16), b_g)
            for q in range(SSM_HPG // 2):
                k0 = g * SSM_HPG + 2 * q
                lhs = []
                a_cols = []
                for k in (k0, k0 + 1):
                    a_col = _col_bcast(acs, k, CHUNK)
                    a_cols.append(a_col)
                    seg = a_col - acs_t[k:k + 1, :]
                    decay = jnp.exp(jnp.where(causal, seg, -jnp.inf))
                    lhs.append((scores * decay * dt_t[k:k + 1, :]).astype(BF16))
                for a_col in a_cols:
                    lhs.append((c_g * jnp.exp(a_col)).astype(BF16))
                lsl = slice((k0 // 2) * LANES, (k0 // 2 + 1) * LANES)
                x_pair = xs_c[:, lsl]
                h_pair = ht[:, lsl]
                zero = jnp.zeros_like(x_pair)
                rhs = jnp.concatenate(
                    [jnp.where(lo_half, x_pair, zero), jnp.where(lo_half, zero, x_pair),
                     jnp.where(lo_half, h_pair, zero), jnp.where(lo_half, zero, h_pair)],
                    axis=0).astype(BF16)
                y_pair = _dot(jnp.concatenate(lhs, axis=1), rhs)
                ybuf[sl, lsl] = y_pair + x_pair * dsk_ref[:, lsl]
                f_pair = _pair_select(_col_bcast(fdec, k0, LANES), _col_bcast(fdec, k0 + 1, LANES))
                xw.append(x_pair * f_pair)
                cdec.append(_pair_select(a_cols[0][CHUNK - 1:CHUNK, :], a_cols[1][CHUNK - 1:CHUNK, :]))
        xw = jnp.concatenate(xw, axis=1).astype(BF16)
        st = []
        for g in range(SSM_GROUPS):
            b_g = bm[sl, g * SSM_STATE:(g + 1) * SSM_STATE].astype(BF16)
            st.append(_dot_tn(b_g, xw[:, g * SSM_HPG * SSM_HEADDIM:(g + 1) * SSM_HPG * SSM_HEADDIM]))
        ht_ref[...] = ht * jnp.exp(jnp.concatenate(cdec, axis=1)) + jnp.concatenate(st, axis=1)

    z = _dot(xn, wz_ref[...])
    o_ref[...] = _gate_norm_out(ybuf[...], z, pool_out, x, sn_ref[...], wo_ref[...])

    ubuf[0:POOL_HIST, :] = ubuf[tc:tc + POOL_HIST, :]
    cbuf[0:CONV_HIST, :] = cbuf[tc:tc + CONV_HIST, :]

    @pl.when(c == nc - 1)
    def _():
        opool_ref[...] = ubuf[POOL_HIST - POOL_BUF:POOL_HIST, :]
        oconv_ref[...] = cbuf[CONV_HIST - (CONV_W - 1):CONV_HIST, :]
        ossm_ref[...] = ht_ref[...].T.reshape(SSM_HEADS, SSM_HEADDIM, SSM_STATE)


def _mix_prompt(x, w, *, tc):
    b, s, d = x.shape
    assert s % tc == 0 and tc % CHUNK == 0
    consts = [w["g_mix"], w["w_u"], w["w_z"], w["w_xbc"], w["w_dt"], w["conv_w"], w["conv_b"],
              w["dt_bias"], w["a_log"], w["d_skip"], w["ssm_norm"], w["w_pool"], w["pool_scale"],
              w["w_out"]]
    tile = pl.BlockSpec((None, tc, d), lambda i, j: (i, j, 0))
    return pl.pallas_call(
        functools.partial(_mix_prompt_kernel, tc=tc),
        grid=(b, s // tc),
        in_specs=[tile] + [_const_spec(a.shape) for a in consts],
        out_specs=[
            tile,
            pl.BlockSpec((None, POOL_BUF, D_POOL), lambda i, j: (i, 0, 0)),
            pl.BlockSpec((None, CONV_W - 1, CONV_DIM), lambda i, j: (i, 0, 0)),
            pl.BlockSpec((None, SSM_HEADS, SSM_HEADDIM, SSM_STATE), lambda i, j: (i, 0, 0, 0)),
        ],
        out_shape=[
            jax.ShapeDtypeStruct((b, s, d), F32),
            jax.ShapeDtypeStruct((b, POOL_BUF, D_POOL), F32),
            jax.ShapeDtypeStruct((b, CONV_W - 1, CONV_DIM), F32),
            jax.ShapeDtypeStruct((b, SSM_HEADS, SSM_HEADDIM, SSM_STATE), F32),
        ],
        scratch_shapes=[
            pltpu.VMEM((POOL_HIST + tc, D_POOL), F32),
            pltpu.VMEM((CONV_HIST + tc, CONV_DIM), F32),
            pltpu.VMEM((SSM_STATE, D_SSM), F32),
            pltpu.VMEM((tc, D_SSM), F32),
        ],
        compiler_params=_cparams(("arbitrary", "arbitrary")),
        name="mix_prompt",
    )(x, *consts)


def _attend(q, k_ref, v_ref):
    outs = []
    for h in range(MEM_HEADS):
        hs = slice(h * MEM_HD, (h + 1) * MEM_HD)
        s = _dot_nt(q[:, hs].astype(BF16), k_ref[:, hs].astype(BF16)) * (MEM_HD ** -0.5)
        e = jnp.exp(s - jnp.max(s, axis=-1, keepdims=True))
        p = e / jnp.sum(e, axis=-1, keepdims=True)
        outs.append(_dot(p.astype(BF16), v_ref[:, hs].astype(BF16)))
    return jnp.concatenate(outs, axis=-1)


def _attn_prompt_kernel(x_ref, g_ref, wq_ref, k_ref, v_ref, wo_ref, o_ref):
    x = x_ref[...]
    q = _dot(_rms(x, g_ref[...]).astype(BF16), wq_ref[...])
    o = _attend(q, k_ref, v_ref).astype(BF16)
    o_ref[...] = x + _dot(o, wo_ref[...])


def _attn_prompt(x, k, v, g, wq, wo, *, tq):
    b, s, d = x.shape
    tile = pl.BlockSpec((None, tq, d), lambda i, j: (i, j, 0))
    mem = pl.BlockSpec((None, N_MEM, d), lambda i, j: (i, 0, 0))
    return pl.pallas_call(
        _attn_prompt_kernel,
        grid=(b, s // tq),
        in_specs=[tile, _const_spec(g.shape), _const_spec(wq.shape), mem, mem, _const_spec(wo.shape)],
        out_specs=tile,
        out_shape=jax.ShapeDtypeStruct((b, s, d), F32),
        compiler_params=_cparams(("parallel", "parallel")),
        name="attn_prompt",
    )(x, g, wq, k, v, wo)


def _attn_sample_kernel(q_ref, k_ref, v_ref, o_ref):
    o_ref[...] = _attend(q_ref[...], k_ref, v_ref)


def _attn_sample(q, k, v):
    b, tq, d = q.shape
    qs = pl.BlockSpec((None, tq, d), lambda i: (i, 0, 0))
    mem = pl.BlockSpec((None, N_MEM, d), lambda i: (i, 0, 0))
    return pl.pallas_call(
        _attn_sample_kernel,
        grid=(b,),
        in_specs=[qs, mem, mem],
        out_specs=qs,
        out_shape=jax.ShapeDtypeStruct((b, tq, d), F32),
        compiler_params=_cparams(("parallel",)),
        name="attn_sample",
    )(q, k, v)


def _mix_sample_pre_kernel(x_ref, sp_ref, sc_ref, g_ref, wu_ref, wz_ref, wx_ref, wdt_ref,
                           cw_ref, cb_ref, dtb_ref, alog_ref, dsk_ref, wp_ref, ps_ref, ex_ref,
                           pool_ref, z_ref, ydx_ref, expa_ref, xwt_ref, b_ref, c_ref, dec_ref,
                           npool_ref, nconv_ref, *, nb, t, start_pos):
    def rows(i):
        return slice(i * nb, (i + 1) * nb)

    x = x_ref[...]
    xn = _rms(x, g_ref[...]).astype(BF16)
    z_ref[...] = _dot(xn, wz_ref[...])

    u = _dot(xn, wu_ref[...])
    ext = [sp_ref[rows(i), :] for i in range(POOL_BUF)] + [u[rows(i)] for i in range(t)]
    ps = []
    for i in range(t):
        sums, cnts = [], []
        for w in POOL_WINDOWS:
            s = ext[POOL_BUF + i]
            for j in range(1, w):
                s = s + ext[POOL_BUF + i - j]
            sums.append(s)
            cnts.append(jnp.full((nb, D_POOL), float(min(start_pos + i + 1, w)), F32))
        mean = _lane_group_select(sums, POOL_GW) / _lane_group_select(cnts, POOL_GW)
        ps.append(mean - ext[POOL_BUF + i])
    p = jnp.concatenate(ps, axis=0).astype(BF16)
    pool_ref[...] = _dot(p, wp_ref[...]) * ps_ref[...]
    for i in range(POOL_BUF):
        npool_ref[rows(i), :] = ext[t + i]

    xr = _dot(xn, wx_ref[...])
    cext = [sc_ref[rows(i), :] for i in range(CONV_W - 1)] + [xr[rows(i)] for i in range(t)]
    xbc = []
    for i in range(t):
        acc = cext[i] * cw_ref[0:1, :]
        for k in range(1, CONV_W):
            acc = acc + cext[i + k] * cw_ref[k:k + 1, :]
        xbc.append(_silu(acc + cb_ref[...]))
    for i in range(CONV_W - 1):
        nconv_ref[rows(i), :] = cext[t + i]
    xs = [v[:, :D_SSM] for v in xbc]
    bm = [v[:, D_SSM:D_SSM + SSM_GROUPS * SSM_STATE] for v in xbc]
    cm = [v[:, D_SSM + SSM_GROUPS * SSM_STATE:] for v in xbc]
    for i in range(t):
        b_ref[rows(i), :] = bm[i]
        c_ref[rows(i), :] = cm[i]

    dt_all = jax.nn.softplus(_dot(xn, wdt_ref[...]) + dtb_ref[...])
    a_neg = -jnp.exp(alog_ref[...])
    dt = [dt_all[rows(i)] for i in range(t)]
    acs = []
    for i in range(t):
        da = dt[i] * a_neg
        acs.append(da if i == 0 else acs[-1] + da)
    dec_ref[...] = jnp.exp(acs[-1])

    def expand(v, exact):
        if exact:
            return jnp.dot(v, ex_ref[...], precision=lax.Precision.HIGHEST,
                           preferred_element_type=F32)
        return _dot(v.astype(BF16), ex_ref[...].astype(BF16))

    lane = lax.broadcasted_iota(jnp.int32, (nb, LANES), 1)
    xw = []
    for i in range(t):
        ydx = xs[i] * dsk_ref[...]
        for s in range(i + 1):
            sc = [jnp.sum(cm[i][:, g * SSM_STATE:(g + 1) * SSM_STATE]
                          * bm[s][:, g * SSM_STATE:(g + 1) * SSM_STATE], axis=-1, keepdims=True)
                  for g in range(SSM_GROUPS)]
            sc = jnp.where(lane < SSM_HPG, sc[0], sc[1])
            coef = sc * jnp.exp(acs[i] - acs[s]) * dt[s]
            ydx = ydx + expand(coef, False) * xs[s]
        ydx_ref[rows(i), :] = ydx
        expa_ref[rows(i), :] = expand(jnp.exp(acs[i]), True)
        xw.append(xs[i] * expand(jnp.exp(acs[-1] - acs[i]) * dt[i], True))
    xwt_ref[...] = jnp.concatenate(xw, axis=0).T.astype(BF16)


def _mix_sample_pre(x, sp, sc, w, *, nb, t, start_pos):
    n = nb * t
    consts = [w["g_mix"], w["w_u"], w["w_z"], w["w_xbc"], w["w_dt"], w["conv_w"], w["conv_b"],
              w["dt_bias"], w["a_log"], w["d_skip"], w["w_pool"], w["pool_scale"], w["expand"]]
    ins = [x, sp, sc] + consts
    outs = [
        jax.ShapeDtypeStruct((n, D_POOL), F32),
        jax.ShapeDtypeStruct((n, D_SSM), F32),
        jax.ShapeDtypeStruct((n, D_SSM), F32),
        jax.ShapeDtypeStruct((n, D_SSM), F32),
        jax.ShapeDtypeStruct((D_SSM, n), BF16),
        jax.ShapeDtypeStruct((n, SSM_GROUPS * SSM_STATE), F32),
        jax.ShapeDtypeStruct((n, SSM_GROUPS * SSM_STATE), F32),
        jax.ShapeDtypeStruct((nb, LANES), F32),
        jax.ShapeDtypeStruct((POOL_BUF * nb, D_POOL), F32),
        jax.ShapeDtypeStruct(((CONV_W - 1) * nb, CONV_DIM), F32),
    ]
    return pl.pallas_call(
        functools.partial(_mix_sample_pre_kernel, nb=nb, t=t, start_pos=start_pos),
        grid=(1,),
        in_specs=[_const_spec(a.shape) for a in ins],
        out_specs=[pl.BlockSpec(o.shape, lambda i: (0, 0)) for o in outs],
        out_shape=outs,
        compiler_params=_cparams(("arbitrary",)),
        name="mix_sample_pre",
    )(*ins)


def _ssd_sample_kernel(dec_ref, h0_ref, c_ref, ydx_ref, expa_ref, xwt_ref, b_ref,
                       y_ref, hout_ref, *, nb, t, bb):
    j = pl.program_id(0)
    gw = SSM_HPG * SSM_HEADDIM
    c_blk = c_ref[...].reshape(t * bb, SSM_GROUPS * SSM_STATE)
    row_seq = lax.broadcasted_iota(jnp.int32, (t * bb, SSM_STATE), 0) % bb
    all_seq = lax.broadcasted_iota(jnp.int32, (t * nb, SSM_STATE), 0) % nb
    b_all = b_ref[...]
    yoff = [jnp.zeros((t * bb, gw), F32) for _ in range(SSM_GROUPS)]
    for i in range(bb):
        seq = j * bb + i
        for g in range(SSM_GROUPS):
            gs = slice(g * SSM_STATE, (g + 1) * SSM_STATE)
            h0 = h0_ref[i, g * SSM_HPG:(g + 1) * SSM_HPG].reshape(gw, SSM_STATE)
            c_i = jnp.where(row_seq == i, c_blk[:, gs], 0.0).astype(BF16)
            yoff[g] = yoff[g] + _dot_nt(c_i, h0.astype(BF16))
            b_i = jnp.where(all_seq == seq, b_all[:, gs], 0.0).astype(BF16)
            st = _dot(xwt_ref[g * gw:(g + 1) * gw, :], b_i)
            for k in range(SSM_HPG):
                hd = g * SSM_HPG + k
                hout_ref[i, hd] = (h0_ref[i, hd] * dec_ref[seq * SSM_HEADS + hd]
                                   + st[k * SSM_HEADDIM:(k + 1) * SSM_HEADDIM])
    yoff = jnp.concatenate(yoff, axis=1).reshape(t, bb, D_SSM)
    y_ref[...] = ydx_ref[...] + yoff * expa_ref[...]


def _ssd_sample(dec, h0, c, ydx, expa, xwt, bmat, *, nb, t, bb):
    blk = lambda d: pl.BlockSpec((t, bb, d), lambda i: (0, i, 0))
    hblk = pl.BlockSpec((bb, SSM_HEADS, SSM_HEADDIM, SSM_STATE), lambda i: (i, 0, 0, 0))
    return pl.pallas_call(
        functools.partial(_ssd_sample_kernel, nb=nb, t=t, bb=bb),
        grid=(nb // bb,),
        in_specs=[pl.BlockSpec(memory_space=pltpu.SMEM), hblk, blk(SSM_GROUPS * SSM_STATE),
                  blk(D_SSM), blk(D_SSM), _const_spec(xwt.shape), _const_spec(bmat.shape)],
        out_specs=[blk(D_SSM), hblk],
        out_shape=[jax.ShapeDtypeStruct((t, nb, D_SSM), F32),
                   jax.ShapeDtypeStruct(h0.shape, F32)],
        compiler_params=_cparams(("parallel",)),
        name="ssd_sample",
    )(dec, h0, c, ydx, expa, xwt, bmat)


def _mix_sample_post_kernel(y_ref, z_ref, pool_ref, x_ref, sn_ref, wo_ref, gq_ref, wq_ref,
                            x2_ref, q_ref):
    x2 = _gate_norm_out(y_ref[...], z_ref[...], pool_ref[...], x_ref[...], sn_ref[...], wo_ref[...])
    x2_ref[...] = x2
    q_ref[...] = _dot(_rms(x2, gq_ref[...]).astype(BF16), wq_ref[...])


def _mix_sample_post(y, z, pool, x, sn, wo, gq, wq):
    ins = [y, z, pool, x, sn, wo, gq, wq]
    outs = [jax.ShapeDtypeStruct(x.shape, F32)] * 2
    return pl.pallas_call(
        _mix_sample_post_kernel,
        grid=(1,),
        in_specs=[_const_spec(a.shape) for a in ins],
        out_specs=[pl.BlockSpec(x.shape, lambda i: (0, 0))] * 2,
        out_shape=outs,
        compiler_params=_cparams(("arbitrary",)),
        name="mix_sample_post",
    )(*ins)


def _oproj_kernel(x_ref, o_ref, wo_ref, out_ref):
    out_ref[...] = x_ref[...] + _dot(o_ref[...].astype(BF16), wo_ref[...])


def _oproj(x, o, wo):
    ins = [x, o, wo]
    return pl.pallas_call(
        _oproj_kernel,
        grid=(1,),
        in_specs=[_const_spec(a.shape) for a in ins],
        out_specs=pl.BlockSpec(x.shape, lambda i: (0, 0)),
        out_shape=jax.ShapeDtypeStruct(x.shape, F32),
        compiler_params=_cparams(("arbitrary",)),
        name="oproj",
    )(*ins)


def _prep_weights(g_ffn1, w1_gate, w1_up, w1_down, g_mix, w_in, conv_w, conv_b, dt_bias, a_log,
                  d_skip, ssm_norm, w_pool, pool_scale, w_out, g_mem, w_mem_k, w_mem_v, g_xq,
                  w_xq, w_xo, g_ffn2, w2_gate, w2_up, w2_down, g_final):
    row = lambda v: v.reshape(1, -1).astype(F32)
    pad_heads = lambda v: jnp.pad(v.astype(F32), (0, LANES - SSM_HEADS)).reshape(1, LANES)
    o1 = D_POOL
    o2 = o1 + D_SSM
    o3 = o2 + CONV_DIM
    w_pool_bd = jnp.zeros((D_POOL, D_POOL), F32)
    for g in range(len(POOL_WINDOWS)):
        w_pool_bd = w_pool_bd.at[g * POOL_GW:(g + 1) * POOL_GW, g * POOL_GW:(g + 1) * POOL_GW].set(w_pool[g])
    head_of_chan = jnp.arange(D_SSM) // SSM_HEADDIM
    expand = (jnp.arange(LANES)[:, None] == head_of_chan[None, :]).astype(F32)
    return dict(
        g_ffn1=row(g_ffn1), w1_gate=w1_gate.astype(BF16), w1_up=w1_up.astype(BF16),
        w1_down=w1_down.astype(BF16),
        g_mix=row(g_mix), w_u=w_in[:, :o1].astype(BF16), w_z=w_in[:, o1:o2].astype(BF16),
        w_xbc=w_in[:, o2:o3].astype(BF16),
        w_dt=jnp.pad(w_in[:, o3:], ((0, 0), (0, LANES - SSM_HEADS))).astype(BF16),
        conv_w=conv_w.astype(F32), conv_b=row(conv_b), dt_bias=pad_heads(dt_bias),
        a_log=pad_heads(a_log), d_skip=row(jnp.repeat(d_skip, SSM_HEADDIM)), ssm_norm=row(ssm_norm),
        w_pool=w_pool_bd.astype(BF16), pool_scale=row(pool_scale), w_out=w_out.astype(BF16),
        g_mem=row(g_mem), w_mem_k=w_mem_k.astype(BF16), w_mem_v=w_mem_v.astype(BF16),
        g_xq=row(g_xq), w_xq=w_xq.astype(BF16), w_xo=w_xo.astype(BF16),
        g_ffn2=row(g_ffn2), w2_gate=w2_gate.astype(BF16), w2_up=w2_up.astype(BF16),
        w2_down=w2_down.astype(BF16), g_final=row(g_final), expand=expand,
    )


def _prompt_layer(x_prompt, mem_prompt, w):
    b, s, d = x_prompt.shape
    k, v = _memkv(mem_prompt.reshape(b * N_MEM, d), w["g_mem"], w["w_mem_k"], w["w_mem_v"], tm=512)
    k = k.reshape(b, N_MEM, d)
    v = v.reshape(b, N_MEM, d)
    x = _ffn(x_prompt.reshape(b * s, d), w["g_ffn1"], w["w1_gate"], w["w1_up"], w["w1_down"],
             w["g_final"], final=False, tm=512)
    x, pool, conv, ssm = _mix_prompt(x.reshape(b, s, d), w, tc=256)
    x = _attn_prompt(x, k, v, w["g_xq"], w["w_xq"], w["w_xo"], tq=512)
    y = _ffn(x.reshape(b * s, d), w["g_ffn2"], w["w2_gate"], w["w2_up"], w["w2_down"],
             w["g_final"], final=True, tm=512)
    return y.reshape(b, s, d), k, v, pool, conv, ssm


def _sample_layer(x_sample, mem_k, mem_v, state_pool, state_conv, state_ssm, start_pos, w):
    nb, t, d = x_sample.shape
    n = nb * t
    tmaj = lambda a: jnp.swapaxes(a, 0, 1).reshape(-1, a.shape[-1])
    bmaj = lambda a, r: jnp.swapaxes(a.reshape(r, nb, -1), 0, 1)
    x = _ffn(tmaj(x_sample), w["g_ffn1"], w["w1_gate"], w["w1_up"], w["w1_down"], w["g_final"],
             final=False, tm=n)
    (pool_out, z, ydx, expa, xwt, bmat, cmat, dec, new_pool, new_conv) = _mix_sample_pre(
        x, tmaj(state_pool), tmaj(state_conv), w, nb=nb, t=t, start_pos=start_pos)
    r3 = lambda a: a.reshape(t, nb, a.shape[-1])
    y, new_ssm = _ssd_sample(dec[:, :SSM_HEADS].reshape(-1), state_ssm, r3(cmat), r3(ydx), r3(expa),
                             xwt, bmat, nb=nb, t=t, bb=8)
    x2, q = _mix_sample_post(y.reshape(n, D_SSM), z, pool_out, x, w["ssm_norm"], w["w_out"],
                             w["g_xq"], w["w_xq"])
    tq = 8
    q = jnp.pad(bmaj(q, t), ((0, 0), (0, tq - t), (0, 0)))
    o = _attn_sample(q, mem_k.reshape(nb, N_MEM, d), mem_v.reshape(nb, N_MEM, d))
    x3 = _oproj(bmaj(x2, t).reshape(n, d), o[:, :t].reshape(n, d), w["w_xo"])
    y = _ffn(x3, w["g_ffn2"], w["w2_gate"], w["w2_up"], w["w2_down"], w["g_final"], final=True, tm=n)
    return (y.reshape(nb, t, d), bmaj(new_pool, POOL_BUF), bmaj(new_conv, CONV_W - 1), new_ssm)


def kernel(x_prompt, x_sample, mem_prompt, cache_mem_k, cache_mem_v, state_pool, state_conv, state_ssm,
           g_ffn1, w1_gate, w1_up, w1_down, g_mix, w_in, conv_w, conv_b, dt_bias, a_log, d_skip,
           ssm_norm, w_pool, pool_scale, w_out, g_mem, w_mem_k, w_mem_v, g_xq, w_xq, w_xo,
           g_ffn2, w2_gate, w2_up, w2_down, g_final):
    assert g_ffn1.shape[0] == 1, "single-layer model"
    past_len = 16384
    w = _prep_weights(g_ffn1[0], w1_gate[0], w1_up[0], w1_down[0], g_mix[0], w_in[0], conv_w[0],
                      conv_b[0], dt_bias[0], a_log[0], d_skip[0], ssm_norm[0], w_pool[0],
                      pool_scale[0], w_out[0], g_mem[0], w_mem_k[0], w_mem_v[0], g_xq[0], w_xq[0],
                      w_xo[0], g_ffn2[0], w2_gate[0], w2_up[0], w2_down[0], g_final)
    bp = x_prompt.shape[0]
    y_p, k_p, v_p, pool_p, conv_p, ssm_p = _prompt_layer(x_prompt, mem_prompt, w)
    y_s, pool_s, conv_s, ssm_s = _sample_layer(x_sample, cache_mem_k[0], cache_mem_v[0], state_pool[0],
                                               state_conv[0], state_ssm[0], past_len, w)
    kv_shape = (1, bp, N_MEM, MEM_HEADS, MEM_HD)
    return (y_p, y_s, k_p.reshape(kv_shape), v_p.reshape(kv_shape), pool_p[None], conv_p[None],
            ssm_p[None], pool_s[None], conv_s[None], ssm_s[None])
```

```python
import functools

import jax
import jax.numpy as jnp
from jax import lax
from jax.experimental import pallas as pl
from jax.experimental.pallas import tpu as pltpu

F32 = jnp.float32
BF16 = jnp.bfloat16

D_MODEL = 1024
D_POOL = 256
POOL_WINDOWS = (2, 4, 8, 16)
POOL_GW = 64
POOL_BUF = 15
D_SSM = 768
SSM_HEADDIM = 64
SSM_HEADS = 12
SSM_GROUPS = 2
SSM_HPG = 6
SSM_STATE = 128
CONV_W = 4
CONV_DIM = 1280
N_MEM = 256
MEM_HEADS = 4
MEM_HD = 256
D_FF = 2816
EPS = 1e-6
LANES = 128
CHUNK = 128
POOL_HIST = 32
CONV_HIST = 8
VMEM_LIMIT = 56 * 1024 * 1024


def _cparams(sem):
    return pltpu.CompilerParams(dimension_semantics=sem, vmem_limit_bytes=VMEM_LIMIT)


def _const_spec(shape):
    nd = len(shape)
    return pl.BlockSpec(shape, lambda *_: (0,) * nd, pipeline_mode=pl.Buffered(1))


def _rms(x, g):
    ms = jnp.mean(x * x, axis=-1, keepdims=True)
    return x * lax.rsqrt(ms + EPS) * g


def _silu(x):
    return x * jax.nn.sigmoid(x)


def _dot(a, b):
    return jnp.dot(a, b, preferred_element_type=F32)


def _dot_nt(a, b):
    return lax.dot_general(a, b, (((1,), (1,)), ((), ())), preferred_element_type=F32)


def _dot_tn(a, b):
    return lax.dot_general(a, b, (((0,), (0,)), ((), ())), preferred_element_type=F32)


def _ffn_kernel(x_ref, g_ref, wg_ref, wu_ref, wd_ref, gf_ref, o_ref, *, final):
    x = x_ref[...]
    xn = _rms(x, g_ref[...]).astype(BF16)
    gate = _dot(xn, wg_ref[...])
    up = _dot(xn, wu_ref[...])
    h = (_silu(gate) * up).astype(BF16)
    out = x + 0.5 * _dot(h, wd_ref[...])
    if final:
        out = _rms(out, gf_ref[...])
    o_ref[...] = out


def _ffn(x, g, wg, wu, wd, gf, *, final, tm):
    t, d = x.shape
    assert t % tm == 0
    return pl.pallas_call(
        functools.partial(_ffn_kernel, final=final),
        grid=(t // tm,),
        in_specs=[
            pl.BlockSpec((tm, d), lambda i: (i, 0)),
            _const_spec(g.shape), _const_spec(wg.shape), _const_spec(wu.shape),
            _const_spec(wd.shape), _const_spec(gf.shape),
        ],
        out_specs=pl.BlockSpec((tm, d), lambda i: (i, 0)),
        out_shape=jax.ShapeDtypeStruct((t, d), F32),
        compiler_params=_cparams(("parallel",)),
        name="ffn_final" if final else "ffn",
    )(x, g, wg, wu, wd, gf)


def _memkv_kernel(m_ref, g_ref, wk_ref, wv_ref, k_ref, v_ref):
    mn = _rms(m_ref[...], g_ref[...]).astype(BF16)
    k_ref[...] = _dot(mn, wk_ref[...])
    v_ref[...] = _dot(mn, wv_ref[...])


def _memkv(mem, g, wk, wv, *, tm):
    t, d = mem.shape
    return pl.pallas_call(
        _memkv_kernel,
        grid=(t // tm,),
        in_specs=[pl.BlockSpec((tm, d), lambda i: (i, 0)), _const_spec(g.shape),
                  _const_spec(wk.shape), _const_spec(wv.shape)],
        out_specs=[pl.BlockSpec((tm, d), lambda i: (i, 0))] * 2,
        out_shape=[jax.ShapeDtypeStruct((t, d), F32)] * 2,
        compiler_params=_cparams(("parallel",)),
        name="memkv",
    )(mem, g, wk, wv)


def _lane_group_select(vals, width):
    lane = lax.broadcasted_iota(jnp.int32, vals[0].shape, 1)
    out = vals[-1]
    for g in range(len(vals) - 2, -1, -1):
        out = jnp.where(lane < (g + 1) * width, vals[g], out)
    return out


def _pair_select(a, b):
    lane = lax.broadcasted_iota(jnp.int32, a.shape, 1)
    return jnp.where(lane < SSM_HEADDIM, a, b)


def _col_bcast(m, k, width):
    return jnp.broadcast_to(m[:, k:k + 1], (m.shape[0], width))


def _gate_norm_out(y, z, pool_out, x_res, ssm_norm, w_out):
    yn = _rms(y * _silu(z), ssm_norm)
    cat = jnp.concatenate([pool_out, yn], axis=-1).astype(BF16)
    return x_res + _dot(cat, w_out)


def _mix_prompt_kernel(x_ref, g_ref, wu_ref, wz_ref, wx_ref, wdt_ref, cw_ref, cb_ref,
                       dtb_ref, alog_ref, dsk_ref, sn_ref, wp_ref, ps_ref, wo_ref,
                       o_ref, opool_ref, oconv_ref, ossm_ref,
                       ubuf, cbuf, ht_ref, ybuf, *, tc):
    c = pl.program_id(1)
    nc = pl.num_programs(1)

    @pl.when(c == 0)
    def _():
        ubuf[0:POOL_HIST, :] = jnp.zeros((POOL_HIST, D_POOL), F32)
        cbuf[0:CONV_HIST, :] = jnp.zeros((CONV_HIST, CONV_DIM), F32)
        ht_ref[...] = jnp.zeros_like(ht_ref)

    x = x_ref[...]
    xn = _rms(x, g_ref[...]).astype(BF16)

    u = _dot(xn, wu_ref[...])
    ubuf[POOL_HIST:POOL_HIST + tc, :] = u
    e = ubuf[...]
    s2 = e + pltpu.roll(e, 1, 0)
    s4 = s2 + pltpu.roll(s2, 2, 0)
    s8 = s4 + pltpu.roll(s4, 4, 0)
    s16 = s8 + pltpu.roll(s8, 8, 0)
    sums = _lane_group_select([s[POOL_HIST:] for s in (s2, s4, s8, s16)], POOL_GW)
    pos = c * tc + lax.broadcasted_iota(jnp.int32, (tc, D_POOL), 0)
    win = _lane_group_select(
        [jnp.full((tc, D_POOL), w, jnp.int32) for w in POOL_WINDOWS], POOL_GW)
    cnt = jnp.minimum(pos + 1, win).astype(F32)
    p = (sums / cnt - u).astype(BF16)
    pool_out = _dot(p, wp_ref[...]) * ps_ref[...]

    xbc_raw = _dot(xn, wx_ref[...])
    cbuf[CONV_HIST:CONV_HIST + tc, :] = xbc_raw
    ce = cbuf[...]
    acc = ce * cw_ref[CONV_W - 1:CONV_W, :]
    for j in range(1, CONV_W):
        acc = acc + pltpu.roll(ce, j, 0) * cw_ref[CONV_W - 1 - j:CONV_W - j, :]
    xbc = _silu(acc[CONV_HIST:] + cb_ref[...])
    xs = xbc[:, :D_SSM]
    bm = xbc[:, D_SSM:D_SSM + SSM_GROUPS * SSM_STATE]
    cm = xbc[:, D_SSM + SSM_GROUPS * SSM_STATE:]

    dt = jax.nn.softplus(_dot(xn, wdt_ref[...]) + dtb_ref[...])
    a_neg = -jnp.exp(alog_ref[...])
    row = lax.broadcasted_iota(jnp.int32, (CHUNK, CHUNK), 0)
    col = lax.broadcasted_iota(jnp.int32, (CHUNK, CHUNK), 1)
    causal = row >= col
    lane1 = lax.broadcasted_iota(jnp.int32, (CHUNK, LANES), 1)
    lo_half = lane1 < SSM_HEADDIM
    for j in range(tc // CHUNK):
        sl = slice(j * CHUNK, (j + 1) * CHUNK)
        dt_c = dt[sl]
        acs = dt_c * a_neg
        sh = 1
        while sh < CHUNK:
            acs = acs + jnp.where(row >= sh, pltpu.roll(acs, sh, 0), 0.0)
            sh *= 2
        acs_last = acs[CHUNK - 1:CHUNK, :]
        fdec = jnp.exp(acs_last - acs) * dt_c
        acs_t = acs.T
        dt_t = dt_c.T
        xs_c = xs[sl]
        ht = ht_ref[...]
        cdec = []
        xw = []
        for g in range(SSM_GROUPS):
            c_g = cm[sl, g * SSM_STATE:(g + 1) * SSM_STATE]
            b_g = bm[sl, g * SSM_STATE:(g + 1) * SSM_STATE].astype(BF16)
            scores = _dot_nt(c_g.astype(BF16), b_g)
            for q in range(SSM_HPG // 2):
                k0 = g * SSM_HPG + 2 * q
                lhs = []
                a_cols = []
                for k in (k0, k0 + 1):
                    a_col = _col_bcast(acs, k, CHUNK)
                    a_cols.append(a_col)
                    seg = a_col - acs_t[k:k + 1, :]
                    decay = jnp.exp(jnp.where(causal, seg, -jnp.inf))
                    lhs.append((scores * decay * dt_t[k:k + 1, :]).astype(BF16))
                for a_col in a_cols:
                    lhs.append((c_g * jnp.exp(a_col)).astype(BF16))
                lsl = slice((k0 // 2) * LANES, (k0 // 2 + 1) * LANES)
                x_pair = xs_c[:, lsl]
                h_pair = ht[:, lsl]
                zero = jnp.zeros_like(x_pair)
                rhs = jnp.concatenate(
                    [jnp.where(lo_half, x_pair, zero), jnp.where(lo_half, zero, x_pair),
                     jnp.where(lo_half, h_pair, zero), jnp.where(lo_half, zero, h_pair)],
                    axis=0).astype(BF16)
                y_pair = _dot(jnp.concatenate(lhs, axis=1), rhs)
                ybuf[sl, lsl] = y_pair + x_pair * dsk_ref[:, lsl]
                f_pair = _pair_select(_col_bcast(fdec, k0, LANES), _col_bcast(fdec, k0 + 1, LANES))
                xw.append(x_pair * f_pair)
                cdec.append(_pair_select(a_cols[0][CHUNK - 1:CHUNK, :], a_cols[1][CHUNK - 1:CHUNK, :]))
        xw = jnp.concatenate(xw, axis=1).astype(BF16)
        st = []
        for g in range(SSM_GROUPS):
            b_g = bm[sl, g * SSM_STATE:(g + 1) * SSM_STATE].astype(BF16)
            st.append(_dot_tn(b_g, xw[:, g * SSM_HPG * SSM_HEADDIM:(g + 1) * SSM_HPG * SSM_HEADDIM]))
        ht_ref[...] = ht * jnp.exp(jnp.concatenate(cdec, axis=1)) + jnp.concatenate(st, axis=1)

    z = _dot(xn, wz_ref[...])
    o_ref[...] = _gate_norm_out(ybuf[...], z, pool_out, x, sn_ref[...], wo_ref[...])

    ubuf[0:POOL_HIST, :] = ubuf[tc:tc + POOL_HIST, :]
    cbuf[0:CONV_HIST, :] = cbuf[tc:tc + CONV_HIST, :]

    @pl.when(c == nc - 1)
    def _():
        opool_ref[...] = ubuf[POOL_HIST - POOL_BUF:POOL_HIST, :]
        oconv_ref[...] = cbuf[CONV_HIST - (CONV_W - 1):CONV_HIST, :]
        ossm_ref[...] = ht_ref[...].T.reshape(SSM_HEADS, SSM_HEADDIM, SSM_STATE)


def _mix_prompt(x, w, *, tc):
    b, s, d = x.shape
    assert s % tc == 0 and tc % CHUNK == 0
    consts = [w["g_mix"], w["w_u"], w["w_z"], w["w_xbc"], w["w_dt"], w["conv_w"], w["conv_b"],
              w["dt_bias"], w["a_log"], w["d_skip"], w["ssm_norm"], w["w_pool"], w["pool_scale"],
              w["w_out"]]
    tile = pl.BlockSpec((None, tc, d), lambda i, j: (i, j, 0))
    return pl.pallas_call(
        functools.partial(_mix_prompt_kernel, tc=tc),
        grid=(b, s // tc),
        in_specs=[tile] + [_const_spec(a.shape) for a in consts],
        out_specs=[
            tile,
            pl.BlockSpec((None, POOL_BUF, D_POOL), lambda i, j: (i, 0, 0)),
            pl.BlockSpec((None, CONV_W - 1, CONV_DIM), lambda i, j: (i, 0, 0)),
            pl.BlockSpec((None, SSM_HEADS, SSM_HEADDIM, SSM_STATE), lambda i, j: (i, 0, 0, 0)),
        ],
        out_shape=[
            jax.ShapeDtypeStruct((b, s, d), F32),
            jax.ShapeDtypeStruct((b, POOL_BUF, D_POOL), F32),
            jax.ShapeDtypeStruct((b, CONV_W - 1, CONV_DIM), F32),
            jax.ShapeDtypeStruct((b, SSM_HEADS, SSM_HEADDIM, SSM_STATE), F32),
        ],
        scratch_shapes=[
            pltpu.VMEM((POOL_HIST + tc, D_POOL), F32),
            pltpu.VMEM((CONV_HIST + tc, CONV_DIM), F32),
            pltpu.VMEM((SSM_STATE, D_SSM), F32),
            pltpu.VMEM((tc, D_SSM), F32),
        ],
        compiler_params=_cparams(("arbitrary", "arbitrary")),
        name="mix_prompt",
    )(x, *consts)


def _attend(q, k_ref, v_ref):
    outs = []
    for h in range(MEM_HEADS):
        hs = slice(h * MEM_HD, (h + 1) * MEM_HD)
        s = _dot_nt(q[:, hs].astype(BF16), k_ref[:, hs].astype(BF16)) * (MEM_HD ** -0.5)
        e = jnp.exp(s - jnp.max(s, axis=-1, keepdims=True))
        p = e / jnp.sum(e, axis=-1, keepdims=True)
        outs.append(_dot(p.astype(BF16), v_ref[:, hs].astype(BF16)))
    return jnp.concatenate(outs, axis=-1)


def _attn_prompt_kernel(x_ref, g_ref, wq_ref, k_ref, v_ref, wo_ref, o_ref):
    x = x_ref[...]
    q = _dot(_rms(x, g_ref[...]).astype(BF16), wq_ref[...])
    o = _attend(q, k_ref, v_ref).astype(BF16)
    o_ref[...] = x + _dot(o, wo_ref[...])


def _attn_prompt(x, k, v, g, wq, wo, *, tq):
    b, s, d = x.shape
    tile = pl.BlockSpec((None, tq, d), lambda i, j: (i, j, 0))
    mem = pl.BlockSpec((None, N_MEM, d), lambda i, j: (i, 0, 0))
    return pl.pallas_call(
        _attn_prompt_kernel,
        grid=(b, s // tq),
        in_specs=[tile, _const_spec(g.shape), _const_spec(wq.shape), mem, mem, _const_spec(wo.shape)],
        out_specs=tile,
        out_shape=jax.ShapeDtypeStruct((b, s, d), F32),
        compiler_params=_cparams(("parallel", "parallel")),
        name="attn_prompt",
    )(x, g, wq, k, v, wo)


def _attn_sample_kernel(q_ref, k_ref, v_ref, o_ref, *, bb):
    nq = q_ref.shape[1]
    rows = N_MEM * MEM_HEADS
    q_head = lax.broadcasted_iota(jnp.int32, (nq, rows), 0) % MEM_HEADS
    kv_head = lax.broadcasted_iota(jnp.int32, (nq, rows), 1) % MEM_HEADS
    same_head = q_head == kv_head
    q_all = q_ref[...].reshape(bb * nq, MEM_HD)
    q_all = jnp.concatenate([q_all, jnp.zeros((LANES - bb * nq, MEM_HD), F32)], axis=0).astype(BF16)
    s_t = [_dot_nt(k_ref[i].reshape(rows, MEM_HD).astype(BF16), q_all) for i in range(bb)]
    ps = []
    for i in range(bb):
        s = s_t[i].T[i * nq:(i + 1) * nq] * (MEM_HD ** -0.5)
        s = jnp.where(same_head, s, -jnp.inf)
        e = jnp.exp(s - jnp.max(s, axis=-1, keepdims=True))
        ps.append((e / jnp.sum(e, axis=-1, keepdims=True)).astype(BF16))
    for i in range(bb):
        o_ref[i] = _dot(ps[i], v_ref[i].reshape(rows, MEM_HD).astype(BF16))


def _attn_sample(q, k, v, *, bb):
    nb, nq, hd = q.shape
    qs = pl.BlockSpec((bb, nq, hd), lambda i: (i, 0, 0))
    mem = pl.BlockSpec((bb, N_MEM, MEM_HEADS, hd), lambda i: (i, 0, 0, 0))
    return pl.pallas_call(
        functools.partial(_attn_sample_kernel, bb=bb),
        grid=(nb // bb,),
        in_specs=[qs, mem, mem],
        out_specs=qs,
        out_shape=jax.ShapeDtypeStruct((nb, nq, hd), F32),
        compiler_params=_cparams(("parallel",)),
        name="attn_sample",
    )(q, k, v)


def _mix_sample_pre_kernel(x_ref, sp_ref, sc_ref, g_ref, wu_ref, wz_ref, wx_ref, wdt_ref,
                           cw_ref, cb_ref, dtb_ref, alog_ref, dsk_ref, wp_ref, ps_ref, ex_ref,
                           pool_ref, z_ref, ydx_ref, expa_ref, xwt_ref, b_ref, c_ref, dec_ref,
                           npool_ref, nconv_ref, *, nb, t, start_pos):
    def rows(i):
        return slice(i * nb, (i + 1) * nb)

    x = x_ref[...]
    xn = _rms(x, g_ref[...]).astype(BF16)
    z_ref[...] = _dot(xn, wz_ref[...])

    u = _dot(xn, wu_ref[...])
    ext = [sp_ref[rows(i), :] for i in range(POOL_BUF)] + [u[rows(i)] for i in range(t)]
    ps = []
    for i in range(t):
        sums, cnts = [], []
        for w in POOL_WINDOWS:
            s = ext[POOL_BUF + i]
            for j in range(1, w):
                s = s + ext[POOL_BUF + i - j]
            sums.append(s)
            cnts.append(jnp.full((nb, D_POOL), float(min(start_pos + i + 1, w)), F32))
        mean = _lane_group_select(sums, POOL_GW) / _lane_group_select(cnts, POOL_GW)
        ps.append(mean - ext[POOL_BUF + i])
    p = jnp.concatenate(ps, axis=0).astype(BF16)
    pool_ref[...] = _dot(p, wp_ref[...]) * ps_ref[...]
    for i in range(POOL_BUF):
        npool_ref[rows(i), :] = ext[t + i]

    xr = _dot(xn, wx_ref[...])
    cext = [sc_ref[rows(i), :] for i in range(CONV_W - 1)] + [xr[rows(i)] for i in range(t)]
    xbc = []
    for i in range(t):
        acc = cext[i] * cw_ref[0:1, :]
        for k in range(1, CONV_W):
            acc = acc + cext[i + k] * cw_ref[k:k + 1, :]
        xbc.append(_silu(acc + cb_ref[...]))
    for i in range(CONV_W - 1):
        nconv_ref[rows(i), :] = cext[t + i]
    xs = [v[:, :D_SSM] for v in xbc]
    bm = [v[:, D_SSM:D_SSM + SSM_GROUPS * SSM_STATE] for v in xbc]
    cm = [v[:, D_SSM + SSM_GROUPS * SSM_STATE:] for v in xbc]
    for i in range(t):
        b_ref[rows(i), :] = bm[i]
        c_ref[rows(i), :] = cm[i]

    dt_all = jax.nn.softplus(_dot(xn, wdt_ref[...]) + dtb_ref[...])
    a_neg = -jnp.exp(alog_ref[...])
    dt = [dt_all[rows(i)] for i in range(t)]
    acs = []
    for i in range(t):
        da = dt[i] * a_neg
        acs.append(da if i == 0 else acs[-1] + da)
    dec_ref[...] = jnp.exp(acs[-1])

    def expand(v, exact):
        if exact:
            return jnp.dot(v, ex_ref[...], precision=lax.Precision.HIGHEST,
                           preferred_element_type=F32)
        return _dot(v.astype(BF16), ex_ref[...].astype(BF16))

    lane = lax.broadcasted_iota(jnp.int32, (nb, LANES), 1)
    xw = []
    for i in range(t):
        ydx = xs[i] * dsk_ref[...]
        for s in range(i + 1):
            sc = [jnp.sum(cm[i][:, g * SSM_STATE:(g + 1) * SSM_STATE]
                          * bm[s][:, g * SSM_STATE:(g + 1) * SSM_STATE], axis=-1, keepdims=True)
                  for g in range(SSM_GROUPS)]
            sc = jnp.where(lane < SSM_HPG, sc[0], sc[1])
            coef = sc * jnp.exp(acs[i] - acs[s]) * dt[s]
            ydx = ydx + expand(coef, False) * xs[s]
        ydx_ref[rows(i), :] = ydx
        expa_ref[rows(i), :] = expand(jnp.exp(acs[i]), True)
        xw.append(xs[i] * expand(jnp.exp(acs[-1] - acs[i]) * dt[i], True))
    xwt_ref[...] = jnp.concatenate(xw, axis=0).T.astype(BF16)


def _mix_sample_pre(x, sp, sc, w, *, nb, t, start_pos):
    n = nb * t
    consts = [w["g_mix"], w["w_u"], w["w_z"], w["w_xbc"], w["w_dt"], w["conv_w"], w["conv_b"],
              w["dt_bias"], w["a_log"], w["d_skip"], w["w_pool"], w["pool_scale"], w["expand"]]
    ins = [x, sp, sc] + consts
    outs = [
        jax.ShapeDtypeStruct((n, D_POOL), F32),
        jax.ShapeDtypeStruct((n, D_SSM), F32),
        jax.ShapeDtypeStruct((n, D_SSM), F32),
        jax.ShapeDtypeStruct((n, D_SSM), F32),
        jax.ShapeDtypeStruct((D_SSM, n), BF16),
        jax.ShapeDtypeStruct((n, SSM_GROUPS * SSM_STATE), F32),
        jax.ShapeDtypeStruct((n, SSM_GROUPS * SSM_STATE), F32),
        jax.ShapeDtypeStruct((nb, LANES), F32),
        jax.ShapeDtypeStruct((POOL_BUF * nb, D_POOL), F32),
        jax.ShapeDtypeStruct(((CONV_W - 1) * nb, CONV_DIM), F32),
    ]
    return pl.pallas_call(
        functools.partial(_mix_sample_pre_kernel, nb=nb, t=t, start_pos=start_pos),
        grid=(1,),
        in_specs=[_const_spec(a.shape) for a in ins],
        out_specs=[pl.BlockSpec(o.shape, lambda i: (0, 0)) for o in outs],
        out_shape=outs,
        compiler_params=_cparams(("arbitrary",)),
        name="mix_sample_pre",
    )(*ins)


def _ssd_sample_kernel(dec_ref, h0_ref, c_ref, ydx_ref, expa_ref, xwt_ref, b_ref,
                       y_ref, hout_ref, *, nb, t, bb):
    j = pl.program_id(0)
    gw = SSM_HPG * SSM_HEADDIM
    c_blk = c_ref[...].reshape(t * bb, SSM_GROUPS * SSM_STATE)
    row_seq = lax.broadcasted_iota(jnp.int32, (t * bb, SSM_STATE), 0) % bb
    all_seq = lax.broadcasted_iota(jnp.int32, (t * nb, SSM_STATE), 0) % nb
    b_all = b_ref[...]
    yoff = [jnp.zeros((t * bb, gw), F32) for _ in range(SSM_GROUPS)]
    for i in range(bb):
        seq = j * bb + i
        for g in range(SSM_GROUPS):
            gs = slice(g * SSM_STATE, (g + 1) * SSM_STATE)
            h0 = h0_ref[i, g * SSM_HPG:(g + 1) * SSM_HPG].reshape(gw, SSM_STATE)
            c_i = jnp.where(row_seq == i, c_blk[:, gs], 0.0).astype(BF16)
            yoff[g] = yoff[g] + _dot_nt(c_i, h0.astype(BF16))
            b_i = jnp.where(all_seq == seq, b_all[:, gs], 0.0).astype(BF16)
            st = _dot(xwt_ref[g * gw:(g + 1) * gw, :], b_i)
            for k in range(SSM_HPG):
                hd = g * SSM_HPG + k
                hout_ref[i, hd] = (h0_ref[i, hd] * dec_ref[seq * SSM_HEADS + hd]
                                   + st[k * SSM_HEADDIM:(k + 1) * SSM_HEADDIM])
    yoff = jnp.concatenate(yoff, axis=1).reshape(t, bb, D_SSM)
    y_ref[...] = ydx_ref[...] + yoff * expa_ref[...]


def _ssd_sample(dec, h0, c, ydx, expa, xwt, bmat, *, nb, t, bb):
    blk = lambda d: pl.BlockSpec((t, bb, d), lambda i: (0, i, 0))
    hblk = pl.BlockSpec((bb, SSM_HEADS, SSM_HEADDIM, SSM_STATE), lambda i: (i, 0, 0, 0))
    return pl.pallas_call(
        functools.partial(_ssd_sample_kernel, nb=nb, t=t, bb=bb),
        grid=(nb // bb,),
        in_specs=[pl.BlockSpec(memory_space=pltpu.SMEM), hblk, blk(SSM_GROUPS * SSM_STATE),
                  blk(D_SSM), blk(D_SSM), _const_spec(xwt.shape), _const_spec(bmat.shape)],
        out_specs=[blk(D_SSM), hblk],
        out_shape=[jax.ShapeDtypeStruct((t, nb, D_SSM), F32),
                   jax.ShapeDtypeStruct(h0.shape, F32)],
        compiler_params=_cparams(("parallel",)),
        name="ssd_sample",
    )(dec, h0, c, ydx, expa, xwt, bmat)


def _mix_sample_post_kernel(y_ref, z_ref, pool_ref, x_ref, sn_ref, wo_ref, gq_ref, wq_ref,
                            x2_ref, q_ref):
    x2 = _gate_norm_out(y_ref[...], z_ref[...], pool_ref[...], x_ref[...], sn_ref[...], wo_ref[...])
    x2_ref[...] = x2
    q_ref[...] = _dot(_rms(x2, gq_ref[...]).astype(BF16), wq_ref[...])


def _mix_sample_post(y, z, pool, x, sn, wo, gq, wq):
    ins = [y, z, pool, x, sn, wo, gq, wq]
    outs = [jax.ShapeDtypeStruct(x.shape, F32)] * 2
    return pl.pallas_call(
        _mix_sample_post_kernel,
        grid=(1,),
        in_specs=[_const_spec(a.shape) for a in ins],
        out_specs=[pl.BlockSpec(x.shape, lambda i: (0, 0))] * 2,
        out_shape=outs,
        compiler_params=_cparams(("arbitrary",)),
        name="mix_sample_post",
    )(*ins)


def _oproj_kernel(x_ref, o_ref, wo_ref, out_ref):
    out_ref[...] = x_ref[...] + _dot(o_ref[...].astype(BF16), wo_ref[...])


def _oproj(x, o, wo):
    ins = [x, o, wo]
    return pl.pallas_call(
        _oproj_kernel,
        grid=(1,),
        in_specs=[_const_spec(a.shape) for a in ins],
        out_specs=pl.BlockSpec(x.shape, lambda i: (0, 0)),
        out_shape=jax.ShapeDtypeStruct(x.shape, F32),
        compiler_params=_cparams(("arbitrary",)),
        name="oproj",
    )(*ins)


def _prep_weights(g_ffn1, w1_gate, w1_up, w1_down, g_mix, w_in, conv_w, conv_b, dt_bias, a_log,
                  d_skip, ssm_norm, w_pool, pool_scale, w_out, g_mem, w_mem_k, w_mem_v, g_xq,
                  w_xq, w_xo, g_ffn2, w2_gate, w2_up, w2_down, g_final):
    row = lambda v: v.reshape(1, -1).astype(F32)
    pad_heads = lambda v: jnp.pad(v.astype(F32), (0, LANES - SSM_HEADS)).reshape(1, LANES)
    o1 = D_POOL
    o2 = o1 + D_SSM
    o3 = o2 + CONV_DIM
    w_pool_bd = jnp.zeros((D_POOL, D_POOL), F32)
    for g in range(len(POOL_WINDOWS)):
        w_pool_bd = w_pool_bd.at[g * POOL_GW:(g + 1) * POOL_GW, g * POOL_GW:(g + 1) * POOL_GW].set(w_pool[g])
    head_of_chan = jnp.arange(D_SSM) // SSM_HEADDIM
    expand = (jnp.arange(LANES)[:, None] == head_of_chan[None, :]).astype(F32)
    return dict(
        g_ffn1=row(g_ffn1), w1_gate=w1_gate.astype(BF16), w1_up=w1_up.astype(BF16),
        w1_down=w1_down.astype(BF16),
        g_mix=row(g_mix), w_u=w_in[:, :o1].astype(BF16), w_z=w_in[:, o1:o2].astype(BF16),
        w_xbc=w_in[:, o2:o3].astype(BF16),
        w_dt=jnp.pad(w_in[:, o3:], ((0, 0), (0, LANES - SSM_HEADS))).astype(BF16),
        conv_w=conv_w.astype(F32), conv_b=row(conv_b), dt_bias=pad_heads(dt_bias),
        a_log=pad_heads(a_log), d_skip=row(jnp.repeat(d_skip, SSM_HEADDIM)), ssm_norm=row(ssm_norm),
        w_pool=w_pool_bd.astype(BF16), pool_scale=row(pool_scale), w_out=w_out.astype(BF16),
        g_mem=row(g_mem), w_mem_k=w_mem_k.astype(BF16), w_mem_v=w_mem_v.astype(BF16),
        g_xq=row(g_xq), w_xq=w_xq.astype(BF16), w_xo=w_xo.astype(BF16),
        g_ffn2=row(g_ffn2), w2_gate=w2_gate.astype(BF16), w2_up=w2_up.astype(BF16),
        w2_down=w2_down.astype(BF16), g_final=row(g_final), expand=expand,
    )


def _prompt_layer(x_prompt, mem_prompt, w):
    b, s, d = x_prompt.shape
    k, v = _memkv(mem_prompt.reshape(b * N_MEM, d), w["g_mem"], w["w_mem_k"], w["w_mem_v"], tm=512)
    k = k.reshape(b, N_MEM, d)
    v = v.reshape(b, N_MEM, d)
    x = _ffn(x_prompt.reshape(b * s, d), w["g_ffn1"], w["w1_gate"], w["w1_up"], w["w1_down"],
             w["g_final"], final=False, tm=512)
    x, pool, conv, ssm = _mix_prompt(x.reshape(b, s, d), w, tc=256)
    x = _attn_prompt(x, k, v, w["g_xq"], w["w_xq"], w["w_xo"], tq=512)
    y = _ffn(x.reshape(b * s, d), w["g_ffn2"], w["w2_gate"], w["w2_up"], w["w2_down"],
             w["g_final"], final=True, tm=512)
    return y.reshape(b, s, d), k, v, pool, conv, ssm


def _sample_layer(x_sample, mem_k, mem_v, state_pool, state_conv, state_ssm, start_pos, w):
    nb, t, d = x_sample.shape
    n = nb * t
    tmaj = lambda a: jnp.swapaxes(a, 0, 1).reshape(-1, a.shape[-1])
    bmaj = lambda a, r: jnp.swapaxes(a.reshape(r, nb, -1), 0, 1)
    x = _ffn(tmaj(x_sample), w["g_ffn1"], w["w1_gate"], w["w1_up"], w["w1_down"], w["g_final"],
             final=False, tm=n)
    (pool_out, z, ydx, expa, xwt, bmat, cmat, dec, new_pool, new_conv) = _mix_sample_pre(
        x, tmaj(state_pool), tmaj(state_conv), w, nb=nb, t=t, start_pos=start_pos)
    r3 = lambda a: a.reshape(t, nb, a.shape[-1])
    y, new_ssm = _ssd_sample(dec[:, :SSM_HEADS].reshape(-1), state_ssm, r3(cmat), r3(ydx), r3(expa),
                             xwt, bmat, nb=nb, t=t, bb=8)
    x2, q = _mix_sample_post(y.reshape(n, D_SSM), z, pool_out, x, w["ssm_norm"], w["w_out"],
                             w["g_xq"], w["w_xq"])
    o = _attn_sample(bmaj(q, t).reshape(nb, t * MEM_HEADS, MEM_HD), mem_k, mem_v, bb=4)
    x3 = _oproj(bmaj(x2, t).reshape(n, d), o.reshape(n, d), w["w_xo"])
    y = _ffn(x3, w["g_ffn2"], w["w2_gate"], w["w2_up"], w["w2_down"], w["g_final"], final=True, tm=n)
    return (y.reshape(nb, t, d), bmaj(new_pool, POOL_BUF), bmaj(new_conv, CONV_W - 1), new_ssm)


def kernel(x_prompt, x_sample, mem_prompt, cache_mem_k, cache_mem_v, state_pool, state_conv, state_ssm,
           g_ffn1, w1_gate, w1_up, w1_down, g_mix, w_in, conv_w, conv_b, dt_bias, a_log, d_skip,
           ssm_norm, w_pool, pool_scale, w_out, g_mem, w_mem_k, w_mem_v, g_xq, w_xq, w_xo,
           g_ffn2, w2_gate, w2_up, w2_down, g_final):
    assert g_ffn1.shape[0] == 1, "single-layer model"
    past_len = 16384
    w = _prep_weights(g_ffn1[0], w1_gate[0], w1_up[0], w1_down[0], g_mix[0], w_in[0], conv_w[0],
                      conv_b[0], dt_bias[0], a_log[0], d_skip[0], ssm_norm[0], w_pool[0],
                      pool_scale[0], w_out[0], g_mem[0], w_mem_k[0], w_mem_v[0], g_xq[0], w_xq[0],
                      w_xo[0], g_ffn2[0], w2_gate[0], w2_up[0], w2_down[0], g_final)
    bp = x_prompt.shape[0]
    y_p, k_p, v_p, pool_p, conv_p, ssm_p = _prompt_layer(x_prompt, mem_prompt, w)
    y_s, pool_s, conv_s, ssm_s = _sample_layer(x_sample, cache_mem_k[0], cache_mem_v[0], state_pool[0],
                                               state_conv[0], state_ssm[0], past_len, w)
    kv_shape = (1, bp, N_MEM, MEM_HEADS, MEM_HD)
    return (y_p, y_s, k_p.reshape(kv_shape), v_p.reshape(kv_shape), pool_p[None], conv_p[None],
            ssm_p[None], pool_s[None], conv_s[None], ssm_s[None])
```

```python
import functools

import jax
import jax.numpy as jnp
from jax import lax
from jax.experimental import pallas as pl
from jax.experimental.pallas import tpu as pltpu

F32 = jnp.float32
BF16 = jnp.bfloat16

D_MODEL = 1024
D_POOL = 256
POOL_WINDOWS = (2, 4, 8, 16)
POOL_GW = 64
POOL_BUF = 15
D_SSM = 768
SSM_HEADDIM = 64
SSM_HEADS = 12
SSM_GROUPS = 2
SSM_HPG = 6
SSM_STATE = 128
CONV_W = 4
CONV_DIM = 1280
N_MEM = 256
MEM_HEADS = 4
MEM_HD = 256
D_FF = 2816
EPS = 1e-6
LANES = 128
CHUNK = 128
POOL_HIST = 32
CONV_HIST = 8
SCAN_PAD = CHUNK // 2
VMEM_LIMIT = 56 * 1024 * 1024


def _cparams(sem):
    return pltpu.CompilerParams(dimension_semantics=sem, vmem_limit_bytes=VMEM_LIMIT)


def _const_spec(shape):
    nd = len(shape)
    return pl.BlockSpec(shape, lambda *_: (0,) * nd, pipeline_mode=pl.Buffered(1))


def _rms(x, g):
    ms = jnp.mean(x * x, axis=-1, keepdims=True)
    return x * lax.rsqrt(ms + EPS) * g


def _silu(x):
    return x * jax.nn.sigmoid(x)


def _dot(a, b):
    return jnp.dot(a, b, preferred_element_type=F32)


def _dot_nt(a, b):
    return lax.dot_general(a, b, (((1,), (1,)), ((), ())), preferred_element_type=F32)


def _dot_tn(a, b):
    return lax.dot_general(a, b, (((0,), (0,)), ((), ())), preferred_element_type=F32)


def _ffn_kernel(x_ref, g_ref, wg_ref, wu_ref, wd_ref, gf_ref, o_ref, *, final):
    x = x_ref[...]
    xn = _rms(x, g_ref[...]).astype(BF16)
    gate = _dot(xn, wg_ref[...])
    up = _dot(xn, wu_ref[...])
    h = (_silu(gate) * up).astype(BF16)
    out = x + 0.5 * _dot(h, wd_ref[...])
    if final:
        out = _rms(out, gf_ref[...])
    o_ref[...] = out


def _ffn(x, g, wg, wu, wd, gf, *, final, tm):
    t, d = x.shape
    assert t % tm == 0
    return pl.pallas_call(
        functools.partial(_ffn_kernel, final=final),
        grid=(t // tm,),
        in_specs=[
            pl.BlockSpec((tm, d), lambda i: (i, 0)),
            _const_spec(g.shape), _const_spec(wg.shape), _const_spec(wu.shape),
            _const_spec(wd.shape), _const_spec(gf.shape),
        ],
        out_specs=pl.BlockSpec((tm, d), lambda i: (i, 0)),
        out_shape=jax.ShapeDtypeStruct((t, d), F32),
        compiler_params=_cparams(("parallel",)),
        name="ffn_final" if final else "ffn",
    )(x, g, wg, wu, wd, gf)


def _memkv_kernel(m_ref, g_ref, wk_ref, wv_ref, k_ref, v_ref, kb_ref, vb_ref):
    mn = _rms(m_ref[...], g_ref[...]).astype(BF16)
    k = _dot(mn, wk_ref[...])
    v = _dot(mn, wv_ref[...])
    k_ref[...] = k
    v_ref[...] = v
    kb_ref[...] = k.astype(BF16)
    vb_ref[...] = v.astype(BF16)


def _memkv(mem, g, wk, wv, *, tm):
    t, d = mem.shape
    return pl.pallas_call(
        _memkv_kernel,
        grid=(t // tm,),
        in_specs=[pl.BlockSpec((tm, d), lambda i: (i, 0)), _const_spec(g.shape),
                  _const_spec(wk.shape), _const_spec(wv.shape)],
        out_specs=[pl.BlockSpec((tm, d), lambda i: (i, 0))] * 4,
        out_shape=[jax.ShapeDtypeStruct((t, d), F32)] * 2 + [jax.ShapeDtypeStruct((t, d), BF16)] * 2,
        compiler_params=_cparams(("parallel",)),
        name="memkv",
    )(mem, g, wk, wv)


def _lane_group_select(vals, width):
    lane = lax.broadcasted_iota(jnp.int32, vals[0].shape, 1)
    out = vals[-1]
    for g in range(len(vals) - 2, -1, -1):
        out = jnp.where(lane < (g + 1) * width, vals[g], out)
    return out


def _col_bcast(m, k, width):
    return jnp.broadcast_to(m[:, k:k + 1], (m.shape[0], width))


def _gate_norm_out(y, z, pool_out, x_res, ssm_norm, w_out):
    yn = _rms(y * _silu(z), ssm_norm)
    cat = jnp.concatenate([pool_out, yn], axis=-1).astype(BF16)
    return x_res + _dot(cat, w_out)


def _mix_prompt_kernel(x_ref, g_ref, wu_ref, wz_ref, wx_ref, wdt_ref, cw_ref, cb_ref,
                       dtb_ref, alog_ref, dsk_ref, sn_ref, wp_ref, ps_ref, wo_ref,
                       o_ref, opool_ref, oconv_ref, ossm_ref,
                       ubuf, s2buf, s4buf, s8buf, cbuf, abuf, ht_ref, ybuf, *, tc, nseq):
    c = pl.program_id(1)
    nc = pl.num_programs(1)
    n_pool_slabs = D_POOL // LANES
    n_conv_slabs = CONV_DIM // LANES
    n_chunks = tc // CHUNK
    rp = POOL_HIST + tc
    seqs = range(nseq)

    def slab(s):
        return slice(s * LANES, (s + 1) * LANES)

    @pl.when(c == 0)
    def _():
        ubuf[:, :, 0:POOL_HIST, :] = jnp.zeros((nseq, n_pool_slabs, POOL_HIST, LANES), F32)
        cbuf[:, :, 0:CONV_HIST, :] = jnp.zeros((nseq, n_conv_slabs, CONV_HIST, LANES), F32)
        abuf[:, :, 0:SCAN_PAD, :] = jnp.zeros((nseq, n_chunks, SCAN_PAD, LANES), F32)
        ht_ref[...] = jnp.zeros_like(ht_ref)

    lo_half = lax.broadcasted_iota(jnp.int32, (tc, LANES), 1) < POOL_GW
    pos1 = c * tc + lax.broadcasted_iota(jnp.int32, (tc, LANES), 0) + 1
    a_neg = -jnp.exp(alog_ref[...])
    row = lax.broadcasted_iota(jnp.int32, (CHUNK, CHUNK), 0)
    col = lax.broadcasted_iota(jnp.int32, (CHUNK, CHUNK), 1)
    causal = row >= col
    lo = lax.broadcasted_iota(jnp.int32, (CHUNK, LANES), 1) < SSM_HEADDIM
    mask_lo = jnp.where(lo, 1.0, 0.0).astype(BF16)
    mask_hi = jnp.where(lo, 0.0, 1.0).astype(BF16)
    gw = SSM_HPG * SSM_HEADDIM
    n_x = D_SSM // LANES
    st8 = {}

    def head(i):
        x = x_ref[i]
        xn = _rms(x, g_ref[...]).astype(BF16)
        st8[i] = dict(x=x, xn=xn, xbc=[None] * n_conv_slabs)

    def conv_block(i, s0, ns):
        xn = st8[i]["xn"]
        raw = _dot(xn, wx_ref[:, s0 * LANES:(s0 + ns) * LANES])
        for t in range(ns):
            s = s0 + t
            cbuf[i, s, CONV_HIST:CONV_HIST + tc, :] = raw[:, slab(t)]
            acc = cbuf[i, s, CONV_HIST:CONV_HIST + tc, :] * cw_ref[CONV_W - 1:CONV_W, slab(s)]
            for j in range(1, CONV_W):
                acc = acc + (cbuf[i, s, CONV_HIST - j:CONV_HIST - j + tc, :]
                             * cw_ref[CONV_W - 1 - j:CONV_W - j, slab(s)])
            st8[i]["xbc"][s] = _silu(acc + cb_ref[:, slab(s)])

    def dt_proj(i):
        st8[i]["dt"] = jax.nn.softplus(_dot(st8[i]["xn"], wdt_ref[...]) + dtb_ref[...])

    def pool(i):
        u = _dot(st8[i]["xn"], wu_ref[...])
        for s in range(n_pool_slabs):
            ubuf[i, s, POOL_HIST:rp, :] = u[:, slab(s)]
            s2buf[i, s, 8:rp, :] = ubuf[i, s, 8:rp, :] + ubuf[i, s, 7:rp - 1, :]
        s4buf[i, 16:rp, :] = s2buf[i, 1, 16:rp, :] + s2buf[i, 1, 14:rp - 2, :]
        s8buf[i, 24:rp, :] = s4buf[i, 24:rp, :] + s4buf[i, 20:rp - 4, :]
        win_sums = [
            jnp.where(lo_half, s2buf[i, 0, POOL_HIST:rp, :],
                      s2buf[i, 0, POOL_HIST:rp, :] + s2buf[i, 0, POOL_HIST - 2:rp - 2, :]),
            jnp.where(lo_half, s8buf[i, POOL_HIST:rp, :],
                      s8buf[i, POOL_HIST:rp, :] + s8buf[i, POOL_HIST - 8:rp - 8, :]),
        ]
        ps = []
        for s in range(n_pool_slabs):
            win = jnp.where(lo_half, POOL_WINDOWS[2 * s], POOL_WINDOWS[2 * s + 1])
            cnt = jnp.minimum(pos1, win).astype(F32)
            ps.append(win_sums[s] / cnt - u[:, slab(s)])
        st8[i]["pool_out"] = _dot(jnp.concatenate(ps, axis=1).astype(BF16), wp_ref[...]) * ps_ref[...]

    def ssd_chunk(i, j):
        xbc = st8[i]["xbc"]
        sl = slice(j * CHUNK, (j + 1) * CHUNK)
        dt_c = st8[i]["dt"][sl]
        acs = dt_c * a_neg
        sh = 1
        while sh < CHUNK:
            abuf[i, j, SCAN_PAD:SCAN_PAD + CHUNK, :] = acs
            acs = acs + abuf[i, j, SCAN_PAD - sh:SCAN_PAD - sh + CHUNK, :]
            sh *= 2
        acs_last = acs[CHUNK - 1:CHUNK, :]
        fdec = jnp.exp(acs_last - acs) * dt_c
        src_t = (acs - jnp.log(dt_c)).T
        ht = ht_ref[i]
        ht_b = ht.astype(BF16)
        cdec, st = [], []
        for g in range(SSM_GROUPS):
            c_g = xbc[n_x + SSM_GROUPS + g][sl].astype(BF16)
            b_g = xbc[n_x + g][sl].astype(BF16)
            scores = _dot_nt(c_g, b_g)
            y_off = _dot(c_g, ht_b[:, g * gw:(g + 1) * gw])
            xw = []
            for q in range(SSM_HPG // 2):
                k0 = g * SSM_HPG + 2 * q
                pair = k0 // 2
                a_cols = [_col_bcast(acs, k, CHUNK) for k in (k0, k0 + 1)]
                lhs = [(scores * jnp.exp(jnp.where(causal, a_cols[h] - src_t[k0 + h:k0 + h + 1, :], -jnp.inf))
                        ).astype(BF16) for h in range(2)]
                x_pair = xbc[pair][sl]
                x_b = x_pair.astype(BF16)
                rhs = jnp.concatenate([x_b * mask_lo, x_b * mask_hi], axis=0)
                e_pair = jnp.exp(jnp.where(lo, a_cols[0], a_cols[1]))
                ybuf[i, sl, slab(pair)] = (_dot(jnp.concatenate(lhs, axis=1), rhs)
                                           + y_off[:, slab(q)] * e_pair + x_pair * dsk_ref[:, slab(pair)])
                f_pair = jnp.where(lo, _col_bcast(fdec, k0, LANES), _col_bcast(fdec, k0 + 1, LANES))
                xw.append((x_pair * f_pair).astype(BF16))
                cdec.append(e_pair[CHUNK - 1:CHUNK, :])
            st.append(_dot_tn(b_g, jnp.concatenate(xw, axis=1)))
        ht_ref[i] = ht * jnp.concatenate(cdec, axis=1) + jnp.concatenate(st, axis=1)

    def z_proj(i):
        st8[i]["zs"] = _silu(_dot(st8[i]["xn"], wz_ref[...]))

    def tail(i):
        d = st8[i]
        yn = _rms(ybuf[i] * d["zs"], sn_ref[...])
        cat = jnp.concatenate([d["pool_out"], yn], axis=-1).astype(BF16)
        o_ref[i] = d["x"] + _dot(cat, wo_ref[...])

    def mid(i):
        conv_block(i, n_x, 2 * SSM_GROUPS)
        dt_proj(i)
        for s0 in range(0, n_x, 2):
            conv_block(i, s0, 2)

    for i in seqs:
        head(i)
    for i in seqs:
        mid(i)
    for i in seqs:
        pool(i)
    for j in range(n_chunks):
        for i in seqs:
            ssd_chunk(i, j)
        if j == 0:
            for i in seqs:
                z_proj(i)
    for i in seqs:
        tail(i)

    ubuf[:, :, 0:POOL_HIST, :] = ubuf[:, :, tc:tc + POOL_HIST, :]
    cbuf[:, :, 0:CONV_HIST, :] = cbuf[:, :, tc:tc + CONV_HIST, :]

    @pl.when(c == nc - 1)
    def _():
        for i in seqs:
            for s in range(n_pool_slabs):
                opool_ref[i, :, slab(s)] = ubuf[i, s, POOL_HIST - POOL_BUF:POOL_HIST, :]
            for s in range(n_conv_slabs):
                oconv_ref[i, :, slab(s)] = cbuf[i, s, CONV_HIST - (CONV_W - 1):CONV_HIST, :]
            ossm_ref[i] = ht_ref[i].T.reshape(SSM_HEADS, SSM_HEADDIM, SSM_STATE)


def _mix_prompt(x, w, *, tc, nseq):
    b, s, d = x.shape
    assert s % tc == 0 and tc % CHUNK == 0 and b % nseq == 0
    consts = [w["g_mix"], w["w_u"], w["w_z"], w["w_xbc"], w["w_dt"], w["conv_w"], w["conv_b"],
              w["dt_bias"], w["a_log"], w["d_skip"], w["ssm_norm"], w["w_pool"], w["pool_scale"],
              w["w_out"]]
    tile = pl.BlockSpec((nseq, tc, d), lambda i, j: (i, j, 0))
    return pl.pallas_call(
        functools.partial(_mix_prompt_kernel, tc=tc, nseq=nseq),
        grid=(b // nseq, s // tc),
        in_specs=[tile] + [_const_spec(a.shape) for a in consts],
        out_specs=[
            tile,
            pl.BlockSpec((nseq, POOL_BUF, D_POOL), lambda i, j: (i, 0, 0)),
            pl.BlockSpec((nseq, CONV_W - 1, CONV_DIM), lambda i, j: (i, 0, 0)),
            pl.BlockSpec((nseq, SSM_HEADS, SSM_HEADDIM, SSM_STATE), lambda i, j: (i, 0, 0, 0)),
        ],
        out_shape=[
            jax.ShapeDtypeStruct((b, s, d), F32),
            jax.ShapeDtypeStruct((b, POOL_BUF, D_POOL), F32),
            jax.ShapeDtypeStruct((b, CONV_W - 1, CONV_DIM), F32),
            jax.ShapeDtypeStruct((b, SSM_HEADS, SSM_HEADDIM, SSM_STATE), F32),
        ],
        scratch_shapes=[
            pltpu.VMEM((nseq, D_POOL // LANES, POOL_HIST + tc, LANES), F32),
            pltpu.VMEM((nseq, D_POOL // LANES, POOL_HIST + tc, LANES), F32),
            pltpu.VMEM((nseq, POOL_HIST + tc, LANES), F32),
            pltpu.VMEM((nseq, POOL_HIST + tc, LANES), F32),
            pltpu.VMEM((nseq, CONV_DIM // LANES, CONV_HIST + tc, LANES), F32),
            pltpu.VMEM((nseq, tc // CHUNK, SCAN_PAD + CHUNK, LANES), F32),
            pltpu.VMEM((nseq, SSM_STATE, D_SSM), F32),
            pltpu.VMEM((nseq, tc, D_SSM), F32),
        ],
        compiler_params=_cparams(("arbitrary", "arbitrary")),
        name="mix_prompt",
    )(x, *consts)


def _attend(q, k_ref, v_ref):
    hs = [slice(h * MEM_HD, (h + 1) * MEM_HD) for h in range(MEM_HEADS)]
    q = q.astype(BF16)
    s = [_dot_nt(q[:, hs[h]], k_ref[:, hs[h]]) * (MEM_HD ** -0.5) for h in range(MEM_HEADS)]
    p = []
    for h in range(MEM_HEADS):
        e = jnp.exp(s[h] - jnp.max(s[h], axis=-1, keepdims=True))
        p.append((e / jnp.sum(e, axis=-1, keepdims=True)).astype(BF16))
    return jnp.concatenate([_dot(p[h], v_ref[:, hs[h]]) for h in range(MEM_HEADS)], axis=-1)


def _attn_prompt_kernel(x_ref, g_ref, wq_ref, k_ref, v_ref, wo_ref, o_ref):
    x = x_ref[...]
    q = _dot(_rms(x, g_ref[...]).astype(BF16), wq_ref[...])
    o = _attend(q, k_ref, v_ref).astype(BF16)
    o_ref[...] = x + _dot(o, wo_ref[...])


def _attn_prompt(x, k, v, g, wq, wo, *, tq):
    b, s, d = x.shape
    tile = pl.BlockSpec((None, tq, d), lambda i, j: (i, j, 0))
    mem = pl.BlockSpec((None, N_MEM, d), lambda i, j: (i, 0, 0))
    return pl.pallas_call(
        _attn_prompt_kernel,
        grid=(b, s // tq),
        in_specs=[tile, _const_spec(g.shape), _const_spec(wq.shape), mem, mem, _const_spec(wo.shape)],
        out_specs=tile,
        out_shape=jax.ShapeDtypeStruct((b, s, d), F32),
        compiler_params=_cparams(("parallel", "parallel")),
        name="attn_prompt",
    )(x, g, wq, k, v, wo)


def _attn_sample_kernel(q_ref, k_ref, v_ref, o_ref, *, bb):
    nq = q_ref.shape[1]
    rows = N_MEM * MEM_HEADS
    q_head = lax.broadcasted_iota(jnp.int32, (nq, rows), 0) % MEM_HEADS
    kv_head = lax.broadcasted_iota(jnp.int32, (nq, rows), 1) % MEM_HEADS
    same_head = q_head == kv_head
    q_all = q_ref[...].reshape(bb * nq, MEM_HD)
    q_all = jnp.concatenate([q_all, jnp.zeros((LANES - bb * nq, MEM_HD), F32)], axis=0).astype(BF16)
    s_t = [_dot_nt(k_ref[i].reshape(rows, MEM_HD).astype(BF16), q_all) for i in range(bb)]
    ps = []
    for i in range(bb):
        s = s_t[i].T[i * nq:(i + 1) * nq] * (MEM_HD ** -0.5)
        s = jnp.where(same_head, s, -jnp.inf)
        e = jnp.exp(s - jnp.max(s, axis=-1, keepdims=True))
        ps.append((e / jnp.sum(e, axis=-1, keepdims=True)).astype(BF16))
    for i in range(bb):
        o_ref[i] = _dot(ps[i], v_ref[i].reshape(rows, MEM_HD).astype(BF16))


def _attn_sample(q, k, v, *, bb):
    nb, nq, hd = q.shape
    qs = pl.BlockSpec((bb, nq, hd), lambda i: (i, 0, 0))
    mem = pl.BlockSpec((bb, N_MEM, MEM_HEADS, hd), lambda i: (i, 0, 0, 0))
    return pl.pallas_call(
        functools.partial(_attn_sample_kernel, bb=bb),
        grid=(nb // bb,),
        in_specs=[qs, mem, mem],
        out_specs=qs,
        out_shape=jax.ShapeDtypeStruct((nb, nq, hd), F32),
        compiler_params=_cparams(("parallel",)),
        name="attn_sample",
    )(q, k, v)


def _mix_sample_pre_kernel(x_ref, sp_ref, sc_ref, g_ref, wu_ref, wz_ref, wx_ref, wdt_ref,
                           cw_ref, cb_ref, dtb_ref, alog_ref, dsk_ref, wp_ref, ps_ref, ex_ref,
                           pool_ref, z_ref, ydx_ref, expa_ref, xwt_ref, b_ref, c_ref, dec_ref,
                           npool_ref, nconv_ref, *, nb, t, start_pos):
    def rows(i):
        return slice(i * nb, (i + 1) * nb)

    x = x_ref[...]
    xn = _rms(x, g_ref[...]).astype(BF16)
    z_ref[...] = _dot(xn, wz_ref[...])

    u = _dot(xn, wu_ref[...])
    ext = [sp_ref[rows(i), :] for i in range(POOL_BUF)] + [u[rows(i)] for i in range(t)]
    ps = []
    for i in range(t):
        sums, cnts = [], []
        for w in POOL_WINDOWS:
            s = ext[POOL_BUF + i]
            for j in range(1, w):
                s = s + ext[POOL_BUF + i - j]
            sums.append(s)
            cnts.append(jnp.full((nb, D_POOL), float(min(start_pos + i + 1, w)), F32))
        mean = _lane_group_select(sums, POOL_GW) / _lane_group_select(cnts, POOL_GW)
        ps.append(mean - ext[POOL_BUF + i])
    p = jnp.concatenate(ps, axis=0).astype(BF16)
    pool_ref[...] = _dot(p, wp_ref[...]) * ps_ref[...]
    for i in range(POOL_BUF):
        npool_ref[rows(i), :] = ext[t + i]

    xr = _dot(xn, wx_ref[...])
    cext = [sc_ref[rows(i), :] for i in range(CONV_W - 1)] + [xr[rows(i)] for i in range(t)]
    xbc = []
    for i in range(t):
        acc = cext[i] * cw_ref[0:1, :]
        for k in range(1, CONV_W):
            acc = acc + cext[i + k] * cw_ref[k:k + 1, :]
        xbc.append(_silu(acc + cb_ref[...]))
    for i in range(CONV_W - 1):
        nconv_ref[rows(i), :] = cext[t + i]
    xs = [v[:, :D_SSM] for v in xbc]
    bm = [v[:, D_SSM:D_SSM + SSM_GROUPS * SSM_STATE] for v in xbc]
    cm = [v[:, D_SSM + SSM_GROUPS * SSM_STATE:] for v in xbc]
    for i in range(t):
        b_ref[rows(i), :] = bm[i]
        c_ref[rows(i), :] = cm[i]

    dt_all = jax.nn.softplus(_dot(xn, wdt_ref[...]) + dtb_ref[...])
    a_neg = -jnp.exp(alog_ref[...])
    dt = [dt_all[rows(i)] for i in range(t)]
    acs = []
    for i in range(t):
        da = dt[i] * a_neg
        acs.append(da if i == 0 else acs[-1] + da)
    dec_ref[...] = jnp.exp(acs[-1])

    def expand(v, exact):
        if exact:
            return jnp.dot(v, ex_ref[...], precision=lax.Precision.HIGHEST,
                           preferred_element_type=F32)
        return _dot(v.astype(BF16), ex_ref[...].astype(BF16))

    lane = lax.broadcasted_iota(jnp.int32, (nb, LANES), 1)
    xw = []
    for i in range(t):
        ydx = xs[i] * dsk_ref[...]
        for s in range(i + 1):
            sc = [jnp.sum(cm[i][:, g * SSM_STATE:(g + 1) * SSM_STATE]
                          * bm[s][:, g * SSM_STATE:(g + 1) * SSM_STATE], axis=-1, keepdims=True)
                  for g in range(SSM_GROUPS)]
            sc = jnp.where(lane < SSM_HPG, sc[0], sc[1])
            coef = sc * jnp.exp(acs[i] - acs[s]) * dt[s]
            ydx = ydx + expand(coef, False) * xs[s]
        ydx_ref[rows(i), :] = ydx
        expa_ref[rows(i), :] = expand(jnp.exp(acs[i]), True)
        xw.append(xs[i] * expand(jnp.exp(acs[-1] - acs[i]) * dt[i], True))
    xwt_ref[...] = jnp.concatenate(xw, axis=0).T.astype(BF16)


def _mix_sample_pre(x, sp, sc, w, *, nb, t, start_pos):
    n = nb * t
    consts = [w["g_mix"], w["w_u"], w["w_z"], w["w_xbc"], w["w_dt"], w["conv_w"], w["conv_b"],
              w["dt_bias"], w["a_log"], w["d_skip"], w["w_pool"], w["pool_scale"], w["expand"]]
    ins = [x, sp, sc] + consts
    outs = [
        jax.ShapeDtypeStruct((n, D_POOL), F32),
        jax.ShapeDtypeStruct((n, D_SSM), F32),
        jax.ShapeDtypeStruct((n, D_SSM), F32),
        jax.ShapeDtypeStruct((n, D_SSM), F32),
        jax.ShapeDtypeStruct((D_SSM, n), BF16),
        jax.ShapeDtypeStruct((n, SSM_GROUPS * SSM_STATE), F32),
        jax.ShapeDtypeStruct((n, SSM_GROUPS * SSM_STATE), F32),
        jax.ShapeDtypeStruct((nb, LANES), F32),
        jax.ShapeDtypeStruct((POOL_BUF * nb, D_POOL), F32),
        jax.ShapeDtypeStruct(((CONV_W - 1) * nb, CONV_DIM), F32),
    ]
    return pl.pallas_call(
        functools.partial(_mix_sample_pre_kernel, nb=nb, t=t, start_pos=start_pos),
        grid=(1,),
        in_specs=[_const_spec(a.shape) for a in ins],
        out_specs=[pl.BlockSpec(o.shape, lambda i: (0, 0)) for o in outs],
        out_shape=outs,
        compiler_params=_cparams(("arbitrary",)),
        name="mix_sample_pre",
    )(*ins)


def _ssd_sample_kernel(dec_ref, h0_ref, c_ref, ydx_ref, expa_ref, xwt_ref, b_ref,
                       y_ref, hout_ref, *, nb, t, bb):
    j = pl.program_id(0)
    gw = SSM_HPG * SSM_HEADDIM
    c_blk = c_ref[...].reshape(t * bb, SSM_GROUPS * SSM_STATE)
    row_seq = lax.broadcasted_iota(jnp.int32, (t * bb, SSM_STATE), 0) % bb
    all_seq = lax.broadcasted_iota(jnp.int32, (t * nb, SSM_STATE), 0) % nb
    b_all = b_ref[...]
    yoff = [jnp.zeros((t * bb, gw), F32) for _ in range(SSM_GROUPS)]
    for i in range(bb):
        seq = j * bb + i
        for g in range(SSM_GROUPS):
            gs = slice(g * SSM_STATE, (g + 1) * SSM_STATE)
            h0 = h0_ref[i, g * SSM_HPG:(g + 1) * SSM_HPG].reshape(gw, SSM_STATE)
            c_i = jnp.where(row_seq == i, c_blk[:, gs], 0.0).astype(BF16)
            yoff[g] = yoff[g] + _dot_nt(c_i, h0.astype(BF16))
            b_i = jnp.where(all_seq == seq, b_all[:, gs], 0.0).astype(BF16)
            st = _dot(xwt_ref[g * gw:(g + 1) * gw, :], b_i)
            for k in range(SSM_HPG):
                hd = g * SSM_HPG + k
                hout_ref[i, hd] = (h0_ref[i, hd] * dec_ref[seq * SSM_HEADS + hd]
                                   + st[k * SSM_HEADDIM:(k + 1) * SSM_HEADDIM])
    yoff = jnp.concatenate(yoff, axis=1).reshape(t, bb, D_SSM)
    y_ref[...] = ydx_ref[...] + yoff * expa_ref[...]


def _ssd_sample(dec, h0, c, ydx, expa, xwt, bmat, *, nb, t, bb):
    blk = lambda d: pl.BlockSpec((t, bb, d), lambda i: (0, i, 0))
    hblk = pl.BlockSpec((bb, SSM_HEADS, SSM_HEADDIM, SSM_STATE), lambda i: (i, 0, 0, 0))
    return pl.pallas_call(
        functools.partial(_ssd_sample_kernel, nb=nb, t=t, bb=bb),
        grid=(nb // bb,),
        in_specs=[pl.BlockSpec(memory_space=pltpu.SMEM), hblk, blk(SSM_GROUPS * SSM_STATE),
                  blk(D_SSM), blk(D_SSM), _const_spec(xwt.shape), _const_spec(bmat.shape)],
        out_specs=[blk(D_SSM), hblk],
        out_shape=[jax.ShapeDtypeStruct((t, nb, D_SSM), F32),
                   jax.ShapeDtypeStruct(h0.shape, F32)],
        compiler_params=_cparams(("parallel",)),
        name="ssd_sample",
    )(dec, h0, c, ydx, expa, xwt, bmat)


def _mix_sample_post_kernel(y_ref, z_ref, pool_ref, x_ref, sn_ref, wo_ref, gq_ref, wq_ref,
                            x2_ref, q_ref):
    x2 = _gate_norm_out(y_ref[...], z_ref[...], pool_ref[...], x_ref[...], sn_ref[...], wo_ref[...])
    x2_ref[...] = x2
    q_ref[...] = _dot(_rms(x2, gq_ref[...]).astype(BF16), wq_ref[...])


def _mix_sample_post(y, z, pool, x, sn, wo, gq, wq):
    ins = [y, z, pool, x, sn, wo, gq, wq]
    outs = [jax.ShapeDtypeStruct(x.shape, F32)] * 2
    return pl.pallas_call(
        _mix_sample_post_kernel,
        grid=(1,),
        in_specs=[_const_spec(a.shape) for a in ins],
        out_specs=[pl.BlockSpec(x.shape, lambda i: (0, 0))] * 2,
        out_shape=outs,
        compiler_params=_cparams(("arbitrary",)),
        name="mix_sample_post",
    )(*ins)


def _oproj_kernel(x_ref, o_ref, wo_ref, out_ref):
    out_ref[...] = x_ref[...] + _dot(o_ref[...].astype(BF16), wo_ref[...])


def _oproj(x, o, wo):
    ins = [x, o, wo]
    return pl.pallas_call(
        _oproj_kernel,
        grid=(1,),
        in_specs=[_const_spec(a.shape) for a in ins],
        out_specs=pl.BlockSpec(x.shape, lambda i: (0, 0)),
        out_shape=jax.ShapeDtypeStruct(x.shape, F32),
        compiler_params=_cparams(("arbitrary",)),
        name="oproj",
    )(*ins)


def _prep_weights(g_ffn1, w1_gate, w1_up, w1_down, g_mix, w_in, conv_w, conv_b, dt_bias, a_log,
                  d_skip, ssm_norm, w_pool, pool_scale, w_out, g_mem, w_mem_k, w_mem_v, g_xq,
                  w_xq, w_xo, g_ffn2, w2_gate, w2_up, w2_down, g_final):
    row = lambda v: v.reshape(1, -1).astype(F32)
    pad_heads = lambda v: jnp.pad(v.astype(F32), (0, LANES - SSM_HEADS)).reshape(1, LANES)
    o1 = D_POOL
    o2 = o1 + D_SSM
    o3 = o2 + CONV_DIM
    w_pool_bd = jnp.zeros((D_POOL, D_POOL), F32)
    for g in range(len(POOL_WINDOWS)):
        w_pool_bd = w_pool_bd.at[g * POOL_GW:(g + 1) * POOL_GW, g * POOL_GW:(g + 1) * POOL_GW].set(w_pool[g])
    head_of_chan = jnp.arange(D_SSM) // SSM_HEADDIM
    expand = (jnp.arange(LANES)[:, None] == head_of_chan[None, :]).astype(F32)
    return dict(
        g_ffn1=row(g_ffn1), w1_gate=w1_gate.astype(BF16), w1_up=w1_up.astype(BF16),
        w1_down=w1_down.astype(BF16),
        g_mix=row(g_mix), w_u=w_in[:, :o1].astype(BF16), w_z=w_in[:, o1:o2].astype(BF16),
        w_xbc=w_in[:, o2:o3].astype(BF16),
        w_dt=jnp.pad(w_in[:, o3:], ((0, 0), (0, LANES - SSM_HEADS))).astype(BF16),
        conv_w=conv_w.astype(F32), conv_b=row(conv_b), dt_bias=pad_heads(dt_bias),
        a_log=pad_heads(a_log), d_skip=row(jnp.repeat(d_skip, SSM_HEADDIM)), ssm_norm=row(ssm_norm),
        w_pool=w_pool_bd.astype(BF16), pool_scale=row(pool_scale), w_out=w_out.astype(BF16),
        g_mem=row(g_mem), w_mem_k=w_mem_k.astype(BF16), w_mem_v=w_mem_v.astype(BF16),
        g_xq=row(g_xq), w_xq=w_xq.astype(BF16), w_xo=w_xo.astype(BF16),
        g_ffn2=row(g_ffn2), w2_gate=w2_gate.astype(BF16), w2_up=w2_up.astype(BF16),
        w2_down=w2_down.astype(BF16), g_final=row(g_final), expand=expand,
    )


def _prompt_layer(x_prompt, mem_prompt, w):
    b, s, d = x_prompt.shape
    k, v, k_b, v_b = _memkv(mem_prompt.reshape(b * N_MEM, d), w["g_mem"], w["w_mem_k"], w["w_mem_v"], tm=512)
    x = _ffn(x_prompt.reshape(b * s, d), w["g_ffn1"], w["w1_gate"], w["w1_up"], w["w1_down"],
             w["g_final"], final=False, tm=512)
    x, pool, conv, ssm = _mix_prompt(x.reshape(b, s, d), w, tc=256, nseq=2)
    x = _attn_prompt(x, k_b.reshape(b, N_MEM, d), v_b.reshape(b, N_MEM, d), w["g_xq"], w["w_xq"], w["w_xo"],
                     tq=512)
    y = _ffn(x.reshape(b * s, d), w["g_ffn2"], w["w2_gate"], w["w2_up"], w["w2_down"],
             w["g_final"], final=True, tm=512)
    return y.reshape(b, s, d), k, v, pool, conv, ssm


def _sample_layer(x_sample, mem_k, mem_v, state_pool, state_conv, state_ssm, start_pos, w):
    nb, t, d = x_sample.shape
    n = nb * t
    tmaj = lambda a: jnp.swapaxes(a, 0, 1).reshape(-1, a.shape[-1])
    bmaj = lambda a, r: jnp.swapaxes(a.reshape(r, nb, -1), 0, 1)
    x = _ffn(tmaj(x_sample), w["g_ffn1"], w["w1_gate"], w["w1_up"], w["w1_down"], w["g_final"],
             final=False, tm=n)
    (pool_out, z, ydx, expa, xwt, bmat, cmat, dec, new_pool, new_conv) = _mix_sample_pre(
        x, tmaj(state_pool), tmaj(state_conv), w, nb=nb, t=t, start_pos=start_pos)
    r3 = lambda a: a.reshape(t, nb, a.shape[-1])
    y, new_ssm = _ssd_sample(dec[:, :SSM_HEADS].reshape(-1), state_ssm, r3(cmat), r3(ydx), r3(expa),
                             xwt, bmat, nb=nb, t=t, bb=8)
    x2, q = _mix_sample_post(y.reshape(n, D_SSM), z, pool_out, x, w["ssm_norm"], w["w_out"],
                             w["g_xq"], w["w_xq"])
    o = _attn_sample(bmaj(q, t).reshape(nb, t * MEM_HEADS, MEM_HD), mem_k, mem_v, bb=4)
    x3 = _oproj(bmaj(x2, t).reshape(n, d), o.reshape(n, d), w["w_xo"])
    y = _ffn(x3, w["g_ffn2"], w["w2_gate"], w["w2_up"], w["w2_down"], w["g_final"], final=True, tm=n)
    return (y.reshape(nb, t, d), bmaj(new_pool, POOL_BUF), bmaj(new_conv, CONV_W - 1), new_ssm)


def kernel(x_prompt, x_sample, mem_prompt, cache_mem_k, cache_mem_v, state_pool, state_conv, state_ssm,
           g_ffn1, w1_gate, w1_up, w1_down, g_mix, w_in, conv_w, conv_b, dt_bias, a_log, d_skip,
           ssm_norm, w_pool, pool_scale, w_out, g_mem, w_mem_k, w_mem_v, g_xq, w_xq, w_xo,
           g_ffn2, w2_gate, w2_up, w2_down, g_final):
    assert g_ffn1.shape[0] == 1, "single-layer model"
    past_len = 16384
    w = _prep_weights(g_ffn1[0], w1_gate[0], w1_up[0], w1_down[0], g_mix[0], w_in[0], conv_w[0],
                      conv_b[0], dt_bias[0], a_log[0], d_skip[0], ssm_norm[0], w_pool[0],
                      pool_scale[0], w_out[0], g_mem[0], w_mem_k[0], w_mem_v[0], g_xq[0], w_xq[0],
                      w_xo[0], g_ffn2[0], w2_gate[0], w2_up[0], w2_down[0], g_final)
    bp = x_prompt.shape[0]
    y_p, k_p, v_p, pool_p, conv_p, ssm_p = _prompt_layer(x_prompt, mem_prompt, w)
    y_s, pool_s, conv_s, ssm_s = _sample_layer(x_sample, cache_mem_k[0], cache_mem_v[0], state_pool[0],
                                               state_conv[0], state_ssm[0], past_len, w)
    kv_shape = (1, bp, N_MEM, MEM_HEADS, MEM_HD)
    return (y_p, y_s, k_p.reshape(kv_shape), v_p.reshape(kv_shape), pool_p[None], conv_p[None],
            ssm_p[None], pool_s[None], conv_s[None], ssm_s[None])
```

```python
import functools

import jax
import jax.numpy as jnp
from jax import lax
from jax.experimental import pallas as pl
from jax.experimental.pallas import tpu as pltpu

F32 = jnp.float32
BF16 = jnp.bfloat16

D_MODEL = 1024
D_POOL = 256
POOL_WINDOWS = (2, 4, 8, 16)
POOL_GW = 64
POOL_BUF = 15
D_SSM = 768
SSM_HEADDIM = 64
SSM_HEADS = 12
SSM_GROUPS = 2
SSM_HPG = 6
SSM_STATE = 128
CONV_W = 4
CONV_DIM = 1280
N_MEM = 256
MEM_HEADS = 4
MEM_HD = 256
D_FF = 2816
EPS = 1e-6
LANES = 128
CHUNK = 128
POOL_HIST = 32
CONV_HIST = 8
SCAN_PAD = CHUNK // 2
VMEM_LIMIT = 56 * 1024 * 1024


def _cparams(sem):
    return pltpu.CompilerParams(dimension_semantics=sem, vmem_limit_bytes=VMEM_LIMIT)


def _const_spec(shape):
    nd = len(shape)
    return pl.BlockSpec(shape, lambda *_: (0,) * nd, pipeline_mode=pl.Buffered(1))


def _rms(x, g):
    ms = jnp.mean(x * x, axis=-1, keepdims=True)
    return x * lax.rsqrt(ms + EPS) * g


def _silu(x):
    return x * jax.nn.sigmoid(x)


def _dot(a, b):
    return jnp.dot(a, b, preferred_element_type=F32)


def _dot_nt(a, b):
    return lax.dot_general(a, b, (((1,), (1,)), ((), ())), preferred_element_type=F32)


def _dot_tn(a, b):
    return lax.dot_general(a, b, (((0,), (0,)), ((), ())), preferred_element_type=F32)


FFN_W_STEPS = 8


def _ffn_kernel(xp_ref, xs_ref, g_ref, wg_ref, wu_ref, wd_ref, gf_ref, op_ref, os_ref,
                wg_s, wu_s, wd_s, *, final, n_p):
    i = pl.program_id(0)
    rows_gu = wg_ref.shape[0]
    rows_d = wd_ref.shape[0]

    @pl.when(i < FFN_W_STEPS)
    def _():
        r = pl.multiple_of(i * rows_gu, rows_gu)
        wg_s[pl.ds(r, rows_gu), :] = wg_ref[...].astype(BF16)
        wu_s[pl.ds(r, rows_gu), :] = wu_ref[...].astype(BF16)
        r = pl.multiple_of(i * rows_d, rows_d)
        wd_s[pl.ds(r, rows_d), :] = wd_ref[...].astype(BF16)

    def tile(x_ref, o_ref):
        x = x_ref[...]
        xn = _rms(x, g_ref[...]).astype(BF16)
        gate = _dot(xn, wg_s[...])
        up = _dot(xn, wu_s[...])
        h = (_silu(gate) * up).astype(BF16)
        out = x + 0.5 * _dot(h, wd_s[...])
        if final:
            out = _rms(out, gf_ref[...])
        o_ref[...] = out

    @pl.when(jnp.logical_and(i >= FFN_W_STEPS, i < FFN_W_STEPS + n_p))
    def _():
        tile(xp_ref, op_ref)

    @pl.when(i == FFN_W_STEPS + n_p)
    def _():
        tile(xs_ref, os_ref)


def _ffn(xp, xs, g, wg, wu, wd, gf, *, final, tm):
    tp, d = xp.shape
    dff = wg.shape[1]
    assert tp % tm == 0 and d % FFN_W_STEPS == 0 and dff % FFN_W_STEPS == 0
    n_p = tp // tm
    p_tile = pl.BlockSpec((tm, d), lambda i: (jnp.clip(i - FFN_W_STEPS, 0, n_p - 1), 0))
    s_tile = pl.BlockSpec(xs.shape, lambda i: (0, 0))
    w_chunk = lambda rows, cols: pl.BlockSpec((rows, cols), lambda i: (jnp.minimum(i, FFN_W_STEPS - 1), 0))
    return pl.pallas_call(
        functools.partial(_ffn_kernel, final=final, n_p=n_p),
        grid=(FFN_W_STEPS + n_p + 1,),
        in_specs=[
            p_tile, _const_spec(xs.shape), _const_spec(g.shape),
            w_chunk(d // FFN_W_STEPS, dff), w_chunk(d // FFN_W_STEPS, dff), w_chunk(dff // FFN_W_STEPS, d),
            _const_spec(gf.shape),
        ],
        out_specs=[p_tile, s_tile],
        out_shape=[jax.ShapeDtypeStruct(xp.shape, F32), jax.ShapeDtypeStruct(xs.shape, F32)],
        scratch_shapes=[pltpu.VMEM((d, dff), BF16), pltpu.VMEM((d, dff), BF16), pltpu.VMEM((dff, d), BF16)],
        compiler_params=_cparams(("arbitrary",)),
        name="ffn_final" if final else "ffn",
    )(xp, xs, g, wg, wu, wd, gf)


def _memkv_kernel(m_ref, g_ref, wk_ref, wv_ref, k_ref, v_ref, kb_ref, vb_ref):
    mn = _rms(m_ref[...], g_ref[...]).astype(BF16)
    k = _dot(mn, wk_ref[...])
    v = _dot(mn, wv_ref[...])
    k_ref[...] = k
    v_ref[...] = v
    kb_ref[...] = k.astype(BF16)
    vb_ref[...] = v.astype(BF16)


def _memkv(mem, g, wk, wv, *, tm):
    t, d = mem.shape
    return pl.pallas_call(
        _memkv_kernel,
        grid=(t // tm,),
        in_specs=[pl.BlockSpec((tm, d), lambda i: (i, 0)), _const_spec(g.shape),
                  _const_spec(wk.shape), _const_spec(wv.shape)],
        out_specs=[pl.BlockSpec((tm, d), lambda i: (i, 0))] * 4,
        out_shape=[jax.ShapeDtypeStruct((t, d), F32)] * 2 + [jax.ShapeDtypeStruct((t, d), BF16)] * 2,
        compiler_params=_cparams(("parallel",)),
        name="memkv",
    )(mem, g, wk, wv)


def _lane_group_select(vals, width):
    lane = lax.broadcasted_iota(jnp.int32, vals[0].shape, 1)
    out = vals[-1]
    for g in range(len(vals) - 2, -1, -1):
        out = jnp.where(lane < (g + 1) * width, vals[g], out)
    return out


def _col_bcast(m, k, width):
    return jnp.broadcast_to(m[:, k:k + 1], (m.shape[0], width))


def _gate_norm_out(y, z, pool_out, x_res, ssm_norm, w_out):
    yn = _rms(y * _silu(z), ssm_norm)
    cat = jnp.concatenate([pool_out, yn], axis=-1).astype(BF16)
    return x_res + _dot(cat, w_out)


def _mix_prompt_kernel(x_ref, g_ref, wu_ref, wz_ref, wx_ref, wdt_ref, cw_ref, cb_ref,
                       dtb_ref, alog_ref, dsk_ref, sn_ref, wp_ref, ps_ref, wo_ref,
                       o_ref, opool_ref, oconv_ref, ossm_ref,
                       ubuf, s2buf, s4buf, s8buf, cbuf, abuf, ht_ref, ybuf, *, tc, nseq):
    c = pl.program_id(1)
    nc = pl.num_programs(1)
    n_pool_slabs = D_POOL // LANES
    n_conv_slabs = CONV_DIM // LANES
    n_chunks = tc // CHUNK
    rp = POOL_HIST + tc
    seqs = range(nseq)

    def slab(s):
        return slice(s * LANES, (s + 1) * LANES)

    @pl.when(c == 0)
    def _():
        ubuf[:, :, 0:POOL_HIST, :] = jnp.zeros((nseq, n_pool_slabs, POOL_HIST, LANES), F32)
        cbuf[:, :, 0:CONV_HIST, :] = jnp.zeros((nseq, n_conv_slabs, CONV_HIST, LANES), F32)
        abuf[:, :, 0:SCAN_PAD, :] = jnp.zeros((nseq, n_chunks, SCAN_PAD, LANES), F32)
        ht_ref[...] = jnp.zeros_like(ht_ref)

    lo_half = lax.broadcasted_iota(jnp.int32, (tc, LANES), 1) < POOL_GW
    pos1 = c * tc + lax.broadcasted_iota(jnp.int32, (tc, LANES), 0) + 1
    a_neg = -jnp.exp(alog_ref[...])
    row = lax.broadcasted_iota(jnp.int32, (CHUNK, CHUNK), 0)
    col = lax.broadcasted_iota(jnp.int32, (CHUNK, CHUNK), 1)
    causal = row >= col
    lo = lax.broadcasted_iota(jnp.int32, (CHUNK, LANES), 1) < SSM_HEADDIM
    mask_lo = jnp.where(lo, 1.0, 0.0).astype(BF16)
    mask_hi = jnp.where(lo, 0.0, 1.0).astype(BF16)
    gw = SSM_HPG * SSM_HEADDIM
    n_x = D_SSM // LANES
    st8 = {}

    def head(i):
        x = x_ref[i]
        xn = _rms(x, g_ref[...]).astype(BF16)
        st8[i] = dict(x=x, xn=xn, xbc=[None] * n_conv_slabs)

    def conv_block(i, s0, ns):
        xn = st8[i]["xn"]
        raw = _dot(xn, wx_ref[:, s0 * LANES:(s0 + ns) * LANES])
        for t in range(ns):
            s = s0 + t
            cbuf[i, s, CONV_HIST:CONV_HIST + tc, :] = raw[:, slab(t)]
            acc = cbuf[i, s, CONV_HIST:CONV_HIST + tc, :] * cw_ref[CONV_W - 1:CONV_W, slab(s)]
            for j in range(1, CONV_W):
                acc = acc + (cbuf[i, s, CONV_HIST - j:CONV_HIST - j + tc, :]
                             * cw_ref[CONV_W - 1 - j:CONV_W - j, slab(s)])
            st8[i]["xbc"][s] = _silu(acc + cb_ref[:, slab(s)])

    def dt_proj(i):
        st8[i]["dt"] = jax.nn.softplus(_dot(st8[i]["xn"], wdt_ref[...]) + dtb_ref[...])

    def pool(i):
        u = _dot(st8[i]["xn"], wu_ref[...])
        for s in range(n_pool_slabs):
            ubuf[i, s, POOL_HIST:rp, :] = u[:, slab(s)]
            s2buf[i, s, 8:rp, :] = ubuf[i, s, 8:rp, :] + ubuf[i, s, 7:rp - 1, :]
        s4buf[i, 16:rp, :] = s2buf[i, 1, 16:rp, :] + s2buf[i, 1, 14:rp - 2, :]
        s8buf[i, 24:rp, :] = s4buf[i, 24:rp, :] + s4buf[i, 20:rp - 4, :]
        win_sums = [
            jnp.where(lo_half, s2buf[i, 0, POOL_HIST:rp, :],
                      s2buf[i, 0, POOL_HIST:rp, :] + s2buf[i, 0, POOL_HIST - 2:rp - 2, :]),
            jnp.where(lo_half, s8buf[i, POOL_HIST:rp, :],
                      s8buf[i, POOL_HIST:rp, :] + s8buf[i, POOL_HIST - 8:rp - 8, :]),
        ]
        ps = []
        for s in range(n_pool_slabs):
            win = jnp.where(lo_half, POOL_WINDOWS[2 * s], POOL_WINDOWS[2 * s + 1])
            cnt = jnp.minimum(pos1, win).astype(F32)
            ps.append(win_sums[s] / cnt - u[:, slab(s)])
        st8[i]["pool_out"] = _dot(jnp.concatenate(ps, axis=1).astype(BF16), wp_ref[...]) * ps_ref[...]

    def ssd_chunk(i, j):
        xbc = st8[i]["xbc"]
        sl = slice(j * CHUNK, (j + 1) * CHUNK)
        dt_c = st8[i]["dt"][sl]
        acs = dt_c * a_neg
        sh = 1
        while sh < CHUNK:
            abuf[i, j, SCAN_PAD:SCAN_PAD + CHUNK, :] = acs
            acs = acs + abuf[i, j, SCAN_PAD - sh:SCAN_PAD - sh + CHUNK, :]
            sh *= 2
        acs_last = acs[CHUNK - 1:CHUNK, :]
        fdec = jnp.exp(acs_last - acs) * dt_c
        src_t = (acs - jnp.log(dt_c)).T
        ht = ht_ref[i]
        ht_b = ht.astype(BF16)
        cdec, st = [], []
        for g in range(SSM_GROUPS):
            c_g = xbc[n_x + SSM_GROUPS + g][sl].astype(BF16)
            b_g = xbc[n_x + g][sl].astype(BF16)
            scores = _dot_nt(c_g, b_g)
            y_off = _dot(c_g, ht_b[:, g * gw:(g + 1) * gw])
            xw = []
            for q in range(SSM_HPG // 2):
                k0 = g * SSM_HPG + 2 * q
                pair = k0 // 2
                a_cols = [_col_bcast(acs, k, CHUNK) for k in (k0, k0 + 1)]
                lhs = [(scores * jnp.exp(jnp.where(causal, a_cols[h] - src_t[k0 + h:k0 + h + 1, :], -jnp.inf))
                        ).astype(BF16) for h in range(2)]
                x_pair = xbc[pair][sl]
                x_b = x_pair.astype(BF16)
                rhs = jnp.concatenate([x_b * mask_lo, x_b * mask_hi], axis=0)
                e_pair = jnp.exp(jnp.where(lo, a_cols[0], a_cols[1]))
                ybuf[i, sl, slab(pair)] = (_dot(jnp.concatenate(lhs, axis=1), rhs)
                                           + y_off[:, slab(q)] * e_pair + x_pair * dsk_ref[:, slab(pair)])
                f_pair = jnp.where(lo, _col_bcast(fdec, k0, LANES), _col_bcast(fdec, k0 + 1, LANES))
                xw.append((x_pair * f_pair).astype(BF16))
                cdec.append(e_pair[CHUNK - 1:CHUNK, :])
            st.append(_dot_tn(b_g, jnp.concatenate(xw, axis=1)))
        ht_ref[i] = ht * jnp.concatenate(cdec, axis=1) + jnp.concatenate(st, axis=1)

    def z_proj(i):
        st8[i]["zs"] = _silu(_dot(st8[i]["xn"], wz_ref[...]))

    def tail(i):
        d = st8[i]
        yn = _rms(ybuf[i] * d["zs"], sn_ref[...])
        cat = jnp.concatenate([d["pool_out"], yn], axis=-1).astype(BF16)
        o_ref[i] = d["x"] + _dot(cat, wo_ref[...])

    def mid(i):
        conv_block(i, n_x, 2 * SSM_GROUPS)
        dt_proj(i)
        for s0 in range(0, n_x, 2):
            conv_block(i, s0, 2)

    for i in seqs:
        head(i)
    for i in seqs:
        mid(i)
    for i in seqs:
        pool(i)
    for j in range(n_chunks):
        for i in seqs:
            ssd_chunk(i, j)
        if j == 0:
            for i in seqs:
                z_proj(i)
    for i in seqs:
        tail(i)

    ubuf[:, :, 0:POOL_HIST, :] = ubuf[:, :, tc:tc + POOL_HIST, :]
    cbuf[:, :, 0:CONV_HIST, :] = cbuf[:, :, tc:tc + CONV_HIST, :]

    @pl.when(c == nc - 1)
    def _():
        for i in seqs:
            for s in range(n_pool_slabs):
                opool_ref[i, :, slab(s)] = ubuf[i, s, POOL_HIST - POOL_BUF:POOL_HIST, :]
            for s in range(n_conv_slabs):
                oconv_ref[i, :, slab(s)] = cbuf[i, s, CONV_HIST - (CONV_W - 1):CONV_HIST, :]
            ossm_ref[i] = ht_ref[i].T.reshape(SSM_HEADS, SSM_HEADDIM, SSM_STATE)


def _mix_prompt(x, w, *, tc, nseq):
    b, s, d = x.shape
    assert s % tc == 0 and tc % CHUNK == 0 and b % nseq == 0
    consts = [w["g_mix"], w["w_u"], w["w_z"], w["w_xbc"], w["w_dt"], w["conv_w"], w["conv_b"],
              w["dt_bias"], w["a_log"], w["d_skip"], w["ssm_norm"], w["w_pool"], w["pool_scale"],
              w["w_out"]]
    tile = pl.BlockSpec((nseq, tc, d), lambda i, j: (i, j, 0))
    return pl.pallas_call(
        functools.partial(_mix_prompt_kernel, tc=tc, nseq=nseq),
        grid=(b // nseq, s // tc),
        in_specs=[tile] + [_const_spec(a.shape) for a in consts],
        out_specs=[
            tile,
            pl.BlockSpec((nseq, POOL_BUF, D_POOL), lambda i, j: (i, 0, 0)),
            pl.BlockSpec((nseq, CONV_W - 1, CONV_DIM), lambda i, j: (i, 0, 0)),
            pl.BlockSpec((nseq, SSM_HEADS, SSM_HEADDIM, SSM_STATE), lambda i, j: (i, 0, 0, 0)),
        ],
        out_shape=[
            jax.ShapeDtypeStruct((b, s, d), F32),
            jax.ShapeDtypeStruct((b, POOL_BUF, D_POOL), F32),
            jax.ShapeDtypeStruct((b, CONV_W - 1, CONV_DIM), F32),
            jax.ShapeDtypeStruct((b, SSM_HEADS, SSM_HEADDIM, SSM_STATE), F32),
        ],
        scratch_shapes=[
            pltpu.VMEM((nseq, D_POOL // LANES, POOL_HIST + tc, LANES), F32),
            pltpu.VMEM((nseq, D_POOL // LANES, POOL_HIST + tc, LANES), F32),
            pltpu.VMEM((nseq, POOL_HIST + tc, LANES), F32),
            pltpu.VMEM((nseq, POOL_HIST + tc, LANES), F32),
            pltpu.VMEM((nseq, CONV_DIM // LANES, CONV_HIST + tc, LANES), F32),
            pltpu.VMEM((nseq, tc // CHUNK, SCAN_PAD + CHUNK, LANES), F32),
            pltpu.VMEM((nseq, SSM_STATE, D_SSM), F32),
            pltpu.VMEM((nseq, tc, D_SSM), F32),
        ],
        compiler_params=_cparams(("arbitrary", "arbitrary")),
        name="mix_prompt",
    )(x, *consts)


def _attend(q, k_ref, v_ref):
    hs = [slice(h * MEM_HD, (h + 1) * MEM_HD) for h in range(MEM_HEADS)]
    q = q.astype(BF16)
    s = [_dot_nt(q[:, hs[h]], k_ref[:, hs[h]]) * (MEM_HD ** -0.5) for h in range(MEM_HEADS)]
    p = []
    for h in range(MEM_HEADS):
        e = jnp.exp(s[h] - jnp.max(s[h], axis=-1, keepdims=True))
        p.append((e / jnp.sum(e, axis=-1, keepdims=True)).astype(BF16))
    return jnp.concatenate([_dot(p[h], v_ref[:, hs[h]]) for h in range(MEM_HEADS)], axis=-1)


def _attn_prompt_kernel(x_ref, g_ref, wq_ref, k_ref, v_ref, wo_ref, o_ref):
    x = x_ref[...]
    q = _dot(_rms(x, g_ref[...]).astype(BF16), wq_ref[...])
    o = _attend(q, k_ref, v_ref).astype(BF16)
    o_ref[...] = x + _dot(o, wo_ref[...])


def _attn_prompt(x, k, v, g, wq, wo, *, tq):
    b, s, d = x.shape
    tile = pl.BlockSpec((None, tq, d), lambda i, j: (i, j, 0))
    mem = pl.BlockSpec((None, N_MEM, d), lambda i, j: (i, 0, 0))
    return pl.pallas_call(
        _attn_prompt_kernel,
        grid=(b, s // tq),
        in_specs=[tile, _const_spec(g.shape), _const_spec(wq.shape), mem, mem, _const_spec(wo.shape)],
        out_specs=tile,
        out_shape=jax.ShapeDtypeStruct((b, s, d), F32),
        compiler_params=_cparams(("parallel", "parallel")),
        name="attn_prompt",
    )(x, g, wq, k, v, wo)


def _attn_sample_kernel(q_ref, k_ref, v_ref, o_ref, *, bb):
    nq = q_ref.shape[1]
    rows = N_MEM * MEM_HEADS
    q_head = lax.broadcasted_iota(jnp.int32, (nq, rows), 0) % MEM_HEADS
    kv_head = lax.broadcasted_iota(jnp.int32, (nq, rows), 1) % MEM_HEADS
    same_head = q_head == kv_head
    q_all = q_ref[...].reshape(bb * nq, MEM_HD)
    q_all = jnp.concatenate([q_all, jnp.zeros((LANES - bb * nq, MEM_HD), F32)], axis=0).astype(BF16)
    s_t = [_dot_nt(k_ref[i].reshape(rows, MEM_HD).astype(BF16), q_all) for i in range(bb)]
    ps = []
    for i in range(bb):
        s = s_t[i].T[i * nq:(i + 1) * nq] * (MEM_HD ** -0.5)
        s = jnp.where(same_head, s, -jnp.inf)
        e = jnp.exp(s - jnp.max(s, axis=-1, keepdims=True))
        ps.append((e / jnp.sum(e, axis=-1, keepdims=True)).astype(BF16))
    for i in range(bb):
        o_ref[i] = _dot(ps[i], v_ref[i].reshape(rows, MEM_HD).astype(BF16))


def _attn_sample(q, k, v, *, bb):
    nb, nq, hd = q.shape
    qs = pl.BlockSpec((bb, nq, hd), lambda i: (i, 0, 0))
    mem = pl.BlockSpec((bb, N_MEM, MEM_HEADS, hd), lambda i: (i, 0, 0, 0))
    return pl.pallas_call(
        functools.partial(_attn_sample_kernel, bb=bb),
        grid=(nb // bb,),
        in_specs=[qs, mem, mem],
        out_specs=qs,
        out_shape=jax.ShapeDtypeStruct((nb, nq, hd), F32),
        compiler_params=_cparams(("parallel",)),
        name="attn_sample",
    )(q, k, v)


def _mix_sample_pre_kernel(x_ref, sp_ref, sc_ref, g_ref, wu_ref, wz_ref, wx_ref, wdt_ref,
                           cw_ref, cb_ref, dtb_ref, alog_ref, dsk_ref, wp_ref, ps_ref, ex_ref,
                           pool_ref, z_ref, ydx_ref, expa_ref, xwt_ref, b_ref, c_ref, dec_ref,
                           npool_ref, nconv_ref, *, nb, t, start_pos):
    def rows(i):
        return slice(i * nb, (i + 1) * nb)

    x = x_ref[...]
    xn = _rms(x, g_ref[...]).astype(BF16)
    z_ref[...] = _dot(xn, wz_ref[...])

    u = _dot(xn, wu_ref[...])
    ext = [sp_ref[rows(i), :] for i in range(POOL_BUF)] + [u[rows(i)] for i in range(t)]
    ps = []
    for i in range(t):
        sums, cnts = [], []
        for w in POOL_WINDOWS:
            s = ext[POOL_BUF + i]
            for j in range(1, w):
                s = s + ext[POOL_BUF + i - j]
            sums.append(s)
            cnts.append(jnp.full((nb, D_POOL), float(min(start_pos + i + 1, w)), F32))
        mean = _lane_group_select(sums, POOL_GW) / _lane_group_select(cnts, POOL_GW)
        ps.append(mean - ext[POOL_BUF + i])
    p = jnp.concatenate(ps, axis=0).astype(BF16)
    pool_ref[...] = _dot(p, wp_ref[...]) * ps_ref[...]
    for i in range(POOL_BUF):
        npool_ref[rows(i), :] = ext[t + i]

    xr = _dot(xn, wx_ref[...])
    cext = [sc_ref[rows(i), :] for i in range(CONV_W - 1)] + [xr[rows(i)] for i in range(t)]
    xbc = []
    for i in range(t):
        acc = cext[i] * cw_ref[0:1, :]
        for k in range(1, CONV_W):
            acc = acc + cext[i + k] * cw_ref[k:k + 1, :]
        xbc.append(_silu(acc + cb_ref[...]))
    for i in range(CONV_W - 1):
        nconv_ref[rows(i), :] = cext[t + i]
    xs = [v[:, :D_SSM] for v in xbc]
    bm = [v[:, D_SSM:D_SSM + SSM_GROUPS * SSM_STATE] for v in xbc]
    cm = [v[:, D_SSM + SSM_GROUPS * SSM_STATE:] for v in xbc]
    for i in range(t):
        b_ref[rows(i), :] = bm[i]
        c_ref[rows(i), :] = cm[i]

    dt_all = jax.nn.softplus(_dot(xn, wdt_ref[...]) + dtb_ref[...])
    a_neg = -jnp.exp(alog_ref[...])
    dt = [dt_all[rows(i)] for i in range(t)]
    acs = []
    for i in range(t):
        da = dt[i] * a_neg
        acs.append(da if i == 0 else acs[-1] + da)
    dec_ref[...] = jnp.exp(acs[-1])

    def expand(v, exact):
        if exact:
            return jnp.dot(v, ex_ref[...], precision=lax.Precision.HIGHEST,
                           preferred_element_type=F32)
        return _dot(v.astype(BF16), ex_ref[...].astype(BF16))

    lane = lax.broadcasted_iota(jnp.int32, (nb, LANES), 1)
    xw = []
    for i in range(t):
        ydx = xs[i] * dsk_ref[...]
        for s in range(i + 1):
            sc = [jnp.sum(cm[i][:, g * SSM_STATE:(g + 1) * SSM_STATE]
                          * bm[s][:, g * SSM_STATE:(g + 1) * SSM_STATE], axis=-1, keepdims=True)
                  for g in range(SSM_GROUPS)]
            sc = jnp.where(lane < SSM_HPG, sc[0], sc[1])
            coef = sc * jnp.exp(acs[i] - acs[s]) * dt[s]
            ydx = ydx + expand(coef, False) * xs[s]
        ydx_ref[rows(i), :] = ydx
        expa_ref[rows(i), :] = expand(jnp.exp(acs[i]), True)
        xw.append(xs[i] * expand(jnp.exp(acs[-1] - acs[i]) * dt[i], True))
    xwt_ref[...] = jnp.concatenate(xw, axis=0).T.astype(BF16)


def _mix_sample_pre(x, sp, sc, w, *, nb, t, start_pos):
    n = nb * t
    consts = [w["g_mix"], w["w_u"], w["w_z"], w["w_xbc"], w["w_dt"], w["conv_w"], w["conv_b"],
              w["dt_bias"], w["a_log"], w["d_skip"], w["w_pool"], w["pool_scale"], w["expand"]]
    ins = [x, sp, sc] + consts
    outs = [
        jax.ShapeDtypeStruct((n, D_POOL), F32),
        jax.ShapeDtypeStruct((n, D_SSM), F32),
        jax.ShapeDtypeStruct((n, D_SSM), F32),
        jax.ShapeDtypeStruct((n, D_SSM), F32),
        jax.ShapeDtypeStruct((D_SSM, n), BF16),
        jax.ShapeDtypeStruct((n, SSM_GROUPS * SSM_STATE), F32),
        jax.ShapeDtypeStruct((n, SSM_GROUPS * SSM_STATE), F32),
        jax.ShapeDtypeStruct((nb, LANES), F32),
        jax.ShapeDtypeStruct((POOL_BUF * nb, D_POOL), F32),
        jax.ShapeDtypeStruct(((CONV_W - 1) * nb, CONV_DIM), F32),
    ]
    return pl.pallas_call(
        functools.partial(_mix_sample_pre_kernel, nb=nb, t=t, start_pos=start_pos),
        grid=(1,),
        in_specs=[_const_spec(a.shape) for a in ins],
        out_specs=[pl.BlockSpec(o.shape, lambda i: (0, 0)) for o in outs],
        out_shape=outs,
        compiler_params=_cparams(("arbitrary",)),
        name="mix_sample_pre",
    )(*ins)


def _ssd_sample_kernel(dec_ref, h0_ref, c_ref, ydx_ref, expa_ref, xwt_ref, b_ref,
                       y_ref, hout_ref, *, nb, t, bb):
    j = pl.program_id(0)
    gw = SSM_HPG * SSM_HEADDIM
    c_blk = c_ref[...].reshape(t * bb, SSM_GROUPS * SSM_STATE)
    row_seq = lax.broadcasted_iota(jnp.int32, (t * bb, SSM_STATE), 0) % bb
    all_seq = lax.broadcasted_iota(jnp.int32, (t * nb, SSM_STATE), 0) % nb
    b_all = b_ref[...]
    yoff = [jnp.zeros((t * bb, gw), F32) for _ in range(SSM_GROUPS)]
    for i in range(bb):
        seq = j * bb + i
        for g in range(SSM_GROUPS):
            gs = slice(g * SSM_STATE, (g + 1) * SSM_STATE)
            h0 = h0_ref[i, g * SSM_HPG:(g + 1) * SSM_HPG].reshape(gw, SSM_STATE)
            c_i = jnp.where(row_seq == i, c_blk[:, gs], 0.0).astype(BF16)
            yoff[g] = yoff[g] + _dot_nt(c_i, h0.astype(BF16))
            b_i = jnp.where(all_seq == seq, b_all[:, gs], 0.0).astype(BF16)
            st = _dot(xwt_ref[g * gw:(g + 1) * gw, :], b_i)
            for k in range(SSM_HPG):
                hd = g * SSM_HPG + k
                hout_ref[i, hd] = (h0_ref[i, hd] * dec_ref[seq * SSM_HEADS + hd]
                                   + st[k * SSM_HEADDIM:(k + 1) * SSM_HEADDIM])
    yoff = jnp.concatenate(yoff, axis=1).reshape(t, bb, D_SSM)
    y_ref[...] = ydx_ref[...] + yoff * expa_ref[...]


def _ssd_sample(dec, h0, c, ydx, expa, xwt, bmat, *, nb, t, bb):
    blk = lambda d: pl.BlockSpec((t, bb, d), lambda i: (0, i, 0))
    hblk = pl.BlockSpec((bb, SSM_HEADS, SSM_HEADDIM, SSM_STATE), lambda i: (i, 0, 0, 0))
    return pl.pallas_call(
        functools.partial(_ssd_sample_kernel, nb=nb, t=t, bb=bb),
        grid=(nb // bb,),
        in_specs=[pl.BlockSpec(memory_space=pltpu.SMEM), hblk, blk(SSM_GROUPS * SSM_STATE),
                  blk(D_SSM), blk(D_SSM), _const_spec(xwt.shape), _const_spec(bmat.shape)],
        out_specs=[blk(D_SSM), hblk],
        out_shape=[jax.ShapeDtypeStruct((t, nb, D_SSM), F32),
                   jax.ShapeDtypeStruct(h0.shape, F32)],
        compiler_params=_cparams(("parallel",)),
        name="ssd_sample",
    )(dec, h0, c, ydx, expa, xwt, bmat)


def _mix_sample_post_kernel(y_ref, z_ref, pool_ref, x_ref, sn_ref, wo_ref, gq_ref, wq_ref,
                            x2_ref, q_ref):
    x2 = _gate_norm_out(y_ref[...], z_ref[...], pool_ref[...], x_ref[...], sn_ref[...], wo_ref[...])
    x2_ref[...] = x2
    q_ref[...] = _dot(_rms(x2, gq_ref[...]).astype(BF16), wq_ref[...])


def _mix_sample_post(y, z, pool, x, sn, wo, gq, wq):
    ins = [y, z, pool, x, sn, wo, gq, wq]
    outs = [jax.ShapeDtypeStruct(x.shape, F32)] * 2
    return pl.pallas_call(
        _mix_sample_post_kernel,
        grid=(1,),
        in_specs=[_const_spec(a.shape) for a in ins],
        out_specs=[pl.BlockSpec(x.shape, lambda i: (0, 0))] * 2,
        out_shape=outs,
        compiler_params=_cparams(("arbitrary",)),
        name="mix_sample_post",
    )(*ins)


def _oproj_kernel(x_ref, o_ref, wo_ref, out_ref):
    out_ref[...] = x_ref[...] + _dot(o_ref[...].astype(BF16), wo_ref[...])


def _oproj(x, o, wo):
    ins = [x, o, wo]
    return pl.pallas_call(
        _oproj_kernel,
        grid=(1,),
        in_specs=[_const_spec(a.shape) for a in ins],
        out_specs=pl.BlockSpec(x.shape, lambda i: (0, 0)),
        out_shape=jax.ShapeDtypeStruct(x.shape, F32),
        compiler_params=_cparams(("arbitrary",)),
        name="oproj",
    )(*ins)


def _prep_weights(g_ffn1, w1_gate, w1_up, w1_down, g_mix, w_in, conv_w, conv_b, dt_bias, a_log,
                  d_skip, ssm_norm, w_pool, pool_scale, w_out, g_mem, w_mem_k, w_mem_v, g_xq,
                  w_xq, w_xo, g_ffn2, w2_gate, w2_up, w2_down, g_final):
    row = lambda v: v.reshape(1, -1).astype(F32)
    pad_heads = lambda v: jnp.pad(v.astype(F32), (0, LANES - SSM_HEADS)).reshape(1, LANES)
    o1 = D_POOL
    o2 = o1 + D_SSM
    o3 = o2 + CONV_DIM
    w_pool_bd = jnp.zeros((D_POOL, D_POOL), F32)
    for g in range(len(POOL_WINDOWS)):
        w_pool_bd = w_pool_bd.at[g * POOL_GW:(g + 1) * POOL_GW, g * POOL_GW:(g + 1) * POOL_GW].set(w_pool[g])
    head_of_chan = jnp.arange(D_SSM) // SSM_HEADDIM
    expand = (jnp.arange(LANES)[:, None] == head_of_chan[None, :]).astype(F32)
    return dict(
        g_ffn1=row(g_ffn1), w1_gate=w1_gate, w1_up=w1_up, w1_down=w1_down,
        g_mix=row(g_mix), w_u=w_in[:, :o1].astype(BF16), w_z=w_in[:, o1:o2].astype(BF16),
        w_xbc=w_in[:, o2:o3].astype(BF16),
        w_dt=jnp.pad(w_in[:, o3:], ((0, 0), (0, LANES - SSM_HEADS))).astype(BF16),
        conv_w=conv_w.astype(F32), conv_b=row(conv_b), dt_bias=pad_heads(dt_bias),
        a_log=pad_heads(a_log), d_skip=row(jnp.repeat(d_skip, SSM_HEADDIM)), ssm_norm=row(ssm_norm),
        w_pool=w_pool_bd.astype(BF16), pool_scale=row(pool_scale), w_out=w_out.astype(BF16),
        g_mem=row(g_mem), w_mem_k=w_mem_k.astype(BF16), w_mem_v=w_mem_v.astype(BF16),
        g_xq=row(g_xq), w_xq=w_xq.astype(BF16), w_xo=w_xo.astype(BF16),
        g_ffn2=row(g_ffn2), w2_gate=w2_gate, w2_up=w2_up, w2_down=w2_down,
        g_final=row(g_final), expand=expand,
    )


def _layer(x_prompt, x_sample, mem_prompt, mem_k, mem_v, state_pool, state_conv, state_ssm, start_pos, w):
    b, s, d = x_prompt.shape
    nb, t, _ = x_sample.shape
    n = nb * t
    tmaj = lambda a: jnp.swapaxes(a, 0, 1).reshape(-1, a.shape[-1])
    bmaj = lambda a, r: jnp.swapaxes(a.reshape(r, nb, -1), 0, 1)
    r3 = lambda a: a.reshape(t, nb, a.shape[-1])

    k, v, k_b, v_b = _memkv(mem_prompt.reshape(b * N_MEM, d), w["g_mem"], w["w_mem_k"], w["w_mem_v"], tm=512)
    xp, xs = _ffn(x_prompt.reshape(b * s, d), tmaj(x_sample), w["g_ffn1"], w["w1_gate"], w["w1_up"],
                  w["w1_down"], w["g_final"], final=False, tm=512)

    xp, pool_p, conv_p, ssm_p = _mix_prompt(xp.reshape(b, s, d), w, tc=256, nseq=2)
    xp = _attn_prompt(xp, k_b.reshape(b, N_MEM, d), v_b.reshape(b, N_MEM, d), w["g_xq"], w["w_xq"], w["w_xo"],
                      tq=512)

    (pool_out, z, ydx, expa, xwt, bmat, cmat, dec, new_pool, new_conv) = _mix_sample_pre(
        xs, tmaj(state_pool), tmaj(state_conv), w, nb=nb, t=t, start_pos=start_pos)
    y, ssm_s = _ssd_sample(dec[:, :SSM_HEADS].reshape(-1), state_ssm, r3(cmat), r3(ydx), r3(expa),
                           xwt, bmat, nb=nb, t=t, bb=8)
    x2, q = _mix_sample_post(y.reshape(n, D_SSM), z, pool_out, xs, w["ssm_norm"], w["w_out"],
                             w["g_xq"], w["w_xq"])
    o = _attn_sample(bmaj(q, t).reshape(nb, t * MEM_HEADS, MEM_HD), mem_k, mem_v, bb=4)
    xs = _oproj(bmaj(x2, t).reshape(n, d), o.reshape(n, d), w["w_xo"])

    yp, ys = _ffn(xp.reshape(b * s, d), xs, w["g_ffn2"], w["w2_gate"], w["w2_up"], w["w2_down"],
                  w["g_final"], final=True, tm=512)
    return (yp.reshape(b, s, d), ys.reshape(nb, t, d), k, v, pool_p, conv_p, ssm_p,
            bmaj(new_pool, POOL_BUF), bmaj(new_conv, CONV_W - 1), ssm_s)


def kernel(x_prompt, x_sample, mem_prompt, cache_mem_k, cache_mem_v, state_pool, state_conv, state_ssm,
           g_ffn1, w1_gate, w1_up, w1_down, g_mix, w_in, conv_w, conv_b, dt_bias, a_log, d_skip,
           ssm_norm, w_pool, pool_scale, w_out, g_mem, w_mem_k, w_mem_v, g_xq, w_xq, w_xo,
           g_ffn2, w2_gate, w2_up, w2_down, g_final):
    assert g_ffn1.shape[0] == 1, "single-layer model"
    past_len = 16384
    w = _prep_weights(g_ffn1[0], w1_gate[0], w1_up[0], w1_down[0], g_mix[0], w_in[0], conv_w[0],
                      conv_b[0], dt_bias[0], a_log[0], d_skip[0], ssm_norm[0], w_pool[0],
                      pool_scale[0], w_out[0], g_mem[0], w_mem_k[0], w_mem_v[0], g_xq[0], w_xq[0],
                      w_xo[0], g_ffn2[0], w2_gate[0], w2_up[0], w2_down[0], g_final)
    bp = x_prompt.shape[0]
    y_p, y_s, k_p, v_p, pool_p, conv_p, ssm_p, pool_s, conv_s, ssm_s = _layer(
        x_prompt, x_sample, mem_prompt, cache_mem_k[0], cache_mem_v[0], state_pool[0], state_conv[0],
        state_ssm[0], past_len, w)
    kv_shape = (1, bp, N_MEM, MEM_HEADS, MEM_HD)
    return (y_p, y_s, k_p.reshape(kv_shape), v_p.reshape(kv_shape), pool_p[None], conv_p[None],
            ssm_p[None], pool_s[None], conv_s[None], ssm_s[None])
```

```python
import functools

import jax
import jax.numpy as jnp
from jax import lax
from jax.experimental import pallas as pl
from jax.experimental.pallas import tpu as pltpu

F32 = jnp.float32
BF16 = jnp.bfloat16

D_MODEL = 1024
D_POOL = 256
POOL_WINDOWS = (2, 4, 8, 16)
POOL_GW = 64
POOL_BUF = 15
D_SSM = 768
SSM_HEADDIM = 64
SSM_HEADS = 12
SSM_GROUPS = 2
SSM_HPG = 6
SSM_STATE = 128
CONV_W = 4
CONV_DIM = 1280
N_MEM = 256
MEM_HEADS = 4
MEM_HD = 256
D_FF = 2816
EPS = 1e-6
LANES = 128
CHUNK = 128
POOL_HIST = 32
CONV_HIST = 8
SCAN_PAD = CHUNK // 2
VMEM_LIMIT = 56 * 1024 * 1024


def _cparams(sem):
    return pltpu.CompilerParams(dimension_semantics=sem, vmem_limit_bytes=VMEM_LIMIT)


def _const_spec(shape):
    nd = len(shape)
    return pl.BlockSpec(shape, lambda *_: (0,) * nd, pipeline_mode=pl.Buffered(1))


def _rms(x, g):
    ms = jnp.mean(x * x, axis=-1, keepdims=True)
    return x * lax.rsqrt(ms + EPS) * g


def _silu(x):
    return x * jax.nn.sigmoid(x)


def _dot(a, b):
    return jnp.dot(a, b, preferred_element_type=F32)


def _dot_nt(a, b):
    return lax.dot_general(a, b, (((1,), (1,)), ((), ())), preferred_element_type=F32)


def _dot_tn(a, b):
    return lax.dot_general(a, b, (((0,), (0,)), ((), ())), preferred_element_type=F32)


FFN_W_STEPS = 8


def _ffn_kernel(xp_ref, xs_ref, g_ref, wg_ref, wu_ref, wd_ref, gf_ref, op_ref, os_ref,
                wg_s, wu_s, wd_s, *, final, n_p):
    i = pl.program_id(0)
    rows_gu = wg_ref.shape[0]
    rows_d = wd_ref.shape[0]

    @pl.when(i < FFN_W_STEPS)
    def _():
        r = pl.multiple_of(i * rows_gu, rows_gu)
        wg_s[pl.ds(r, rows_gu), :] = wg_ref[...].astype(BF16)
        wu_s[pl.ds(r, rows_gu), :] = wu_ref[...].astype(BF16)
        r = pl.multiple_of(i * rows_d, rows_d)
        wd_s[pl.ds(r, rows_d), :] = wd_ref[...].astype(BF16)

    def tile(x_ref, o_ref):
        x = x_ref[...]
        xn = _rms(x, g_ref[...]).astype(BF16)
        gate = _dot(xn, wg_s[...])
        up = _dot(xn, wu_s[...])
        h = (_silu(gate) * up).astype(BF16)
        out = x + 0.5 * _dot(h, wd_s[...])
        if final:
            out = _rms(out, gf_ref[...])
        o_ref[...] = out

    @pl.when(jnp.logical_and(i >= FFN_W_STEPS, i < FFN_W_STEPS + n_p))
    def _():
        tile(xp_ref, op_ref)

    @pl.when(i == FFN_W_STEPS + n_p)
    def _():
        tile(xs_ref, os_ref)


def _ffn(xp, xs, g, wg, wu, wd, gf, *, final, tm):
    tp, d = xp.shape
    dff = wg.shape[1]
    assert tp % tm == 0 and d % FFN_W_STEPS == 0 and dff % FFN_W_STEPS == 0
    n_p = tp // tm
    p_tile = pl.BlockSpec((tm, d), lambda i: (jnp.clip(i - FFN_W_STEPS, 0, n_p - 1), 0))
    s_tile = pl.BlockSpec(xs.shape, lambda i: (0, 0))
    w_chunk = lambda rows, cols: pl.BlockSpec((rows, cols), lambda i: (jnp.minimum(i, FFN_W_STEPS - 1), 0))
    return pl.pallas_call(
        functools.partial(_ffn_kernel, final=final, n_p=n_p),
        grid=(FFN_W_STEPS + n_p + 1,),
        in_specs=[
            p_tile, _const_spec(xs.shape), _const_spec(g.shape),
            w_chunk(d // FFN_W_STEPS, dff), w_chunk(d // FFN_W_STEPS, dff), w_chunk(dff // FFN_W_STEPS, d),
            _const_spec(gf.shape),
        ],
        out_specs=[p_tile, s_tile],
        out_shape=[jax.ShapeDtypeStruct(xp.shape, F32), jax.ShapeDtypeStruct(xs.shape, F32)],
        scratch_shapes=[pltpu.VMEM((d, dff), BF16), pltpu.VMEM((d, dff), BF16), pltpu.VMEM((dff, d), BF16)],
        compiler_params=_cparams(("arbitrary",)),
        name="ffn_final" if final else "ffn",
    )(xp, xs, g, wg, wu, wd, gf)


def _memkv_kernel(m_ref, g_ref, wk_ref, wv_ref, k_ref, v_ref, kb_ref, vb_ref, wk_s, wv_s):
    @pl.when(pl.program_id(0) == 0)
    def _():
        wk_s[...] = wk_ref[...].astype(BF16)
        wv_s[...] = wv_ref[...].astype(BF16)

    mn = _rms(m_ref[...], g_ref[...]).astype(BF16)
    k = _dot(mn, wk_s[...])
    v = _dot(mn, wv_s[...])
    for h in range(MEM_HEADS):
        k_ref[:, h, :] = k[:, h * MEM_HD:(h + 1) * MEM_HD]
        v_ref[:, h, :] = v[:, h * MEM_HD:(h + 1) * MEM_HD]
    kb_ref[...] = k.astype(BF16)
    vb_ref[...] = v.astype(BF16)


def _memkv(mem, g, wk, wv):
    b, n_mem, d = mem.shape
    row_blk = pl.BlockSpec((None, n_mem, d), lambda i: (i, 0, 0))
    head_blk = pl.BlockSpec((None, n_mem, MEM_HEADS, MEM_HD), lambda i: (i, 0, 0, 0))
    return pl.pallas_call(
        _memkv_kernel,
        grid=(b,),
        in_specs=[row_blk, _const_spec(g.shape), _const_spec(wk.shape), _const_spec(wv.shape)],
        out_specs=[head_blk, head_blk, row_blk, row_blk],
        out_shape=[jax.ShapeDtypeStruct((b, n_mem, MEM_HEADS, MEM_HD), F32)] * 2
        + [jax.ShapeDtypeStruct((b, n_mem, d), BF16)] * 2,
        scratch_shapes=[pltpu.VMEM(wk.shape, BF16), pltpu.VMEM(wv.shape, BF16)],
        compiler_params=_cparams(("arbitrary",)),
        name="memkv",
    )(mem, g, wk, wv)


def _lane_group_select(vals, width):
    lane = lax.broadcasted_iota(jnp.int32, vals[0].shape, 1)
    out = vals[-1]
    for g in range(len(vals) - 2, -1, -1):
        out = jnp.where(lane < (g + 1) * width, vals[g], out)
    return out


def _col_bcast(m, k, width):
    return jnp.broadcast_to(m[:, k:k + 1], (m.shape[0], width))


def _gate_norm_out(y, z, pool_out, x_res, ssm_norm, w_out):
    yn = _rms(y * _silu(z), ssm_norm)
    cat = jnp.concatenate([pool_out, yn], axis=-1).astype(BF16)
    return x_res + _dot(cat, w_out)


def _mix_prompt_kernel(x_ref, g_ref, win_ref, wdt_ref, cw_ref, cb_ref,
                       dtb_ref, alog_ref, dsk_ref, sn_ref, wp_ref, ps_ref, wout_ref,
                       o_ref, opool_ref, oconv_ref, ossm_ref,
                       wu_s, wz_s, wx_s, wo_s,
                       ubuf, s2buf, s4buf, s8buf, cbuf, abuf, ht_ref, ybuf, *, tc, nseq):
    c = pl.program_id(1)
    nc = pl.num_programs(1)
    n_pool_slabs = D_POOL // LANES
    n_conv_slabs = CONV_DIM // LANES
    n_chunks = tc // CHUNK
    rp = POOL_HIST + tc
    seqs = range(nseq)

    def slab(s):
        return slice(s * LANES, (s + 1) * LANES)

    @pl.when(jnp.logical_and(pl.program_id(0) == 0, c == 0))
    def _():
        wu_s[...] = win_ref[:, 0:D_POOL].astype(BF16)
        wz_s[...] = win_ref[:, D_POOL:D_POOL + D_SSM].astype(BF16)
        wx_s[...] = win_ref[:, D_POOL + D_SSM:D_POOL + D_SSM + CONV_DIM].astype(BF16)
        wo_s[...] = wout_ref[...].astype(BF16)

    @pl.when(c == 0)
    def _():
        ubuf[:, :, 0:POOL_HIST, :] = jnp.zeros((nseq, n_pool_slabs, POOL_HIST, LANES), F32)
        cbuf[:, :, 0:CONV_HIST, :] = jnp.zeros((nseq, n_conv_slabs, CONV_HIST, LANES), F32)
        abuf[:, :, 0:SCAN_PAD, :] = jnp.zeros((nseq, n_chunks, SCAN_PAD, LANES), F32)
        ht_ref[...] = jnp.zeros_like(ht_ref)

    lo_half = lax.broadcasted_iota(jnp.int32, (tc, LANES), 1) < POOL_GW
    pos1 = c * tc + lax.broadcasted_iota(jnp.int32, (tc, LANES), 0) + 1
    a_neg = -jnp.exp(alog_ref[...])
    row = lax.broadcasted_iota(jnp.int32, (CHUNK, CHUNK), 0)
    col = lax.broadcasted_iota(jnp.int32, (CHUNK, CHUNK), 1)
    causal = row >= col
    lo = lax.broadcasted_iota(jnp.int32, (CHUNK, LANES), 1) < SSM_HEADDIM
    mask_lo = jnp.where(lo, 1.0, 0.0).astype(BF16)
    mask_hi = jnp.where(lo, 0.0, 1.0).astype(BF16)
    gw = SSM_HPG * SSM_HEADDIM
    n_x = D_SSM // LANES
    st8 = {}

    def head(i):
        x = x_ref[i]
        xn = _rms(x, g_ref[...]).astype(BF16)
        st8[i] = dict(x=x, xn=xn, xbc=[None] * n_conv_slabs)

    def conv_block(i, s0, ns):
        xn = st8[i]["xn"]
        raw = _dot(xn, wx_s[:, s0 * LANES:(s0 + ns) * LANES])
        for t in range(ns):
            s = s0 + t
            cbuf[i, s, CONV_HIST:CONV_HIST + tc, :] = raw[:, slab(t)]
            acc = cbuf[i, s, CONV_HIST:CONV_HIST + tc, :] * cw_ref[CONV_W - 1:CONV_W, slab(s)]
            for j in range(1, CONV_W):
                acc = acc + (cbuf[i, s, CONV_HIST - j:CONV_HIST - j + tc, :]
                             * cw_ref[CONV_W - 1 - j:CONV_W - j, slab(s)])
            st8[i]["xbc"][s] = _silu(acc + cb_ref[:, slab(s)])

    def dt_proj(i):
        st8[i]["dt"] = jax.nn.softplus(_dot(st8[i]["xn"], wdt_ref[...]) + dtb_ref[...])

    def pool(i):
        u = _dot(st8[i]["xn"], wu_s[...])
        for s in range(n_pool_slabs):
            ubuf[i, s, POOL_HIST:rp, :] = u[:, slab(s)]
            s2buf[i, s, 8:rp, :] = ubuf[i, s, 8:rp, :] + ubuf[i, s, 7:rp - 1, :]
        s4buf[i, 16:rp, :] = s2buf[i, 1, 16:rp, :] + s2buf[i, 1, 14:rp - 2, :]
        s8buf[i, 24:rp, :] = s4buf[i, 24:rp, :] + s4buf[i, 20:rp - 4, :]
        win_sums = [
            jnp.where(lo_half, s2buf[i, 0, POOL_HIST:rp, :],
                      s2buf[i, 0, POOL_HIST:rp, :] + s2buf[i, 0, POOL_HIST - 2:rp - 2, :]),
            jnp.where(lo_half, s8buf[i, POOL_HIST:rp, :],
                      s8buf[i, POOL_HIST:rp, :] + s8buf[i, POOL_HIST - 8:rp - 8, :]),
        ]
        ps = []
        for s in range(n_pool_slabs):
            win = jnp.where(lo_half, POOL_WINDOWS[2 * s], POOL_WINDOWS[2 * s + 1])
            cnt = jnp.minimum(pos1, win).astype(F32)
            ps.append(win_sums[s] / cnt - u[:, slab(s)])
        st8[i]["pool_out"] = _dot(jnp.concatenate(ps, axis=1).astype(BF16), wp_ref[...]) * ps_ref[...]

    def ssd_chunk(i, j):
        xbc = st8[i]["xbc"]
        sl = slice(j * CHUNK, (j + 1) * CHUNK)
        dt_c = st8[i]["dt"][sl]
        acs = dt_c * a_neg
        sh = 1
        while sh < CHUNK:
            abuf[i, j, SCAN_PAD:SCAN_PAD + CHUNK, :] = acs
            acs = acs + abuf[i, j, SCAN_PAD - sh:SCAN_PAD - sh + CHUNK, :]
            sh *= 2
        acs_last = acs[CHUNK - 1:CHUNK, :]
        fdec = jnp.exp(acs_last - acs) * dt_c
        src_t = (acs - jnp.log(dt_c)).T
        ht = ht_ref[i]
        ht_b = ht.astype(BF16)
        cdec, st = [], []
        for g in range(SSM_GROUPS):
            c_g = xbc[n_x + SSM_GROUPS + g][sl].astype(BF16)
            b_g = xbc[n_x + g][sl].astype(BF16)
            scores = _dot_nt(c_g, b_g)
            y_off = _dot(c_g, ht_b[:, g * gw:(g + 1) * gw])
            xw = []
            for q in range(SSM_HPG // 2):
                k0 = g * SSM_HPG + 2 * q
                pair = k0 // 2
                a_cols = [_col_bcast(acs, k, CHUNK) for k in (k0, k0 + 1)]
                lhs = [(scores * jnp.exp(jnp.where(causal, a_cols[h] - src_t[k0 + h:k0 + h + 1, :], -jnp.inf))
                        ).astype(BF16) for h in range(2)]
                x_pair = xbc[pair][sl]
                x_b = x_pair.astype(BF16)
                rhs = jnp.concatenate([x_b * mask_lo, x_b * mask_hi], axis=0)
                e_pair = jnp.exp(jnp.where(lo, a_cols[0], a_cols[1]))
                ybuf[i, sl, slab(pair)] = (_dot(jnp.concatenate(lhs, axis=1), rhs)
                                           + y_off[:, slab(q)] * e_pair + x_pair * dsk_ref[:, slab(pair)])
                f_pair = jnp.where(lo, _col_bcast(fdec, k0, LANES), _col_bcast(fdec, k0 + 1, LANES))
                xw.append((x_pair * f_pair).astype(BF16))
                cdec.append(e_pair[CHUNK - 1:CHUNK, :])
            st.append(_dot_tn(b_g, jnp.concatenate(xw, axis=1)))
        ht_ref[i] = ht * jnp.concatenate(cdec, axis=1) + jnp.concatenate(st, axis=1)

    def z_proj(i):
        st8[i]["zs"] = _silu(_dot(st8[i]["xn"], wz_s[...]))

    def tail(i):
        d = st8[i]
        yn = _rms(ybuf[i] * d["zs"], sn_ref[...])
        cat = jnp.concatenate([d["pool_out"], yn], axis=-1).astype(BF16)
        o_ref[i] = d["x"] + _dot(cat, wo_s[...])

    def mid(i):
        conv_block(i, n_x, 2 * SSM_GROUPS)
        dt_proj(i)
        for s0 in range(0, n_x, 2):
            conv_block(i, s0, 2)

    for i in seqs:
        head(i)
    for i in seqs:
        mid(i)
    for i in seqs:
        pool(i)
    for j in range(n_chunks):
        for i in seqs:
            ssd_chunk(i, j)
        if j == 0:
            for i in seqs:
                z_proj(i)
    for i in seqs:
        tail(i)

    ubuf[:, :, 0:POOL_HIST, :] = ubuf[:, :, tc:tc + POOL_HIST, :]
    cbuf[:, :, 0:CONV_HIST, :] = cbuf[:, :, tc:tc + CONV_HIST, :]

    @pl.when(c == nc - 1)
    def _():
        for i in seqs:
            for s in range(n_pool_slabs):
                opool_ref[i, :, slab(s)] = ubuf[i, s, POOL_HIST - POOL_BUF:POOL_HIST, :]
            for s in range(n_conv_slabs):
                oconv_ref[i, :, slab(s)] = cbuf[i, s, CONV_HIST - (CONV_W - 1):CONV_HIST, :]
            ossm_ref[i] = ht_ref[i].T.reshape(SSM_HEADS, SSM_HEADDIM, SSM_STATE)


def _mix_prompt(x, w, *, tc, nseq):
    b, s, d = x.shape
    assert s % tc == 0 and tc % CHUNK == 0 and b % nseq == 0
    consts = [w["g_mix"], w["w_in"], w["w_dt"], w["conv_w"], w["conv_b"],
              w["dt_bias"], w["a_log"], w["d_skip"], w["ssm_norm"], w["w_pool"], w["pool_scale"],
              w["w_out"]]
    tile = pl.BlockSpec((nseq, tc, d), lambda i, j: (i, j, 0))
    return pl.pallas_call(
        functools.partial(_mix_prompt_kernel, tc=tc, nseq=nseq),
        grid=(b // nseq, s // tc),
        in_specs=[tile] + [_const_spec(a.shape) for a in consts],
        out_specs=[
            tile,
            pl.BlockSpec((nseq, POOL_BUF, D_POOL), lambda i, j: (i, 0, 0)),
            pl.BlockSpec((nseq, CONV_W - 1, CONV_DIM), lambda i, j: (i, 0, 0)),
            pl.BlockSpec((nseq, SSM_HEADS, SSM_HEADDIM, SSM_STATE), lambda i, j: (i, 0, 0, 0)),
        ],
        out_shape=[
            jax.ShapeDtypeStruct((b, s, d), F32),
            jax.ShapeDtypeStruct((b, POOL_BUF, D_POOL), F32),
            jax.ShapeDtypeStruct((b, CONV_W - 1, CONV_DIM), F32),
            jax.ShapeDtypeStruct((b, SSM_HEADS, SSM_HEADDIM, SSM_STATE), F32),
        ],
        scratch_shapes=[
            pltpu.VMEM((d, D_POOL), BF16), pltpu.VMEM((d, D_SSM), BF16),
            pltpu.VMEM((d, CONV_DIM), BF16), pltpu.VMEM((d, d), BF16),
            pltpu.VMEM((nseq, D_POOL // LANES, POOL_HIST + tc, LANES), F32),
            pltpu.VMEM((nseq, D_POOL // LANES, POOL_HIST + tc, LANES), F32),
            pltpu.VMEM((nseq, POOL_HIST + tc, LANES), F32),
            pltpu.VMEM((nseq, POOL_HIST + tc, LANES), F32),
            pltpu.VMEM((nseq, CONV_DIM // LANES, CONV_HIST + tc, LANES), F32),
            pltpu.VMEM((nseq, tc // CHUNK, SCAN_PAD + CHUNK, LANES), F32),
            pltpu.VMEM((nseq, SSM_STATE, D_SSM), F32),
            pltpu.VMEM((nseq, tc, D_SSM), F32),
        ],
        compiler_params=_cparams(("arbitrary", "arbitrary")),
        name="mix_prompt",
    )(x, *consts)


def _attend(q, k_ref, v_ref):
    hs = [slice(h * MEM_HD, (h + 1) * MEM_HD) for h in range(MEM_HEADS)]
    q = q.astype(BF16)
    s = [_dot_nt(q[:, hs[h]], k_ref[:, hs[h]]) * (MEM_HD ** -0.5) for h in range(MEM_HEADS)]
    p = []
    for h in range(MEM_HEADS):
        e = jnp.exp(s[h] - jnp.max(s[h], axis=-1, keepdims=True))
        p.append((e / jnp.sum(e, axis=-1, keepdims=True)).astype(BF16))
    return jnp.concatenate([_dot(p[h], v_ref[:, hs[h]]) for h in range(MEM_HEADS)], axis=-1)


def _attn_prompt_kernel(x_ref, g_ref, wq_ref, k_ref, v_ref, wo_ref, o_ref, wq_s, wo_s):
    @pl.when(jnp.logical_and(pl.program_id(0) == 0, pl.program_id(1) == 0))
    def _():
        wq_s[...] = wq_ref[...].astype(BF16)
        wo_s[...] = wo_ref[...].astype(BF16)

    x = x_ref[...]
    q = _dot(_rms(x, g_ref[...]).astype(BF16), wq_s[...])
    o = _attend(q, k_ref, v_ref).astype(BF16)
    o_ref[...] = x + _dot(o, wo_s[...])


def _attn_prompt(x, k, v, g, wq, wo, *, tq):
    b, s, d = x.shape
    tile = pl.BlockSpec((None, tq, d), lambda i, j: (i, j, 0))
    mem = pl.BlockSpec((None, N_MEM, d), lambda i, j: (i, 0, 0))
    return pl.pallas_call(
        _attn_prompt_kernel,
        grid=(b, s // tq),
        in_specs=[tile, _const_spec(g.shape), _const_spec(wq.shape), mem, mem, _const_spec(wo.shape)],
        out_specs=tile,
        out_shape=jax.ShapeDtypeStruct((b, s, d), F32),
        scratch_shapes=[pltpu.VMEM(wq.shape, BF16), pltpu.VMEM(wo.shape, BF16)],
        compiler_params=_cparams(("arbitrary", "arbitrary")),
        name="attn_prompt",
    )(x, g, wq, k, v, wo)


def _attn_sample_kernel(q_ref, k_ref, v_ref, o_ref, *, bb):
    nq = q_ref.shape[1]
    rows = N_MEM * MEM_HEADS
    q_head = lax.broadcasted_iota(jnp.int32, (nq, rows), 0) % MEM_HEADS
    kv_head = lax.broadcasted_iota(jnp.int32, (nq, rows), 1) % MEM_HEADS
    same_head = q_head == kv_head
    q_all = q_ref[...].reshape(bb * nq, MEM_HD)
    q_all = jnp.concatenate([q_all, jnp.zeros((LANES - bb * nq, MEM_HD), F32)], axis=0).astype(BF16)
    s_t = [_dot_nt(k_ref[i].reshape(rows, MEM_HD).astype(BF16), q_all) for i in range(bb)]
    ps = []
    for i in range(bb):
        s = s_t[i].T[i * nq:(i + 1) * nq] * (MEM_HD ** -0.5)
        s = jnp.where(same_head, s, -jnp.inf)
        e = jnp.exp(s - jnp.max(s, axis=-1, keepdims=True))
        ps.append((e / jnp.sum(e, axis=-1, keepdims=True)).astype(BF16))
    for i in range(bb):
        o_ref[i] = _dot(ps[i], v_ref[i].reshape(rows, MEM_HD).astype(BF16))


def _attn_sample(q, k, v, *, bb):
    nb, nq, hd = q.shape
    qs = pl.BlockSpec((bb, nq, hd), lambda i: (i, 0, 0))
    mem = pl.BlockSpec((bb, N_MEM, MEM_HEADS, hd), lambda i: (i, 0, 0, 0))
    return pl.pallas_call(
        functools.partial(_attn_sample_kernel, bb=bb),
        grid=(nb // bb,),
        in_specs=[qs, mem, mem],
        out_specs=qs,
        out_shape=jax.ShapeDtypeStruct((nb, nq, hd), F32),
        compiler_params=_cparams(("parallel",)),
        name="attn_sample",
    )(q, k, v)


def _mix_sample_pre_kernel(x_ref, sp_ref, sc_ref, g_ref, win_ref, wdt_ref,
                           cw_ref, cb_ref, dtb_ref, alog_ref, dsk_ref, wp_ref, ps_ref, ex_ref,
                           pool_ref, z_ref, ydx_ref, expa_ref, xwt_ref, b_ref, c_ref, dec_ref,
                           npool_ref, nconv_ref, *, nb, t, start_pos):
    def rows(i):
        return slice(i * nb, (i + 1) * nb)

    x = x_ref[...]
    xn = _rms(x, g_ref[...]).astype(BF16)
    z_ref[...] = _dot(xn, win_ref[:, D_POOL:D_POOL + D_SSM].astype(BF16))

    u = _dot(xn, win_ref[:, 0:D_POOL].astype(BF16))
    ext = [sp_ref[rows(i), :] for i in range(POOL_BUF)] + [u[rows(i)] for i in range(t)]
    ps = []
    for i in range(t):
        sums, cnts = [], []
        for w in POOL_WINDOWS:
            s = ext[POOL_BUF + i]
            for j in range(1, w):
                s = s + ext[POOL_BUF + i - j]
            sums.append(s)
            cnts.append(jnp.full((nb, D_POOL), float(min(start_pos + i + 1, w)), F32))
        mean = _lane_group_select(sums, POOL_GW) / _lane_group_select(cnts, POOL_GW)
        ps.append(mean - ext[POOL_BUF + i])
    p = jnp.concatenate(ps, axis=0).astype(BF16)
    pool_ref[...] = _dot(p, wp_ref[...]) * ps_ref[...]
    for i in range(POOL_BUF):
        npool_ref[rows(i), :] = ext[t + i]

    xr = _dot(xn, win_ref[:, D_POOL + D_SSM:D_POOL + D_SSM + CONV_DIM].astype(BF16))
    cext = [sc_ref[rows(i), :] for i in range(CONV_W - 1)] + [xr[rows(i)] for i in range(t)]
    xbc = []
    for i in range(t):
        acc = cext[i] * cw_ref[0:1, :]
        for k in range(1, CONV_W):
            acc = acc + cext[i + k] * cw_ref[k:k + 1, :]
        xbc.append(_silu(acc + cb_ref[...]))
    for i in range(CONV_W - 1):
        nconv_ref[rows(i), :] = cext[t + i]
    xs = [v[:, :D_SSM] for v in xbc]
    bm = [v[:, D_SSM:D_SSM + SSM_GROUPS * SSM_STATE] for v in xbc]
    cm = [v[:, D_SSM + SSM_GROUPS * SSM_STATE:] for v in xbc]
    for i in range(t):
        b_ref[rows(i), :] = bm[i]
        c_ref[rows(i), :] = cm[i]

    dt_all = jax.nn.softplus(_dot(xn, wdt_ref[...]) + dtb_ref[...])
    a_neg = -jnp.exp(alog_ref[...])
    dt = [dt_all[rows(i)] for i in range(t)]
    acs = []
    for i in range(t):
        da = dt[i] * a_neg
        acs.append(da if i == 0 else acs[-1] + da)
    dec_ref[...] = jnp.exp(acs[-1])

    def expand(v, exact):
        if exact:
            return jnp.dot(v, ex_ref[...], precision=lax.Precision.HIGHEST,
                           preferred_element_type=F32)
        return _dot(v.astype(BF16), ex_ref[...].astype(BF16))

    lane = lax.broadcasted_iota(jnp.int32, (nb, LANES), 1)
    xw = []
    for i in range(t):
        ydx = xs[i] * dsk_ref[...]
        for s in range(i + 1):
            sc = [jnp.sum(cm[i][:, g * SSM_STATE:(g + 1) * SSM_STATE]
                          * bm[s][:, g * SSM_STATE:(g + 1) * SSM_STATE], axis=-1, keepdims=True)
                  for g in range(SSM_GROUPS)]
            sc = jnp.where(lane < SSM_HPG, sc[0], sc[1])
            coef = sc * jnp.exp(acs[i] - acs[s]) * dt[s]
            ydx = ydx + expand(coef, False) * xs[s]
        ydx_ref[rows(i), :] = ydx
        expa_ref[rows(i), :] = expand(jnp.exp(acs[i]), True)
        xw.append(xs[i] * expand(jnp.exp(acs[-1] - acs[i]) * dt[i], True))
    xwt_ref[...] = jnp.concatenate(xw, axis=0).T.astype(BF16)


def _mix_sample_pre(x, sp, sc, w, *, nb, t, start_pos):
    n = nb * t
    consts = [w["g_mix"], w["w_in"], w["w_dt"], w["conv_w"], w["conv_b"],
              w["dt_bias"], w["a_log"], w["d_skip"], w["w_pool"], w["pool_scale"], w["expand"]]
    ins = [x, sp, sc] + consts
    outs = [
        jax.ShapeDtypeStruct((n, D_POOL), F32),
        jax.ShapeDtypeStruct((n, D_SSM), F32),
        jax.ShapeDtypeStruct((n, D_SSM), F32),
        jax.ShapeDtypeStruct((n, D_SSM), F32),
        jax.ShapeDtypeStruct((D_SSM, n), BF16),
        jax.ShapeDtypeStruct((n, SSM_GROUPS * SSM_STATE), F32),
        jax.ShapeDtypeStruct((n, SSM_GROUPS * SSM_STATE), F32),
        jax.ShapeDtypeStruct((nb, LANES), F32),
        jax.ShapeDtypeStruct((POOL_BUF * nb, D_POOL), F32),
        jax.ShapeDtypeStruct(((CONV_W - 1) * nb, CONV_DIM), F32),
    ]
    return pl.pallas_call(
        functools.partial(_mix_sample_pre_kernel, nb=nb, t=t, start_pos=start_pos),
        grid=(1,),
        in_specs=[_const_spec(a.shape) for a in ins],
        out_specs=[pl.BlockSpec(o.shape, lambda i: (0, 0)) for o in outs],
        out_shape=outs,
        compiler_params=_cparams(("arbitrary",)),
        name="mix_sample_pre",
    )(*ins)


def _ssd_sample_kernel(dec_ref, h0_ref, c_ref, ydx_ref, expa_ref, xwt_ref, b_ref,
                       y_ref, hout_ref, *, nb, t, bb):
    j = pl.program_id(0)
    gw = SSM_HPG * SSM_HEADDIM
    c_blk = c_ref[...].reshape(t * bb, SSM_GROUPS * SSM_STATE)
    row_seq = lax.broadcasted_iota(jnp.int32, (t * bb, SSM_STATE), 0) % bb
    all_seq = lax.broadcasted_iota(jnp.int32, (t * nb, SSM_STATE), 0) % nb
    b_all = b_ref[...]
    yoff = [jnp.zeros((t * bb, gw), F32) for _ in range(SSM_GROUPS)]
    for i in range(bb):
        seq = j * bb + i
        for g in range(SSM_GROUPS):
            gs = slice(g * SSM_STATE, (g + 1) * SSM_STATE)
            h0 = h0_ref[i, g * SSM_HPG:(g + 1) * SSM_HPG].reshape(gw, SSM_STATE)
            c_i = jnp.where(row_seq == i, c_blk[:, gs], 0.0).astype(BF16)
            yoff[g] = yoff[g] + _dot_nt(c_i, h0.astype(BF16))
            b_i = jnp.where(all_seq == seq, b_all[:, gs], 0.0).astype(BF16)
            st = _dot(xwt_ref[g * gw:(g + 1) * gw, :], b_i)
            for k in range(SSM_HPG):
                hd = g * SSM_HPG + k
                hout_ref[i, hd] = (h0_ref[i, hd] * dec_ref[seq * SSM_HEADS + hd]
                                   + st[k * SSM_HEADDIM:(k + 1) * SSM_HEADDIM])
    yoff = jnp.concatenate(yoff, axis=1).reshape(t, bb, D_SSM)
    y_ref[...] = ydx_ref[...] + yoff * expa_ref[...]


def _ssd_sample(dec, h0, c, ydx, expa, xwt, bmat, *, nb, t, bb):
    blk = lambda d: pl.BlockSpec((t, bb, d), lambda i: (0, i, 0))
    hblk = pl.BlockSpec((bb, SSM_HEADS, SSM_HEADDIM, SSM_STATE), lambda i: (i, 0, 0, 0))
    return pl.pallas_call(
        functools.partial(_ssd_sample_kernel, nb=nb, t=t, bb=bb),
        grid=(nb // bb,),
        in_specs=[pl.BlockSpec(memory_space=pltpu.SMEM), hblk, blk(SSM_GROUPS * SSM_STATE),
                  blk(D_SSM), blk(D_SSM), _const_spec(xwt.shape), _const_spec(bmat.shape)],
        out_specs=[blk(D_SSM), hblk],
        out_shape=[jax.ShapeDtypeStruct((t, nb, D_SSM), F32),
                   jax.ShapeDtypeStruct(h0.shape, F32)],
        compiler_params=_cparams(("parallel",)),
        name="ssd_sample",
    )(dec, h0, c, ydx, expa, xwt, bmat)


def _mix_sample_post_kernel(y_ref, z_ref, pool_ref, x_ref, sn_ref, wo_ref, gq_ref, wq_ref,
                            x2_ref, q_ref):
    x2 = _gate_norm_out(y_ref[...], z_ref[...], pool_ref[...], x_ref[...], sn_ref[...],
                        wo_ref[...].astype(BF16))
    x2_ref[...] = x2
    q_ref[...] = _dot(_rms(x2, gq_ref[...]).astype(BF16), wq_ref[...].astype(BF16))


def _mix_sample_post(y, z, pool, x, sn, wo, gq, wq):
    ins = [y, z, pool, x, sn, wo, gq, wq]
    outs = [jax.ShapeDtypeStruct(x.shape, F32)] * 2
    return pl.pallas_call(
        _mix_sample_post_kernel,
        grid=(1,),
        in_specs=[_const_spec(a.shape) for a in ins],
        out_specs=[pl.BlockSpec(x.shape, lambda i: (0, 0))] * 2,
        out_shape=outs,
        compiler_params=_cparams(("arbitrary",)),
        name="mix_sample_post",
    )(*ins)


def _oproj_kernel(x_ref, o_ref, wo_ref, out_ref):
    out_ref[...] = x_ref[...] + _dot(o_ref[...].astype(BF16), wo_ref[...].astype(BF16))


def _oproj(x, o, wo):
    ins = [x, o, wo]
    return pl.pallas_call(
        _oproj_kernel,
        grid=(1,),
        in_specs=[_const_spec(a.shape) for a in ins],
        out_specs=pl.BlockSpec(x.shape, lambda i: (0, 0)),
        out_shape=jax.ShapeDtypeStruct(x.shape, F32),
        compiler_params=_cparams(("arbitrary",)),
        name="oproj",
    )(*ins)


def _prep_weights(g_ffn1, w1_gate, w1_up, w1_down, g_mix, w_in, conv_w, conv_b, dt_bias, a_log,
                  d_skip, ssm_norm, w_pool, pool_scale, w_out, g_mem, w_mem_k, w_mem_v, g_xq,
                  w_xq, w_xo, g_ffn2, w2_gate, w2_up, w2_down, g_final):
    row = lambda v: v.reshape(1, -1).astype(F32)
    pad_heads = lambda v: jnp.pad(v.astype(F32), (0, LANES - SSM_HEADS)).reshape(1, LANES)
    o1 = D_POOL
    o2 = o1 + D_SSM
    o3 = o2 + CONV_DIM
    w_pool_bd = jnp.zeros((D_POOL, D_POOL), F32)
    for g in range(len(POOL_WINDOWS)):
        w_pool_bd = w_pool_bd.at[g * POOL_GW:(g + 1) * POOL_GW, g * POOL_GW:(g + 1) * POOL_GW].set(w_pool[g])
    head_of_chan = jnp.arange(D_SSM) // SSM_HEADDIM
    expand = (jnp.arange(LANES)[:, None] == head_of_chan[None, :]).astype(F32)
    return dict(
        g_ffn1=row(g_ffn1), w1_gate=w1_gate, w1_up=w1_up, w1_down=w1_down,
        g_mix=row(g_mix), w_in=w_in,
        w_dt=jnp.pad(w_in[:, o3:], ((0, 0), (0, LANES - SSM_HEADS))).astype(BF16),
        conv_w=conv_w.astype(F32), conv_b=row(conv_b), dt_bias=pad_heads(dt_bias),
        a_log=pad_heads(a_log), d_skip=row(jnp.repeat(d_skip, SSM_HEADDIM)), ssm_norm=row(ssm_norm),
        w_pool=w_pool_bd.astype(BF16), pool_scale=row(pool_scale), w_out=w_out,
        g_mem=row(g_mem), w_mem_k=w_mem_k, w_mem_v=w_mem_v,
        g_xq=row(g_xq), w_xq=w_xq, w_xo=w_xo,
        g_ffn2=row(g_ffn2), w2_gate=w2_gate, w2_up=w2_up, w2_down=w2_down,
        g_final=row(g_final), expand=expand,
    )


def _layer(x_prompt, x_sample, mem_prompt, mem_k, mem_v, state_pool, state_conv, state_ssm, start_pos, w):
    b, s, d = x_prompt.shape
    nb, t, _ = x_sample.shape
    n = nb * t
    tmaj = lambda a: jnp.swapaxes(a, 0, 1).reshape(-1, a.shape[-1])
    bmaj = lambda a, r: jnp.swapaxes(a.reshape(r, nb, -1), 0, 1)
    r3 = lambda a: a.reshape(t, nb, a.shape[-1])

    k, v, k_b, v_b = _memkv(mem_prompt, w["g_mem"], w["w_mem_k"], w["w_mem_v"])
    xp, xs = _ffn(x_prompt.reshape(b * s, d), tmaj(x_sample), w["g_ffn1"], w["w1_gate"], w["w1_up"],
                  w["w1_down"], w["g_final"], final=False, tm=512)

    xp, pool_p, conv_p, ssm_p = _mix_prompt(xp.reshape(b, s, d), w, tc=256, nseq=2)
    xp = _attn_prompt(xp, k_b, v_b, w["g_xq"], w["w_xq"], w["w_xo"], tq=512)

    (pool_out, z, ydx, expa, xwt, bmat, cmat, dec, new_pool, new_conv) = _mix_sample_pre(
        xs, tmaj(state_pool), tmaj(state_conv), w, nb=nb, t=t, start_pos=start_pos)
    y, ssm_s = _ssd_sample(dec[:, :SSM_HEADS].reshape(-1), state_ssm, r3(cmat), r3(ydx), r3(expa),
                           xwt, bmat, nb=nb, t=t, bb=8)
    x2, q = _mix_sample_post(y.reshape(n, D_SSM), z, pool_out, xs, w["ssm_norm"], w["w_out"],
                             w["g_xq"], w["w_xq"])
    o = _attn_sample(bmaj(q, t).reshape(nb, t * MEM_HEADS, MEM_HD), mem_k, mem_v, bb=4)
    xs = _oproj(bmaj(x2, t).reshape(n, d), o.reshape(n, d), w["w_xo"])

    yp, ys = _ffn(xp.reshape(b * s, d), xs, w["g_ffn2"], w["w2_gate"], w["w2_up"], w["w2_down"],
                  w["g_final"], final=True, tm=512)
    return (yp.reshape(b, s, d), ys.reshape(nb, t, d), k, v, pool_p, conv_p, ssm_p,
            bmaj(new_pool, POOL_BUF), bmaj(new_conv, CONV_W - 1), ssm_s)


def kernel(x_prompt, x_sample, mem_prompt, cache_mem_k, cache_mem_v, state_pool, state_conv, state_ssm,
           g_ffn1, w1_gate, w1_up, w1_down, g_mix, w_in, conv_w, conv_b, dt_bias, a_log, d_skip,
           ssm_norm, w_pool, pool_scale, w_out, g_mem, w_mem_k, w_mem_v, g_xq, w_xq, w_xo,
           g_ffn2, w2_gate, w2_up, w2_down, g_final):
    assert g_ffn1.shape[0] == 1, "single-layer model"
    past_len = 16384
    w = _prep_weights(g_ffn1[0], w1_gate[0], w1_up[0], w1_down[0], g_mix[0], w_in[0], conv_w[0],
                      conv_b[0], dt_bias[0], a_log[0], d_skip[0], ssm_norm[0], w_pool[0],
                      pool_scale[0], w_out[0], g_mem[0], w_mem_k[0], w_mem_v[0], g_xq[0], w_xq[0],
                      w_xo[0], g_ffn2[0], w2_gate[0], w2_up[0], w2_down[0], g_final)
    bp = x_prompt.shape[0]
    y_p, y_s, k_p, v_p, pool_p, conv_p, ssm_p, pool_s, conv_s, ssm_s = _layer(
        x_prompt, x_sample, mem_prompt, cache_mem_k[0], cache_mem_v[0], state_pool[0], state_conv[0],
        state_ssm[0], past_len, w)
    kv_shape = (1, bp, N_MEM, MEM_HEADS, MEM_HD)
    return (y_p, y_s, k_p.reshape(kv_shape), v_p.reshape(kv_shape), pool_p[None], conv_p[None],
            ssm_p[None], pool_s[None], conv_s[None], ssm_s[None])
```

```python
import functools

import jax
import jax.numpy as jnp
from jax import lax
from jax.experimental import pallas as pl
from jax.experimental.pallas import tpu as pltpu

F32 = jnp.float32
BF16 = jnp.bfloat16

D_MODEL = 1024
D_POOL = 256
POOL_WINDOWS = (2, 4, 8, 16)
POOL_GW = 64
POOL_BUF = 15
D_SSM = 768
SSM_HEADDIM = 64
SSM_HEADS = 12
SSM_GROUPS = 2
SSM_HPG = 6
SSM_STATE = 128
CONV_W = 4
CONV_DIM = 1280
D_PROJ = D_POOL + D_SSM + CONV_DIM
N_MEM = 256
MEM_HEADS = 4
MEM_HD = 256
D_FF = 2816
EPS = 1e-6
LANES = 128
CHUNK = 128
POOL_HIST = 32
CONV_HIST = 8
SCAN_PAD = CHUNK // 2
VMEM_LIMIT = 56 * 1024 * 1024


def _cparams(sem):
    return pltpu.CompilerParams(dimension_semantics=sem, vmem_limit_bytes=VMEM_LIMIT)


def _const_spec(shape):
    nd = len(shape)
    return pl.BlockSpec(shape, lambda *_: (0,) * nd, pipeline_mode=pl.Buffered(1))


def _rms(x, g):
    ms = jnp.mean(x * x, axis=-1, keepdims=True)
    return x * lax.rsqrt(ms + EPS) * g


def _silu(x):
    return x * jax.nn.sigmoid(x)


def _dot(a, b):
    return jnp.dot(a, b, preferred_element_type=F32)


def _dot_nt(a, b):
    return lax.dot_general(a, b, (((1,), (1,)), ((), ())), preferred_element_type=F32)


def _dot_tn(a, b):
    return lax.dot_general(a, b, (((0,), (0,)), ((), ())), preferred_element_type=F32)


FFN_W_STEPS = 8


def _ffn_kernel(xp_ref, xs_ref, g_ref, wg_ref, wu_ref, wd_ref, gf_ref, op_ref, os_ref,
                wg_s, wu_s, wd_s, *, final, n_p):
    i = pl.program_id(0)
    rows_gu = wg_ref.shape[0]
    rows_d = wd_ref.shape[0]

    @pl.when(i < FFN_W_STEPS)
    def _():
        r = pl.multiple_of(i * rows_gu, rows_gu)
        wg_s[pl.ds(r, rows_gu), :] = wg_ref[...].astype(BF16)
        wu_s[pl.ds(r, rows_gu), :] = wu_ref[...].astype(BF16)
        r = pl.multiple_of(i * rows_d, rows_d)
        wd_s[pl.ds(r, rows_d), :] = wd_ref[...].astype(BF16)

    def tile(x_ref, o_ref):
        x = x_ref[...]
        xn = _rms(x, g_ref[...]).astype(BF16)
        gate = _dot(xn, wg_s[...])
        up = _dot(xn, wu_s[...])
        h = (_silu(gate) * up).astype(BF16)
        out = x + 0.5 * _dot(h, wd_s[...])
        if final:
            out = _rms(out, gf_ref[...])
        o_ref[...] = out

    @pl.when(jnp.logical_and(i >= FFN_W_STEPS, i < FFN_W_STEPS + n_p))
    def _():
        tile(xp_ref, op_ref)

    @pl.when(i == FFN_W_STEPS + n_p)
    def _():
        tile(xs_ref, os_ref)


def _ffn(xp, xs, g, wg, wu, wd, gf, *, final, tm):
    tp, d = xp.shape
    dff = wg.shape[1]
    assert tp % tm == 0 and d % FFN_W_STEPS == 0 and dff % FFN_W_STEPS == 0
    n_p = tp // tm
    p_tile = pl.BlockSpec((tm, d), lambda i: (jnp.clip(i - FFN_W_STEPS, 0, n_p - 1), 0))
    s_tile = pl.BlockSpec(xs.shape, lambda i: (0, 0))
    w_chunk = lambda rows, cols: pl.BlockSpec((rows, cols), lambda i: (jnp.minimum(i, FFN_W_STEPS - 1), 0))
    return pl.pallas_call(
        functools.partial(_ffn_kernel, final=final, n_p=n_p),
        grid=(FFN_W_STEPS + n_p + 1,),
        in_specs=[
            p_tile, _const_spec(xs.shape), _const_spec(g.shape),
            w_chunk(d // FFN_W_STEPS, dff), w_chunk(d // FFN_W_STEPS, dff), w_chunk(dff // FFN_W_STEPS, d),
            _const_spec(gf.shape),
        ],
        out_specs=[p_tile, s_tile],
        out_shape=[jax.ShapeDtypeStruct(xp.shape, F32), jax.ShapeDtypeStruct(xs.shape, F32)],
        scratch_shapes=[pltpu.VMEM((d, dff), BF16), pltpu.VMEM((d, dff), BF16), pltpu.VMEM((dff, d), BF16)],
        compiler_params=_cparams(("arbitrary",)),
        name="ffn_final" if final else "ffn",
    )(xp, xs, g, wg, wu, wd, gf)


def _memkv_kernel(m_ref, g_ref, wk_ref, wv_ref, k_ref, v_ref, kb_ref, vb_ref, wk_s, wv_s):
    @pl.when(pl.program_id(0) == 0)
    def _():
        wk_s[...] = wk_ref[...].astype(BF16)
        wv_s[...] = wv_ref[...].astype(BF16)

    mn = _rms(m_ref[...], g_ref[...]).astype(BF16)
    k = _dot(mn, wk_s[...])
    v = _dot(mn, wv_s[...])
    for h in range(MEM_HEADS):
        k_ref[:, h, :] = k[:, h * MEM_HD:(h + 1) * MEM_HD]
        v_ref[:, h, :] = v[:, h * MEM_HD:(h + 1) * MEM_HD]
    kb_ref[...] = k.astype(BF16)
    vb_ref[...] = v.astype(BF16)


def _memkv(mem, g, wk, wv):
    b, n_mem, d = mem.shape
    row_blk = pl.BlockSpec((None, n_mem, d), lambda i: (i, 0, 0))
    head_blk = pl.BlockSpec((None, n_mem, MEM_HEADS, MEM_HD), lambda i: (i, 0, 0, 0))
    return pl.pallas_call(
        _memkv_kernel,
        grid=(b,),
        in_specs=[row_blk, _const_spec(g.shape), _const_spec(wk.shape), _const_spec(wv.shape)],
        out_specs=[head_blk, head_blk, row_blk, row_blk],
        out_shape=[jax.ShapeDtypeStruct((b, n_mem, MEM_HEADS, MEM_HD), F32)] * 2
        + [jax.ShapeDtypeStruct((b, n_mem, d), BF16)] * 2,
        scratch_shapes=[pltpu.VMEM(wk.shape, BF16), pltpu.VMEM(wv.shape, BF16)],
        compiler_params=_cparams(("arbitrary",)),
        name="memkv",
    )(mem, g, wk, wv)


def _lane_group_select(vals, width):
    lane = lax.broadcasted_iota(jnp.int32, vals[0].shape, 1)
    out = vals[-1]
    for g in range(len(vals) - 2, -1, -1):
        out = jnp.where(lane < (g + 1) * width, vals[g], out)
    return out


def _col_bcast(m, k, width):
    return jnp.broadcast_to(m[:, k:k + 1], (m.shape[0], width))


def _gate_norm_out(y, z, pool_out, x_res, ssm_norm, w_out):
    yn = _rms(y * _silu(z), ssm_norm)
    cat = jnp.concatenate([pool_out, yn], axis=-1).astype(BF16)
    return x_res + _dot(cat, w_out)


def _mix_prompt_kernel(x_ref, g_ref, win_ref, cw_ref, cb_ref,
                       dtb_ref, alog_ref, dsk_ref, sn_ref, wp_ref, ps_ref, wout_ref,
                       o_ref, opool_ref, oconv_ref, ossm_ref, owin_ref,
                       wu_s, wz_s, wx_s, wdt_s, wo_s,
                       ubuf, s2buf, s4buf, s8buf, cbuf, abuf, ht_ref, ybuf, *, tc, nseq):
    c = pl.program_id(1)
    nc = pl.num_programs(1)
    n_pool_slabs = D_POOL // LANES
    n_conv_slabs = CONV_DIM // LANES
    n_chunks = tc // CHUNK
    rp = POOL_HIST + tc
    seqs = range(nseq)

    def slab(s):
        return slice(s * LANES, (s + 1) * LANES)

    @pl.when(jnp.logical_and(pl.program_id(0) == 0, c == 0))
    def _():
        wu_s[...] = win_ref[0:D_POOL, :].T.astype(BF16)
        wz_s[...] = win_ref[D_POOL:D_POOL + D_SSM, :].T.astype(BF16)
        wx_s[...] = win_ref[D_POOL + D_SSM:D_PROJ, :].T.astype(BF16)
        wdt_s[...] = win_ref[D_PROJ:D_PROJ + LANES, :].T.astype(BF16)
        wo_s[...] = wout_ref[...].astype(BF16)
        owin_ref[:, 0:D_POOL] = wu_s[...]
        owin_ref[:, D_POOL:D_POOL + D_SSM] = wz_s[...]
        owin_ref[:, D_POOL + D_SSM:D_PROJ] = wx_s[...]
        owin_ref[:, D_PROJ:D_PROJ + LANES] = wdt_s[...]

    @pl.when(c == 0)
    def _():
        ubuf[:, :, 0:POOL_HIST, :] = jnp.zeros((nseq, n_pool_slabs, POOL_HIST, LANES), F32)
        cbuf[:, :, 0:CONV_HIST, :] = jnp.zeros((nseq, n_conv_slabs, CONV_HIST, LANES), F32)
        abuf[:, :, 0:SCAN_PAD, :] = jnp.zeros((nseq, n_chunks, SCAN_PAD, LANES), F32)
        ht_ref[...] = jnp.zeros_like(ht_ref)

    lo_half = lax.broadcasted_iota(jnp.int32, (tc, LANES), 1) < POOL_GW
    pos1 = c * tc + lax.broadcasted_iota(jnp.int32, (tc, LANES), 0) + 1
    a_neg = -jnp.exp(alog_ref[...])
    row = lax.broadcasted_iota(jnp.int32, (CHUNK, CHUNK), 0)
    col = lax.broadcasted_iota(jnp.int32, (CHUNK, CHUNK), 1)
    causal = row >= col
    lo = lax.broadcasted_iota(jnp.int32, (CHUNK, LANES), 1) < SSM_HEADDIM
    mask_lo = jnp.where(lo, 1.0, 0.0).astype(BF16)
    mask_hi = jnp.where(lo, 0.0, 1.0).astype(BF16)
    gw = SSM_HPG * SSM_HEADDIM
    n_x = D_SSM // LANES
    st8 = {}

    def head(i):
        x = x_ref[i]
        xn = _rms(x, g_ref[...]).astype(BF16)
        st8[i] = dict(x=x, xn=xn, xbc=[None] * n_conv_slabs)

    def conv_block(i, s0, ns):
        xn = st8[i]["xn"]
        raw = _dot(xn, wx_s[:, s0 * LANES:(s0 + ns) * LANES])
        for t in range(ns):
            s = s0 + t
            cbuf[i, s, CONV_HIST:CONV_HIST + tc, :] = raw[:, slab(t)]
            acc = cbuf[i, s, CONV_HIST:CONV_HIST + tc, :] * cw_ref[CONV_W - 1:CONV_W, slab(s)]
            for j in range(1, CONV_W):
                acc = acc + (cbuf[i, s, CONV_HIST - j:CONV_HIST - j + tc, :]
                             * cw_ref[CONV_W - 1 - j:CONV_W - j, slab(s)])
            st8[i]["xbc"][s] = _silu(acc + cb_ref[:, slab(s)])

    def dt_proj(i):
        st8[i]["dt"] = jax.nn.softplus(_dot(st8[i]["xn"], wdt_s[...]) + dtb_ref[...])

    def pool(i):
        u = _dot(st8[i]["xn"], wu_s[...])
        for s in range(n_pool_slabs):
            ubuf[i, s, POOL_HIST:rp, :] = u[:, slab(s)]
            s2buf[i, s, 8:rp, :] = ubuf[i, s, 8:rp, :] + ubuf[i, s, 7:rp - 1, :]
        s4buf[i, 16:rp, :] = s2buf[i, 1, 16:rp, :] + s2buf[i, 1, 14:rp - 2, :]
        s8buf[i, 24:rp, :] = s4buf[i, 24:rp, :] + s4buf[i, 20:rp - 4, :]
        win_sums = [
            jnp.where(lo_half, s2buf[i, 0, POOL_HIST:rp, :],
                      s2buf[i, 0, POOL_HIST:rp, :] + s2buf[i, 0, POOL_HIST - 2:rp - 2, :]),
            jnp.where(lo_half, s8buf[i, POOL_HIST:rp, :],
                      s8buf[i, POOL_HIST:rp, :] + s8buf[i, POOL_HIST - 8:rp - 8, :]),
        ]
        ps = []
        for s in range(n_pool_slabs):
            win = jnp.where(lo_half, POOL_WINDOWS[2 * s], POOL_WINDOWS[2 * s + 1])
            cnt = jnp.minimum(pos1, win).astype(F32)
            ps.append(win_sums[s] / cnt - u[:, slab(s)])
        st8[i]["pool_out"] = _dot(jnp.concatenate(ps, axis=1).astype(BF16), wp_ref[...]) * ps_ref[...]

    def ssd_chunk(i, j):
        xbc = st8[i]["xbc"]
        sl = slice(j * CHUNK, (j + 1) * CHUNK)
        dt_c = st8[i]["dt"][sl]
        acs = dt_c * a_neg
        sh = 1
        while sh < CHUNK:
            abuf[i, j, SCAN_PAD:SCAN_PAD + CHUNK, :] = acs
            acs = acs + abuf[i, j, SCAN_PAD - sh:SCAN_PAD - sh + CHUNK, :]
            sh *= 2
        acs_last = acs[CHUNK - 1:CHUNK, :]
        fdec = jnp.exp(acs_last - acs) * dt_c
        src_t = (acs - jnp.log(dt_c)).T
        ht = ht_ref[i]
        ht_b = ht.astype(BF16)
        cdec, st = [], []
        for g in range(SSM_GROUPS):
            c_g = xbc[n_x + SSM_GROUPS + g][sl].astype(BF16)
            b_g = xbc[n_x + g][sl].astype(BF16)
            scores = _dot_nt(c_g, b_g)
            y_off = _dot(c_g, ht_b[:, g * gw:(g + 1) * gw])
            xw = []
            for q in range(SSM_HPG // 2):
                k0 = g * SSM_HPG + 2 * q
                pair = k0 // 2
                a_cols = [_col_bcast(acs, k, CHUNK) for k in (k0, k0 + 1)]
                lhs = [(scores * jnp.exp(jnp.where(causal, a_cols[h] - src_t[k0 + h:k0 + h + 1, :], -jnp.inf))
                        ).astype(BF16) for h in range(2)]
                x_pair = xbc[pair][sl]
                x_b = x_pair.astype(BF16)
                rhs = jnp.concatenate([x_b * mask_lo, x_b * mask_hi], axis=0)
                e_pair = jnp.exp(jnp.where(lo, a_cols[0], a_cols[1]))
                ybuf[i, sl, slab(pair)] = (_dot(jnp.concatenate(lhs, axis=1), rhs)
                                           + y_off[:, slab(q)] * e_pair + x_pair * dsk_ref[:, slab(pair)])
                f_pair = jnp.where(lo, _col_bcast(fdec, k0, LANES), _col_bcast(fdec, k0 + 1, LANES))
                xw.append((x_pair * f_pair).astype(BF16))
                cdec.append(e_pair[CHUNK - 1:CHUNK, :])
            st.append(_dot_tn(b_g, jnp.concatenate(xw, axis=1)))
        ht_ref[i] = ht * jnp.concatenate(cdec, axis=1) + jnp.concatenate(st, axis=1)

    def z_proj(i):
        st8[i]["zs"] = _silu(_dot(st8[i]["xn"], wz_s[...]))

    def tail(i):
        d = st8[i]
        yn = _rms(ybuf[i] * d["zs"], sn_ref[...])
        cat = jnp.concatenate([d["pool_out"], yn], axis=-1).astype(BF16)
        o_ref[i] = d["x"] + _dot(cat, wo_s[...])

    def mid(i):
        conv_block(i, n_x, 2 * SSM_GROUPS)
        dt_proj(i)
        for s0 in range(0, n_x, 2):
            conv_block(i, s0, 2)

    for i in seqs:
        head(i)
    for i in seqs:
        mid(i)
    for i in seqs:
        pool(i)
    for j in range(n_chunks):
        for i in seqs:
            ssd_chunk(i, j)
        if j == 0:
            for i in seqs:
                z_proj(i)
    for i in seqs:
        tail(i)

    ubuf[:, :, 0:POOL_HIST, :] = ubuf[:, :, tc:tc + POOL_HIST, :]
    cbuf[:, :, 0:CONV_HIST, :] = cbuf[:, :, tc:tc + CONV_HIST, :]

    @pl.when(c == nc - 1)
    def _():
        for i in seqs:
            for s in range(n_pool_slabs):
                opool_ref[i, :, slab(s)] = ubuf[i, s, POOL_HIST - POOL_BUF:POOL_HIST, :]
            for s in range(n_conv_slabs):
                oconv_ref[i, :, slab(s)] = cbuf[i, s, CONV_HIST - (CONV_W - 1):CONV_HIST, :]
            ossm_ref[i] = ht_ref[i].T.reshape(SSM_HEADS, SSM_HEADDIM, SSM_STATE)


def _mix_prompt(x, w, *, tc, nseq):
    b, s, d = x.shape
    assert s % tc == 0 and tc % CHUNK == 0 and b % nseq == 0
    consts = [w["g_mix"], w["w_in"], w["conv_w"], w["conv_b"],
              w["dt_bias"], w["a_log"], w["d_skip"], w["ssm_norm"], w["w_pool"], w["pool_scale"],
              w["w_out"]]
    tile = pl.BlockSpec((nseq, tc, d), lambda i, j: (i, j, 0))
    return pl.pallas_call(
        functools.partial(_mix_prompt_kernel, tc=tc, nseq=nseq),
        grid=(b // nseq, s // tc),
        in_specs=[tile] + [_const_spec(a.shape) for a in consts],
        out_specs=[
            tile,
            pl.BlockSpec((nseq, POOL_BUF, D_POOL), lambda i, j: (i, 0, 0)),
            pl.BlockSpec((nseq, CONV_W - 1, CONV_DIM), lambda i, j: (i, 0, 0)),
            pl.BlockSpec((nseq, SSM_HEADS, SSM_HEADDIM, SSM_STATE), lambda i, j: (i, 0, 0, 0)),
            pl.BlockSpec((d, D_PROJ + LANES), lambda i, j: (0, 0)),
        ],
        out_shape=[
            jax.ShapeDtypeStruct((b, s, d), F32),
            jax.ShapeDtypeStruct((b, POOL_BUF, D_POOL), F32),
            jax.ShapeDtypeStruct((b, CONV_W - 1, CONV_DIM), F32),
            jax.ShapeDtypeStruct((b, SSM_HEADS, SSM_HEADDIM, SSM_STATE), F32),
            jax.ShapeDtypeStruct((d, D_PROJ + LANES), BF16),
        ],
        scratch_shapes=[
            pltpu.VMEM((d, D_POOL), BF16), pltpu.VMEM((d, D_SSM), BF16),
            pltpu.VMEM((d, CONV_DIM), BF16), pltpu.VMEM((d, LANES), BF16), pltpu.VMEM((d, d), BF16),
            pltpu.VMEM((nseq, D_POOL // LANES, POOL_HIST + tc, LANES), F32),
            pltpu.VMEM((nseq, D_POOL // LANES, POOL_HIST + tc, LANES), F32),
            pltpu.VMEM((nseq, POOL_HIST + tc, LANES), F32),
            pltpu.VMEM((nseq, POOL_HIST + tc, LANES), F32),
            pltpu.VMEM((nseq, CONV_DIM // LANES, CONV_HIST + tc, LANES), F32),
            pltpu.VMEM((nseq, tc // CHUNK, SCAN_PAD + CHUNK, LANES), F32),
            pltpu.VMEM((nseq, SSM_STATE, D_SSM), F32),
            pltpu.VMEM((nseq, tc, D_SSM), F32),
        ],
        compiler_params=_cparams(("arbitrary", "arbitrary")),
        name="mix_prompt",
    )(x, *consts)


def _attend(q, k_ref, v_ref):
    hs = [slice(h * MEM_HD, (h + 1) * MEM_HD) for h in range(MEM_HEADS)]
    q = q.astype(BF16)
    s = [_dot_nt(q[:, hs[h]], k_ref[:, hs[h]]) * (MEM_HD ** -0.5) for h in range(MEM_HEADS)]
    p = []
    for h in range(MEM_HEADS):
        e = jnp.exp(s[h] - jnp.max(s[h], axis=-1, keepdims=True))
        p.append((e / jnp.sum(e, axis=-1, keepdims=True)).astype(BF16))
    return jnp.concatenate([_dot(p[h], v_ref[:, hs[h]]) for h in range(MEM_HEADS)], axis=-1)


def _attn_prompt_kernel(x_ref, g_ref, wq_ref, k_ref, v_ref, wo_ref, o_ref, wq_s, wo_s):
    @pl.when(jnp.logical_and(pl.program_id(0) == 0, pl.program_id(1) == 0))
    def _():
        wq_s[...] = wq_ref[...].astype(BF16)
        wo_s[...] = wo_ref[...].astype(BF16)

    x = x_ref[...]
    q = _dot(_rms(x, g_ref[...]).astype(BF16), wq_s[...])
    o = _attend(q, k_ref, v_ref).astype(BF16)
    o_ref[...] = x + _dot(o, wo_s[...])


def _attn_prompt(x, k, v, g, wq, wo, *, tq):
    b, s, d = x.shape
    tile = pl.BlockSpec((None, tq, d), lambda i, j: (i, j, 0))
    mem = pl.BlockSpec((None, N_MEM, d), lambda i, j: (i, 0, 0))
    return pl.pallas_call(
        _attn_prompt_kernel,
        grid=(b, s // tq),
        in_specs=[tile, _const_spec(g.shape), _const_spec(wq.shape), mem, mem, _const_spec(wo.shape)],
        out_specs=tile,
        out_shape=jax.ShapeDtypeStruct((b, s, d), F32),
        scratch_shapes=[pltpu.VMEM(wq.shape, BF16), pltpu.VMEM(wo.shape, BF16)],
        compiler_params=_cparams(("arbitrary", "arbitrary")),
        name="attn_prompt",
    )(x, g, wq, k, v, wo)


def _attn_sample_kernel(q_ref, k_ref, v_ref, o_ref, *, bb):
    nq = q_ref.shape[1]
    rows = N_MEM * MEM_HEADS
    q_head = lax.broadcasted_iota(jnp.int32, (nq, rows), 0) % MEM_HEADS
    kv_head = lax.broadcasted_iota(jnp.int32, (nq, rows), 1) % MEM_HEADS
    same_head = q_head == kv_head
    q_all = q_ref[...].reshape(bb * nq, MEM_HD)
    q_all = jnp.concatenate([q_all, jnp.zeros((LANES - bb * nq, MEM_HD), F32)], axis=0).astype(BF16)
    s_t = [_dot_nt(k_ref[i].reshape(rows, MEM_HD).astype(BF16), q_all) for i in range(bb)]
    ps = []
    for i in range(bb):
        s = s_t[i].T[i * nq:(i + 1) * nq] * (MEM_HD ** -0.5)
        s = jnp.where(same_head, s, -jnp.inf)
        e = jnp.exp(s - jnp.max(s, axis=-1, keepdims=True))
        ps.append((e / jnp.sum(e, axis=-1, keepdims=True)).astype(BF16))
    for i in range(bb):
        o_ref[i] = _dot(ps[i], v_ref[i].reshape(rows, MEM_HD).astype(BF16))


def _attn_sample(q, k, v, *, bb):
    nb, nq, hd = q.shape
    qs = pl.BlockSpec((bb, nq, hd), lambda i: (i, 0, 0))
    mem = pl.BlockSpec((bb, N_MEM, MEM_HEADS, hd), lambda i: (i, 0, 0, 0))
    return pl.pallas_call(
        functools.partial(_attn_sample_kernel, bb=bb),
        grid=(nb // bb,),
        in_specs=[qs, mem, mem],
        out_specs=qs,
        out_shape=jax.ShapeDtypeStruct((nb, nq, hd), F32),
        compiler_params=_cparams(("parallel",)),
        name="attn_sample",
    )(q, k, v)


def _mix_sample_pre_kernel(x_ref, sp_ref, sc_ref, g_ref, win_ref,
                           cw_ref, cb_ref, dtb_ref, alog_ref, dsk_ref, wp_ref, ps_ref, ex_ref,
                           pool_ref, z_ref, ydx_ref, expa_ref, xwt_ref, b_ref, c_ref, dec_ref,
                           npool_ref, nconv_ref, *, nb, t, start_pos):
    def rows(i):
        return slice(i * nb, (i + 1) * nb)

    x = x_ref[...]
    xn = _rms(x, g_ref[...]).astype(BF16)
    z_ref[...] = _dot(xn, win_ref[:, D_POOL:D_POOL + D_SSM])

    u = _dot(xn, win_ref[:, 0:D_POOL])
    ext = [sp_ref[rows(i), :] for i in range(POOL_BUF)] + [u[rows(i)] for i in range(t)]
    ps = []
    for i in range(t):
        sums, cnts = [], []
        for w in POOL_WINDOWS:
            s = ext[POOL_BUF + i]
            for j in range(1, w):
                s = s + ext[POOL_BUF + i - j]
            sums.append(s)
            cnts.append(jnp.full((nb, D_POOL), float(min(start_pos + i + 1, w)), F32))
        mean = _lane_group_select(sums, POOL_GW) / _lane_group_select(cnts, POOL_GW)
        ps.append(mean - ext[POOL_BUF + i])
    p = jnp.concatenate(ps, axis=0).astype(BF16)
    pool_ref[...] = _dot(p, wp_ref[...]) * ps_ref[...]
    for i in range(POOL_BUF):
        npool_ref[rows(i), :] = ext[t + i]

    xr = _dot(xn, win_ref[:, D_POOL + D_SSM:D_PROJ])
    cext = [sc_ref[rows(i), :] for i in range(CONV_W - 1)] + [xr[rows(i)] for i in range(t)]
    xbc = []
    for i in range(t):
        acc = cext[i] * cw_ref[0:1, :]
        for k in range(1, CONV_W):
            acc = acc + cext[i + k] * cw_ref[k:k + 1, :]
        xbc.append(_silu(acc + cb_ref[...]))
    for i in range(CONV_W - 1):
        nconv_ref[rows(i), :] = cext[t + i]
    xs = [v[:, :D_SSM] for v in xbc]
    bm = [v[:, D_SSM:D_SSM + SSM_GROUPS * SSM_STATE] for v in xbc]
    cm = [v[:, D_SSM + SSM_GROUPS * SSM_STATE:] for v in xbc]
    for i in range(t):
        b_ref[rows(i), :] = bm[i]
        c_ref[rows(i), :] = cm[i]

    dt_all = jax.nn.softplus(_dot(xn, win_ref[:, D_PROJ:D_PROJ + LANES]) + dtb_ref[...])
    a_neg = -jnp.exp(alog_ref[...])
    dt = [dt_all[rows(i)] for i in range(t)]
    acs = []
    for i in range(t):
        da = dt[i] * a_neg
        acs.append(da if i == 0 else acs[-1] + da)
    dec_ref[...] = jnp.exp(acs[-1])

    def expand(v, exact):
        if exact:
            return jnp.dot(v, ex_ref[...], precision=lax.Precision.HIGHEST,
                           preferred_element_type=F32)
        return _dot(v.astype(BF16), ex_ref[...].astype(BF16))

    lane = lax.broadcasted_iota(jnp.int32, (nb, LANES), 1)
    xw = []
    for i in range(t):
        ydx = xs[i] * dsk_ref[...]
        for s in range(i + 1):
            sc = [jnp.sum(cm[i][:, g * SSM_STATE:(g + 1) * SSM_STATE]
                          * bm[s][:, g * SSM_STATE:(g + 1) * SSM_STATE], axis=-1, keepdims=True)
                  for g in range(SSM_GROUPS)]
            sc = jnp.where(lane < SSM_HPG, sc[0], sc[1])
            coef = sc * jnp.exp(acs[i] - acs[s]) * dt[s]
            ydx = ydx + expand(coef, False) * xs[s]
        ydx_ref[rows(i), :] = ydx
        expa_ref[rows(i), :] = expand(jnp.exp(acs[i]), True)
        xw.append(xs[i] * expand(jnp.exp(acs[-1] - acs[i]) * dt[i], True))
    xwt_ref[...] = jnp.concatenate(xw, axis=0).T.astype(BF16)


def _mix_sample_pre(x, sp, sc, w_in_bf, w, *, nb, t, start_pos):
    n = nb * t
    consts = [w["g_mix"], w_in_bf, w["conv_w"], w["conv_b"],
              w["dt_bias"], w["a_log"], w["d_skip"], w["w_pool"], w["pool_scale"], w["expand"]]
    ins = [x, sp, sc] + consts
    outs = [
        jax.ShapeDtypeStruct((n, D_POOL), F32),
        jax.ShapeDtypeStruct((n, D_SSM), F32),
        jax.ShapeDtypeStruct((n, D_SSM), F32),
        jax.ShapeDtypeStruct((n, D_SSM), F32),
        jax.ShapeDtypeStruct((D_SSM, n), BF16),
        jax.ShapeDtypeStruct((n, SSM_GROUPS * SSM_STATE), F32),
        jax.ShapeDtypeStruct((n, SSM_GROUPS * SSM_STATE), F32),
        jax.ShapeDtypeStruct((nb, LANES), F32),
        jax.ShapeDtypeStruct((POOL_BUF * nb, D_POOL), F32),
        jax.ShapeDtypeStruct(((CONV_W - 1) * nb, CONV_DIM), F32),
    ]
    return pl.pallas_call(
        functools.partial(_mix_sample_pre_kernel, nb=nb, t=t, start_pos=start_pos),
        grid=(1,),
        in_specs=[_const_spec(a.shape) for a in ins],
        out_specs=[pl.BlockSpec(o.shape, lambda i: (0, 0)) for o in outs],
        out_shape=outs,
        compiler_params=_cparams(("arbitrary",)),
        name="mix_sample_pre",
    )(*ins)


def _ssd_sample_kernel(dec_ref, h0_ref, c_ref, ydx_ref, expa_ref, xwt_ref, b_ref,
                       y_ref, hout_ref, *, nb, t, bb):
    j = pl.program_id(0)
    gw = SSM_HPG * SSM_HEADDIM
    c_blk = c_ref[...].reshape(t * bb, SSM_GROUPS * SSM_STATE)
    row_seq = lax.broadcasted_iota(jnp.int32, (t * bb, SSM_STATE), 0) % bb
    all_seq = lax.broadcasted_iota(jnp.int32, (t * nb, SSM_STATE), 0) % nb
    b_all = b_ref[...]
    yoff = [jnp.zeros((t * bb, gw), F32) for _ in range(SSM_GROUPS)]
    for i in range(bb):
        seq = j * bb + i
        for g in range(SSM_GROUPS):
            gs = slice(g * SSM_STATE, (g + 1) * SSM_STATE)
            h0 = h0_ref[i, g * SSM_HPG:(g + 1) * SSM_HPG].reshape(gw, SSM_STATE)
            c_i = jnp.where(row_seq == i, c_blk[:, gs], 0.0).astype(BF16)
            yoff[g] = yoff[g] + _dot_nt(c_i, h0.astype(BF16))
            b_i = jnp.where(all_seq == seq, b_all[:, gs], 0.0).astype(BF16)
            st = _dot(xwt_ref[g * gw:(g + 1) * gw, :], b_i)
            for k in range(SSM_HPG):
                hd = g * SSM_HPG + k
                hout_ref[i, hd] = (h0_ref[i, hd] * dec_ref[seq * SSM_HEADS + hd]
                                   + st[k * SSM_HEADDIM:(k + 1) * SSM_HEADDIM])
    yoff = jnp.concatenate(yoff, axis=1).reshape(t, bb, D_SSM)
    y_ref[...] = ydx_ref[...] + yoff * expa_ref[...]


def _ssd_sample(dec, h0, c, ydx, expa, xwt, bmat, *, nb, t, bb):
    blk = lambda d: pl.BlockSpec((t, bb, d), lambda i: (0, i, 0))
    hblk = pl.BlockSpec((bb, SSM_HEADS, SSM_HEADDIM, SSM_STATE), lambda i: (i, 0, 0, 0))
    return pl.pallas_call(
        functools.partial(_ssd_sample_kernel, nb=nb, t=t, bb=bb),
        grid=(nb // bb,),
        in_specs=[pl.BlockSpec(memory_space=pltpu.SMEM), hblk, blk(SSM_GROUPS * SSM_STATE),
                  blk(D_SSM), blk(D_SSM), _const_spec(xwt.shape), _const_spec(bmat.shape)],
        out_specs=[blk(D_SSM), hblk],
        out_shape=[jax.ShapeDtypeStruct((t, nb, D_SSM), F32),
                   jax.ShapeDtypeStruct(h0.shape, F32)],
        compiler_params=_cparams(("parallel",)),
        name="ssd_sample",
    )(dec, h0, c, ydx, expa, xwt, bmat)


def _mix_sample_post_kernel(y_ref, z_ref, pool_ref, x_ref, sn_ref, wo_ref, gq_ref, wq_ref,
                            x2_ref, q_ref):
    x2 = _gate_norm_out(y_ref[...], z_ref[...], pool_ref[...], x_ref[...], sn_ref[...],
                        wo_ref[...].astype(BF16))
    x2_ref[...] = x2
    q_ref[...] = _dot(_rms(x2, gq_ref[...]).astype(BF16), wq_ref[...].astype(BF16))


def _mix_sample_post(y, z, pool, x, sn, wo, gq, wq):
    ins = [y, z, pool, x, sn, wo, gq, wq]
    outs = [jax.ShapeDtypeStruct(x.shape, F32)] * 2
    return pl.pallas_call(
        _mix_sample_post_kernel,
        grid=(1,),
        in_specs=[_const_spec(a.shape) for a in ins],
        out_specs=[pl.BlockSpec(x.shape, lambda i: (0, 0))] * 2,
        out_shape=outs,
        compiler_params=_cparams(("arbitrary",)),
        name="mix_sample_post",
    )(*ins)


def _oproj_kernel(x_ref, o_ref, wo_ref, out_ref):
    out_ref[...] = x_ref[...] + _dot(o_ref[...].astype(BF16), wo_ref[...].astype(BF16))


def _oproj(x, o, wo):
    ins = [x, o, wo]
    return pl.pallas_call(
        _oproj_kernel,
        grid=(1,),
        in_specs=[_const_spec(a.shape) for a in ins],
        out_specs=pl.BlockSpec(x.shape, lambda i: (0, 0)),
        out_shape=jax.ShapeDtypeStruct(x.shape, F32),
        compiler_params=_cparams(("arbitrary",)),
        name="oproj",
    )(*ins)


def _prep_weights(g_ffn1, w1_gate, w1_up, w1_down, g_mix, w_in, conv_w, conv_b, dt_bias, a_log,
                  d_skip, ssm_norm, w_pool, pool_scale, w_out, g_mem, w_mem_k, w_mem_v, g_xq,
                  w_xq, w_xo, g_ffn2, w2_gate, w2_up, w2_down, g_final):
    row = lambda v: v.reshape(1, -1).astype(F32)
    pad_heads = lambda v: jnp.pad(v.astype(F32), (0, LANES - SSM_HEADS)).reshape(1, LANES)
    w_pool_bd = jnp.zeros((D_POOL, D_POOL), F32)
    for g in range(len(POOL_WINDOWS)):
        w_pool_bd = w_pool_bd.at[g * POOL_GW:(g + 1) * POOL_GW, g * POOL_GW:(g + 1) * POOL_GW].set(w_pool[g])
    head_of_chan = jnp.arange(D_SSM) // SSM_HEADDIM
    expand = (jnp.arange(LANES)[:, None] == head_of_chan[None, :]).astype(F32)
    return dict(
        g_ffn1=row(g_ffn1), w1_gate=w1_gate, w1_up=w1_up, w1_down=w1_down,
        g_mix=row(g_mix), w_in=jnp.pad(jnp.transpose(w_in), ((0, LANES - SSM_HEADS), (0, 0))),
        conv_w=conv_w.astype(F32), conv_b=row(conv_b), dt_bias=pad_heads(dt_bias),
        a_log=pad_heads(a_log), d_skip=row(jnp.repeat(d_skip, SSM_HEADDIM)), ssm_norm=row(ssm_norm),
        w_pool=w_pool_bd.astype(BF16), pool_scale=row(pool_scale), w_out=w_out,
        g_mem=row(g_mem), w_mem_k=w_mem_k, w_mem_v=w_mem_v,
        g_xq=row(g_xq), w_xq=w_xq, w_xo=w_xo,
        g_ffn2=row(g_ffn2), w2_gate=w2_gate, w2_up=w2_up, w2_down=w2_down,
        g_final=row(g_final), expand=expand,
    )


def _layer(x_prompt, x_sample, mem_prompt, mem_k, mem_v, state_pool, state_conv, state_ssm, start_pos, w):
    b, s, d = x_prompt.shape
    nb, t, _ = x_sample.shape
    n = nb * t
    tmaj = lambda a: jnp.swapaxes(a, 0, 1).reshape(-1, a.shape[-1])
    bmaj = lambda a, r: jnp.swapaxes(a.reshape(r, nb, -1), 0, 1)
    r3 = lambda a: a.reshape(t, nb, a.shape[-1])

    k, v, k_b, v_b = _memkv(mem_prompt, w["g_mem"], w["w_mem_k"], w["w_mem_v"])
    xp, xs = _ffn(x_prompt.reshape(b * s, d), tmaj(x_sample), w["g_ffn1"], w["w1_gate"], w["w1_up"],
                  w["w1_down"], w["g_final"], final=False, tm=512)

    xp, pool_p, conv_p, ssm_p, w_in_bf = _mix_prompt(xp.reshape(b, s, d), w, tc=256, nseq=2)
    xp = _attn_prompt(xp, k_b, v_b, w["g_xq"], w["w_xq"], w["w_xo"], tq=512)

    (pool_out, z, ydx, expa, xwt, bmat, cmat, dec, new_pool, new_conv) = _mix_sample_pre(
        xs, tmaj(state_pool), tmaj(state_conv), w_in_bf, w, nb=nb, t=t, start_pos=start_pos)
    y, ssm_s = _ssd_sample(dec[:, :SSM_HEADS].reshape(-1), state_ssm, r3(cmat), r3(ydx), r3(expa),
                           xwt, bmat, nb=nb, t=t, bb=8)
    x2, q = _mix_sample_post(y.reshape(n, D_SSM), z, pool_out, xs, w["ssm_norm"], w["w_out"],
                             w["g_xq"], w["w_xq"])
    o = _attn_sample(bmaj(q, t).reshape(nb, t * MEM_HEADS, MEM_HD), mem_k, mem_v, bb=4)
    xs = _oproj(bmaj(x2, t).reshape(n, d), o.reshape(n, d), w["w_xo"])

    yp, ys = _ffn(xp.reshape(b * s, d), xs, w["g_ffn2"], w["w2_gate"], w["w2_up"], w["w2_down"],
                  w["g_final"], final=True, tm=512)
    return (yp.reshape(b, s, d), ys.reshape(nb, t, d), k, v, pool_p, conv_p, ssm_p,
            bmaj(new_pool, POOL_BUF), bmaj(new_conv, CONV_W - 1), ssm_s)


def kernel(x_prompt, x_sample, mem_prompt, cache_mem_k, cache_mem_v, state_pool, state_conv, state_ssm,
           g_ffn1, w1_gate, w1_up, w1_down, g_mix, w_in, conv_w, conv_b, dt_bias, a_log, d_skip,
           ssm_norm, w_pool, pool_scale, w_out, g_mem, w_mem_k, w_mem_v, g_xq, w_xq, w_xo,
           g_ffn2, w2_gate, w2_up, w2_down, g_final):
    assert g_ffn1.shape[0] == 1, "single-layer model"
    past_len = 16384
    w = _prep_weights(g_ffn1[0], w1_gate[0], w1_up[0], w1_down[0], g_mix[0], w_in[0], conv_w[0],
                      conv_b[0], dt_bias[0], a_log[0], d_skip[0], ssm_norm[0], w_pool[0],
                      pool_scale[0], w_out[0], g_mem[0], w_mem_k[0], w_mem_v[0], g_xq[0], w_xq[0],
                      w_xo[0], g_ffn2[0], w2_gate[0], w2_up[0], w2_down[0], g_final)
    bp = x_prompt.shape[0]
    y_p, y_s, k_p, v_p, pool_p, conv_p, ssm_p, pool_s, conv_s, ssm_s = _layer(
        x_prompt, x_sample, mem_prompt, cache_mem_k[0], cache_mem_v[0], state_pool[0], state_conv[0],
        state_ssm[0], past_len, w)
    kv_shape = (1, bp, N_MEM, MEM_HEADS, MEM_HD)
    return (y_p, y_s, k_p.reshape(kv_shape), v_p.reshape(kv_shape), pool_p[None], conv_p[None],
            ssm_p[None], pool_s[None], conv_s[None], ssm_s[None])
```

```python
import functools

import jax
import jax.numpy as jnp
from jax import lax
from jax.experimental import pallas as pl
from jax.experimental.pallas import tpu as pltpu

F32 = jnp.float32
BF16 = jnp.bfloat16

D_MODEL = 1024
D_POOL = 256
POOL_WINDOWS = (2, 4, 8, 16)
POOL_GW = 64
POOL_BUF = 15
D_SSM = 768
SSM_HEADDIM = 64
SSM_HEADS = 12
SSM_GROUPS = 2
SSM_HPG = 6
SSM_STATE = 128
CONV_W = 4
CONV_DIM = 1280
D_PROJ = D_POOL + D_SSM + CONV_DIM
N_MEM = 256
MEM_HEADS = 4
MEM_HD = 256
D_FF = 2816
EPS = 1e-6
LANES = 128
CHUNK = 128
POOL_HIST = 32
CONV_HIST = 8
SCAN_PAD = CHUNK // 2
VMEM_LIMIT = 56 * 1024 * 1024


def _cparams(sem):
    return pltpu.CompilerParams(dimension_semantics=sem, vmem_limit_bytes=VMEM_LIMIT)


def _const_spec(shape):
    nd = len(shape)
    return pl.BlockSpec(shape, lambda *_: (0,) * nd, pipeline_mode=pl.Buffered(1))


def _rms(x, g):
    ms = jnp.mean(x * x, axis=-1, keepdims=True)
    return x * lax.rsqrt(ms + EPS) * g


def _silu(x):
    return x * jax.nn.sigmoid(x)


def _dot(a, b):
    return jnp.dot(a, b, preferred_element_type=F32)


def _dot_nt(a, b):
    return lax.dot_general(a, b, (((1,), (1,)), ((), ())), preferred_element_type=F32)


def _dot_tn(a, b):
    return lax.dot_general(a, b, (((0,), (0,)), ((), ())), preferred_element_type=F32)


FFN_W_STEPS = 8


def _ffn_kernel(xp_ref, xs_ref, g_ref, wg_ref, wu_ref, wd_ref, gf_ref, op_ref, os_ref,
                wg_s, wu_s, wd_s, *, final, n_p):
    i = pl.program_id(0)
    rows_gu = wg_ref.shape[0]
    rows_d = wd_ref.shape[0]

    @pl.when(i < FFN_W_STEPS)
    def _():
        r = pl.multiple_of(i * rows_gu, rows_gu)
        wg_s[pl.ds(r, rows_gu), :] = wg_ref[...].astype(BF16)
        wu_s[pl.ds(r, rows_gu), :] = wu_ref[...].astype(BF16)
        r = pl.multiple_of(i * rows_d, rows_d)
        wd_s[pl.ds(r, rows_d), :] = wd_ref[...].astype(BF16)

    def tile(x_ref, o_ref):
        x = x_ref[...]
        xn = _rms(x, g_ref[...]).astype(BF16)
        gate = _dot(xn, wg_s[...])
        up = _dot(xn, wu_s[...])
        h = (_silu(gate) * up).astype(BF16)
        out = x + 0.5 * _dot(h, wd_s[...])
        if final:
            out = _rms(out, gf_ref[...])
        o_ref[...] = out

    @pl.when(jnp.logical_and(i >= FFN_W_STEPS, i < FFN_W_STEPS + n_p))
    def _():
        tile(xp_ref, op_ref)

    @pl.when(i == FFN_W_STEPS + n_p)
    def _():
        tile(xs_ref, os_ref)


def _ffn(xp, xs, g, wg, wu, wd, gf, *, final, tm):
    tp, d = xp.shape
    dff = wg.shape[1]
    assert tp % tm == 0 and d % FFN_W_STEPS == 0 and dff % FFN_W_STEPS == 0
    n_p = tp // tm
    p_tile = pl.BlockSpec((tm, d), lambda i: (jnp.clip(i - FFN_W_STEPS, 0, n_p - 1), 0))
    s_tile = pl.BlockSpec(xs.shape, lambda i: (0, 0))
    w_chunk = lambda rows, cols: pl.BlockSpec((rows, cols), lambda i: (jnp.minimum(i, FFN_W_STEPS - 1), 0))
    return pl.pallas_call(
        functools.partial(_ffn_kernel, final=final, n_p=n_p),
        grid=(FFN_W_STEPS + n_p + 1,),
        in_specs=[
            p_tile, _const_spec(xs.shape), _const_spec(g.shape),
            w_chunk(d // FFN_W_STEPS, dff), w_chunk(d // FFN_W_STEPS, dff), w_chunk(dff // FFN_W_STEPS, d),
            _const_spec(gf.shape),
        ],
        out_specs=[p_tile, s_tile],
        out_shape=[jax.ShapeDtypeStruct(xp.shape, F32), jax.ShapeDtypeStruct(xs.shape, F32)],
        scratch_shapes=[pltpu.VMEM((d, dff), BF16), pltpu.VMEM((d, dff), BF16), pltpu.VMEM((dff, d), BF16)],
        compiler_params=_cparams(("arbitrary",)),
        name="ffn_final" if final else "ffn",
    )(xp, xs, g, wg, wu, wd, gf)


def _memkv_kernel(m_ref, g_ref, wk_ref, wv_ref, k_ref, v_ref, kb_ref, vb_ref, wk_s, wv_s):
    @pl.when(pl.program_id(0) == 0)
    def _():
        wk_s[...] = wk_ref[...].astype(BF16)
        wv_s[...] = wv_ref[...].astype(BF16)

    mn = _rms(m_ref[...], g_ref[...]).astype(BF16)
    k = _dot(mn, wk_s[...])
    v = _dot(mn, wv_s[...])
    for h in range(MEM_HEADS):
        k_ref[:, h, :] = k[:, h * MEM_HD:(h + 1) * MEM_HD]
        v_ref[:, h, :] = v[:, h * MEM_HD:(h + 1) * MEM_HD]
    kb_ref[...] = k.astype(BF16)
    vb_ref[...] = v.astype(BF16)


def _memkv(mem, g, wk, wv):
    b, n_mem, d = mem.shape
    row_blk = pl.BlockSpec((None, n_mem, d), lambda i: (i, 0, 0))
    head_blk = pl.BlockSpec((None, n_mem, MEM_HEADS, MEM_HD), lambda i: (i, 0, 0, 0))
    return pl.pallas_call(
        _memkv_kernel,
        grid=(b,),
        in_specs=[row_blk, _const_spec(g.shape), _const_spec(wk.shape), _const_spec(wv.shape)],
        out_specs=[head_blk, head_blk, row_blk, row_blk],
        out_shape=[jax.ShapeDtypeStruct((b, n_mem, MEM_HEADS, MEM_HD), F32)] * 2
        + [jax.ShapeDtypeStruct((b, n_mem, d), BF16)] * 2,
        scratch_shapes=[pltpu.VMEM(wk.shape, BF16), pltpu.VMEM(wv.shape, BF16)],
        compiler_params=_cparams(("arbitrary",)),
        name="memkv",
    )(mem, g, wk, wv)


def _lane_group_select(vals, width):
    lane = lax.broadcasted_iota(jnp.int32, vals[0].shape, 1)
    out = vals[-1]
    for g in range(len(vals) - 2, -1, -1):
        out = jnp.where(lane < (g + 1) * width, vals[g], out)
    return out


def _col_bcast(m, k, width):
    return jnp.broadcast_to(m[:, k:k + 1], (m.shape[0], width))


def _gate_norm_out(y, z, pool_out, x_res, ssm_norm, w_out):
    yn = _rms(y * _silu(z), ssm_norm)
    cat = jnp.concatenate([pool_out, yn], axis=-1).astype(BF16)
    return x_res + _dot(cat, w_out)


def _mix_prompt_kernel(x_ref, g_ref, win_ref, cw_ref, cb_ref,
                       dtb_ref, alog_ref, dsk_ref, sn_ref, wp_ref, ps_ref, wout_ref,
                       o_ref, opool_ref, oconv_ref, ossm_ref, owin_ref,
                       wu_s, wz_s, wx_s, wdt_s, wo_s,
                       ubuf, s2buf, s4buf, s8buf, cbuf, abuf, ht_ref, ybuf, *, tc, nseq):
    c = pl.program_id(1)
    nc = pl.num_programs(1)
    n_pool_slabs = D_POOL // LANES
    n_conv_slabs = CONV_DIM // LANES
    n_chunks = tc // CHUNK
    rp = POOL_HIST + tc
    seqs = range(nseq)

    def slab(s):
        return slice(s * LANES, (s + 1) * LANES)

    @pl.when(jnp.logical_and(pl.program_id(0) == 0, c == 0))
    def _():
        wu_s[...] = win_ref[0:D_POOL, :].T.astype(BF16)
        wz_s[...] = win_ref[D_POOL:D_POOL + D_SSM, :].T.astype(BF16)
        wx_s[...] = win_ref[D_POOL + D_SSM:D_PROJ, :].T.astype(BF16)
        wdt_s[...] = win_ref[D_PROJ:D_PROJ + LANES, :].T.astype(BF16)
        wo_s[...] = wout_ref[...].astype(BF16)
        owin_ref[:, 0:D_POOL] = wu_s[...]
        owin_ref[:, D_POOL:D_POOL + D_SSM] = wz_s[...]
        owin_ref[:, D_POOL + D_SSM:D_PROJ] = wx_s[...]
        owin_ref[:, D_PROJ:D_PROJ + LANES] = wdt_s[...]

    @pl.when(c == 0)
    def _():
        ubuf[:, :, 0:POOL_HIST, :] = jnp.zeros((nseq, n_pool_slabs, POOL_HIST, LANES), F32)
        cbuf[:, :, 0:CONV_HIST, :] = jnp.zeros((nseq, n_conv_slabs, CONV_HIST, LANES), F32)
        abuf[:, :, 0:SCAN_PAD, :] = jnp.zeros((nseq, n_chunks, SCAN_PAD, LANES), F32)
        ht_ref[...] = jnp.zeros_like(ht_ref)

    lo_half = lax.broadcasted_iota(jnp.int32, (tc, LANES), 1) < POOL_GW
    pos1 = c * tc + lax.broadcasted_iota(jnp.int32, (tc, LANES), 0) + 1
    a_neg = -jnp.exp(alog_ref[...])
    row = lax.broadcasted_iota(jnp.int32, (CHUNK, CHUNK), 0)
    col = lax.broadcasted_iota(jnp.int32, (CHUNK, CHUNK), 1)
    causal = row >= col
    lo = lax.broadcasted_iota(jnp.int32, (CHUNK, LANES), 1) < SSM_HEADDIM
    mask_lo = jnp.where(lo, 1.0, 0.0).astype(BF16)
    mask_hi = jnp.where(lo, 0.0, 1.0).astype(BF16)
    gw = SSM_HPG * SSM_HEADDIM
    n_x = D_SSM // LANES
    st8 = {}

    def head(i):
        x = x_ref[i]
        xn = _rms(x, g_ref[...]).astype(BF16)
        st8[i] = dict(x=x, xn=xn, xbc=[None] * n_conv_slabs)

    def conv_block(i, s0, ns):
        xn = st8[i]["xn"]
        raw = _dot(xn, wx_s[:, s0 * LANES:(s0 + ns) * LANES])
        for t in range(ns):
            s = s0 + t
            cbuf[i, s, CONV_HIST:CONV_HIST + tc, :] = raw[:, slab(t)]
            acc = cbuf[i, s, CONV_HIST:CONV_HIST + tc, :] * cw_ref[CONV_W - 1:CONV_W, slab(s)]
            for j in range(1, CONV_W):
                acc = acc + (cbuf[i, s, CONV_HIST - j:CONV_HIST - j + tc, :]
                             * cw_ref[CONV_W - 1 - j:CONV_W - j, slab(s)])
            st8[i]["xbc"][s] = _silu(acc + cb_ref[:, slab(s)])

    def dt_proj(i):
        st8[i]["dt"] = jax.nn.softplus(_dot(st8[i]["xn"], wdt_s[...]) + dtb_ref[...])

    def pool(i):
        u = _dot(st8[i]["xn"], wu_s[...])
        for s in range(n_pool_slabs):
            ubuf[i, s, POOL_HIST:rp, :] = u[:, slab(s)]
            s2buf[i, s, 8:rp, :] = ubuf[i, s, 8:rp, :] + ubuf[i, s, 7:rp - 1, :]
        s4buf[i, 16:rp, :] = s2buf[i, 1, 16:rp, :] + s2buf[i, 1, 14:rp - 2, :]
        s8buf[i, 24:rp, :] = s4buf[i, 24:rp, :] + s4buf[i, 20:rp - 4, :]
        win_sums = [
            jnp.where(lo_half, s2buf[i, 0, POOL_HIST:rp, :],
                      s2buf[i, 0, POOL_HIST:rp, :] + s2buf[i, 0, POOL_HIST - 2:rp - 2, :]),
            jnp.where(lo_half, s8buf[i, POOL_HIST:rp, :],
                      s8buf[i, POOL_HIST:rp, :] + s8buf[i, POOL_HIST - 8:rp - 8, :]),
        ]
        ps = []
        for s in range(n_pool_slabs):
            win = jnp.where(lo_half, POOL_WINDOWS[2 * s], POOL_WINDOWS[2 * s + 1])
            cnt = jnp.minimum(pos1, win).astype(F32)
            ps.append(win_sums[s] / cnt - u[:, slab(s)])
        st8[i]["pool_out"] = _dot(jnp.concatenate(ps, axis=1).astype(BF16), wp_ref[...]) * ps_ref[...]

    def ssd_chunk(i, j):
        xbc = st8[i]["xbc"]
        sl = slice(j * CHUNK, (j + 1) * CHUNK)
        dt_c = st8[i]["dt"][sl]
        acs = dt_c * a_neg
        sh = 1
        while sh < CHUNK:
            abuf[i, j, SCAN_PAD:SCAN_PAD + CHUNK, :] = acs
            acs = acs + abuf[i, j, SCAN_PAD - sh:SCAN_PAD - sh + CHUNK, :]
            sh *= 2
        acs_last = acs[CHUNK - 1:CHUNK, :]
        fdec = jnp.exp(acs_last - acs) * dt_c
        src_t = (acs - jnp.log(dt_c)).T
        ht = ht_ref[i]
        ht_b = ht.astype(BF16)
        cdec, st = [], []
        for g in range(SSM_GROUPS):
            c_g = xbc[n_x + SSM_GROUPS + g][sl].astype(BF16)
            b_g = xbc[n_x + g][sl].astype(BF16)
            scores = _dot_nt(c_g, b_g)
            y_off = _dot(c_g, ht_b[:, g * gw:(g + 1) * gw])
            xw = []
            for q in range(SSM_HPG // 2):
                k0 = g * SSM_HPG + 2 * q
                pair = k0 // 2
                a_cols = [_col_bcast(acs, k, CHUNK) for k in (k0, k0 + 1)]
                lhs = [(scores * jnp.exp(jnp.where(causal, a_cols[h] - src_t[k0 + h:k0 + h + 1, :], -jnp.inf))
                        ).astype(BF16) for h in range(2)]
                x_pair = xbc[pair][sl]
                x_b = x_pair.astype(BF16)
                rhs = jnp.concatenate([x_b * mask_lo, x_b * mask_hi], axis=0)
                e_pair = jnp.exp(jnp.where(lo, a_cols[0], a_cols[1]))
                ybuf[i, sl, slab(pair)] = (_dot(jnp.concatenate(lhs, axis=1), rhs)
                                           + y_off[:, slab(q)] * e_pair + x_pair * dsk_ref[:, slab(pair)])
                f_pair = jnp.where(lo, _col_bcast(fdec, k0, LANES), _col_bcast(fdec, k0 + 1, LANES))
                xw.append((x_pair * f_pair).astype(BF16))
                cdec.append(e_pair[CHUNK - 1:CHUNK, :])
            st.append(_dot_tn(b_g, jnp.concatenate(xw, axis=1)))
        ht_ref[i] = ht * jnp.concatenate(cdec, axis=1) + jnp.concatenate(st, axis=1)

    def z_proj(i):
        st8[i]["zs"] = _silu(_dot(st8[i]["xn"], wz_s[...]))

    def tail(i):
        d = st8[i]
        yn = _rms(ybuf[i] * d["zs"], sn_ref[...])
        cat = jnp.concatenate([d["pool_out"], yn], axis=-1).astype(BF16)
        o_ref[i] = d["x"] + _dot(cat, wo_s[...])

    def mid(i):
        conv_block(i, n_x, 2 * SSM_GROUPS)
        dt_proj(i)
        for s0 in range(0, n_x, 2):
            conv_block(i, s0, 2)

    for i in seqs:
        head(i)
    for i in seqs:
        mid(i)
    for i in seqs:
        pool(i)
    for j in range(n_chunks):
        for i in seqs:
            ssd_chunk(i, j)
        if j == 0:
            for i in seqs:
                z_proj(i)
    for i in seqs:
        tail(i)

    ubuf[:, :, 0:POOL_HIST, :] = ubuf[:, :, tc:tc + POOL_HIST, :]
    cbuf[:, :, 0:CONV_HIST, :] = cbuf[:, :, tc:tc + CONV_HIST, :]

    @pl.when(c == nc - 1)
    def _():
        for i in seqs:
            for s in range(n_pool_slabs):
                opool_ref[i, :, slab(s)] = ubuf[i, s, POOL_HIST - POOL_BUF:POOL_HIST, :]
            for s in range(n_conv_slabs):
                oconv_ref[i, :, slab(s)] = cbuf[i, s, CONV_HIST - (CONV_W - 1):CONV_HIST, :]
            ossm_ref[i] = ht_ref[i].T.reshape(SSM_HEADS, SSM_HEADDIM, SSM_STATE)


def _mix_prompt(x, w, *, tc, nseq):
    b, s, d = x.shape
    assert s % tc == 0 and tc % CHUNK == 0 and b % nseq == 0
    consts = [w["g_mix"], w["w_in"], w["conv_w"], w["conv_b"],
              w["dt_bias"], w["a_log"], w["d_skip"], w["ssm_norm"], w["w_pool"], w["pool_scale"],
              w["w_out"]]
    tile = pl.BlockSpec((nseq, tc, d), lambda i, j: (i, j, 0))
    return pl.pallas_call(
        functools.partial(_mix_prompt_kernel, tc=tc, nseq=nseq),
        grid=(b // nseq, s // tc),
        in_specs=[tile] + [_const_spec(a.shape) for a in consts],
        out_specs=[
            tile,
            pl.BlockSpec((nseq, POOL_BUF, D_POOL), lambda i, j: (i, 0, 0)),
            pl.BlockSpec((nseq, CONV_W - 1, CONV_DIM), lambda i, j: (i, 0, 0)),
            pl.BlockSpec((nseq, SSM_HEADS, SSM_HEADDIM, SSM_STATE), lambda i, j: (i, 0, 0, 0)),
            pl.BlockSpec((d, D_PROJ + LANES), lambda i, j: (0, 0)),
        ],
        out_shape=[
            jax.ShapeDtypeStruct((b, s, d), F32),
            jax.ShapeDtypeStruct((b, POOL_BUF, D_POOL), F32),
            jax.ShapeDtypeStruct((b, CONV_W - 1, CONV_DIM), F32),
            jax.ShapeDtypeStruct((b, SSM_HEADS, SSM_HEADDIM, SSM_STATE), F32),
            jax.ShapeDtypeStruct((d, D_PROJ + LANES), BF16),
        ],
        scratch_shapes=[
            pltpu.VMEM((d, D_POOL), BF16), pltpu.VMEM((d, D_SSM), BF16),
            pltpu.VMEM((d, CONV_DIM), BF16), pltpu.VMEM((d, LANES), BF16), pltpu.VMEM((d, d), BF16),
            pltpu.VMEM((nseq, D_POOL // LANES, POOL_HIST + tc, LANES), F32),
            pltpu.VMEM((nseq, D_POOL // LANES, POOL_HIST + tc, LANES), F32),
            pltpu.VMEM((nseq, POOL_HIST + tc, LANES), F32),
            pltpu.VMEM((nseq, POOL_HIST + tc, LANES), F32),
            pltpu.VMEM((nseq, CONV_DIM // LANES, CONV_HIST + tc, LANES), F32),
            pltpu.VMEM((nseq, tc // CHUNK, SCAN_PAD + CHUNK, LANES), F32),
            pltpu.VMEM((nseq, SSM_STATE, D_SSM), F32),
            pltpu.VMEM((nseq, tc, D_SSM), F32),
        ],
        compiler_params=_cparams(("arbitrary", "arbitrary")),
        name="mix_prompt",
    )(x, *consts)


def _attn_kernel(x_ref, g_ref, wq_ref, k_ref, v_ref, wo_ref, qs_ref, ks_ref, vs_ref,
                 o_ref, os_ref, wq_s, wo_s, *, bb):
    @pl.when(jnp.logical_and(pl.program_id(0) == 0, pl.program_id(1) == 0))
    def _():
        wq_s[...] = wq_ref[...].astype(BF16)
        wo_s[...] = wo_ref[...].astype(BF16)

    hs = [slice(h * MEM_HD, (h + 1) * MEM_HD) for h in range(MEM_HEADS)]
    scale = MEM_HD ** -0.5
    nq = qs_ref.shape[1]
    rows = N_MEM * MEM_HEADS

    def softmax(s):
        e = jnp.exp(s - jnp.max(s, axis=-1, keepdims=True))
        return (e / jnp.sum(e, axis=-1, keepdims=True)).astype(BF16)

    q_head = lax.broadcasted_iota(jnp.int32, (nq, rows), 0) % MEM_HEADS
    kv_head = lax.broadcasted_iota(jnp.int32, (nq, rows), 1) % MEM_HEADS
    same_head = q_head == kv_head

    q_all = qs_ref[...].reshape(bb * nq, MEM_HD)
    q_all = jnp.concatenate([q_all, jnp.zeros((LANES - bb * nq, MEM_HD), F32)], axis=0).astype(BF16)
    s_s = [_dot_nt(ks_ref[i].reshape(rows, MEM_HD).astype(BF16), q_all) for i in range(bb)]

    x = x_ref[...]
    q = _dot(_rms(x, g_ref[...]).astype(BF16), wq_s[...]).astype(BF16)

    p_s = [softmax(jnp.where(same_head, s_s[i].T[i * nq:(i + 1) * nq] * scale, -jnp.inf)) for i in range(bb)]

    s_p = [_dot_nt(q[:, hs[h]], k_ref[:, hs[h]]) * scale for h in range(MEM_HEADS)]

    for i in range(bb):
        os_ref[i] = _dot(p_s[i], vs_ref[i].reshape(rows, MEM_HD).astype(BF16))

    p_p = [softmax(s) for s in s_p]
    o = jnp.concatenate([_dot(p_p[h], v_ref[:, hs[h]]) for h in range(MEM_HEADS)], axis=-1).astype(BF16)
    o_ref[...] = x + _dot(o, wo_s[...])


def _attn(x, k, v, g, wq, wo, qs, ks, vs, *, tq):
    b, s, d = x.shape
    nb, nq, hd = qs.shape
    nj = s // tq
    assert nb % (b * nj) == 0
    bb = nb // (b * nj)
    assert bb * nq <= LANES
    tile = pl.BlockSpec((None, tq, d), lambda i, j: (i, j, 0))
    mem = pl.BlockSpec((None, N_MEM, d), lambda i, j: (i, 0, 0))
    s_q = pl.BlockSpec((bb, nq, hd), lambda i, j: (i * nj + j, 0, 0))
    s_mem = pl.BlockSpec((bb, N_MEM, MEM_HEADS, hd), lambda i, j: (i * nj + j, 0, 0, 0))
    return pl.pallas_call(
        functools.partial(_attn_kernel, bb=bb),
        grid=(b, nj),
        in_specs=[tile, _const_spec(g.shape), _const_spec(wq.shape), mem, mem, _const_spec(wo.shape),
                  s_q, s_mem, s_mem],
        out_specs=[tile, s_q],
        out_shape=[jax.ShapeDtypeStruct((b, s, d), F32), jax.ShapeDtypeStruct((nb, nq, hd), F32)],
        scratch_shapes=[pltpu.VMEM(wq.shape, BF16), pltpu.VMEM(wo.shape, BF16)],
        compiler_params=_cparams(("arbitrary", "arbitrary")),
        name="attn",
    )(x, g, wq, k, v, wo, qs, ks, vs)


def _mix_sample_pre_kernel(x_ref, sp_ref, sc_ref, g_ref, win_ref,
                           cw_ref, cb_ref, dtb_ref, alog_ref, dsk_ref, wp_ref, ps_ref, ex_ref,
                           pool_ref, z_ref, ydx_ref, expa_ref, xwt_ref, b_ref, c_ref, dec_ref,
                           npool_ref, nconv_ref, *, nb, t, start_pos):
    def rows(i):
        return slice(i * nb, (i + 1) * nb)

    x = x_ref[...]
    xn = _rms(x, g_ref[...]).astype(BF16)
    z_ref[...] = _dot(xn, win_ref[:, D_POOL:D_POOL + D_SSM])

    u = _dot(xn, win_ref[:, 0:D_POOL])
    ext = [sp_ref[rows(i), :] for i in range(POOL_BUF)] + [u[rows(i)] for i in range(t)]
    ps = []
    for i in range(t):
        sums, cnts = [], []
        for w in POOL_WINDOWS:
            s = ext[POOL_BUF + i]
            for j in range(1, w):
                s = s + ext[POOL_BUF + i - j]
            sums.append(s)
            cnts.append(jnp.full((nb, D_POOL), float(min(start_pos + i + 1, w)), F32))
        mean = _lane_group_select(sums, POOL_GW) / _lane_group_select(cnts, POOL_GW)
        ps.append(mean - ext[POOL_BUF + i])
    p = jnp.concatenate(ps, axis=0).astype(BF16)
    pool_ref[...] = _dot(p, wp_ref[...]) * ps_ref[...]
    for i in range(POOL_BUF):
        npool_ref[rows(i), :] = ext[t + i]

    xr = _dot(xn, win_ref[:, D_POOL + D_SSM:D_PROJ])
    cext = [sc_ref[rows(i), :] for i in range(CONV_W - 1)] + [xr[rows(i)] for i in range(t)]
    xbc = []
    for i in range(t):
        acc = cext[i] * cw_ref[0:1, :]
        for k in range(1, CONV_W):
            acc = acc + cext[i + k] * cw_ref[k:k + 1, :]
        xbc.append(_silu(acc + cb_ref[...]))
    for i in range(CONV_W - 1):
        nconv_ref[rows(i), :] = cext[t + i]
    xs = [v[:, :D_SSM] for v in xbc]
    bm = [v[:, D_SSM:D_SSM + SSM_GROUPS * SSM_STATE] for v in xbc]
    cm = [v[:, D_SSM + SSM_GROUPS * SSM_STATE:] for v in xbc]
    for i in range(t):
        b_ref[rows(i), :] = bm[i]
        c_ref[rows(i), :] = cm[i]

    dt_all = jax.nn.softplus(_dot(xn, win_ref[:, D_PROJ:D_PROJ + LANES]) + dtb_ref[...])
    a_neg = -jnp.exp(alog_ref[...])
    dt = [dt_all[rows(i)] for i in range(t)]
    acs = []
    for i in range(t):
        da = dt[i] * a_neg
        acs.append(da if i == 0 else acs[-1] + da)
    dec_ref[...] = jnp.exp(acs[-1])

    def expand(v, exact):
        if exact:
            return jnp.dot(v, ex_ref[...], precision=lax.Precision.HIGHEST,
                           preferred_element_type=F32)
        return _dot(v.astype(BF16), ex_ref[...].astype(BF16))

    lane = lax.broadcasted_iota(jnp.int32, (nb, LANES), 1)
    xw = []
    for i in range(t):
        ydx = xs[i] * dsk_ref[...]
        for s in range(i + 1):
            sc = [jnp.sum(cm[i][:, g * SSM_STATE:(g + 1) * SSM_STATE]
                          * bm[s][:, g * SSM_STATE:(g + 1) * SSM_STATE], axis=-1, keepdims=True)
                  for g in range(SSM_GROUPS)]
            sc = jnp.where(lane < SSM_HPG, sc[0], sc[1])
            coef = sc * jnp.exp(acs[i] - acs[s]) * dt[s]
            ydx = ydx + expand(coef, False) * xs[s]
        ydx_ref[rows(i), :] = ydx
        expa_ref[rows(i), :] = expand(jnp.exp(acs[i]), True)
        xw.append(xs[i] * expand(jnp.exp(acs[-1] - acs[i]) * dt[i], True))
    xwt_ref[...] = jnp.concatenate(xw, axis=0).T.astype(BF16)


def _mix_sample_pre(x, sp, sc, w_in_bf, w, *, nb, t, start_pos):
    n = nb * t
    consts = [w["g_mix"], w_in_bf, w["conv_w"], w["conv_b"],
              w["dt_bias"], w["a_log"], w["d_skip"], w["w_pool"], w["pool_scale"], w["expand"]]
    ins = [x, sp, sc] + consts
    outs = [
        jax.ShapeDtypeStruct((n, D_POOL), F32),
        jax.ShapeDtypeStruct((n, D_SSM), F32),
        jax.ShapeDtypeStruct((n, D_SSM), F32),
        jax.ShapeDtypeStruct((n, D_SSM), F32),
        jax.ShapeDtypeStruct((D_SSM, n), BF16),
        jax.ShapeDtypeStruct((n, SSM_GROUPS * SSM_STATE), F32),
        jax.ShapeDtypeStruct((n, SSM_GROUPS * SSM_STATE), F32),
        jax.ShapeDtypeStruct((nb, LANES), F32),
        jax.ShapeDtypeStruct((POOL_BUF * nb, D_POOL), F32),
        jax.ShapeDtypeStruct(((CONV_W - 1) * nb, CONV_DIM), F32),
    ]
    return pl.pallas_call(
        functools.partial(_mix_sample_pre_kernel, nb=nb, t=t, start_pos=start_pos),
        grid=(1,),
        in_specs=[_const_spec(a.shape) for a in ins],
        out_specs=[pl.BlockSpec(o.shape, lambda i: (0, 0)) for o in outs],
        out_shape=outs,
        compiler_params=_cparams(("arbitrary",)),
        name="mix_sample_pre",
    )(*ins)


def _ssd_sample_kernel(dec_ref, h0_ref, c_ref, ydx_ref, expa_ref, xwt_ref, b_ref,
                       y_ref, hout_ref, *, nb, t, bb):
    j = pl.program_id(0)
    gw = SSM_HPG * SSM_HEADDIM
    c_blk = c_ref[...].reshape(t * bb, SSM_GROUPS * SSM_STATE)
    row_seq = lax.broadcasted_iota(jnp.int32, (t * bb, SSM_STATE), 0) % bb
    all_seq = lax.broadcasted_iota(jnp.int32, (t * nb, SSM_STATE), 0) % nb
    b_all = b_ref[...]
    yoff = [jnp.zeros((t * bb, gw), F32) for _ in range(SSM_GROUPS)]
    for i in range(bb):
        seq = j * bb + i
        for g in range(SSM_GROUPS):
            gs = slice(g * SSM_STATE, (g + 1) * SSM_STATE)
            h0 = h0_ref[i, g * SSM_HPG:(g + 1) * SSM_HPG].reshape(gw, SSM_STATE)
            c_i = jnp.where(row_seq == i, c_blk[:, gs], 0.0).astype(BF16)
            yoff[g] = yoff[g] + _dot_nt(c_i, h0.astype(BF16))
            b_i = jnp.where(all_seq == seq, b_all[:, gs], 0.0).astype(BF16)
            st = _dot(xwt_ref[g * gw:(g + 1) * gw, :], b_i)
            for k in range(SSM_HPG):
                hd = g * SSM_HPG + k
                hout_ref[i, hd] = (h0_ref[i, hd] * dec_ref[seq * SSM_HEADS + hd]
                                   + st[k * SSM_HEADDIM:(k + 1) * SSM_HEADDIM])
    yoff = jnp.concatenate(yoff, axis=1).reshape(t, bb, D_SSM)
    y_ref[...] = ydx_ref[...] + yoff * expa_ref[...]


def _ssd_sample(dec, h0, c, ydx, expa, xwt, bmat, *, nb, t, bb):
    blk = lambda d: pl.BlockSpec((t, bb, d), lambda i: (0, i, 0))
    hblk = pl.BlockSpec((bb, SSM_HEADS, SSM_HEADDIM, SSM_STATE), lambda i: (i, 0, 0, 0))
    return pl.pallas_call(
        functools.partial(_ssd_sample_kernel, nb=nb, t=t, bb=bb),
        grid=(nb // bb,),
        in_specs=[pl.BlockSpec(memory_space=pltpu.SMEM), hblk, blk(SSM_GROUPS * SSM_STATE),
                  blk(D_SSM), blk(D_SSM), _const_spec(xwt.shape), _const_spec(bmat.shape)],
        out_specs=[blk(D_SSM), hblk],
        out_shape=[jax.ShapeDtypeStruct((t, nb, D_SSM), F32),
                   jax.ShapeDtypeStruct(h0.shape, F32)],
        compiler_params=_cparams(("parallel",)),
        name="ssd_sample",
    )(dec, h0, c, ydx, expa, xwt, bmat)


def _mix_sample_post_kernel(y_ref, z_ref, pool_ref, x_ref, sn_ref, wo_ref, gq_ref, wq_ref,
                            x2_ref, q_ref):
    x2 = _gate_norm_out(y_ref[...], z_ref[...], pool_ref[...], x_ref[...], sn_ref[...],
                        wo_ref[...].astype(BF16))
    x2_ref[...] = x2
    q_ref[...] = _dot(_rms(x2, gq_ref[...]).astype(BF16), wq_ref[...].astype(BF16))


def _mix_sample_post(y, z, pool, x, sn, wo, gq, wq):
    ins = [y, z, pool, x, sn, wo, gq, wq]
    outs = [jax.ShapeDtypeStruct(x.shape, F32)] * 2
    return pl.pallas_call(
        _mix_sample_post_kernel,
        grid=(1,),
        in_specs=[_const_spec(a.shape) for a in ins],
        out_specs=[pl.BlockSpec(x.shape, lambda i: (0, 0))] * 2,
        out_shape=outs,
        compiler_params=_cparams(("arbitrary",)),
        name="mix_sample_post",
    )(*ins)


def _oproj_kernel(x_ref, o_ref, wo_ref, out_ref):
    out_ref[...] = x_ref[...] + _dot(o_ref[...].astype(BF16), wo_ref[...].astype(BF16))


def _oproj(x, o, wo):
    ins = [x, o, wo]
    return pl.pallas_call(
        _oproj_kernel,
        grid=(1,),
        in_specs=[_const_spec(a.shape) for a in ins],
        out_specs=pl.BlockSpec(x.shape, lambda i: (0, 0)),
        out_shape=jax.ShapeDtypeStruct(x.shape, F32),
        compiler_params=_cparams(("arbitrary",)),
        name="oproj",
    )(*ins)


def _prep_weights(g_ffn1, w1_gate, w1_up, w1_down, g_mix, w_in, conv_w, conv_b, dt_bias, a_log,
                  d_skip, ssm_norm, w_pool, pool_scale, w_out, g_mem, w_mem_k, w_mem_v, g_xq,
                  w_xq, w_xo, g_ffn2, w2_gate, w2_up, w2_down, g_final):
    row = lambda v: v.reshape(1, -1).astype(F32)
    pad_heads = lambda v: jnp.pad(v.astype(F32), (0, LANES - SSM_HEADS)).reshape(1, LANES)
    w_pool_bd = jnp.zeros((D_POOL, D_POOL), F32)
    for g in range(len(POOL_WINDOWS)):
        w_pool_bd = w_pool_bd.at[g * POOL_GW:(g + 1) * POOL_GW, g * POOL_GW:(g + 1) * POOL_GW].set(w_pool[g])
    head_of_chan = jnp.arange(D_SSM) // SSM_HEADDIM
    expand = (jnp.arange(LANES)[:, None] == head_of_chan[None, :]).astype(F32)
    return dict(
        g_ffn1=row(g_ffn1), w1_gate=w1_gate, w1_up=w1_up, w1_down=w1_down,
        g_mix=row(g_mix), w_in=jnp.pad(jnp.transpose(w_in), ((0, LANES - SSM_HEADS), (0, 0))),
        conv_w=conv_w.astype(F32), conv_b=row(conv_b), dt_bias=pad_heads(dt_bias),
        a_log=pad_heads(a_log), d_skip=row(jnp.repeat(d_skip, SSM_HEADDIM)), ssm_norm=row(ssm_norm),
        w_pool=w_pool_bd.astype(BF16), pool_scale=row(pool_scale), w_out=w_out,
        g_mem=row(g_mem), w_mem_k=w_mem_k, w_mem_v=w_mem_v,
        g_xq=row(g_xq), w_xq=w_xq, w_xo=w_xo,
        g_ffn2=row(g_ffn2), w2_gate=w2_gate, w2_up=w2_up, w2_down=w2_down,
        g_final=row(g_final), expand=expand,
    )


def _layer(x_prompt, x_sample, mem_prompt, mem_k, mem_v, state_pool, state_conv, state_ssm, start_pos, w):
    b, s, d = x_prompt.shape
    nb, t, _ = x_sample.shape
    n = nb * t
    tmaj = lambda a: jnp.swapaxes(a, 0, 1).reshape(-1, a.shape[-1])
    bmaj = lambda a, r: jnp.swapaxes(a.reshape(r, nb, -1), 0, 1)
    r3 = lambda a: a.reshape(t, nb, a.shape[-1])

    k, v, k_b, v_b = _memkv(mem_prompt, w["g_mem"], w["w_mem_k"], w["w_mem_v"])
    xp, xs = _ffn(x_prompt.reshape(b * s, d), tmaj(x_sample), w["g_ffn1"], w["w1_gate"], w["w1_up"],
                  w["w1_down"], w["g_final"], final=False, tm=512)

    xp, pool_p, conv_p, ssm_p, w_in_bf = _mix_prompt(xp.reshape(b, s, d), w, tc=256, nseq=2)

    (pool_out, z, ydx, expa, xwt, bmat, cmat, dec, new_pool, new_conv) = _mix_sample_pre(
        xs, tmaj(state_pool), tmaj(state_conv), w_in_bf, w, nb=nb, t=t, start_pos=start_pos)
    y, ssm_s = _ssd_sample(dec[:, :SSM_HEADS].reshape(-1), state_ssm, r3(cmat), r3(ydx), r3(expa),
                           xwt, bmat, nb=nb, t=t, bb=8)
    x2, q = _mix_sample_post(y.reshape(n, D_SSM), z, pool_out, xs, w["ssm_norm"], w["w_out"],
                             w["g_xq"], w["w_xq"])

    xp, o = _attn(xp, k_b, v_b, w["g_xq"], w["w_xq"], w["w_xo"],
                  bmaj(q, t).reshape(nb, t * MEM_HEADS, MEM_HD), mem_k, mem_v, tq=512)
    xs = _oproj(bmaj(x2, t).reshape(n, d), o.reshape(n, d), w["w_xo"])

    yp, ys = _ffn(xp.reshape(b * s, d), xs, w["g_ffn2"], w["w2_gate"], w["w2_up"], w["w2_down"],
                  w["g_final"], final=True, tm=512)
    return (yp.reshape(b, s, d), ys.reshape(nb, t, d), k, v, pool_p, conv_p, ssm_p,
            bmaj(new_pool, POOL_BUF), bmaj(new_conv, CONV_W - 1), ssm_s)


def kernel(x_prompt, x_sample, mem_prompt, cache_mem_k, cache_mem_v, state_pool, state_conv, state_ssm,
           g_ffn1, w1_gate, w1_up, w1_down, g_mix, w_in, conv_w, conv_b, dt_bias, a_log, d_skip,
           ssm_norm, w_pool, pool_scale, w_out, g_mem, w_mem_k, w_mem_v, g_xq, w_xq, w_xo,
           g_ffn2, w2_gate, w2_up, w2_down, g_final):
    assert g_ffn1.shape[0] == 1, "single-layer model"
    past_len = 16384
    w = _prep_weights(g_ffn1[0], w1_gate[0], w1_up[0], w1_down[0], g_mix[0], w_in[0], conv_w[0],
                      conv_b[0], dt_bias[0], a_log[0], d_skip[0], ssm_norm[0], w_pool[0],
                      pool_scale[0], w_out[0], g_mem[0], w_mem_k[0], w_mem_v[0], g_xq[0], w_xq[0],
                      w_xo[0], g_ffn2[0], w2_gate[0], w2_up[0], w2_down[0], g_final)
    bp = x_prompt.shape[0]
    y_p, y_s, k_p, v_p, pool_p, conv_p, ssm_p, pool_s, conv_s, ssm_s = _layer(
        x_prompt, x_sample, mem_prompt, cache_mem_k[0], cache_mem_v[0], state_pool[0], state_conv[0],
        state_ssm[0], past_len, w)
    kv_shape = (1, bp, N_MEM, MEM_HEADS, MEM_HD)
    return (y_p, y_s, k_p.reshape(kv_shape), v_p.reshape(kv_shape), pool_p[None], conv_p[None],
            ssm_p[None], pool_s[None], conv_s[None], ssm_s[None])
```

```python
import functools

import jax
import jax.numpy as jnp
from jax import lax
from jax.experimental import pallas as pl
from jax.experimental.pallas import tpu as pltpu

F32 = jnp.float32
BF16 = jnp.bfloat16

D_MODEL = 1024
D_POOL = 256
POOL_WINDOWS = (2, 4, 8, 16)
POOL_GW = 64
POOL_BUF = 15
D_SSM = 768
SSM_HEADDIM = 64
SSM_HEADS = 12
SSM_GROUPS = 2
SSM_HPG = 6
SSM_STATE = 128
CONV_W = 4
CONV_DIM = 1280
D_PROJ = D_POOL + D_SSM + CONV_DIM
N_MEM = 256
MEM_HEADS = 4
MEM_HD = 256
D_FF = 2816
EPS = 1e-6
LANES = 128
CHUNK = 128
POOL_HIST = 32
CONV_HIST = 8
SCAN_PAD = CHUNK // 2
VMEM_LIMIT = 56 * 1024 * 1024


def _cparams(sem):
    return pltpu.CompilerParams(dimension_semantics=sem, vmem_limit_bytes=VMEM_LIMIT)


def _const_spec(shape):
    nd = len(shape)
    return pl.BlockSpec(shape, lambda *_: (0,) * nd, pipeline_mode=pl.Buffered(1))


def _rms(x, g):
    ms = jnp.mean(x * x, axis=-1, keepdims=True)
    return x * lax.rsqrt(ms + EPS) * g


def _silu(x):
    return x * jax.nn.sigmoid(x)


def _dot(a, b):
    return jnp.dot(a, b, preferred_element_type=F32)


def _dot_nt(a, b):
    return lax.dot_general(a, b, (((1,), (1,)), ((), ())), preferred_element_type=F32)


def _dot_tn(a, b):
    return lax.dot_general(a, b, (((0,), (0,)), ((), ())), preferred_element_type=F32)


FFN_W_STEPS = 16


def _ffn_kernel(xp_ref, xs_ref, g_ref, wg_ref, wu_ref, wd_ref, gf_ref, op_ref, os_ref,
                wg_s, wu_s, wd_s, *, final, n_p):
    i = pl.program_id(0)
    rows_gu = wg_ref.shape[0]
    rows_d = wd_ref.shape[0]

    @pl.when(i < FFN_W_STEPS)
    def _():
        r = pl.multiple_of(i * rows_gu, rows_gu)
        wg_s[pl.ds(r, rows_gu), :] = wg_ref[...].astype(BF16)
        wu_s[pl.ds(r, rows_gu), :] = wu_ref[...].astype(BF16)
        r = pl.multiple_of(i * rows_d, rows_d)
        wd_s[pl.ds(r, rows_d), :] = wd_ref[...].astype(BF16)

    def tile(x_ref, o_ref):
        x = x_ref[...]
        xn = _rms(x, g_ref[...]).astype(BF16)
        gate = _dot(xn, wg_s[...])
        up = _dot(xn, wu_s[...])
        h = (_silu(gate) * up).astype(BF16)
        out = x + 0.5 * _dot(h, wd_s[...])
        if final:
            out = _rms(out, gf_ref[...])
        o_ref[...] = out

    @pl.when(jnp.logical_and(i >= FFN_W_STEPS, i < FFN_W_STEPS + n_p))
    def _():
        tile(xp_ref, op_ref)

    @pl.when(i == FFN_W_STEPS + n_p)
    def _():
        tile(xs_ref, os_ref)


def _ffn(xp, xs, g, wg, wu, wd, gf, *, final, tm):
    tp, d = xp.shape
    dff = wg.shape[1]
    assert tp % tm == 0 and d % FFN_W_STEPS == 0 and dff % FFN_W_STEPS == 0
    n_p = tp // tm
    p_tile = pl.BlockSpec((tm, d), lambda i: (jnp.clip(i - FFN_W_STEPS, 0, n_p - 1), 0))
    s_tile = pl.BlockSpec(xs.shape, lambda i: (0, 0))
    w_chunk = lambda rows, cols: pl.BlockSpec((rows, cols), lambda i: (jnp.minimum(i, FFN_W_STEPS - 1), 0))
    return pl.pallas_call(
        functools.partial(_ffn_kernel, final=final, n_p=n_p),
        grid=(FFN_W_STEPS + n_p + 1,),
        in_specs=[
            p_tile, _const_spec(xs.shape), _const_spec(g.shape),
            w_chunk(d // FFN_W_STEPS, dff), w_chunk(d // FFN_W_STEPS, dff), w_chunk(dff // FFN_W_STEPS, d),
            _const_spec(gf.shape),
        ],
        out_specs=[p_tile, s_tile],
        out_shape=[jax.ShapeDtypeStruct(xp.shape, F32), jax.ShapeDtypeStruct(xs.shape, F32)],
        scratch_shapes=[pltpu.VMEM((d, dff), BF16), pltpu.VMEM((d, dff), BF16), pltpu.VMEM((dff, d), BF16)],
        compiler_params=_cparams(("arbitrary",)),
        name="ffn_final" if final else "ffn",
    )(xp, xs, g, wg, wu, wd, gf)


def _memkv_kernel(m_ref, g_ref, wk_ref, wv_ref, k_ref, v_ref, kb_ref, vb_ref, wk_s, wv_s):
    @pl.when(pl.program_id(0) == 0)
    def _():
        wk_s[...] = wk_ref[...].astype(BF16)
        wv_s[...] = wv_ref[...].astype(BF16)

    mn = _rms(m_ref[...], g_ref[...]).astype(BF16)
    k = _dot(mn, wk_s[...])
    v = _dot(mn, wv_s[...])
    for h in range(MEM_HEADS):
        k_ref[:, h, :] = k[:, h * MEM_HD:(h + 1) * MEM_HD]
        v_ref[:, h, :] = v[:, h * MEM_HD:(h + 1) * MEM_HD]
    kb_ref[...] = k.astype(BF16)
    vb_ref[...] = v.astype(BF16)


def _memkv(mem, g, wk, wv):
    b, n_mem, d = mem.shape
    row_blk = pl.BlockSpec((None, n_mem, d), lambda i: (i, 0, 0))
    head_blk = pl.BlockSpec((None, n_mem, MEM_HEADS, MEM_HD), lambda i: (i, 0, 0, 0))
    return pl.pallas_call(
        _memkv_kernel,
        grid=(b,),
        in_specs=[row_blk, _const_spec(g.shape), _const_spec(wk.shape), _const_spec(wv.shape)],
        out_specs=[head_blk, head_blk, row_blk, row_blk],
        out_shape=[jax.ShapeDtypeStruct((b, n_mem, MEM_HEADS, MEM_HD), F32)] * 2
        + [jax.ShapeDtypeStruct((b, n_mem, d), BF16)] * 2,
        scratch_shapes=[pltpu.VMEM(wk.shape, BF16), pltpu.VMEM(wv.shape, BF16)],
        compiler_params=_cparams(("arbitrary",)),
        name="memkv",
    )(mem, g, wk, wv)


def _lane_group_select(vals, width):
    lane = lax.broadcasted_iota(jnp.int32, vals[0].shape, 1)
    out = vals[-1]
    for g in range(len(vals) - 2, -1, -1):
        out = jnp.where(lane < (g + 1) * width, vals[g], out)
    return out


def _col_bcast(m, k, width):
    return jnp.broadcast_to(m[:, k:k + 1], (m.shape[0], width))


def _gate_norm_out(y, z, pool_out, x_res, ssm_norm, w_out):
    yn = _rms(y * _silu(z), ssm_norm)
    cat = jnp.concatenate([pool_out, yn], axis=-1).astype(BF16)
    return x_res + _dot(cat, w_out)


def _mix_prompt_kernel(x_ref, g_ref, win_ref, cw_ref, cb_ref,
                       dtb_ref, alog_ref, dsk_ref, sn_ref, wp_ref, ps_ref, wout_ref,
                       o_ref, opool_ref, oconv_ref, ossm_ref, owin_ref,
                       wu_s, wz_s, wx_s, wdt_s, wo_s,
                       ubuf, s2buf, s4buf, s8buf, cbuf, abuf, ht_ref, ybuf, *, tc, nseq):
    c = pl.program_id(1)
    nc = pl.num_programs(1)
    n_pool_slabs = D_POOL // LANES
    n_conv_slabs = CONV_DIM // LANES
    n_chunks = tc // CHUNK
    rp = POOL_HIST + tc
    seqs = range(nseq)

    def slab(s):
        return slice(s * LANES, (s + 1) * LANES)

    @pl.when(jnp.logical_and(pl.program_id(0) == 0, c == 0))
    def _():
        wu_s[...] = win_ref[0:D_POOL, :].T.astype(BF16)
        wz_s[...] = win_ref[D_POOL:D_POOL + D_SSM, :].T.astype(BF16)
        wx_s[...] = win_ref[D_POOL + D_SSM:D_PROJ, :].T.astype(BF16)
        wdt_s[...] = win_ref[D_PROJ:D_PROJ + LANES, :].T.astype(BF16)
        wo_s[...] = wout_ref[...].astype(BF16)
        owin_ref[:, 0:D_POOL] = wu_s[...]
        owin_ref[:, D_POOL:D_POOL + D_SSM] = wz_s[...]
        owin_ref[:, D_POOL + D_SSM:D_PROJ] = wx_s[...]
        owin_ref[:, D_PROJ:D_PROJ + LANES] = wdt_s[...]

    @pl.when(c == 0)
    def _():
        ubuf[:, :, 0:POOL_HIST, :] = jnp.zeros((nseq, n_pool_slabs, POOL_HIST, LANES), F32)
        cbuf[:, :, 0:CONV_HIST, :] = jnp.zeros((nseq, n_conv_slabs, CONV_HIST, LANES), F32)
        abuf[:, :, 0:SCAN_PAD, :] = jnp.zeros((nseq, n_chunks, SCAN_PAD, LANES), F32)
        ht_ref[...] = jnp.zeros_like(ht_ref)

    lo_half = lax.broadcasted_iota(jnp.int32, (tc, LANES), 1) < POOL_GW
    pos1 = c * tc + lax.broadcasted_iota(jnp.int32, (tc, LANES), 0) + 1
    a_neg = -jnp.exp(alog_ref[...])
    row = lax.broadcasted_iota(jnp.int32, (CHUNK, CHUNK), 0)
    col = lax.broadcasted_iota(jnp.int32, (CHUNK, CHUNK), 1)
    causal = row >= col
    lo = lax.broadcasted_iota(jnp.int32, (CHUNK, LANES), 1) < SSM_HEADDIM
    mask_lo = jnp.where(lo, 1.0, 0.0).astype(BF16)
    mask_hi = jnp.where(lo, 0.0, 1.0).astype(BF16)
    gw = SSM_HPG * SSM_HEADDIM
    n_x = D_SSM // LANES
    st8 = {}

    def head(i):
        x = x_ref[i]
        xn = _rms(x, g_ref[...]).astype(BF16)
        st8[i] = dict(x=x, xn=xn, xbc=[None] * n_conv_slabs)

    def conv_block(i, s0, ns):
        xn = st8[i]["xn"]
        raw = _dot(xn, wx_s[:, s0 * LANES:(s0 + ns) * LANES])
        for t in range(ns):
            s = s0 + t
            cbuf[i, s, CONV_HIST:CONV_HIST + tc, :] = raw[:, slab(t)]
            acc = cbuf[i, s, CONV_HIST:CONV_HIST + tc, :] * cw_ref[CONV_W - 1:CONV_W, slab(s)]
            for j in range(1, CONV_W):
                acc = acc + (cbuf[i, s, CONV_HIST - j:CONV_HIST - j + tc, :]
                             * cw_ref[CONV_W - 1 - j:CONV_W - j, slab(s)])
            st8[i]["xbc"][s] = _silu(acc + cb_ref[:, slab(s)])

    def dt_proj(i):
        st8[i]["dt"] = jax.nn.softplus(_dot(st8[i]["xn"], wdt_s[...]) + dtb_ref[...])

    def pool(i):
        u = _dot(st8[i]["xn"], wu_s[...])
        for s in range(n_pool_slabs):
            ubuf[i, s, POOL_HIST:rp, :] = u[:, slab(s)]
            s2buf[i, s, 8:rp, :] = ubuf[i, s, 8:rp, :] + ubuf[i, s, 7:rp - 1, :]
        s4buf[i, 16:rp, :] = s2buf[i, 1, 16:rp, :] + s2buf[i, 1, 14:rp - 2, :]
        s8buf[i, 24:rp, :] = s4buf[i, 24:rp, :] + s4buf[i, 20:rp - 4, :]
        win_sums = [
            jnp.where(lo_half, s2buf[i, 0, POOL_HIST:rp, :],
                      s2buf[i, 0, POOL_HIST:rp, :] + s2buf[i, 0, POOL_HIST - 2:rp - 2, :]),
            jnp.where(lo_half, s8buf[i, POOL_HIST:rp, :],
                      s8buf[i, POOL_HIST:rp, :] + s8buf[i, POOL_HIST - 8:rp - 8, :]),
        ]
        ps = []
        for s in range(n_pool_slabs):
            win = jnp.where(lo_half, POOL_WINDOWS[2 * s], POOL_WINDOWS[2 * s + 1])
            cnt = jnp.minimum(pos1, win).astype(F32)
            ps.append(win_sums[s] / cnt - u[:, slab(s)])
        st8[i]["pool_out"] = _dot(jnp.concatenate(ps, axis=1).astype(BF16), wp_ref[...]) * ps_ref[...]

    def ssd_chunk(i, j):
        xbc = st8[i]["xbc"]
        sl = slice(j * CHUNK, (j + 1) * CHUNK)
        dt_c = st8[i]["dt"][sl]
        acs = dt_c * a_neg
        sh = 1
        while sh < CHUNK:
            abuf[i, j, SCAN_PAD:SCAN_PAD + CHUNK, :] = acs
            acs = acs + abuf[i, j, SCAN_PAD - sh:SCAN_PAD - sh + CHUNK, :]
            sh *= 2
        acs_last = acs[CHUNK - 1:CHUNK, :]
        fdec = jnp.exp(acs_last - acs) * dt_c
        src_t = (acs - jnp.log(dt_c)).T
        ht = ht_ref[i]
        ht_b = ht.astype(BF16)
        cdec, st = [], []
        for g in range(SSM_GROUPS):
            c_g = xbc[n_x + SSM_GROUPS + g][sl].astype(BF16)
            b_g = xbc[n_x + g][sl].astype(BF16)
            scores = _dot_nt(c_g, b_g)
            y_off = _dot(c_g, ht_b[:, g * gw:(g + 1) * gw])
            xw = []
            for q in range(SSM_HPG // 2):
                k0 = g * SSM_HPG + 2 * q
                pair = k0 // 2
                a_cols = [_col_bcast(acs, k, CHUNK) for k in (k0, k0 + 1)]
                lhs = [(scores * jnp.exp(jnp.where(causal, a_cols[h] - src_t[k0 + h:k0 + h + 1, :], -jnp.inf))
                        ).astype(BF16) for h in range(2)]
                x_pair = xbc[pair][sl]
                x_b = x_pair.astype(BF16)
                rhs = jnp.concatenate([x_b * mask_lo, x_b * mask_hi], axis=0)
                e_pair = jnp.exp(jnp.where(lo, a_cols[0], a_cols[1]))
                ybuf[i, sl, slab(pair)] = (_dot(jnp.concatenate(lhs, axis=1), rhs)
                                           + y_off[:, slab(q)] * e_pair + x_pair * dsk_ref[:, slab(pair)])
                f_pair = jnp.where(lo, _col_bcast(fdec, k0, LANES), _col_bcast(fdec, k0 + 1, LANES))
                xw.append((x_pair * f_pair).astype(BF16))
                cdec.append(e_pair[CHUNK - 1:CHUNK, :])
            st.append(_dot_tn(b_g, jnp.concatenate(xw, axis=1)))
        ht_ref[i] = ht * jnp.concatenate(cdec, axis=1) + jnp.concatenate(st, axis=1)

    def z_proj(i):
        st8[i]["zs"] = _silu(_dot(st8[i]["xn"], wz_s[...]))

    def tail(i):
        d = st8[i]
        yn = _rms(ybuf[i] * d["zs"], sn_ref[...])
        cat = jnp.concatenate([d["pool_out"], yn], axis=-1).astype(BF16)
        o_ref[i] = d["x"] + _dot(cat, wo_s[...])

    def mid(i):
        conv_block(i, n_x, 2 * SSM_GROUPS)
        dt_proj(i)
        for s0 in range(0, n_x, 2):
            conv_block(i, s0, 2)

    for i in seqs:
        head(i)
    for i in seqs:
        mid(i)
    for i in seqs:
        pool(i)
    for j in range(n_chunks):
        for i in seqs:
            ssd_chunk(i, j)
        if j == 0:
            for i in seqs:
                z_proj(i)
    for i in seqs:
        tail(i)

    ubuf[:, :, 0:POOL_HIST, :] = ubuf[:, :, tc:tc + POOL_HIST, :]
    cbuf[:, :, 0:CONV_HIST, :] = cbuf[:, :, tc:tc + CONV_HIST, :]

    @pl.when(c == nc - 1)
    def _():
        for i in seqs:
            for s in range(n_pool_slabs):
                opool_ref[i, :, slab(s)] = ubuf[i, s, POOL_HIST - POOL_BUF:POOL_HIST, :]
            for s in range(n_conv_slabs):
                oconv_ref[i, :, slab(s)] = cbuf[i, s, CONV_HIST - (CONV_W - 1):CONV_HIST, :]
            ossm_ref[i] = ht_ref[i].T.reshape(SSM_HEADS, SSM_HEADDIM, SSM_STATE)


def _mix_prompt(x, w, *, tc, nseq):
    b, s, d = x.shape
    assert s % tc == 0 and tc % CHUNK == 0 and b % nseq == 0
    consts = [w["g_mix"], w["w_in"], w["conv_w"], w["conv_b"],
              w["dt_bias"], w["a_log"], w["d_skip"], w["ssm_norm"], w["w_pool"], w["pool_scale"],
              w["w_out"]]
    tile = pl.BlockSpec((nseq, tc, d), lambda i, j: (i, j, 0))
    return pl.pallas_call(
        functools.partial(_mix_prompt_kernel, tc=tc, nseq=nseq),
        grid=(b // nseq, s // tc),
        in_specs=[tile] + [_const_spec(a.shape) for a in consts],
        out_specs=[
            tile,
            pl.BlockSpec((nseq, POOL_BUF, D_POOL), lambda i, j: (i, 0, 0)),
            pl.BlockSpec((nseq, CONV_W - 1, CONV_DIM), lambda i, j: (i, 0, 0)),
            pl.BlockSpec((nseq, SSM_HEADS, SSM_HEADDIM, SSM_STATE), lambda i, j: (i, 0, 0, 0)),
            pl.BlockSpec((d, D_PROJ + LANES), lambda i, j: (0, 0)),
        ],
        out_shape=[
            jax.ShapeDtypeStruct((b, s, d), F32),
            jax.ShapeDtypeStruct((b, POOL_BUF, D_POOL), F32),
            jax.ShapeDtypeStruct((b, CONV_W - 1, CONV_DIM), F32),
            jax.ShapeDtypeStruct((b, SSM_HEADS, SSM_HEADDIM, SSM_STATE), F32),
            jax.ShapeDtypeStruct((d, D_PROJ + LANES), BF16),
        ],
        scratch_shapes=[
            pltpu.VMEM((d, D_POOL), BF16), pltpu.VMEM((d, D_SSM), BF16),
            pltpu.VMEM((d, CONV_DIM), BF16), pltpu.VMEM((d, LANES), BF16), pltpu.VMEM((d, d), BF16),
            pltpu.VMEM((nseq, D_POOL // LANES, POOL_HIST + tc, LANES), F32),
            pltpu.VMEM((nseq, D_POOL // LANES, POOL_HIST + tc, LANES), F32),
            pltpu.VMEM((nseq, POOL_HIST + tc, LANES), F32),
            pltpu.VMEM((nseq, POOL_HIST + tc, LANES), F32),
            pltpu.VMEM((nseq, CONV_DIM // LANES, CONV_HIST + tc, LANES), F32),
            pltpu.VMEM((nseq, tc // CHUNK, SCAN_PAD + CHUNK, LANES), F32),
            pltpu.VMEM((nseq, SSM_STATE, D_SSM), F32),
            pltpu.VMEM((nseq, tc, D_SSM), F32),
        ],
        compiler_params=_cparams(("arbitrary", "arbitrary")),
        name="mix_prompt",
    )(x, *consts)


def _attn_kernel(x_ref, g_ref, wq_ref, k_ref, v_ref, wo_ref, qs_ref, ks_ref, vs_ref,
                 o_ref, os_ref, wq_s, wo_s, *, bb):
    @pl.when(jnp.logical_and(pl.program_id(0) == 0, pl.program_id(1) == 0))
    def _():
        wq_s[...] = wq_ref[...].astype(BF16)
        wo_s[...] = wo_ref[...].astype(BF16)

    hs = [slice(h * MEM_HD, (h + 1) * MEM_HD) for h in range(MEM_HEADS)]
    scale = MEM_HD ** -0.5
    nq = qs_ref.shape[1]
    rows = N_MEM * MEM_HEADS

    def softmax(s):
        e = jnp.exp(s - jnp.max(s, axis=-1, keepdims=True))
        return (e / jnp.sum(e, axis=-1, keepdims=True)).astype(BF16)

    q_head = lax.broadcasted_iota(jnp.int32, (nq, rows), 0) % MEM_HEADS
    kv_head = lax.broadcasted_iota(jnp.int32, (nq, rows), 1) % MEM_HEADS
    same_head = q_head == kv_head

    q_all = qs_ref[...].reshape(bb * nq, MEM_HD)
    q_all = jnp.concatenate([q_all, jnp.zeros((LANES - bb * nq, MEM_HD), F32)], axis=0).astype(BF16)
    s_s = [_dot_nt(ks_ref[i].reshape(rows, MEM_HD).astype(BF16), q_all) for i in range(bb)]

    x = x_ref[...]
    q = _dot(_rms(x, g_ref[...]).astype(BF16), wq_s[...]).astype(BF16)

    p_s = [softmax(jnp.where(same_head, s_s[i].T[i * nq:(i + 1) * nq] * scale, -jnp.inf)) for i in range(bb)]

    s_p = [_dot_nt(q[:, hs[h]], k_ref[:, hs[h]]) * scale for h in range(MEM_HEADS)]

    for i in range(bb):
        os_ref[i] = _dot(p_s[i], vs_ref[i].reshape(rows, MEM_HD).astype(BF16))

    p_p = [softmax(s) for s in s_p]
    o = jnp.concatenate([_dot(p_p[h], v_ref[:, hs[h]]) for h in range(MEM_HEADS)], axis=-1).astype(BF16)
    o_ref[...] = x + _dot(o, wo_s[...])


def _attn(x, k, v, g, wq, wo, qs, ks, vs, *, tq):
    b, s, d = x.shape
    nb, nq, hd = qs.shape
    nj = s // tq
    assert nb % (b * nj) == 0
    bb = nb // (b * nj)
    assert bb * nq <= LANES
    tile = pl.BlockSpec((None, tq, d), lambda i, j: (i, j, 0))
    mem = pl.BlockSpec((None, N_MEM, d), lambda i, j: (i, 0, 0))
    s_q = pl.BlockSpec((bb, nq, hd), lambda i, j: (i * nj + j, 0, 0))
    s_mem = pl.BlockSpec((bb, N_MEM, MEM_HEADS, hd), lambda i, j: (i * nj + j, 0, 0, 0))
    return pl.pallas_call(
        functools.partial(_attn_kernel, bb=bb),
        grid=(b, nj),
        in_specs=[tile, _const_spec(g.shape), _const_spec(wq.shape), mem, mem, _const_spec(wo.shape),
                  s_q, s_mem, s_mem],
        out_specs=[tile, s_q],
        out_shape=[jax.ShapeDtypeStruct((b, s, d), F32), jax.ShapeDtypeStruct((nb, nq, hd), F32)],
        scratch_shapes=[pltpu.VMEM(wq.shape, BF16), pltpu.VMEM(wo.shape, BF16)],
        compiler_params=_cparams(("arbitrary", "arbitrary")),
        name="attn",
    )(x, g, wq, k, v, wo, qs, ks, vs)


def _mix_sample_pre_kernel(x_ref, sp_ref, sc_ref, g_ref, win_ref,
                           cw_ref, cb_ref, dtb_ref, alog_ref, dsk_ref, wp_ref, ps_ref, ex_ref,
                           pool_ref, z_ref, ydx_ref, expa_ref, xwt_ref, b_ref, c_ref, dec_ref,
                           npool_ref, nconv_ref, *, nb, t, start_pos):
    def rows(i):
        return slice(i * nb, (i + 1) * nb)

    x = x_ref[...]
    xn = _rms(x, g_ref[...]).astype(BF16)
    z_ref[...] = _dot(xn, win_ref[:, D_POOL:D_POOL + D_SSM])

    u = _dot(xn, win_ref[:, 0:D_POOL])
    ext = [sp_ref[rows(i), :] for i in range(POOL_BUF)] + [u[rows(i)] for i in range(t)]
    ps = []
    for i in range(t):
        sums, cnts = [], []
        for w in POOL_WINDOWS:
            s = ext[POOL_BUF + i]
            for j in range(1, w):
                s = s + ext[POOL_BUF + i - j]
            sums.append(s)
            cnts.append(jnp.full((nb, D_POOL), float(min(start_pos + i + 1, w)), F32))
        mean = _lane_group_select(sums, POOL_GW) / _lane_group_select(cnts, POOL_GW)
        ps.append(mean - ext[POOL_BUF + i])
    p = jnp.concatenate(ps, axis=0).astype(BF16)
    pool_ref[...] = _dot(p, wp_ref[...]) * ps_ref[...]
    for i in range(POOL_BUF):
        npool_ref[rows(i), :] = ext[t + i]

    xr = _dot(xn, win_ref[:, D_POOL + D_SSM:D_PROJ])
    cext = [sc_ref[rows(i), :] for i in range(CONV_W - 1)] + [xr[rows(i)] for i in range(t)]
    xbc = []
    for i in range(t):
        acc = cext[i] * cw_ref[0:1, :]
        for k in range(1, CONV_W):
            acc = acc + cext[i + k] * cw_ref[k:k + 1, :]
        xbc.append(_silu(acc + cb_ref[...]))
    for i in range(CONV_W - 1):
        nconv_ref[rows(i), :] = cext[t + i]
    xs = [v[:, :D_SSM] for v in xbc]
    bm = [v[:, D_SSM:D_SSM + SSM_GROUPS * SSM_STATE] for v in xbc]
    cm = [v[:, D_SSM + SSM_GROUPS * SSM_STATE:] for v in xbc]
    for i in range(t):
        b_ref[rows(i), :] = bm[i]
        c_ref[rows(i), :] = cm[i]

    dt_all = jax.nn.softplus(_dot(xn, win_ref[:, D_PROJ:D_PROJ + LANES]) + dtb_ref[...])
    a_neg = -jnp.exp(alog_ref[...])
    dt = [dt_all[rows(i)] for i in range(t)]
    acs = []
    for i in range(t):
        da = dt[i] * a_neg
        acs.append(da if i == 0 else acs[-1] + da)
    dec_ref[...] = jnp.exp(acs[-1])

    def expand(v, exact):
        if exact:
            return jnp.dot(v, ex_ref[...], precision=lax.Precision.HIGHEST,
                           preferred_element_type=F32)
        return _dot(v.astype(BF16), ex_ref[...].astype(BF16))

    lane = lax.broadcasted_iota(jnp.int32, (nb, LANES), 1)
    xw = []
    for i in range(t):
        ydx = xs[i] * dsk_ref[...]
        for s in range(i + 1):
            sc = [jnp.sum(cm[i][:, g * SSM_STATE:(g + 1) * SSM_STATE]
                          * bm[s][:, g * SSM_STATE:(g + 1) * SSM_STATE], axis=-1, keepdims=True)
                  for g in range(SSM_GROUPS)]
            sc = jnp.where(lane < SSM_HPG, sc[0], sc[1])
            coef = sc * jnp.exp(acs[i] - acs[s]) * dt[s]
            ydx = ydx + expand(coef, False) * xs[s]
        ydx_ref[rows(i), :] = ydx
        expa_ref[rows(i), :] = expand(jnp.exp(acs[i]), True)
        xw.append(xs[i] * expand(jnp.exp(acs[-1] - acs[i]) * dt[i], True))
    xwt_ref[...] = jnp.concatenate(xw, axis=0).T.astype(BF16)


def _mix_sample_pre(x, sp, sc, w_in_bf, w, *, nb, t, start_pos):
    n = nb * t
    consts = [w["g_mix"], w_in_bf, w["conv_w"], w["conv_b"],
              w["dt_bias"], w["a_log"], w["d_skip"], w["w_pool"], w["pool_scale"], w["expand"]]
    ins = [x, sp, sc] + consts
    outs = [
        jax.ShapeDtypeStruct((n, D_POOL), F32),
        jax.ShapeDtypeStruct((n, D_SSM), F32),
        jax.ShapeDtypeStruct((n, D_SSM), F32),
        jax.ShapeDtypeStruct((n, D_SSM), F32),
        jax.ShapeDtypeStruct((D_SSM, n), BF16),
        jax.ShapeDtypeStruct((n, SSM_GROUPS * SSM_STATE), F32),
        jax.ShapeDtypeStruct((n, SSM_GROUPS * SSM_STATE), F32),
        jax.ShapeDtypeStruct((nb, LANES), F32),
        jax.ShapeDtypeStruct((POOL_BUF * nb, D_POOL), F32),
        jax.ShapeDtypeStruct(((CONV_W - 1) * nb, CONV_DIM), F32),
    ]
    return pl.pallas_call(
        functools.partial(_mix_sample_pre_kernel, nb=nb, t=t, start_pos=start_pos),
        grid=(1,),
        in_specs=[_const_spec(a.shape) for a in ins],
        out_specs=[pl.BlockSpec(o.shape, lambda i: (0, 0)) for o in outs],
        out_shape=outs,
        compiler_params=_cparams(("arbitrary",)),
        name="mix_sample_pre",
    )(*ins)


def _ssd_sample_kernel(dec_ref, h0_ref, c_ref, ydx_ref, expa_ref, xwt_ref, b_ref,
                       y_ref, hout_ref, *, nb, t, bb):
    j = pl.program_id(0)
    gw = SSM_HPG * SSM_HEADDIM
    c_blk = c_ref[...].reshape(t * bb, SSM_GROUPS * SSM_STATE)
    row_seq = lax.broadcasted_iota(jnp.int32, (t * bb, SSM_STATE), 0) % bb
    all_seq = lax.broadcasted_iota(jnp.int32, (t * nb, SSM_STATE), 0) % nb
    b_all = b_ref[...]
    yoff = [jnp.zeros((t * bb, gw), F32) for _ in range(SSM_GROUPS)]
    for i in range(bb):
        seq = j * bb + i
        for g in range(SSM_GROUPS):
            gs = slice(g * SSM_STATE, (g + 1) * SSM_STATE)
            h0 = h0_ref[i, g * SSM_HPG:(g + 1) * SSM_HPG].reshape(gw, SSM_STATE)
            c_i = jnp.where(row_seq == i, c_blk[:, gs], 0.0).astype(BF16)
            yoff[g] = yoff[g] + _dot_nt(c_i, h0.astype(BF16))
            b_i = jnp.where(all_seq == seq, b_all[:, gs], 0.0).astype(BF16)
            st = _dot(xwt_ref[g * gw:(g + 1) * gw, :], b_i)
            for k in range(SSM_HPG):
                hd = g * SSM_HPG + k
                hout_ref[i, hd] = (h0_ref[i, hd] * dec_ref[seq * SSM_HEADS + hd]
                                   + st[k * SSM_HEADDIM:(k + 1) * SSM_HEADDIM])
    yoff = jnp.concatenate(yoff, axis=1).reshape(t, bb, D_SSM)
    y_ref[...] = ydx_ref[...] + yoff * expa_ref[...]


def _ssd_sample(dec, h0, c, ydx, expa, xwt, bmat, *, nb, t, bb):
    blk = lambda d: pl.BlockSpec((t, bb, d), lambda i: (0, i, 0))
    hblk = pl.BlockSpec((bb, SSM_HEADS, SSM_HEADDIM, SSM_STATE), lambda i: (i, 0, 0, 0))
    return pl.pallas_call(
        functools.partial(_ssd_sample_kernel, nb=nb, t=t, bb=bb),
        grid=(nb // bb,),
        in_specs=[pl.BlockSpec(memory_space=pltpu.SMEM), hblk, blk(SSM_GROUPS * SSM_STATE),
                  blk(D_SSM), blk(D_SSM), _const_spec(xwt.shape), _const_spec(bmat.shape)],
        out_specs=[blk(D_SSM), hblk],
        out_shape=[jax.ShapeDtypeStruct((t, nb, D_SSM), F32),
                   jax.ShapeDtypeStruct(h0.shape, F32)],
        compiler_params=_cparams(("parallel",)),
        name="ssd_sample",
    )(dec, h0, c, ydx, expa, xwt, bmat)


def _mix_sample_post_kernel(y_ref, z_ref, pool_ref, x_ref, sn_ref, wo_ref, gq_ref, wq_ref,
                            x2_ref, q_ref):
    x2 = _gate_norm_out(y_ref[...], z_ref[...], pool_ref[...], x_ref[...], sn_ref[...],
                        wo_ref[...].astype(BF16))
    x2_ref[...] = x2
    q_ref[...] = _dot(_rms(x2, gq_ref[...]).astype(BF16), wq_ref[...].astype(BF16))


def _mix_sample_post(y, z, pool, x, sn, wo, gq, wq):
    ins = [y, z, pool, x, sn, wo, gq, wq]
    outs = [jax.ShapeDtypeStruct(x.shape, F32)] * 2
    return pl.pallas_call(
        _mix_sample_post_kernel,
        grid=(1,),
        in_specs=[_const_spec(a.shape) for a in ins],
        out_specs=[pl.BlockSpec(x.shape, lambda i: (0, 0))] * 2,
        out_shape=outs,
        compiler_params=_cparams(("arbitrary",)),
        name="mix_sample_post",
    )(*ins)


def _oproj_kernel(x_ref, o_ref, wo_ref, out_ref):
    out_ref[...] = x_ref[...] + _dot(o_ref[...].astype(BF16), wo_ref[...].astype(BF16))


def _oproj(x, o, wo):
    ins = [x, o, wo]
    return pl.pallas_call(
        _oproj_kernel,
        grid=(1,),
        in_specs=[_const_spec(a.shape) for a in ins],
        out_specs=pl.BlockSpec(x.shape, lambda i: (0, 0)),
        out_shape=jax.ShapeDtypeStruct(x.shape, F32),
        compiler_params=_cparams(("arbitrary",)),
        name="oproj",
    )(*ins)


def _prep_weights(g_ffn1, w1_gate, w1_up, w1_down, g_mix, w_in, conv_w, conv_b, dt_bias, a_log,
                  d_skip, ssm_norm, w_pool, pool_scale, w_out, g_mem, w_mem_k, w_mem_v, g_xq,
                  w_xq, w_xo, g_ffn2, w2_gate, w2_up, w2_down, g_final):
    row = lambda v: v.reshape(1, -1).astype(F32)
    pad_heads = lambda v: jnp.pad(v.astype(F32), (0, LANES - SSM_HEADS)).reshape(1, LANES)
    w_pool_bd = jnp.zeros((D_POOL, D_POOL), F32)
    for g in range(len(POOL_WINDOWS)):
        w_pool_bd = w_pool_bd.at[g * POOL_GW:(g + 1) * POOL_GW, g * POOL_GW:(g + 1) * POOL_GW].set(w_pool[g])
    head_of_chan = jnp.arange(D_SSM) // SSM_HEADDIM
    expand = (jnp.arange(LANES)[:, None] == head_of_chan[None, :]).astype(F32)
    return dict(
        g_ffn1=row(g_ffn1), w1_gate=w1_gate, w1_up=w1_up, w1_down=w1_down,
        g_mix=row(g_mix), w_in=jnp.pad(jnp.transpose(w_in), ((0, LANES - SSM_HEADS), (0, 0))),
        conv_w=conv_w.astype(F32), conv_b=row(conv_b), dt_bias=pad_heads(dt_bias),
        a_log=pad_heads(a_log), d_skip=row(jnp.repeat(d_skip, SSM_HEADDIM)), ssm_norm=row(ssm_norm),
        w_pool=w_pool_bd.astype(BF16), pool_scale=row(pool_scale), w_out=w_out,
        g_mem=row(g_mem), w_mem_k=w_mem_k, w_mem_v=w_mem_v,
        g_xq=row(g_xq), w_xq=w_xq, w_xo=w_xo,
        g_ffn2=row(g_ffn2), w2_gate=w2_gate, w2_up=w2_up, w2_down=w2_down,
        g_final=row(g_final), expand=expand,
    )


def _layer(x_prompt, x_sample, mem_prompt, mem_k, mem_v, state_pool, state_conv, state_ssm, start_pos, w):
    b, s, d = x_prompt.shape
    nb, t, _ = x_sample.shape
    n = nb * t
    tmaj = lambda a: jnp.swapaxes(a, 0, 1).reshape(-1, a.shape[-1])
    bmaj = lambda a, r: jnp.swapaxes(a.reshape(r, nb, -1), 0, 1)
    r3 = lambda a: a.reshape(t, nb, a.shape[-1])

    k, v, k_b, v_b = _memkv(mem_prompt, w["g_mem"], w["w_mem_k"], w["w_mem_v"])
    xp, xs = _ffn(x_prompt.reshape(b * s, d), tmaj(x_sample), w["g_ffn1"], w["w1_gate"], w["w1_up"],
                  w["w1_down"], w["g_final"], final=False, tm=1024)

    xp, pool_p, conv_p, ssm_p, w_in_bf = _mix_prompt(xp.reshape(b, s, d), w, tc=256, nseq=2)

    (pool_out, z, ydx, expa, xwt, bmat, cmat, dec, new_pool, new_conv) = _mix_sample_pre(
        xs, tmaj(state_pool), tmaj(state_conv), w_in_bf, w, nb=nb, t=t, start_pos=start_pos)
    y, ssm_s = _ssd_sample(dec[:, :SSM_HEADS].reshape(-1), state_ssm, r3(cmat), r3(ydx), r3(expa),
                           xwt, bmat, nb=nb, t=t, bb=8)
    x2, q = _mix_sample_post(y.reshape(n, D_SSM), z, pool_out, xs, w["ssm_norm"], w["w_out"],
                             w["g_xq"], w["w_xq"])

    xp, o = _attn(xp, k_b, v_b, w["g_xq"], w["w_xq"], w["w_xo"],
                  bmaj(q, t).reshape(nb, t * MEM_HEADS, MEM_HD), mem_k, mem_v, tq=512)
    xs = _oproj(bmaj(x2, t).reshape(n, d), o.reshape(n, d), w["w_xo"])

    yp, ys = _ffn(xp.reshape(b * s, d), xs, w["g_ffn2"], w["w2_gate"], w["w2_up"], w["w2_down"],
                  w["g_final"], final=True, tm=1024)
    return (yp.reshape(b, s, d), ys.reshape(nb, t, d), k, v, pool_p, conv_p, ssm_p,
            bmaj(new_pool, POOL_BUF), bmaj(new_conv, CONV_W - 1), ssm_s)


def kernel(x_prompt, x_sample, mem_prompt, cache_mem_k, cache_mem_v, state_pool, state_conv, state_ssm,
           g_ffn1, w1_gate, w1_up, w1_down, g_mix, w_in, conv_w, conv_b, dt_bias, a_log, d_skip,
           ssm_norm, w_pool, pool_scale, w_out, g_mem, w_mem_k, w_mem_v, g_xq, w_xq, w_xo,
           g_ffn2, w2_gate, w2_up, w2_down, g_final):
    assert g_ffn1.shape[0] == 1, "single-layer model"
    past_len = 16384
    w = _prep_weights(g_ffn1[0], w1_gate[0], w1_up[0], w1_down[0], g_mix[0], w_in[0], conv_w[0],
                      conv_b[0], dt_bias[0], a_log[0], d_skip[0], ssm_norm[0], w_pool[0],
                      pool_scale[0], w_out[0], g_mem[0], w_mem_k[0], w_mem_v[0], g_xq[0], w_xq[0],
                      w_xo[0], g_ffn2[0], w2_gate[0], w2_up[0], w2_down[0], g_final)
    bp = x_prompt.shape[0]
    y_p, y_s, k_p, v_p, pool_p, conv_p, ssm_p, pool_s, conv_s, ssm_s = _layer(
        x_prompt, x_sample, mem_prompt, cache_mem_k[0], cache_mem_v[0], state_pool[0], state_conv[0],
        state_ssm[0], past_len, w)
    kv_shape = (1, bp, N_MEM, MEM_HEADS, MEM_HD)
    return (y_p, y_s, k_p.reshape(kv_shape), v_p.reshape(kv_shape), pool_p[None], conv_p[None],
            ssm_p[None], pool_s[None], conv_s[None], ssm_s[None])
```

```python
import functools

import jax
import jax.numpy as jnp
from jax import lax
from jax.experimental import pallas as pl
from jax.experimental.pallas import tpu as pltpu

F32 = jnp.float32
BF16 = jnp.bfloat16

D_MODEL = 1024
D_POOL = 256
POOL_WINDOWS = (2, 4, 8, 16)
POOL_GW = 64
POOL_BUF = 15
D_SSM = 768
SSM_HEADDIM = 64
SSM_HEADS = 12
SSM_GROUPS = 2
SSM_HPG = 6
SSM_STATE = 128
CONV_W = 4
CONV_DIM = 1280
D_PROJ = D_POOL + D_SSM + CONV_DIM
N_MEM = 256
MEM_HEADS = 4
MEM_HD = 256
D_FF = 2816
EPS = 1e-6
LANES = 128
CHUNK = 128
POOL_HIST = 32
CONV_HIST = 8
SCAN_PAD = CHUNK // 2
VMEM_LIMIT = 56 * 1024 * 1024


def _cparams(sem):
    return pltpu.CompilerParams(dimension_semantics=sem, vmem_limit_bytes=VMEM_LIMIT)


def _const_spec(shape):
    nd = len(shape)
    return pl.BlockSpec(shape, lambda *_: (0,) * nd, pipeline_mode=pl.Buffered(1))


def _rms(x, g):
    ms = jnp.mean(x * x, axis=-1, keepdims=True)
    return x * lax.rsqrt(ms + EPS) * g


def _silu(x):
    return x * jax.nn.sigmoid(x)


def _dot(a, b):
    return jnp.dot(a, b, preferred_element_type=F32)


def _dot_nt(a, b):
    return lax.dot_general(a, b, (((1,), (1,)), ((), ())), preferred_element_type=F32)


def _dot_tn(a, b):
    return lax.dot_general(a, b, (((0,), (0,)), ((), ())), preferred_element_type=F32)


FFN_W_STEPS = 16


def _ffn_kernel(xp_ref, xs_ref, g_ref, wg_ref, wu_ref, wd_ref, gf_ref, op_ref, os_ref,
                wg_s, wu_s, wd_s, *, final, n_p):
    i = pl.program_id(0)
    rows_gu = wg_ref.shape[0]
    rows_d = wd_ref.shape[0]

    @pl.when(i < FFN_W_STEPS)
    def _():
        r = pl.multiple_of(i * rows_gu, rows_gu)
        wg_s[pl.ds(r, rows_gu), :] = wg_ref[...].astype(BF16)
        wu_s[pl.ds(r, rows_gu), :] = wu_ref[...].astype(BF16)
        r = pl.multiple_of(i * rows_d, rows_d)
        wd_s[pl.ds(r, rows_d), :] = wd_ref[...].astype(BF16)

    def tile(x_ref, o_ref):
        x = x_ref[...]
        xn = _rms(x, g_ref[...]).astype(BF16)
        gate = _dot(xn, wg_s[...])
        up = _dot(xn, wu_s[...])
        h = (_silu(gate) * up).astype(BF16)
        out = x + 0.5 * _dot(h, wd_s[...])
        if final:
            out = _rms(out, gf_ref[...])
        o_ref[...] = out

    @pl.when(jnp.logical_and(i >= FFN_W_STEPS, i < FFN_W_STEPS + n_p))
    def _():
        tile(xp_ref, op_ref)

    @pl.when(i == FFN_W_STEPS + n_p)
    def _():
        tile(xs_ref, os_ref)


def _ffn(xp, xs, g, wg, wu, wd, gf, *, final, tm):
    tp, d = xp.shape
    dff = wg.shape[1]
    assert tp % tm == 0 and d % FFN_W_STEPS == 0 and dff % FFN_W_STEPS == 0
    n_p = tp // tm
    p_tile = pl.BlockSpec((tm, d), lambda i: (jnp.clip(i - FFN_W_STEPS, 0, n_p - 1), 0))
    s_tile = pl.BlockSpec(xs.shape, lambda i: (0, 0))
    w_chunk = lambda rows, cols: pl.BlockSpec((rows, cols), lambda i: (jnp.minimum(i, FFN_W_STEPS - 1), 0))
    return pl.pallas_call(
        functools.partial(_ffn_kernel, final=final, n_p=n_p),
        grid=(FFN_W_STEPS + n_p + 1,),
        in_specs=[
            p_tile, _const_spec(xs.shape), _const_spec(g.shape),
            w_chunk(d // FFN_W_STEPS, dff), w_chunk(d // FFN_W_STEPS, dff), w_chunk(dff // FFN_W_STEPS, d),
            _const_spec(gf.shape),
        ],
        out_specs=[p_tile, s_tile],
        out_shape=[jax.ShapeDtypeStruct(xp.shape, F32), jax.ShapeDtypeStruct(xs.shape, F32)],
        scratch_shapes=[pltpu.VMEM((d, dff), BF16), pltpu.VMEM((d, dff), BF16), pltpu.VMEM((dff, d), BF16)],
        compiler_params=_cparams(("arbitrary",)),
        name="ffn_final" if final else "ffn",
    )(xp, xs, g, wg, wu, wd, gf)


def _memkv_kernel(m_ref, g_ref, wk_ref, wv_ref, k_ref, v_ref, kb_ref, vb_ref, wk_s, wv_s):
    @pl.when(pl.program_id(0) == 0)
    def _():
        wk_s[...] = wk_ref[...].astype(BF16)
        wv_s[...] = wv_ref[...].astype(BF16)

    mn = _rms(m_ref[...], g_ref[...]).astype(BF16)
    k = _dot(mn, wk_s[...])
    v = _dot(mn, wv_s[...])
    for h in range(MEM_HEADS):
        k_ref[:, h, :] = k[:, h * MEM_HD:(h + 1) * MEM_HD]
        v_ref[:, h, :] = v[:, h * MEM_HD:(h + 1) * MEM_HD]
    kb_ref[...] = k.astype(BF16)
    vb_ref[...] = v.astype(BF16)


def _memkv(mem, g, wk, wv):
    b, n_mem, d = mem.shape
    row_blk = pl.BlockSpec((None, n_mem, d), lambda i: (i, 0, 0))
    head_blk = pl.BlockSpec((None, n_mem, MEM_HEADS, MEM_HD), lambda i: (i, 0, 0, 0))
    return pl.pallas_call(
        _memkv_kernel,
        grid=(b,),
        in_specs=[row_blk, _const_spec(g.shape), _const_spec(wk.shape), _const_spec(wv.shape)],
        out_specs=[head_blk, head_blk, row_blk, row_blk],
        out_shape=[jax.ShapeDtypeStruct((b, n_mem, MEM_HEADS, MEM_HD), F32)] * 2
        + [jax.ShapeDtypeStruct((b, n_mem, d), BF16)] * 2,
        scratch_shapes=[pltpu.VMEM(wk.shape, BF16), pltpu.VMEM(wv.shape, BF16)],
        compiler_params=_cparams(("arbitrary",)),
        name="memkv",
    )(mem, g, wk, wv)


def _lane_group_select(vals, width):
    lane = lax.broadcasted_iota(jnp.int32, vals[0].shape, 1)
    out = vals[-1]
    for g in range(len(vals) - 2, -1, -1):
        out = jnp.where(lane < (g + 1) * width, vals[g], out)
    return out


def _col_bcast(m, k, width):
    return jnp.broadcast_to(m[:, k:k + 1], (m.shape[0], width))


def _gate_norm_out(y, z, pool_out, x_res, ssm_norm, w_out):
    yn = _rms(y * _silu(z), ssm_norm)
    cat = jnp.concatenate([pool_out, yn], axis=-1).astype(BF16)
    return x_res + _dot(cat, w_out)


def _mix_prompt_kernel(x_ref, g_ref, win_ref, cw_ref, cb_ref,
                       dtb_ref, alog_ref, dsk_ref, sn_ref, wp_ref, ps_ref, wout_ref,
                       o_ref, opool_ref, oconv_ref, ossm_ref, owin_ref,
                       wu_s, wz_s, wx_s, wdt_s, wo_s,
                       ubuf, s2buf, s4buf, s8buf, cbuf, abuf, ht_ref, ybuf, *, tc, nseq):
    c = pl.program_id(1)
    nc = pl.num_programs(1)
    n_pool_slabs = D_POOL // LANES
    n_conv_slabs = CONV_DIM // LANES
    n_chunks = tc // CHUNK
    rp = POOL_HIST + tc
    seqs = range(nseq)

    def slab(s):
        return slice(s * LANES, (s + 1) * LANES)

    @pl.when(jnp.logical_and(pl.program_id(0) == 0, c == 0))
    def _():
        wu_s[...] = win_ref[0:D_POOL, :].T.astype(BF16)
        wz_s[...] = win_ref[D_POOL:D_POOL + D_SSM, :].T.astype(BF16)
        wx_s[...] = win_ref[D_POOL + D_SSM:D_PROJ, :].T.astype(BF16)
        wdt_s[...] = win_ref[D_PROJ:D_PROJ + LANES, :].T.astype(BF16)
        wo_s[...] = wout_ref[...].astype(BF16)
        owin_ref[:, 0:D_POOL] = wu_s[...]
        owin_ref[:, D_POOL:D_POOL + D_SSM] = wz_s[...]
        owin_ref[:, D_POOL + D_SSM:D_PROJ] = wx_s[...]
        owin_ref[:, D_PROJ:D_PROJ + LANES] = wdt_s[...]

    @pl.when(c == 0)
    def _():
        ubuf[:, :, 0:POOL_HIST, :] = jnp.zeros((nseq, n_pool_slabs, POOL_HIST, LANES), F32)
        cbuf[:, :, 0:CONV_HIST, :] = jnp.zeros((nseq, n_conv_slabs, CONV_HIST, LANES), F32)
        abuf[:, :, 0:SCAN_PAD, :] = jnp.zeros((nseq, n_chunks, SCAN_PAD, LANES), F32)
        ht_ref[...] = jnp.zeros_like(ht_ref)

    lo_half = lax.broadcasted_iota(jnp.int32, (tc, LANES), 1) < POOL_GW
    pos1 = c * tc + lax.broadcasted_iota(jnp.int32, (tc, LANES), 0) + 1
    a_neg = -jnp.exp(alog_ref[...])
    row = lax.broadcasted_iota(jnp.int32, (CHUNK, CHUNK), 0)
    col = lax.broadcasted_iota(jnp.int32, (CHUNK, CHUNK), 1)
    causal = row >= col
    lo = lax.broadcasted_iota(jnp.int32, (CHUNK, LANES), 1) < SSM_HEADDIM
    mask_lo = jnp.where(lo, 1.0, 0.0).astype(BF16)
    mask_hi = jnp.where(lo, 0.0, 1.0).astype(BF16)
    gw = SSM_HPG * SSM_HEADDIM
    n_x = D_SSM // LANES
    st8 = {}

    def head(i):
        x = x_ref[i]
        xn = _rms(x, g_ref[...]).astype(BF16)
        st8[i] = dict(x=x, xn=xn, xbc=[None] * n_conv_slabs)

    def conv_block(i, s0, ns):
        xn = st8[i]["xn"]
        raw = _dot(xn, wx_s[:, s0 * LANES:(s0 + ns) * LANES])
        for t in range(ns):
            s = s0 + t
            cbuf[i, s, CONV_HIST:CONV_HIST + tc, :] = raw[:, slab(t)]
            acc = cbuf[i, s, CONV_HIST:CONV_HIST + tc, :] * cw_ref[CONV_W - 1:CONV_W, slab(s)]
            for j in range(1, CONV_W):
                acc = acc + (cbuf[i, s, CONV_HIST - j:CONV_HIST - j + tc, :]
                             * cw_ref[CONV_W - 1 - j:CONV_W - j, slab(s)])
            st8[i]["xbc"][s] = _silu(acc + cb_ref[:, slab(s)])

    def dt_proj(i):
        st8[i]["dt"] = jax.nn.softplus(_dot(st8[i]["xn"], wdt_s[...]) + dtb_ref[...])

    def pool(i):
        u = _dot(st8[i]["xn"], wu_s[...])
        for s in range(n_pool_slabs):
            ubuf[i, s, POOL_HIST:rp, :] = u[:, slab(s)]
            s2buf[i, s, 8:rp, :] = ubuf[i, s, 8:rp, :] + ubuf[i, s, 7:rp - 1, :]
        s4buf[i, 16:rp, :] = s2buf[i, 1, 16:rp, :] + s2buf[i, 1, 14:rp - 2, :]
        s8buf[i, 24:rp, :] = s4buf[i, 24:rp, :] + s4buf[i, 20:rp - 4, :]
        win_sums = [
            jnp.where(lo_half, s2buf[i, 0, POOL_HIST:rp, :],
                      s2buf[i, 0, POOL_HIST:rp, :] + s2buf[i, 0, POOL_HIST - 2:rp - 2, :]),
            jnp.where(lo_half, s8buf[i, POOL_HIST:rp, :],
                      s8buf[i, POOL_HIST:rp, :] + s8buf[i, POOL_HIST - 8:rp - 8, :]),
        ]
        ps = []
        for s in range(n_pool_slabs):
            win = jnp.where(lo_half, POOL_WINDOWS[2 * s], POOL_WINDOWS[2 * s + 1])
            cnt = jnp.minimum(pos1, win).astype(F32)
            ps.append(win_sums[s] / cnt - u[:, slab(s)])
        st8[i]["pool_out"] = _dot(jnp.concatenate(ps, axis=1).astype(BF16), wp_ref[...]) * ps_ref[...]

    def ssd_chunk(i, j):
        xbc = st8[i]["xbc"]
        sl = slice(j * CHUNK, (j + 1) * CHUNK)
        dt_c = st8[i]["dt"][sl]
        acs = dt_c * a_neg
        sh = 1
        while sh < CHUNK:
            abuf[i, j, SCAN_PAD:SCAN_PAD + CHUNK, :] = acs
            acs = acs + abuf[i, j, SCAN_PAD - sh:SCAN_PAD - sh + CHUNK, :]
            sh *= 2
        acs_last = acs[CHUNK - 1:CHUNK, :]
        fdec = jnp.exp(acs_last - acs) * dt_c
        src_t = (acs - jnp.log(dt_c)).T
        ht = ht_ref[i]
        ht_b = ht.astype(BF16)
        cdec, st = [], []
        for g in range(SSM_GROUPS):
            c_g = xbc[n_x + SSM_GROUPS + g][sl].astype(BF16)
            b_g = xbc[n_x + g][sl].astype(BF16)
            scores = _dot_nt(c_g, b_g)
            y_off = _dot(c_g, ht_b[:, g * gw:(g + 1) * gw])
            xw = []
            for q in range(SSM_HPG // 2):
                k0 = g * SSM_HPG + 2 * q
                pair = k0 // 2
                a_cols = [_col_bcast(acs, k, CHUNK) for k in (k0, k0 + 1)]
                lhs = [(scores * jnp.exp(jnp.where(causal, a_cols[h] - src_t[k0 + h:k0 + h + 1, :], -jnp.inf))
                        ).astype(BF16) for h in range(2)]
                x_pair = xbc[pair][sl]
                x_b = x_pair.astype(BF16)
                rhs = jnp.concatenate([x_b * mask_lo, x_b * mask_hi], axis=0)
                e_pair = jnp.exp(jnp.where(lo, a_cols[0], a_cols[1]))
                ybuf[i, sl, slab(pair)] = (_dot(jnp.concatenate(lhs, axis=1), rhs)
                                           + y_off[:, slab(q)] * e_pair + x_pair * dsk_ref[:, slab(pair)])
                f_pair = jnp.where(lo, _col_bcast(fdec, k0, LANES), _col_bcast(fdec, k0 + 1, LANES))
                xw.append((x_pair * f_pair).astype(BF16))
                cdec.append(e_pair[CHUNK - 1:CHUNK, :])
            st.append(_dot_tn(b_g, jnp.concatenate(xw, axis=1)))
        ht_ref[i] = ht * jnp.concatenate(cdec, axis=1) + jnp.concatenate(st, axis=1)

    def z_proj(i):
        st8[i]["zs"] = _silu(_dot(st8[i]["xn"], wz_s[...]))

    def tail(i):
        d = st8[i]
        yn = _rms(ybuf[i] * d["zs"], sn_ref[...])
        cat = jnp.concatenate([d["pool_out"], yn], axis=-1).astype(BF16)
        o_ref[i] = d["x"] + _dot(cat, wo_s[...])

    def mid(i):
        conv_block(i, n_x, 2 * SSM_GROUPS)
        dt_proj(i)
        for s0 in range(0, n_x, 2):
            conv_block(i, s0, 2)

    for i in seqs:
        head(i)
    for i in seqs:
        mid(i)
    for i in seqs:
        pool(i)
    for j in range(n_chunks):
        for i in seqs:
            ssd_chunk(i, j)
        if j == 0:
            for i in seqs:
                z_proj(i)
    for i in seqs:
        tail(i)

    ubuf[:, :, 0:POOL_HIST, :] = ubuf[:, :, tc:tc + POOL_HIST, :]
    cbuf[:, :, 0:CONV_HIST, :] = cbuf[:, :, tc:tc + CONV_HIST, :]

    @pl.when(c == nc - 1)
    def _():
        for i in seqs:
            for s in range(n_pool_slabs):
                opool_ref[i, :, slab(s)] = ubuf[i, s, POOL_HIST - POOL_BUF:POOL_HIST, :]
            for s in range(n_conv_slabs):
                oconv_ref[i, :, slab(s)] = cbuf[i, s, CONV_HIST - (CONV_W - 1):CONV_HIST, :]
            ossm_ref[i] = ht_ref[i].T.reshape(SSM_HEADS, SSM_HEADDIM, SSM_STATE)


def _mix_prompt(x, w, *, tc, nseq):
    b, s, d = x.shape
    assert s % tc == 0 and tc % CHUNK == 0 and b % nseq == 0
    consts = [w["g_mix"], w["w_in"], w["conv_w"], w["conv_b"],
              w["dt_bias"], w["a_log"], w["d_skip"], w["ssm_norm"], w["w_pool"], w["pool_scale"],
              w["w_out"]]
    tile = pl.BlockSpec((nseq, tc, d), lambda i, j: (i, j, 0))
    return pl.pallas_call(
        functools.partial(_mix_prompt_kernel, tc=tc, nseq=nseq),
        grid=(b // nseq, s // tc),
        in_specs=[tile] + [_const_spec(a.shape) for a in consts],
        out_specs=[
            tile,
            pl.BlockSpec((nseq, POOL_BUF, D_POOL), lambda i, j: (i, 0, 0)),
            pl.BlockSpec((nseq, CONV_W - 1, CONV_DIM), lambda i, j: (i, 0, 0)),
            pl.BlockSpec((nseq, SSM_HEADS, SSM_HEADDIM, SSM_STATE), lambda i, j: (i, 0, 0, 0)),
            pl.BlockSpec((d, D_PROJ + LANES), lambda i, j: (0, 0)),
        ],
        out_shape=[
            jax.ShapeDtypeStruct((b, s, d), F32),
            jax.ShapeDtypeStruct((b, POOL_BUF, D_POOL), F32),
            jax.ShapeDtypeStruct((b, CONV_W - 1, CONV_DIM), F32),
            jax.ShapeDtypeStruct((b, SSM_HEADS, SSM_HEADDIM, SSM_STATE), F32),
            jax.ShapeDtypeStruct((d, D_PROJ + LANES), BF16),
        ],
        scratch_shapes=[
            pltpu.VMEM((d, D_POOL), BF16), pltpu.VMEM((d, D_SSM), BF16),
            pltpu.VMEM((d, CONV_DIM), BF16), pltpu.VMEM((d, LANES), BF16), pltpu.VMEM((d, d), BF16),
            pltpu.VMEM((nseq, D_POOL // LANES, POOL_HIST + tc, LANES), F32),
            pltpu.VMEM((nseq, D_POOL // LANES, POOL_HIST + tc, LANES), F32),
            pltpu.VMEM((nseq, POOL_HIST + tc, LANES), F32),
            pltpu.VMEM((nseq, POOL_HIST + tc, LANES), F32),
            pltpu.VMEM((nseq, CONV_DIM // LANES, CONV_HIST + tc, LANES), F32),
            pltpu.VMEM((nseq, tc // CHUNK, SCAN_PAD + CHUNK, LANES), F32),
            pltpu.VMEM((nseq, SSM_STATE, D_SSM), F32),
            pltpu.VMEM((nseq, tc, D_SSM), F32),
        ],
        compiler_params=_cparams(("arbitrary", "arbitrary")),
        name="mix_prompt",
    )(x, *consts)


def _attn_kernel(x_ref, g_ref, wq_ref, k_ref, v_ref, wo_ref, qs_ref, ks_ref, vs_ref,
                 o_ref, os_ref, wq_s, wo_s, *, bb):
    @pl.when(jnp.logical_and(pl.program_id(0) == 0, pl.program_id(1) == 0))
    def _():
        wq_s[...] = wq_ref[...].astype(BF16)
        wo_s[...] = wo_ref[...].astype(BF16)

    hs = [slice(h * MEM_HD, (h + 1) * MEM_HD) for h in range(MEM_HEADS)]
    scale = MEM_HD ** -0.5
    nq = qs_ref.shape[1]
    rows = N_MEM * MEM_HEADS

    def softmax(s):
        e = jnp.exp(s - jnp.max(s, axis=-1, keepdims=True))
        return (e / jnp.sum(e, axis=-1, keepdims=True)).astype(BF16)

    q_head = lax.broadcasted_iota(jnp.int32, (nq, rows), 0) % MEM_HEADS
    kv_head = lax.broadcasted_iota(jnp.int32, (nq, rows), 1) % MEM_HEADS
    same_head = q_head == kv_head

    q_all = qs_ref[...].reshape(bb * nq, MEM_HD)
    q_all = jnp.concatenate([q_all, jnp.zeros((LANES - bb * nq, MEM_HD), F32)], axis=0).astype(BF16)
    s_s = [_dot_nt(ks_ref[i].reshape(rows, MEM_HD).astype(BF16), q_all) for i in range(bb)]

    x = x_ref[...]
    q = _dot(_rms(x, g_ref[...]).astype(BF16), wq_s[...]).astype(BF16)

    p_s = [softmax(jnp.where(same_head, s_s[i].T[i * nq:(i + 1) * nq] * scale, -jnp.inf)) for i in range(bb)]

    s_p = [_dot_nt(q[:, hs[h]], k_ref[:, hs[h]]) * scale for h in range(MEM_HEADS)]

    for i in range(bb):
        os_ref[i] = _dot(p_s[i], vs_ref[i].reshape(rows, MEM_HD).astype(BF16))

    p_p = [softmax(s) for s in s_p]
    o = jnp.concatenate([_dot(p_p[h], v_ref[:, hs[h]]) for h in range(MEM_HEADS)], axis=-1).astype(BF16)
    o_ref[...] = x + _dot(o, wo_s[...])


def _attn(x, k, v, g, wq, wo, qs, ks, vs, *, tq):
    b, s, d = x.shape
    nb, nq, hd = qs.shape
    nj = s // tq
    assert nb % (b * nj) == 0
    bb = nb // (b * nj)
    assert bb * nq <= LANES
    tile = pl.BlockSpec((None, tq, d), lambda i, j: (i, j, 0))
    mem = pl.BlockSpec((None, N_MEM, d), lambda i, j: (i, 0, 0))
    s_q = pl.BlockSpec((bb, nq, hd), lambda i, j: (i * nj + j, 0, 0))
    s_mem = pl.BlockSpec((bb, N_MEM, MEM_HEADS, hd), lambda i, j: (i * nj + j, 0, 0, 0))
    return pl.pallas_call(
        functools.partial(_attn_kernel, bb=bb),
        grid=(b, nj),
        in_specs=[tile, _const_spec(g.shape), _const_spec(wq.shape), mem, mem, _const_spec(wo.shape),
                  s_q, s_mem, s_mem],
        out_specs=[tile, s_q],
        out_shape=[jax.ShapeDtypeStruct((b, s, d), F32), jax.ShapeDtypeStruct((nb, nq, hd), F32)],
        scratch_shapes=[pltpu.VMEM(wq.shape, BF16), pltpu.VMEM(wo.shape, BF16)],
        compiler_params=_cparams(("arbitrary", "arbitrary")),
        name="attn",
    )(x, g, wq, k, v, wo, qs, ks, vs)


def _mix_sample_kernel(x_ref, sp_ref, sc_ref, h0_ref, g_ref, win_ref, cw_ref, cb_ref, dtb_ref, alog_ref,
                       dsk_ref, wp_ref, ps_ref, ex_ref, sn_ref, wo_ref, gq_ref, wq_ref,
                       x2_ref, q_ref, npool_ref, nconv_ref, hout_ref,
                       pool_s, z_s, ydx_s, expa_s, xw_s, b_s, c_s, dec_s, y_s, *, nb, t, bb, start_pos):
    j = pl.program_id(0)
    gw = SSM_HPG * SSM_HEADDIM
    n_bc = SSM_GROUPS * SSM_STATE

    def rows(i):
        return slice(i * nb, (i + 1) * nb)

    @pl.when(j == 0)
    def _():
        x = x_ref[...]
        xn = _rms(x, g_ref[...]).astype(BF16)
        z_s[...] = _dot(xn, win_ref[:, D_POOL:D_POOL + D_SSM])

        u = _dot(xn, win_ref[:, 0:D_POOL])
        ext = [sp_ref[rows(i), :] for i in range(POOL_BUF)] + [u[rows(i)] for i in range(t)]
        ps = []
        for i in range(t):
            sums, cnts = [], []
            for w in POOL_WINDOWS:
                s = ext[POOL_BUF + i]
                for k in range(1, w):
                    s = s + ext[POOL_BUF + i - k]
                sums.append(s)
                cnts.append(jnp.full((nb, D_POOL), float(min(start_pos + i + 1, w)), F32))
            mean = _lane_group_select(sums, POOL_GW) / _lane_group_select(cnts, POOL_GW)
            ps.append(mean - ext[POOL_BUF + i])
        p = jnp.concatenate(ps, axis=0).astype(BF16)
        pool_s[...] = _dot(p, wp_ref[...]) * ps_ref[...]
        for i in range(POOL_BUF):
            npool_ref[rows(i), :] = ext[t + i]

        xr = _dot(xn, win_ref[:, D_POOL + D_SSM:D_PROJ])
        cext = [sc_ref[rows(i), :] for i in range(CONV_W - 1)] + [xr[rows(i)] for i in range(t)]
        xbc = []
        for i in range(t):
            acc = cext[i] * cw_ref[0:1, :]
            for k in range(1, CONV_W):
                acc = acc + cext[i + k] * cw_ref[k:k + 1, :]
            xbc.append(_silu(acc + cb_ref[...]))
        for i in range(CONV_W - 1):
            nconv_ref[rows(i), :] = cext[t + i]
        xs = [v[:, :D_SSM] for v in xbc]
        bm = [v[:, D_SSM:D_SSM + n_bc] for v in xbc]
        cm = [v[:, D_SSM + n_bc:] for v in xbc]
        for i in range(t):
            b_s[rows(i), :] = bm[i]
            c_s[rows(i), :] = cm[i]

        dt_all = jax.nn.softplus(_dot(xn, win_ref[:, D_PROJ:D_PROJ + LANES]) + dtb_ref[...])
        a_neg = -jnp.exp(alog_ref[...])
        dt = [dt_all[rows(i)] for i in range(t)]
        acs = []
        for i in range(t):
            da = dt[i] * a_neg
            acs.append(da if i == 0 else acs[-1] + da)
        dec = jnp.exp(acs[-1])
        for k in range(SSM_HEADS):
            dec_s[k] = _col_bcast(dec, k, LANES)

        def expand(v, exact):
            if exact:
                return jnp.dot(v, ex_ref[...], precision=lax.Precision.HIGHEST,
                               preferred_element_type=F32)
            return _dot(v.astype(BF16), ex_ref[...].astype(BF16))

        lane = lax.broadcasted_iota(jnp.int32, (nb, LANES), 1)
        for i in range(t):
            ydx = xs[i] * dsk_ref[...]
            for s in range(i + 1):
                sc = [jnp.sum(cm[i][:, g * SSM_STATE:(g + 1) * SSM_STATE]
                              * bm[s][:, g * SSM_STATE:(g + 1) * SSM_STATE], axis=-1, keepdims=True)
                      for g in range(SSM_GROUPS)]
                sc = jnp.where(lane < SSM_HPG, sc[0], sc[1])
                coef = sc * jnp.exp(acs[i] - acs[s]) * dt[s]
                ydx = ydx + expand(coef, False) * xs[s]
            ydx_s[rows(i), :] = ydx
            expa_s[rows(i), :] = expand(jnp.exp(acs[i]), True)
            xw_s[rows(i), :] = xs[i] * expand(jnp.exp(acs[-1] - acs[i]) * dt[i], True)

    base = pl.multiple_of(j * bb, bb)
    blk_rows = [pl.ds(i * nb + base, bb) for i in range(t)]

    def gather(ref):
        return jnp.concatenate([ref[r, :] for r in blk_rows], axis=0)

    def pad_rows(v):
        return jnp.concatenate([v, jnp.zeros((LANES - t * bb, v.shape[1]), v.dtype)], axis=0)

    c_blk = pad_rows(gather(c_s))
    b_blk = pad_rows(gather(b_s))
    xw_t = pad_rows(gather(xw_s)).T.astype(BF16)
    row_seq = lax.broadcasted_iota(jnp.int32, (LANES, SSM_STATE), 0) % bb
    col_seq = lax.broadcasted_iota(jnp.int32, (SSM_STATE, LANES), 1) % bb
    y_t = []
    for g in range(SSM_GROUPS):
        gs = slice(g * SSM_STATE, (g + 1) * SSM_STATE)
        c_t = c_blk[:, gs].T
        acc = jnp.zeros((gw, LANES), F32)
        for i in range(bb):
            h0 = h0_ref[i, g * SSM_HPG:(g + 1) * SSM_HPG].reshape(gw, SSM_STATE)
            acc = acc + _dot(h0.astype(BF16), jnp.where(col_seq == i, c_t, 0.0).astype(BF16))
            b_i = jnp.where(row_seq == i, b_blk[:, gs], 0.0).astype(BF16)
            st = _dot(xw_t[g * gw:(g + 1) * gw, :], b_i)
            for k in range(SSM_HPG):
                hd = g * SSM_HPG + k
                hout_ref[i, hd] = (h0_ref[i, hd] * dec_s[hd, pl.ds(base + i, 1), :]
                                   + st[k * SSM_HEADDIM:(k + 1) * SSM_HEADDIM])
        y_t.append(acc.T[:t * bb])
    y_blk = gather(ydx_s) + jnp.concatenate(y_t, axis=1) * gather(expa_s)
    for i in range(t):
        y_s[blk_rows[i], :] = y_blk[i * bb:(i + 1) * bb]

    @pl.when(j == pl.num_programs(0) - 1)
    def _():
        x2 = _gate_norm_out(y_s[...], z_s[...], pool_s[...], x_ref[...], sn_ref[...],
                            wo_ref[...].astype(BF16))
        x2_ref[...] = x2
        q_ref[...] = _dot(_rms(x2, gq_ref[...]).astype(BF16), wq_ref[...].astype(BF16))


def _mix_sample(x, sp, sc, h0, w_in_bf, w, *, nb, t, bb, start_pos):
    n, d = x.shape
    assert n == nb * t and nb % bb == 0 and t * bb <= LANES
    consts = [w["g_mix"], w_in_bf, w["conv_w"], w["conv_b"], w["dt_bias"], w["a_log"], w["d_skip"],
              w["w_pool"], w["pool_scale"], w["expand"], w["ssm_norm"], w["w_out"], w["g_xq"], w["w_xq"]]
    hblk = pl.BlockSpec((bb, SSM_HEADS, SSM_HEADDIM, SSM_STATE), lambda i: (i, 0, 0, 0))
    whole = lambda shape: pl.BlockSpec(shape, lambda i: (0,) * len(shape))
    outs = [
        jax.ShapeDtypeStruct((n, d), F32),
        jax.ShapeDtypeStruct((n, d), F32),
        jax.ShapeDtypeStruct((POOL_BUF * nb, D_POOL), F32),
        jax.ShapeDtypeStruct(((CONV_W - 1) * nb, CONV_DIM), F32),
        jax.ShapeDtypeStruct(h0.shape, F32),
    ]
    n_bc = SSM_GROUPS * SSM_STATE
    return pl.pallas_call(
        functools.partial(_mix_sample_kernel, nb=nb, t=t, bb=bb, start_pos=start_pos),
        grid=(nb // bb,),
        in_specs=[_const_spec(x.shape), _const_spec(sp.shape), _const_spec(sc.shape), hblk]
        + [_const_spec(a.shape) for a in consts],
        out_specs=[whole(o.shape) for o in outs[:4]] + [hblk],
        out_shape=outs,
        scratch_shapes=[
            pltpu.VMEM((n, D_POOL), F32),
            pltpu.VMEM((n, D_SSM), F32),
            pltpu.VMEM((n, D_SSM), F32),
            pltpu.VMEM((n, D_SSM), F32),
            pltpu.VMEM((n, D_SSM), F32),
            pltpu.VMEM((n, n_bc), F32),
            pltpu.VMEM((n, n_bc), F32),
            pltpu.VMEM((SSM_HEADS, nb, LANES), F32),
            pltpu.VMEM((n, D_SSM), F32),
        ],
        compiler_params=_cparams(("arbitrary",)),
        name="mix_sample",
    )(x, sp, sc, h0, *consts)


def _oproj_kernel(x_ref, o_ref, wo_ref, out_ref):
    out_ref[...] = x_ref[...] + _dot(o_ref[...].astype(BF16), wo_ref[...].astype(BF16))


def _oproj(x, o, wo):
    ins = [x, o, wo]
    return pl.pallas_call(
        _oproj_kernel,
        grid=(1,),
        in_specs=[_const_spec(a.shape) for a in ins],
        out_specs=pl.BlockSpec(x.shape, lambda i: (0, 0)),
        out_shape=jax.ShapeDtypeStruct(x.shape, F32),
        compiler_params=_cparams(("arbitrary",)),
        name="oproj",
    )(*ins)


def _prep_weights(g_ffn1, w1_gate, w1_up, w1_down, g_mix, w_in, conv_w, conv_b, dt_bias, a_log,
                  d_skip, ssm_norm, w_pool, pool_scale, w_out, g_mem, w_mem_k, w_mem_v, g_xq,
                  w_xq, w_xo, g_ffn2, w2_gate, w2_up, w2_down, g_final):
    row = lambda v: v.reshape(1, -1).astype(F32)
    pad_heads = lambda v: jnp.pad(v.astype(F32), (0, LANES - SSM_HEADS)).reshape(1, LANES)
    w_pool_bd = jnp.zeros((D_POOL, D_POOL), F32)
    for g in range(len(POOL_WINDOWS)):
        w_pool_bd = w_pool_bd.at[g * POOL_GW:(g + 1) * POOL_GW, g * POOL_GW:(g + 1) * POOL_GW].set(w_pool[g])
    head_of_chan = jnp.arange(D_SSM) // SSM_HEADDIM
    expand = (jnp.arange(LANES)[:, None] == head_of_chan[None, :]).astype(F32)
    return dict(
        g_ffn1=row(g_ffn1), w1_gate=w1_gate, w1_up=w1_up, w1_down=w1_down,
        g_mix=row(g_mix), w_in=jnp.pad(jnp.transpose(w_in), ((0, LANES - SSM_HEADS), (0, 0))),
        conv_w=conv_w.astype(F32), conv_b=row(conv_b), dt_bias=pad_heads(dt_bias),
        a_log=pad_heads(a_log), d_skip=row(jnp.repeat(d_skip, SSM_HEADDIM)), ssm_norm=row(ssm_norm),
        w_pool=w_pool_bd.astype(BF16), pool_scale=row(pool_scale), w_out=w_out,
        g_mem=row(g_mem), w_mem_k=w_mem_k, w_mem_v=w_mem_v,
        g_xq=row(g_xq), w_xq=w_xq, w_xo=w_xo,
        g_ffn2=row(g_ffn2), w2_gate=w2_gate, w2_up=w2_up, w2_down=w2_down,
        g_final=row(g_final), expand=expand,
    )


def _layer(x_prompt, x_sample, mem_prompt, mem_k, mem_v, state_pool, state_conv, state_ssm, start_pos, w):
    b, s, d = x_prompt.shape
    nb, t, _ = x_sample.shape
    n = nb * t
    tmaj = lambda a: jnp.swapaxes(a, 0, 1).reshape(-1, a.shape[-1])
    bmaj = lambda a, r: jnp.swapaxes(a.reshape(r, nb, -1), 0, 1)

    k, v, k_b, v_b = _memkv(mem_prompt, w["g_mem"], w["w_mem_k"], w["w_mem_v"])
    xp, xs = _ffn(x_prompt.reshape(b * s, d), tmaj(x_sample), w["g_ffn1"], w["w1_gate"], w["w1_up"],
                  w["w1_down"], w["g_final"], final=False, tm=1024)

    xp, pool_p, conv_p, ssm_p, w_in_bf = _mix_prompt(xp.reshape(b, s, d), w, tc=256, nseq=2)

    x2, q, new_pool, new_conv, ssm_s = _mix_sample(xs, tmaj(state_pool), tmaj(state_conv), state_ssm,
                                                   w_in_bf, w, nb=nb, t=t, bb=8, start_pos=start_pos)

    xp, o = _attn(xp, k_b, v_b, w["g_xq"], w["w_xq"], w["w_xo"],
                  bmaj(q, t).reshape(nb, t * MEM_HEADS, MEM_HD), mem_k, mem_v, tq=512)
    xs = _oproj(bmaj(x2, t).reshape(n, d), o.reshape(n, d), w["w_xo"])

    yp, ys = _ffn(xp.reshape(b * s, d), xs, w["g_ffn2"], w["w2_gate"], w["w2_up"], w["w2_down"],
                  w["g_final"], final=True, tm=1024)
    return (yp.reshape(b, s, d), ys.reshape(nb, t, d), k, v, pool_p, conv_p, ssm_p,
            bmaj(new_pool, POOL_BUF), bmaj(new_conv, CONV_W - 1), ssm_s)


def kernel(x_prompt, x_sample, mem_prompt, cache_mem_k, cache_mem_v, state_pool, state_conv, state_ssm,
           g_ffn1, w1_gate, w1_up, w1_down, g_mix, w_in, conv_w, conv_b, dt_bias, a_log, d_skip,
           ssm_norm, w_pool, pool_scale, w_out, g_mem, w_mem_k, w_mem_v, g_xq, w_xq, w_xo,
           g_ffn2, w2_gate, w2_up, w2_down, g_final):
    assert g_ffn1.shape[0] == 1, "single-layer model"
    past_len = 16384
    w = _prep_weights(g_ffn1[0], w1_gate[0], w1_up[0], w1_down[0], g_mix[0], w_in[0], conv_w[0],
                      conv_b[0], dt_bias[0], a_log[0], d_skip[0], ssm_norm[0], w_pool[0],
                      pool_scale[0], w_out[0], g_mem[0], w_mem_k[0], w_mem_v[0], g_xq[0], w_xq[0],
                      w_xo[0], g_ffn2[0], w2_gate[0], w2_up[0], w2_down[0], g_final)
    bp = x_prompt.shape[0]
    y_p, y_s, k_p, v_p, pool_p, conv_p, ssm_p, pool_s, conv_s, ssm_s = _layer(
        x_prompt, x_sample, mem_prompt, cache_mem_k[0], cache_mem_v[0], state_pool[0], state_conv[0],
        state_ssm[0], past_len, w)
    kv_shape = (1, bp, N_MEM, MEM_HEADS, MEM_HD)
    return (y_p, y_s, k_p.reshape(kv_shape), v_p.reshape(kv_shape), pool_p[None], conv_p[None],
            ssm_p[None], pool_s[None], conv_s[None], ssm_s[None])
```

```python
import functools

import jax
import jax.numpy as jnp
from jax import lax
from jax.experimental import pallas as pl
from jax.experimental.pallas import tpu as pltpu

F32 = jnp.float32
BF16 = jnp.bfloat16

D_MODEL = 1024
D_POOL = 256
POOL_WINDOWS = (2, 4, 8, 16)
POOL_GW = 64
POOL_BUF = 15
D_SSM = 768
SSM_HEADDIM = 64
SSM_HEADS = 12
SSM_GROUPS = 2
SSM_HPG = 6
SSM_STATE = 128
CONV_W = 4
CONV_DIM = 1280
D_PROJ = D_POOL + D_SSM + CONV_DIM
N_MEM = 256
MEM_HEADS = 4
MEM_HD = 256
D_FF = 2816
EPS = 1e-6
LANES = 128
CHUNK = 128
POOL_HIST = 32
CONV_HIST = 8
SCAN_PAD = CHUNK // 2
VMEM_LIMIT = 56 * 1024 * 1024
VMEM_LIMIT_FFN = 60 * 1024 * 1024


def _cparams(sem, vmem_limit=VMEM_LIMIT):
    return pltpu.CompilerParams(dimension_semantics=sem, vmem_limit_bytes=vmem_limit)


def _const_spec(shape):
    nd = len(shape)
    return pl.BlockSpec(shape, lambda *_: (0,) * nd, pipeline_mode=pl.Buffered(1))


def _rms(x, g):
    ms = jnp.mean(x * x, axis=-1, keepdims=True)
    return x * lax.rsqrt(ms + EPS) * g


def _silu(x):
    return x * jax.nn.sigmoid(x)


def _dot(a, b):
    return jnp.dot(a, b, preferred_element_type=F32)


def _dot_nt(a, b):
    return lax.dot_general(a, b, (((1,), (1,)), ((), ())), preferred_element_type=F32)


def _dot_tn(a, b):
    return lax.dot_general(a, b, (((0,), (0,)), ((), ())), preferred_element_type=F32)


FFN_W_STEPS = 16


def _ffn_kernel(xp_ref, xs_ref, g_ref, wg_ref, wu_ref, wd_ref, gf_ref, *rest, final, n_p, attn_out):
    if attn_out:
        ao_ref, wxo_ref, op_ref, os_ref, wg_s, wu_s, wd_s = rest
    else:
        op_ref, os_ref, wg_s, wu_s, wd_s = rest
    i = pl.program_id(0)
    rows_gu = wg_ref.shape[0]
    rows_d = wd_ref.shape[0]

    @pl.when(i < FFN_W_STEPS)
    def _():
        r = pl.multiple_of(i * rows_gu, rows_gu)
        wg_s[pl.ds(r, rows_gu), :] = wg_ref[...].astype(BF16)
        wu_s[pl.ds(r, rows_gu), :] = wu_ref[...].astype(BF16)
        r = pl.multiple_of(i * rows_d, rows_d)
        wd_s[pl.ds(r, rows_d), :] = wd_ref[...].astype(BF16)

    def tile(x, o_ref):
        xn = _rms(x, g_ref[...]).astype(BF16)
        gate = _dot(xn, wg_s[...])
        up = _dot(xn, wu_s[...])
        h = (_silu(gate) * up).astype(BF16)
        out = x + 0.5 * _dot(h, wd_s[...])
        if final:
            out = _rms(out, gf_ref[...])
        o_ref[...] = out

    @pl.when(jnp.logical_and(i >= FFN_W_STEPS, i < FFN_W_STEPS + n_p))
    def _():
        tile(xp_ref[...], op_ref)

    @pl.when(i == FFN_W_STEPS + n_p)
    def _():
        x = xs_ref[...]
        if attn_out:
            x = x + _dot(ao_ref[...].astype(BF16), wxo_ref[...])
        tile(x, os_ref)


def _ffn(xp, xs, g, wg, wu, wd, gf, *, final, tm, attn_out=None):
    tp, d = xp.shape
    dff = wg.shape[1]
    assert tp % tm == 0 and d % FFN_W_STEPS == 0 and dff % FFN_W_STEPS == 0
    n_p = tp // tm
    p_tile = pl.BlockSpec((tm, d), lambda i: (jnp.clip(i - FFN_W_STEPS, 0, n_p - 1), 0))
    s_tile = pl.BlockSpec(xs.shape, lambda i: (0, 0))
    w_chunk = lambda rows, cols: pl.BlockSpec((rows, cols), lambda i: (jnp.minimum(i, FFN_W_STEPS - 1), 0))
    extra = list(attn_out) if attn_out else []
    return pl.pallas_call(
        functools.partial(_ffn_kernel, final=final, n_p=n_p, attn_out=bool(attn_out)),
        grid=(FFN_W_STEPS + n_p + 1,),
        in_specs=[
            p_tile, _const_spec(xs.shape), _const_spec(g.shape),
            w_chunk(d // FFN_W_STEPS, dff), w_chunk(d // FFN_W_STEPS, dff), w_chunk(dff // FFN_W_STEPS, d),
            _const_spec(gf.shape),
        ] + [_const_spec(a.shape) for a in extra],
        out_specs=[p_tile, s_tile],
        out_shape=[jax.ShapeDtypeStruct(xp.shape, F32), jax.ShapeDtypeStruct(xs.shape, F32)],
        scratch_shapes=[pltpu.VMEM((d, dff), BF16), pltpu.VMEM((d, dff), BF16), pltpu.VMEM((dff, d), BF16)],
        compiler_params=_cparams(("arbitrary",), VMEM_LIMIT_FFN),
        name="ffn_final" if final else "ffn",
    )(xp, xs, g, wg, wu, wd, gf, *extra)


def _memkv_kernel(m_ref, g_ref, wk_ref, wv_ref, k_ref, v_ref, kb_ref, vb_ref, wk_s, wv_s):
    @pl.when(pl.program_id(0) == 0)
    def _():
        wk_s[...] = wk_ref[...].astype(BF16)
        wv_s[...] = wv_ref[...].astype(BF16)

    mn = _rms(m_ref[...], g_ref[...]).astype(BF16)
    k = _dot(mn, wk_s[...])
    v = _dot(mn, wv_s[...])
    for h in range(MEM_HEADS):
        k_ref[:, h, :] = k[:, h * MEM_HD:(h + 1) * MEM_HD]
        v_ref[:, h, :] = v[:, h * MEM_HD:(h + 1) * MEM_HD]
    kb_ref[...] = k.astype(BF16)
    vb_ref[...] = v.astype(BF16)


def _memkv(mem, g, wk, wv):
    b, n_mem, d = mem.shape
    row_blk = pl.BlockSpec((None, n_mem, d), lambda i: (i, 0, 0))
    head_blk = pl.BlockSpec((None, n_mem, MEM_HEADS, MEM_HD), lambda i: (i, 0, 0, 0))
    return pl.pallas_call(
        _memkv_kernel,
        grid=(b,),
        in_specs=[row_blk, _const_spec(g.shape), _const_spec(wk.shape), _const_spec(wv.shape)],
        out_specs=[head_blk, head_blk, row_blk, row_blk],
        out_shape=[jax.ShapeDtypeStruct((b, n_mem, MEM_HEADS, MEM_HD), F32)] * 2
        + [jax.ShapeDtypeStruct((b, n_mem, d), BF16)] * 2,
        scratch_shapes=[pltpu.VMEM(wk.shape, BF16), pltpu.VMEM(wv.shape, BF16)],
        compiler_params=_cparams(("arbitrary",)),
        name="memkv",
    )(mem, g, wk, wv)


def _lane_group_select(vals, width):
    lane = lax.broadcasted_iota(jnp.int32, vals[0].shape, 1)
    out = vals[-1]
    for g in range(len(vals) - 2, -1, -1):
        out = jnp.where(lane < (g + 1) * width, vals[g], out)
    return out


def _col_bcast(m, k, width):
    return jnp.broadcast_to(m[:, k:k + 1], (m.shape[0], width))


def _gate_norm_out(y, z, pool_out, x_res, ssm_norm, w_out):
    yn = _rms(y * _silu(z), ssm_norm)
    cat = jnp.concatenate([pool_out, yn], axis=-1).astype(BF16)
    return x_res + _dot(cat, w_out)


def _mix_prompt_kernel(x_ref, g_ref, win_ref, cw_ref, cb_ref,
                       dtb_ref, alog_ref, dsk_ref, sn_ref, wp_ref, ps_ref, wout_ref,
                       o_ref, opool_ref, oconv_ref, ossm_ref, owin_ref,
                       wu_s, wz_s, wx_s, wdt_s, wo_s,
                       ubuf, s2buf, s4buf, s8buf, cbuf, abuf, ht_ref, ybuf, *, tc, nseq):
    c = pl.program_id(1)
    nc = pl.num_programs(1)
    n_pool_slabs = D_POOL // LANES
    n_conv_slabs = CONV_DIM // LANES
    n_chunks = tc // CHUNK
    rp = POOL_HIST + tc
    seqs = range(nseq)

    def slab(s):
        return slice(s * LANES, (s + 1) * LANES)

    @pl.when(jnp.logical_and(pl.program_id(0) == 0, c == 0))
    def _():
        wu_s[...] = win_ref[0:D_POOL, :].T.astype(BF16)
        wz_s[...] = win_ref[D_POOL:D_POOL + D_SSM, :].T.astype(BF16)
        wx_s[...] = win_ref[D_POOL + D_SSM:D_PROJ, :].T.astype(BF16)
        lead = 16 - SSM_HEADS
        place = (lax.broadcasted_iota(jnp.int32, (16, LANES), 0) - lead
                 == lax.broadcasted_iota(jnp.int32, (16, LANES), 1))
        wdt_s[...] = _dot_tn(win_ref[D_PROJ - lead:D_PROJ + SSM_HEADS, :].astype(BF16),
                             jnp.where(place, 1.0, 0.0).astype(BF16)).astype(BF16)
        wo_s[...] = wout_ref[...].astype(BF16)
        owin_ref[:, 0:D_POOL] = wu_s[...]
        owin_ref[:, D_POOL:D_POOL + D_SSM] = wz_s[...]
        owin_ref[:, D_POOL + D_SSM:D_PROJ] = wx_s[...]
        owin_ref[:, D_PROJ:D_PROJ + LANES] = wdt_s[...]

    @pl.when(c == 0)
    def _():
        ubuf[:, :, 0:POOL_HIST, :] = jnp.zeros((nseq, n_pool_slabs, POOL_HIST, LANES), F32)
        cbuf[:, :, 0:CONV_HIST, :] = jnp.zeros((nseq, n_conv_slabs, CONV_HIST, LANES), F32)
        abuf[:, :, 0:SCAN_PAD, :] = jnp.zeros((nseq, n_chunks, SCAN_PAD, LANES), F32)
        ht_ref[...] = jnp.zeros_like(ht_ref)

    lo_half = lax.broadcasted_iota(jnp.int32, (tc, LANES), 1) < POOL_GW
    pos1 = c * tc + lax.broadcasted_iota(jnp.int32, (tc, LANES), 0) + 1
    a_neg = -jnp.exp(alog_ref[...])
    row = lax.broadcasted_iota(jnp.int32, (CHUNK, CHUNK), 0)
    col = lax.broadcasted_iota(jnp.int32, (CHUNK, CHUNK), 1)
    causal = row >= col
    lo = lax.broadcasted_iota(jnp.int32, (CHUNK, LANES), 1) < SSM_HEADDIM
    mask_lo = jnp.where(lo, 1.0, 0.0).astype(BF16)
    mask_hi = jnp.where(lo, 0.0, 1.0).astype(BF16)
    gw = SSM_HPG * SSM_HEADDIM
    n_x = D_SSM // LANES
    st8 = {}

    def head(i):
        x = x_ref[i]
        xn = _rms(x, g_ref[...]).astype(BF16)
        st8[i] = dict(x=x, xn=xn, xbc=[None] * n_conv_slabs)

    def conv_block(i, s0, ns):
        xn = st8[i]["xn"]
        raw = _dot(xn, wx_s[:, s0 * LANES:(s0 + ns) * LANES])
        for t in range(ns):
            s = s0 + t
            cbuf[i, s, CONV_HIST:CONV_HIST + tc, :] = raw[:, slab(t)]
            acc = cbuf[i, s, CONV_HIST:CONV_HIST + tc, :] * cw_ref[CONV_W - 1:CONV_W, slab(s)]
            for j in range(1, CONV_W):
                acc = acc + (cbuf[i, s, CONV_HIST - j:CONV_HIST - j + tc, :]
                             * cw_ref[CONV_W - 1 - j:CONV_W - j, slab(s)])
            st8[i]["xbc"][s] = _silu(acc + cb_ref[:, slab(s)])

    def dt_proj(i):
        st8[i]["dt"] = jax.nn.softplus(_dot(st8[i]["xn"], wdt_s[...]) + dtb_ref[...])

    def pool(i):
        u = _dot(st8[i]["xn"], wu_s[...])
        for s in range(n_pool_slabs):
            ubuf[i, s, POOL_HIST:rp, :] = u[:, slab(s)]
            s2buf[i, s, 8:rp, :] = ubuf[i, s, 8:rp, :] + ubuf[i, s, 7:rp - 1, :]
        s4buf[i, 16:rp, :] = s2buf[i, 1, 16:rp, :] + s2buf[i, 1, 14:rp - 2, :]
        s8buf[i, 24:rp, :] = s4buf[i, 24:rp, :] + s4buf[i, 20:rp - 4, :]
        win_sums = [
            jnp.where(lo_half, s2buf[i, 0, POOL_HIST:rp, :],
                      s2buf[i, 0, POOL_HIST:rp, :] + s2buf[i, 0, POOL_HIST - 2:rp - 2, :]),
            jnp.where(lo_half, s8buf[i, POOL_HIST:rp, :],
                      s8buf[i, POOL_HIST:rp, :] + s8buf[i, POOL_HIST - 8:rp - 8, :]),
        ]
        ps = []
        for s in range(n_pool_slabs):
            win = jnp.where(lo_half, POOL_WINDOWS[2 * s], POOL_WINDOWS[2 * s + 1])
            cnt = jnp.minimum(pos1, win).astype(F32)
            ps.append(win_sums[s] / cnt - u[:, slab(s)])
        st8[i]["pool_out"] = _dot(jnp.concatenate(ps, axis=1).astype(BF16), wp_ref[...]) * ps_ref[...]

    def ssd_chunk(i, j):
        xbc = st8[i]["xbc"]
        sl = slice(j * CHUNK, (j + 1) * CHUNK)
        dt_c = st8[i]["dt"][sl]
        acs = dt_c * a_neg
        sh = 1
        while sh < CHUNK:
            abuf[i, j, SCAN_PAD:SCAN_PAD + CHUNK, :] = acs
            acs = acs + abuf[i, j, SCAN_PAD - sh:SCAN_PAD - sh + CHUNK, :]
            sh *= 2
        acs_last = acs[CHUNK - 1:CHUNK, :]
        fdec = jnp.exp(acs_last - acs) * dt_c
        src_t = (acs - jnp.log(dt_c)).T
        ht = ht_ref[i]
        ht_b = ht.astype(BF16)
        cdec, st = [], []
        for g in range(SSM_GROUPS):
            c_g = xbc[n_x + SSM_GROUPS + g][sl].astype(BF16)
            b_g = xbc[n_x + g][sl].astype(BF16)
            scores = _dot_nt(c_g, b_g)
            y_off = _dot(c_g, ht_b[:, g * gw:(g + 1) * gw])
            xw = []
            for q in range(SSM_HPG // 2):
                k0 = g * SSM_HPG + 2 * q
                pair = k0 // 2
                a_cols = [_col_bcast(acs, k, CHUNK) for k in (k0, k0 + 1)]
                lhs = [(scores * jnp.exp(jnp.where(causal, a_cols[h] - src_t[k0 + h:k0 + h + 1, :], -jnp.inf))
                        ).astype(BF16) for h in range(2)]
                x_pair = xbc[pair][sl]
                x_b = x_pair.astype(BF16)
                rhs = jnp.concatenate([x_b * mask_lo, x_b * mask_hi], axis=0)
                e_pair = jnp.exp(jnp.where(lo, a_cols[0], a_cols[1]))
                ybuf[i, sl, slab(pair)] = (_dot(jnp.concatenate(lhs, axis=1), rhs)
                                           + y_off[:, slab(q)] * e_pair + x_pair * dsk_ref[:, slab(pair)])
                f_pair = jnp.where(lo, _col_bcast(fdec, k0, LANES), _col_bcast(fdec, k0 + 1, LANES))
                xw.append((x_pair * f_pair).astype(BF16))
                cdec.append(e_pair[CHUNK - 1:CHUNK, :])
            st.append(_dot_tn(b_g, jnp.concatenate(xw, axis=1)))
        ht_ref[i] = ht * jnp.concatenate(cdec, axis=1) + jnp.concatenate(st, axis=1)

    def z_proj(i):
        st8[i]["zs"] = _silu(_dot(st8[i]["xn"], wz_s[...]))

    def tail(i):
        d = st8[i]
        yn = _rms(ybuf[i] * d["zs"], sn_ref[...])
        cat = jnp.concatenate([d["pool_out"], yn], axis=-1).astype(BF16)
        o_ref[i] = d["x"] + _dot(cat, wo_s[...])

    def mid(i):
        conv_block(i, n_x, 2 * SSM_GROUPS)
        dt_proj(i)
        for s0 in range(0, n_x, 2):
            conv_block(i, s0, 2)

    for i in seqs:
        head(i)
    for i in seqs:
        mid(i)
    for i in seqs:
        pool(i)
    for j in range(n_chunks):
        for i in seqs:
            ssd_chunk(i, j)
        if j == 0:
            for i in seqs:
                z_proj(i)
    for i in seqs:
        tail(i)

    ubuf[:, :, 0:POOL_HIST, :] = ubuf[:, :, tc:tc + POOL_HIST, :]
    cbuf[:, :, 0:CONV_HIST, :] = cbuf[:, :, tc:tc + CONV_HIST, :]

    @pl.when(c == nc - 1)
    def _():
        for i in seqs:
            for s in range(n_pool_slabs):
                opool_ref[i, :, slab(s)] = ubuf[i, s, POOL_HIST - POOL_BUF:POOL_HIST, :]
            for s in range(n_conv_slabs):
                oconv_ref[i, :, slab(s)] = cbuf[i, s, CONV_HIST - (CONV_W - 1):CONV_HIST, :]
            ossm_ref[i] = ht_ref[i].T.reshape(SSM_HEADS, SSM_HEADDIM, SSM_STATE)


def _mix_prompt(x, w, *, tc, nseq):
    b, s, d = x.shape
    assert s % tc == 0 and tc % CHUNK == 0 and b % nseq == 0
    consts = [w["g_mix"], w["w_in"], w["conv_w"], w["conv_b"],
              w["dt_bias"], w["a_log"], w["d_skip"], w["ssm_norm"], w["w_pool"], w["pool_scale"],
              w["w_out"]]
    tile = pl.BlockSpec((nseq, tc, d), lambda i, j: (i, j, 0))
    return pl.pallas_call(
        functools.partial(_mix_prompt_kernel, tc=tc, nseq=nseq),
        grid=(b // nseq, s // tc),
        in_specs=[tile] + [_const_spec(a.shape) for a in consts],
        out_specs=[
            tile,
            pl.BlockSpec((nseq, POOL_BUF, D_POOL), lambda i, j: (i, 0, 0)),
            pl.BlockSpec((nseq, CONV_W - 1, CONV_DIM), lambda i, j: (i, 0, 0)),
            pl.BlockSpec((nseq, SSM_HEADS, SSM_HEADDIM, SSM_STATE), lambda i, j: (i, 0, 0, 0)),
            pl.BlockSpec((d, D_PROJ + LANES), lambda i, j: (0, 0)),
        ],
        out_shape=[
            jax.ShapeDtypeStruct((b, s, d), F32),
            jax.ShapeDtypeStruct((b, POOL_BUF, D_POOL), F32),
            jax.ShapeDtypeStruct((b, CONV_W - 1, CONV_DIM), F32),
            jax.ShapeDtypeStruct((b, SSM_HEADS, SSM_HEADDIM, SSM_STATE), F32),
            jax.ShapeDtypeStruct((d, D_PROJ + LANES), BF16),
        ],
        scratch_shapes=[
            pltpu.VMEM((d, D_POOL), BF16), pltpu.VMEM((d, D_SSM), BF16),
            pltpu.VMEM((d, CONV_DIM), BF16), pltpu.VMEM((d, LANES), BF16), pltpu.VMEM((d, d), BF16),
            pltpu.VMEM((nseq, D_POOL // LANES, POOL_HIST + tc, LANES), F32),
            pltpu.VMEM((nseq, D_POOL // LANES, POOL_HIST + tc, LANES), F32),
            pltpu.VMEM((nseq, POOL_HIST + tc, LANES), F32),
            pltpu.VMEM((nseq, POOL_HIST + tc, LANES), F32),
            pltpu.VMEM((nseq, CONV_DIM // LANES, CONV_HIST + tc, LANES), F32),
            pltpu.VMEM((nseq, tc // CHUNK, SCAN_PAD + CHUNK, LANES), F32),
            pltpu.VMEM((nseq, SSM_STATE, D_SSM), F32),
            pltpu.VMEM((nseq, tc, D_SSM), F32),
        ],
        compiler_params=_cparams(("arbitrary", "arbitrary")),
        name="mix_prompt",
    )(x, *consts)


def _attn_kernel(x_ref, g_ref, wq_ref, k_ref, v_ref, wo_ref, qs_ref, ks_ref, vs_ref,
                 o_ref, os_ref, owo_ref, wq_s, wo_s, *, bb):
    @pl.when(jnp.logical_and(pl.program_id(0) == 0, pl.program_id(1) == 0))
    def _():
        wq_s[...] = wq_ref[...].astype(BF16)
        wo_s[...] = wo_ref[...].astype(BF16)
        owo_ref[...] = wo_s[...]

    hs = [slice(h * MEM_HD, (h + 1) * MEM_HD) for h in range(MEM_HEADS)]
    scale = MEM_HD ** -0.5
    nq = qs_ref.shape[1]
    rows = N_MEM * MEM_HEADS

    def softmax(s):
        e = jnp.exp(s - jnp.max(s, axis=-1, keepdims=True))
        return (e / jnp.sum(e, axis=-1, keepdims=True)).astype(BF16)

    q_head = lax.broadcasted_iota(jnp.int32, (nq, rows), 0) % MEM_HEADS
    kv_head = lax.broadcasted_iota(jnp.int32, (nq, rows), 1) % MEM_HEADS
    same_head = q_head == kv_head

    q_all = qs_ref[...].reshape(bb * nq, MEM_HD)
    q_all = jnp.concatenate([q_all, jnp.zeros((LANES - bb * nq, MEM_HD), F32)], axis=0).astype(BF16)
    s_s = [_dot_nt(ks_ref[i].reshape(rows, MEM_HD).astype(BF16), q_all) for i in range(bb)]

    x = x_ref[...]
    q = _dot(_rms(x, g_ref[...]).astype(BF16), wq_s[...]).astype(BF16)

    p_s = [softmax(jnp.where(same_head, s_s[i].T[i * nq:(i + 1) * nq] * scale, -jnp.inf)) for i in range(bb)]

    s_p = [_dot_nt(q[:, hs[h]], k_ref[:, hs[h]]) * scale for h in range(MEM_HEADS)]

    for i in range(bb):
        os_ref[i] = _dot(p_s[i], vs_ref[i].reshape(rows, MEM_HD).astype(BF16))

    p_p = [softmax(s) for s in s_p]
    o = jnp.concatenate([_dot(p_p[h], v_ref[:, hs[h]]) for h in range(MEM_HEADS)], axis=-1).astype(BF16)
    o_ref[...] = x + _dot(o, wo_s[...])


def _attn(x, k, v, g, wq, wo, qs, ks, vs, *, tq):
    b, s, d = x.shape
    nb, nq, hd = qs.shape
    nj = s // tq
    assert nb % (b * nj) == 0
    bb = nb // (b * nj)
    assert bb * nq <= LANES
    tile = pl.BlockSpec((None, tq, d), lambda i, j: (i, j, 0))
    mem = pl.BlockSpec((None, N_MEM, d), lambda i, j: (i, 0, 0))
    s_q = pl.BlockSpec((bb, nq, hd), lambda i, j: (i * nj + j, 0, 0))
    s_mem = pl.BlockSpec((bb, N_MEM, MEM_HEADS, hd), lambda i, j: (i * nj + j, 0, 0, 0))
    return pl.pallas_call(
        functools.partial(_attn_kernel, bb=bb),
        grid=(b, nj),
        in_specs=[tile, _const_spec(g.shape), _const_spec(wq.shape), mem, mem, _const_spec(wo.shape),
                  s_q, s_mem, s_mem],
        out_specs=[tile, s_q, pl.BlockSpec(wo.shape, lambda i, j: (0, 0))],
        out_shape=[jax.ShapeDtypeStruct((b, s, d), F32), jax.ShapeDtypeStruct((nb, nq, hd), F32),
                   jax.ShapeDtypeStruct(wo.shape, BF16)],
        scratch_shapes=[pltpu.VMEM(wq.shape, BF16), pltpu.VMEM(wo.shape, BF16)],
        compiler_params=_cparams(("arbitrary", "arbitrary")),
        name="attn",
    )(x, g, wq, k, v, wo, qs, ks, vs)


def _mix_sample_kernel(x_ref, sp_ref, sc_ref, h0_ref, g_ref, win_ref, cw_ref, cb_ref, dtb_ref, alog_ref,
                       dsk_ref, wp_ref, ps_ref, ex_ref, sn_ref, wo_ref, gq_ref, wq_ref,
                       x2_ref, q_ref, npool_ref, nconv_ref, hout_ref,
                       pool_s, z_s, ydx_s, expa_s, xw_s, b_s, c_s, dec_s, y_s, *, nb, t, bb, start_pos):
    j = pl.program_id(0)
    gw = SSM_HPG * SSM_HEADDIM
    n_bc = SSM_GROUPS * SSM_STATE

    def rows(i):
        return slice(i * nb, (i + 1) * nb)

    @pl.when(j == 0)
    def _():
        x = x_ref[...]
        xn = _rms(x, g_ref[...]).astype(BF16)
        z_s[...] = _dot(xn, win_ref[:, D_POOL:D_POOL + D_SSM])

        u = _dot(xn, win_ref[:, 0:D_POOL])
        ext = [sp_ref[rows(i), :] for i in range(POOL_BUF)] + [u[rows(i)] for i in range(t)]
        ps = []
        for i in range(t):
            sums, cnts = [], []
            for w in POOL_WINDOWS:
                s = ext[POOL_BUF + i]
                for k in range(1, w):
                    s = s + ext[POOL_BUF + i - k]
                sums.append(s)
                cnts.append(jnp.full((nb, D_POOL), float(min(start_pos + i + 1, w)), F32))
            mean = _lane_group_select(sums, POOL_GW) / _lane_group_select(cnts, POOL_GW)
            ps.append(mean - ext[POOL_BUF + i])
        p = jnp.concatenate(ps, axis=0).astype(BF16)
        pool_s[...] = _dot(p, wp_ref[...]) * ps_ref[...]
        for i in range(POOL_BUF):
            npool_ref[rows(i), :] = ext[t + i]

        xr = _dot(xn, win_ref[:, D_POOL + D_SSM:D_PROJ])
        cext = [sc_ref[rows(i), :] for i in range(CONV_W - 1)] + [xr[rows(i)] for i in range(t)]
        xbc = []
        for i in range(t):
            acc = cext[i] * cw_ref[0:1, :]
            for k in range(1, CONV_W):
                acc = acc + cext[i + k] * cw_ref[k:k + 1, :]
            xbc.append(_silu(acc + cb_ref[...]))
        for i in range(CONV_W - 1):
            nconv_ref[rows(i), :] = cext[t + i]
        xs = [v[:, :D_SSM] for v in xbc]
        bm = [v[:, D_SSM:D_SSM + n_bc] for v in xbc]
        cm = [v[:, D_SSM + n_bc:] for v in xbc]
        for i in range(t):
            b_s[rows(i), :] = bm[i]
            c_s[rows(i), :] = cm[i]

        dt_all = jax.nn.softplus(_dot(xn, win_ref[:, D_PROJ:D_PROJ + LANES]) + dtb_ref[...])
        a_neg = -jnp.exp(alog_ref[...])
        dt = [dt_all[rows(i)] for i in range(t)]
        acs = []
        for i in range(t):
            da = dt[i] * a_neg
            acs.append(da if i == 0 else acs[-1] + da)
        dec = jnp.exp(acs[-1])
        for k in range(SSM_HEADS):
            dec_s[k] = _col_bcast(dec, k, LANES)

        def expand(v, exact):
            if exact:
                return jnp.dot(v, ex_ref[...], precision=lax.Precision.HIGHEST,
                               preferred_element_type=F32)
            return _dot(v.astype(BF16), ex_ref[...].astype(BF16))

        lane = lax.broadcasted_iota(jnp.int32, (nb, LANES), 1)
        for i in range(t):
            ydx = xs[i] * dsk_ref[...]
            for s in range(i + 1):
                sc = [jnp.sum(cm[i][:, g * SSM_STATE:(g + 1) * SSM_STATE]
                              * bm[s][:, g * SSM_STATE:(g + 1) * SSM_STATE], axis=-1, keepdims=True)
                      for g in range(SSM_GROUPS)]
                sc = jnp.where(lane < SSM_HPG, sc[0], sc[1])
                coef = sc * jnp.exp(acs[i] - acs[s]) * dt[s]
                ydx = ydx + expand(coef, False) * xs[s]
            ydx_s[rows(i), :] = ydx
            expa_s[rows(i), :] = expand(jnp.exp(acs[i]), True)
            xw_s[rows(i), :] = xs[i] * expand(jnp.exp(acs[-1] - acs[i]) * dt[i], True)

    base = pl.multiple_of(j * bb, bb)
    blk_rows = [pl.ds(i * nb + base, bb) for i in range(t)]

    def gather(ref):
        return jnp.concatenate([ref[r, :] for r in blk_rows], axis=0)

    def pad_rows(v):
        return jnp.concatenate([v, jnp.zeros((LANES - t * bb, v.shape[1]), v.dtype)], axis=0)

    c_blk = pad_rows(gather(c_s))
    b_blk = pad_rows(gather(b_s))
    xw_t = pad_rows(gather(xw_s)).T.astype(BF16)
    row_seq = lax.broadcasted_iota(jnp.int32, (LANES, SSM_STATE), 0) % bb
    col_seq = lax.broadcasted_iota(jnp.int32, (SSM_STATE, LANES), 1) % bb
    y_t = []
    for g in range(SSM_GROUPS):
        gs = slice(g * SSM_STATE, (g + 1) * SSM_STATE)
        c_t = c_blk[:, gs].T
        acc = jnp.zeros((gw, LANES), F32)
        for i in range(bb):
            h0 = h0_ref[i, g * SSM_HPG:(g + 1) * SSM_HPG].reshape(gw, SSM_STATE)
            acc = acc + _dot(h0.astype(BF16), jnp.where(col_seq == i, c_t, 0.0).astype(BF16))
            b_i = jnp.where(row_seq == i, b_blk[:, gs], 0.0).astype(BF16)
            st = _dot(xw_t[g * gw:(g + 1) * gw, :], b_i)
            for k in range(SSM_HPG):
                hd = g * SSM_HPG + k
                hout_ref[i, hd] = (h0_ref[i, hd] * dec_s[hd, pl.ds(base + i, 1), :]
                                   + st[k * SSM_HEADDIM:(k + 1) * SSM_HEADDIM])
        y_t.append(acc.T[:t * bb])
    y_blk = gather(ydx_s) + jnp.concatenate(y_t, axis=1) * gather(expa_s)
    for i in range(t):
        y_s[blk_rows[i], :] = y_blk[i * bb:(i + 1) * bb]

    @pl.when(j == pl.num_programs(0) - 1)
    def _():
        x2 = _gate_norm_out(y_s[...], z_s[...], pool_s[...], x_ref[...], sn_ref[...],
                            wo_ref[...].astype(BF16))
        x2_ref[...] = x2
        q_ref[...] = _dot(_rms(x2, gq_ref[...]).astype(BF16), wq_ref[...].astype(BF16))


def _mix_sample(x, sp, sc, h0, w_in_bf, w, *, nb, t, bb, start_pos):
    n, d = x.shape
    assert n == nb * t and nb % bb == 0 and t * bb <= LANES
    consts = [w["g_mix"], w_in_bf, w["conv_w"], w["conv_b"], w["dt_bias"], w["a_log"], w["d_skip"],
              w["w_pool"], w["pool_scale"], w["expand"], w["ssm_norm"], w["w_out"], w["g_xq"], w["w_xq"]]
    hblk = pl.BlockSpec((bb, SSM_HEADS, SSM_HEADDIM, SSM_STATE), lambda i: (i, 0, 0, 0))
    whole = lambda shape: pl.BlockSpec(shape, lambda i: (0,) * len(shape))
    outs = [
        jax.ShapeDtypeStruct((n, d), F32),
        jax.ShapeDtypeStruct((n, d), F32),
        jax.ShapeDtypeStruct((POOL_BUF * nb, D_POOL), F32),
        jax.ShapeDtypeStruct(((CONV_W - 1) * nb, CONV_DIM), F32),
        jax.ShapeDtypeStruct(h0.shape, F32),
    ]
    n_bc = SSM_GROUPS * SSM_STATE
    return pl.pallas_call(
        functools.partial(_mix_sample_kernel, nb=nb, t=t, bb=bb, start_pos=start_pos),
        grid=(nb // bb,),
        in_specs=[_const_spec(x.shape), _const_spec(sp.shape), _const_spec(sc.shape), hblk]
        + [_const_spec(a.shape) for a in consts],
        out_specs=[whole(o.shape) for o in outs[:4]] + [hblk],
        out_shape=outs,
        scratch_shapes=[
            pltpu.VMEM((n, D_POOL), F32),
            pltpu.VMEM((n, D_SSM), F32),
            pltpu.VMEM((n, D_SSM), F32),
            pltpu.VMEM((n, D_SSM), F32),
            pltpu.VMEM((n, D_SSM), F32),
            pltpu.VMEM((n, n_bc), F32),
            pltpu.VMEM((n, n_bc), F32),
            pltpu.VMEM((SSM_HEADS, nb, LANES), F32),
            pltpu.VMEM((n, D_SSM), F32),
        ],
        compiler_params=_cparams(("arbitrary",)),
        name="mix_sample",
    )(x, sp, sc, h0, *consts)


def _prep_weights(g_ffn1, w1_gate, w1_up, w1_down, g_mix, w_in, conv_w, conv_b, dt_bias, a_log,
                  d_skip, ssm_norm, w_pool, pool_scale, w_out, g_mem, w_mem_k, w_mem_v, g_xq,
                  w_xq, w_xo, g_ffn2, w2_gate, w2_up, w2_down, g_final):
    row = lambda v: v.reshape(1, -1).astype(F32)
    pad_heads = lambda v: jnp.pad(v.astype(F32), (0, LANES - SSM_HEADS)).reshape(1, LANES)
    w_pool_bd = jnp.zeros((D_POOL, D_POOL), F32)
    for g in range(len(POOL_WINDOWS)):
        w_pool_bd = w_pool_bd.at[g * POOL_GW:(g + 1) * POOL_GW, g * POOL_GW:(g + 1) * POOL_GW].set(w_pool[g])
    head_of_chan = jnp.arange(D_SSM) // SSM_HEADDIM
    expand = (jnp.arange(LANES)[:, None] == head_of_chan[None, :]).astype(F32)
    return dict(
        g_ffn1=row(g_ffn1), w1_gate=w1_gate, w1_up=w1_up, w1_down=w1_down,
        g_mix=row(g_mix), w_in=jnp.transpose(w_in),
        conv_w=conv_w.astype(F32), conv_b=row(conv_b), dt_bias=pad_heads(dt_bias),
        a_log=pad_heads(a_log), d_skip=row(jnp.repeat(d_skip, SSM_HEADDIM)), ssm_norm=row(ssm_norm),
        w_pool=w_pool_bd.astype(BF16), pool_scale=row(pool_scale), w_out=w_out,
        g_mem=row(g_mem), w_mem_k=w_mem_k, w_mem_v=w_mem_v,
        g_xq=row(g_xq), w_xq=w_xq, w_xo=w_xo,
        g_ffn2=row(g_ffn2), w2_gate=w2_gate, w2_up=w2_up, w2_down=w2_down,
        g_final=row(g_final), expand=expand,
    )


def _layer(x_prompt, x_sample, mem_prompt, mem_k, mem_v, state_pool, state_conv, state_ssm, start_pos, w):
    b, s, d = x_prompt.shape
    nb, t, _ = x_sample.shape
    n = nb * t
    tmaj = lambda a: jnp.swapaxes(a, 0, 1).reshape(-1, a.shape[-1])
    bmaj = lambda a, r: jnp.swapaxes(a.reshape(r, nb, -1), 0, 1)

    k, v, k_b, v_b = _memkv(mem_prompt, w["g_mem"], w["w_mem_k"], w["w_mem_v"])
    xp, xs = _ffn(x_prompt.reshape(b * s, d), tmaj(x_sample), w["g_ffn1"], w["w1_gate"], w["w1_up"],
                  w["w1_down"], w["g_final"], final=False, tm=1024)

    xp, pool_p, conv_p, ssm_p, w_in_bf = _mix_prompt(xp.reshape(b, s, d), w, tc=256, nseq=2)

    x2, q, new_pool, new_conv, ssm_s = _mix_sample(xs, tmaj(state_pool), tmaj(state_conv), state_ssm,
                                                   w_in_bf, w, nb=nb, t=t, bb=8, start_pos=start_pos)

    xp, o, w_xo_bf = _attn(xp, k_b, v_b, w["g_xq"], w["w_xq"], w["w_xo"],
                           bmaj(q, t).reshape(nb, t * MEM_HEADS, MEM_HD), mem_k, mem_v, tq=512)

    yp, ys = _ffn(xp.reshape(b * s, d), bmaj(x2, t).reshape(n, d), w["g_ffn2"], w["w2_gate"], w["w2_up"],
                  w["w2_down"], w["g_final"], final=True, tm=1024, attn_out=(o.reshape(n, d), w_xo_bf))
    return (yp.reshape(b, s, d), ys.reshape(nb, t, d), k, v, pool_p, conv_p, ssm_p,
            bmaj(new_pool, POOL_BUF), bmaj(new_conv, CONV_W - 1), ssm_s)


def kernel(x_prompt, x_sample, mem_prompt, cache_mem_k, cache_mem_v, state_pool, state_conv, state_ssm,
           g_ffn1, w1_gate, w1_up, w1_down, g_mix, w_in, conv_w, conv_b, dt_bias, a_log, d_skip,
           ssm_norm, w_pool, pool_scale, w_out, g_mem, w_mem_k, w_mem_v, g_xq, w_xq, w_xo,
           g_ffn2, w2_gate, w2_up, w2_down, g_final):
    assert g_ffn1.shape[0] == 1, "single-layer model"
    past_len = 16384
    w = _prep_weights(g_ffn1[0], w1_gate[0], w1_up[0], w1_down[0], g_mix[0], w_in[0], conv_w[0],
                      conv_b[0], dt_bias[0], a_log[0], d_skip[0], ssm_norm[0], w_pool[0],
                      pool_scale[0], w_out[0], g_mem[0], w_mem_k[0], w_mem_v[0], g_xq[0], w_xq[0],
                      w_xo[0], g_ffn2[0], w2_gate[0], w2_up[0], w2_down[0], g_final)
    bp = x_prompt.shape[0]
    y_p, y_s, k_p, v_p, pool_p, conv_p, ssm_p, pool_s, conv_s, ssm_s = _layer(
        x_prompt, x_sample, mem_prompt, cache_mem_k[0], cache_mem_v[0], state_pool[0], state_conv[0],
        state_ssm[0], past_len, w)
    kv_shape = (1, bp, N_MEM, MEM_HEADS, MEM_HD)
    return (y_p, y_s, k_p.reshape(kv_shape), v_p.reshape(kv_shape), pool_p[None], conv_p[None],
            ssm_p[None], pool_s[None], conv_s[None], ssm_s[None])
```

```python
import functools

import jax
import jax.numpy as jnp
from jax import lax
from jax.experimental import pallas as pl
from jax.experimental.pallas import tpu as pltpu

F32 = jnp.float32
BF16 = jnp.bfloat16

D_POOL = 256
POOL_WINDOWS = (2, 4, 8, 16)
POOL_GW = 64
POOL_BUF = 15
D_SSM = 768
SSM_HEADDIM = 64
SSM_HEADS = 12
SSM_GROUPS = 2
SSM_HPG = 6
SSM_STATE = 128
CONV_W = 4
CONV_DIM = 1280
D_PROJ = D_POOL + D_SSM + CONV_DIM
N_MEM = 256
MEM_HEADS = 4
MEM_HD = 256
PAST_LEN = 16384
EPS = 1e-6
LANES = 128
CHUNK = 128
POOL_HIST = 32
CONV_HIST = 8
SCAN_PAD = CHUNK // 2

VMEM_LIMIT = 56 * 1024 * 1024
VMEM_LIMIT_FFN = 60 * 1024 * 1024
FFN_ROWS = 1024
FFN_W_STEPS = 16
MIX_ROWS = 256
MIX_SEQS = 2
ATTN_ROWS = 512
STATE_BLOCK = 8


def _cparams(sem, vmem_limit=VMEM_LIMIT):
    return pltpu.CompilerParams(dimension_semantics=sem, vmem_limit_bytes=vmem_limit)


def _const_spec(shape):
    nd = len(shape)
    return pl.BlockSpec(shape, lambda *_: (0,) * nd, pipeline_mode=pl.Buffered(1))


def _rms(x, g):
    ms = jnp.mean(x * x, axis=-1, keepdims=True)
    return x * lax.rsqrt(ms + EPS) * g


def _silu(x):
    return x * jax.nn.sigmoid(x)


def _dot(a, b):
    return jnp.dot(a, b, preferred_element_type=F32)


def _dot_nt(a, b):
    return lax.dot_general(a, b, (((1,), (1,)), ((), ())), preferred_element_type=F32)


def _dot_tn(a, b):
    return lax.dot_general(a, b, (((0,), (0,)), ((), ())), preferred_element_type=F32)


def _ffn_kernel(xp_ref, xs_ref, g_ref, wg_ref, wu_ref, wd_ref, gf_ref, *rest, final, n_p, attn_out):
    if attn_out:
        ao_ref, wxo_ref, op_ref, os_ref, wg_s, wu_s, wd_s = rest
    else:
        op_ref, os_ref, wg_s, wu_s, wd_s = rest
    i = pl.program_id(0)
    rows_gu = wg_ref.shape[0]
    rows_d = wd_ref.shape[0]

    @pl.when(i < FFN_W_STEPS)
    def _():
        r = pl.multiple_of(i * rows_gu, rows_gu)
        wg_s[pl.ds(r, rows_gu), :] = wg_ref[...].astype(BF16)
        wu_s[pl.ds(r, rows_gu), :] = wu_ref[...].astype(BF16)
        r = pl.multiple_of(i * rows_d, rows_d)
        wd_s[pl.ds(r, rows_d), :] = wd_ref[...].astype(BF16)

    def tile(x, o_ref):
        xn = _rms(x, g_ref[...]).astype(BF16)
        gate = _dot(xn, wg_s[...])
        up = _dot(xn, wu_s[...])
        h = (_silu(gate) * up).astype(BF16)
        out = x + 0.5 * _dot(h, wd_s[...])
        if final:
            out = _rms(out, gf_ref[...])
        o_ref[...] = out

    @pl.when(jnp.logical_and(i >= FFN_W_STEPS, i < FFN_W_STEPS + n_p))
    def _():
        tile(xp_ref[...], op_ref)

    @pl.when(i == FFN_W_STEPS + n_p)
    def _():
        x = xs_ref[...]
        if attn_out:
            x = x + _dot(ao_ref[...].astype(BF16), wxo_ref[...])
        tile(x, os_ref)


def _ffn(xp, xs, g, wg, wu, wd, gf, *, final, tm, attn_out=None):
    tp, d = xp.shape
    dff = wg.shape[1]
    assert tp % tm == 0 and d % FFN_W_STEPS == 0 and dff % FFN_W_STEPS == 0
    n_p = tp // tm
    p_tile = pl.BlockSpec((tm, d), lambda i: (jnp.clip(i - FFN_W_STEPS, 0, n_p - 1), 0))
    s_tile = pl.BlockSpec(xs.shape, lambda i: (0, 0))
    w_chunk = lambda rows, cols: pl.BlockSpec((rows, cols), lambda i: (jnp.minimum(i, FFN_W_STEPS - 1), 0))
    extra = list(attn_out) if attn_out else []
    return pl.pallas_call(
        functools.partial(_ffn_kernel, final=final, n_p=n_p, attn_out=bool(attn_out)),
        grid=(FFN_W_STEPS + n_p + 1,),
        in_specs=[
            p_tile, _const_spec(xs.shape), _const_spec(g.shape),
            w_chunk(d // FFN_W_STEPS, dff), w_chunk(d // FFN_W_STEPS, dff), w_chunk(dff // FFN_W_STEPS, d),
            _const_spec(gf.shape),
        ] + [_const_spec(a.shape) for a in extra],
        out_specs=[p_tile, s_tile],
        out_shape=[jax.ShapeDtypeStruct(xp.shape, F32), jax.ShapeDtypeStruct(xs.shape, F32)],
        scratch_shapes=[pltpu.VMEM((d, dff), BF16), pltpu.VMEM((d, dff), BF16), pltpu.VMEM((dff, d), BF16)],
        compiler_params=_cparams(("arbitrary",), VMEM_LIMIT_FFN),
        name="ffn_final" if final else "ffn",
    )(xp, xs, g, wg, wu, wd, gf, *extra)


def _memkv_kernel(m_ref, g_ref, wk_ref, wv_ref, k_ref, v_ref, kb_ref, vb_ref, wk_s, wv_s):
    @pl.when(pl.program_id(0) == 0)
    def _():
        wk_s[...] = wk_ref[...].astype(BF16)
        wv_s[...] = wv_ref[...].astype(BF16)

    mn = _rms(m_ref[...], g_ref[...]).astype(BF16)
    k = _dot(mn, wk_s[...])
    v = _dot(mn, wv_s[...])
    for h in range(MEM_HEADS):
        k_ref[:, h, :] = k[:, h * MEM_HD:(h + 1) * MEM_HD]
        v_ref[:, h, :] = v[:, h * MEM_HD:(h + 1) * MEM_HD]
    kb_ref[...] = k.astype(BF16)
    vb_ref[...] = v.astype(BF16)


def _memkv(mem, g, wk, wv):
    b, n_mem, d = mem.shape
    row_blk = pl.BlockSpec((None, n_mem, d), lambda i: (i, 0, 0))
    head_blk = pl.BlockSpec((None, n_mem, MEM_HEADS, MEM_HD), lambda i: (i, 0, 0, 0))
    return pl.pallas_call(
        _memkv_kernel,
        grid=(b,),
        in_specs=[row_blk, _const_spec(g.shape), _const_spec(wk.shape), _const_spec(wv.shape)],
        out_specs=[head_blk, head_blk, row_blk, row_blk],
        out_shape=[jax.ShapeDtypeStruct((b, n_mem, MEM_HEADS, MEM_HD), F32)] * 2
        + [jax.ShapeDtypeStruct((b, n_mem, d), BF16)] * 2,
        scratch_shapes=[pltpu.VMEM(wk.shape, BF16), pltpu.VMEM(wv.shape, BF16)],
        compiler_params=_cparams(("arbitrary",)),
        name="memkv",
    )(mem, g, wk, wv)


def _lane_group_select(vals, width):
    lane = lax.broadcasted_iota(jnp.int32, vals[0].shape, 1)
    out = vals[-1]
    for g in range(len(vals) - 2, -1, -1):
        out = jnp.where(lane < (g + 1) * width, vals[g], out)
    return out


def _col_bcast(m, k, width):
    return jnp.broadcast_to(m[:, k:k + 1], (m.shape[0], width))


def _gate_norm_out(y, z, pool_out, x_res, ssm_norm, w_out):
    yn = _rms(y * _silu(z), ssm_norm)
    cat = jnp.concatenate([pool_out, yn], axis=-1).astype(BF16)
    return x_res + _dot(cat, w_out)


def _mix_prompt_kernel(x_ref, g_ref, win_ref, cw_ref, cb_ref,
                       dtb_ref, alog_ref, dsk_ref, sn_ref, wp_ref, ps_ref, wout_ref,
                       o_ref, opool_ref, oconv_ref, ossm_ref, owin_ref,
                       wu_s, wz_s, wx_s, wdt_s, wo_s,
                       ubuf, s2buf, s4buf, s8buf, cbuf, abuf, ht_ref, ybuf, *, tc, nseq):
    c = pl.program_id(1)
    nc = pl.num_programs(1)
    n_pool_slabs = D_POOL // LANES
    n_conv_slabs = CONV_DIM // LANES
    n_chunks = tc // CHUNK
    rp = POOL_HIST + tc
    seqs = range(nseq)

    def slab(s):
        return slice(s * LANES, (s + 1) * LANES)

    @pl.when(jnp.logical_and(pl.program_id(0) == 0, c == 0))
    def _():
        wu_s[...] = win_ref[0:D_POOL, :].T.astype(BF16)
        wz_s[...] = win_ref[D_POOL:D_POOL + D_SSM, :].T.astype(BF16)
        wx_s[...] = win_ref[D_POOL + D_SSM:D_PROJ, :].T.astype(BF16)
        lead = 16 - SSM_HEADS
        place = (lax.broadcasted_iota(jnp.int32, (16, LANES), 0) - lead
                 == lax.broadcasted_iota(jnp.int32, (16, LANES), 1))
        wdt_s[...] = _dot_tn(win_ref[D_PROJ - lead:D_PROJ + SSM_HEADS, :].astype(BF16),
                             jnp.where(place, 1.0, 0.0).astype(BF16)).astype(BF16)
        wo_s[...] = wout_ref[...].astype(BF16)
        owin_ref[:, 0:D_POOL] = wu_s[...]
        owin_ref[:, D_POOL:D_POOL + D_SSM] = wz_s[...]
        owin_ref[:, D_POOL + D_SSM:D_PROJ] = wx_s[...]
        owin_ref[:, D_PROJ:D_PROJ + LANES] = wdt_s[...]

    @pl.when(c == 0)
    def _():
        ubuf[:, :, 0:POOL_HIST, :] = jnp.zeros((nseq, n_pool_slabs, POOL_HIST, LANES), F32)
        cbuf[:, :, 0:CONV_HIST, :] = jnp.zeros((nseq, n_conv_slabs, CONV_HIST, LANES), F32)
        abuf[:, :, 0:SCAN_PAD, :] = jnp.zeros((nseq, n_chunks, SCAN_PAD, LANES), F32)
        ht_ref[...] = jnp.zeros_like(ht_ref)

    lo_half = lax.broadcasted_iota(jnp.int32, (tc, LANES), 1) < POOL_GW
    pos1 = c * tc + lax.broadcasted_iota(jnp.int32, (tc, LANES), 0) + 1
    a_neg = -jnp.exp(alog_ref[...])
    row = lax.broadcasted_iota(jnp.int32, (CHUNK, CHUNK), 0)
    col = lax.broadcasted_iota(jnp.int32, (CHUNK, CHUNK), 1)
    causal = row >= col
    lo = lax.broadcasted_iota(jnp.int32, (CHUNK, LANES), 1) < SSM_HEADDIM
    mask_lo = jnp.where(lo, 1.0, 0.0).astype(BF16)
    mask_hi = jnp.where(lo, 0.0, 1.0).astype(BF16)
    gw = SSM_HPG * SSM_HEADDIM
    n_x = D_SSM // LANES
    st8 = {}

    def head(i):
        x = x_ref[i]
        xn = _rms(x, g_ref[...]).astype(BF16)
        st8[i] = dict(x=x, xn=xn, xbc=[None] * n_conv_slabs)

    def conv_block(i, s0, ns):
        xn = st8[i]["xn"]
        raw = _dot(xn, wx_s[:, s0 * LANES:(s0 + ns) * LANES])
        for t in range(ns):
            s = s0 + t
            cbuf[i, s, CONV_HIST:CONV_HIST + tc, :] = raw[:, slab(t)]
            acc = cbuf[i, s, CONV_HIST:CONV_HIST + tc, :] * cw_ref[CONV_W - 1:CONV_W, slab(s)]
            for j in range(1, CONV_W):
                acc = acc + (cbuf[i, s, CONV_HIST - j:CONV_HIST - j + tc, :]
                             * cw_ref[CONV_W - 1 - j:CONV_W - j, slab(s)])
            st8[i]["xbc"][s] = _silu(acc + cb_ref[:, slab(s)])

    def dt_proj(i):
        st8[i]["dt"] = jax.nn.softplus(_dot(st8[i]["xn"], wdt_s[...]) + dtb_ref[...])

    def pool(i):
        u = _dot(st8[i]["xn"], wu_s[...])
        for s in range(n_pool_slabs):
            ubuf[i, s, POOL_HIST:rp, :] = u[:, slab(s)]
            s2buf[i, s, 8:rp, :] = ubuf[i, s, 8:rp, :] + ubuf[i, s, 7:rp - 1, :]
        s4buf[i, 16:rp, :] = s2buf[i, 1, 16:rp, :] + s2buf[i, 1, 14:rp - 2, :]
        s8buf[i, 24:rp, :] = s4buf[i, 24:rp, :] + s4buf[i, 20:rp - 4, :]
        win_sums = [
            jnp.where(lo_half, s2buf[i, 0, POOL_HIST:rp, :],
                      s2buf[i, 0, POOL_HIST:rp, :] + s2buf[i, 0, POOL_HIST - 2:rp - 2, :]),
            jnp.where(lo_half, s8buf[i, POOL_HIST:rp, :],
                      s8buf[i, POOL_HIST:rp, :] + s8buf[i, POOL_HIST - 8:rp - 8, :]),
        ]
        ps = []
        for s in range(n_pool_slabs):
            win = jnp.where(lo_half, POOL_WINDOWS[2 * s], POOL_WINDOWS[2 * s + 1])
            cnt = jnp.minimum(pos1, win).astype(F32)
            ps.append(win_sums[s] / cnt - u[:, slab(s)])
        st8[i]["pool_out"] = _dot(jnp.concatenate(ps, axis=1).astype(BF16), wp_ref[...]) * ps_ref[...]

    def ssd_chunk(i, j):
        xbc = st8[i]["xbc"]
        sl = slice(j * CHUNK, (j + 1) * CHUNK)
        dt_c = st8[i]["dt"][sl]
        acs = dt_c * a_neg
        sh = 1
        while sh < CHUNK:
            abuf[i, j, SCAN_PAD:SCAN_PAD + CHUNK, :] = acs
            acs = acs + abuf[i, j, SCAN_PAD - sh:SCAN_PAD - sh + CHUNK, :]
            sh *= 2
        acs_last = acs[CHUNK - 1:CHUNK, :]
        fdec = jnp.exp(acs_last - acs) * dt_c
        src_t = (acs - jnp.log(dt_c)).T
        ht = ht_ref[i]
        ht_b = ht.astype(BF16)
        cdec, st = [], []
        for g in range(SSM_GROUPS):
            c_g = xbc[n_x + SSM_GROUPS + g][sl].astype(BF16)
            b_g = xbc[n_x + g][sl].astype(BF16)
            scores = _dot_nt(c_g, b_g)
            y_off = _dot(c_g, ht_b[:, g * gw:(g + 1) * gw])
            xw = []
            for q in range(SSM_HPG // 2):
                k0 = g * SSM_HPG + 2 * q
                pair = k0 // 2
                a_cols = [_col_bcast(acs, k, CHUNK) for k in (k0, k0 + 1)]
                lhs = [(scores * jnp.exp(jnp.where(causal, a_cols[h] - src_t[k0 + h:k0 + h + 1, :], -jnp.inf))
                        ).astype(BF16) for h in range(2)]
                x_pair = xbc[pair][sl]
                x_b = x_pair.astype(BF16)
                rhs = jnp.concatenate([x_b * mask_lo, x_b * mask_hi], axis=0)
                e_pair = jnp.exp(jnp.where(lo, a_cols[0], a_cols[1]))
                ybuf[i, sl, slab(pair)] = (_dot(jnp.concatenate(lhs, axis=1), rhs)
                                           + y_off[:, slab(q)] * e_pair + x_pair * dsk_ref[:, slab(pair)])
                f_pair = jnp.where(lo, _col_bcast(fdec, k0, LANES), _col_bcast(fdec, k0 + 1, LANES))
                xw.append((x_pair * f_pair).astype(BF16))
                cdec.append(e_pair[CHUNK - 1:CHUNK, :])
            st.append(_dot_tn(b_g, jnp.concatenate(xw, axis=1)))
        ht_ref[i] = ht * jnp.concatenate(cdec, axis=1) + jnp.concatenate(st, axis=1)

    def z_proj(i):
        st8[i]["zs"] = _silu(_dot(st8[i]["xn"], wz_s[...]))

    def tail(i):
        d = st8[i]
        yn = _rms(ybuf[i] * d["zs"], sn_ref[...])
        cat = jnp.concatenate([d["pool_out"], yn], axis=-1).astype(BF16)
        o_ref[i] = d["x"] + _dot(cat, wo_s[...])

    def mid(i):
        conv_block(i, n_x, 2 * SSM_GROUPS)
        dt_proj(i)
        for s0 in range(0, n_x, 2):
            conv_block(i, s0, 2)

    for i in seqs:
        head(i)
    for i in seqs:
        mid(i)
    for i in seqs:
        pool(i)
    for j in range(n_chunks):
        for i in seqs:
            ssd_chunk(i, j)
        if j == 0:
            for i in seqs:
                z_proj(i)
    for i in seqs:
        tail(i)

    ubuf[:, :, 0:POOL_HIST, :] = ubuf[:, :, tc:tc + POOL_HIST, :]
    cbuf[:, :, 0:CONV_HIST, :] = cbuf[:, :, tc:tc + CONV_HIST, :]

    @pl.when(c == nc - 1)
    def _():
        for i in seqs:
            for s in range(n_pool_slabs):
                opool_ref[i, :, slab(s)] = ubuf[i, s, POOL_HIST - POOL_BUF:POOL_HIST, :]
            for s in range(n_conv_slabs):
                oconv_ref[i, :, slab(s)] = cbuf[i, s, CONV_HIST - (CONV_W - 1):CONV_HIST, :]
            ossm_ref[i] = ht_ref[i].T.reshape(SSM_HEADS, SSM_HEADDIM, SSM_STATE)


def _mix_prompt(x, w, *, tc, nseq):
    b, s, d = x.shape
    assert s % tc == 0 and tc % CHUNK == 0 and b % nseq == 0
    consts = [w["g_mix"], w["w_in"], w["conv_w"], w["conv_b"],
              w["dt_bias"], w["a_log"], w["d_skip"], w["ssm_norm"], w["w_pool"], w["pool_scale"],
              w["w_out"]]
    tile = pl.BlockSpec((nseq, tc, d), lambda i, j: (i, j, 0))
    return pl.pallas_call(
        functools.partial(_mix_prompt_kernel, tc=tc, nseq=nseq),
        grid=(b // nseq, s // tc),
        in_specs=[tile] + [_const_spec(a.shape) for a in consts],
        out_specs=[
            tile,
            pl.BlockSpec((nseq, POOL_BUF, D_POOL), lambda i, j: (i, 0, 0)),
            pl.BlockSpec((nseq, CONV_W - 1, CONV_DIM), lambda i, j: (i, 0, 0)),
            pl.BlockSpec((nseq, SSM_HEADS, SSM_HEADDIM, SSM_STATE), lambda i, j: (i, 0, 0, 0)),
            pl.BlockSpec((d, D_PROJ + LANES), lambda i, j: (0, 0)),
        ],
        out_shape=[
            jax.ShapeDtypeStruct((b, s, d), F32),
            jax.ShapeDtypeStruct((b, POOL_BUF, D_POOL), F32),
            jax.ShapeDtypeStruct((b, CONV_W - 1, CONV_DIM), F32),
            jax.ShapeDtypeStruct((b, SSM_HEADS, SSM_HEADDIM, SSM_STATE), F32),
            jax.ShapeDtypeStruct((d, D_PROJ + LANES), BF16),
        ],
        scratch_shapes=[
            pltpu.VMEM((d, D_POOL), BF16), pltpu.VMEM((d, D_SSM), BF16),
            pltpu.VMEM((d, CONV_DIM), BF16), pltpu.VMEM((d, LANES), BF16), pltpu.VMEM((d, d), BF16),
            pltpu.VMEM((nseq, D_POOL // LANES, POOL_HIST + tc, LANES), F32),
            pltpu.VMEM((nseq, D_POOL // LANES, POOL_HIST + tc, LANES), F32),
            pltpu.VMEM((nseq, POOL_HIST + tc, LANES), F32),
            pltpu.VMEM((nseq, POOL_HIST + tc, LANES), F32),
            pltpu.VMEM((nseq, CONV_DIM // LANES, CONV_HIST + tc, LANES), F32),
            pltpu.VMEM((nseq, tc // CHUNK, SCAN_PAD + CHUNK, LANES), F32),
            pltpu.VMEM((nseq, SSM_STATE, D_SSM), F32),
            pltpu.VMEM((nseq, tc, D_SSM), F32),
        ],
        compiler_params=_cparams(("arbitrary", "arbitrary")),
        name="mix_prompt",
    )(x, *consts)


def _attn_kernel(x_ref, g_ref, wq_ref, k_ref, v_ref, wo_ref, qs_ref, ks_ref, vs_ref,
                 o_ref, os_ref, owo_ref, wq_s, wo_s, *, bb):
    @pl.when(jnp.logical_and(pl.program_id(0) == 0, pl.program_id(1) == 0))
    def _():
        wq_s[...] = wq_ref[...].astype(BF16)
        wo_s[...] = wo_ref[...].astype(BF16)
        owo_ref[...] = wo_s[...]

    hs = [slice(h * MEM_HD, (h + 1) * MEM_HD) for h in range(MEM_HEADS)]
    scale = MEM_HD ** -0.5
    nq = qs_ref.shape[1]
    rows = N_MEM * MEM_HEADS

    def softmax(s):
        e = jnp.exp(s - jnp.max(s, axis=-1, keepdims=True))
        return (e / jnp.sum(e, axis=-1, keepdims=True)).astype(BF16)

    q_head = lax.broadcasted_iota(jnp.int32, (nq, rows), 0) % MEM_HEADS
    kv_head = lax.broadcasted_iota(jnp.int32, (nq, rows), 1) % MEM_HEADS
    same_head = q_head == kv_head

    q_all = qs_ref[...].reshape(bb * nq, MEM_HD)
    q_all = jnp.concatenate([q_all, jnp.zeros((LANES - bb * nq, MEM_HD), F32)], axis=0).astype(BF16)
    s_s = [_dot_nt(ks_ref[i].reshape(rows, MEM_HD).astype(BF16), q_all) for i in range(bb)]

    x = x_ref[...]
    q = _dot(_rms(x, g_ref[...]).astype(BF16), wq_s[...]).astype(BF16)

    p_s = [softmax(jnp.where(same_head, s_s[i].T[i * nq:(i + 1) * nq] * scale, -jnp.inf)) for i in range(bb)]

    s_p = [_dot_nt(q[:, hs[h]], k_ref[:, hs[h]]) * scale for h in range(MEM_HEADS)]

    for i in range(bb):
        os_ref[i] = _dot(p_s[i], vs_ref[i].reshape(rows, MEM_HD).astype(BF16))

    p_p = [softmax(s) for s in s_p]
    o = jnp.concatenate([_dot(p_p[h], v_ref[:, hs[h]]) for h in range(MEM_HEADS)], axis=-1).astype(BF16)
    o_ref[...] = x + _dot(o, wo_s[...])


def _attn(x, k, v, g, wq, wo, qs, ks, vs, *, tq):
    b, s, d = x.shape
    nb, nq, hd = qs.shape
    nj = s // tq
    assert nb % (b * nj) == 0
    bb = nb // (b * nj)
    assert bb * nq <= LANES
    tile = pl.BlockSpec((None, tq, d), lambda i, j: (i, j, 0))
    mem = pl.BlockSpec((None, N_MEM, d), lambda i, j: (i, 0, 0))
    s_q = pl.BlockSpec((bb, nq, hd), lambda i, j: (i * nj + j, 0, 0))
    s_mem = pl.BlockSpec((bb, N_MEM, MEM_HEADS, hd), lambda i, j: (i * nj + j, 0, 0, 0))
    return pl.pallas_call(
        functools.partial(_attn_kernel, bb=bb),
        grid=(b, nj),
        in_specs=[tile, _const_spec(g.shape), _const_spec(wq.shape), mem, mem, _const_spec(wo.shape),
                  s_q, s_mem, s_mem],
        out_specs=[tile, s_q, pl.BlockSpec(wo.shape, lambda i, j: (0, 0))],
        out_shape=[jax.ShapeDtypeStruct((b, s, d), F32), jax.ShapeDtypeStruct((nb, nq, hd), F32),
                   jax.ShapeDtypeStruct(wo.shape, BF16)],
        scratch_shapes=[pltpu.VMEM(wq.shape, BF16), pltpu.VMEM(wo.shape, BF16)],
        compiler_params=_cparams(("arbitrary", "arbitrary")),
        name="attn",
    )(x, g, wq, k, v, wo, qs, ks, vs)


def _mix_sample_kernel(x_ref, sp_ref, sc_ref, h0_ref, g_ref, win_ref, cw_ref, cb_ref, dtb_ref, alog_ref,
                       dsk_ref, wp_ref, ps_ref, ex_ref, sn_ref, wo_ref, gq_ref, wq_ref,
                       x2_ref, q_ref, npool_ref, nconv_ref, hout_ref,
                       pool_s, z_s, ydx_s, expa_s, xw_s, b_s, c_s, dec_s, y_s, *, nb, t, bb, start_pos):
    j = pl.program_id(0)
    gw = SSM_HPG * SSM_HEADDIM
    n_bc = SSM_GROUPS * SSM_STATE

    def rows(i):
        return slice(i * nb, (i + 1) * nb)

    @pl.when(j == 0)
    def _():
        x = x_ref[...]
        xn = _rms(x, g_ref[...]).astype(BF16)
        z_s[...] = _dot(xn, win_ref[:, D_POOL:D_POOL + D_SSM])

        u = _dot(xn, win_ref[:, 0:D_POOL])
        ext = [sp_ref[rows(i), :] for i in range(POOL_BUF)] + [u[rows(i)] for i in range(t)]
        ps = []
        for i in range(t):
            sums, cnts = [], []
            for w in POOL_WINDOWS:
                s = ext[POOL_BUF + i]
                for k in range(1, w):
                    s = s + ext[POOL_BUF + i - k]
                sums.append(s)
                cnts.append(jnp.full((nb, D_POOL), float(min(start_pos + i + 1, w)), F32))
            mean = _lane_group_select(sums, POOL_GW) / _lane_group_select(cnts, POOL_GW)
            ps.append(mean - ext[POOL_BUF + i])
        p = jnp.concatenate(ps, axis=0).astype(BF16)
        pool_s[...] = _dot(p, wp_ref[...]) * ps_ref[...]
        for i in range(POOL_BUF):
            npool_ref[rows(i), :] = ext[t + i]

        xr = _dot(xn, win_ref[:, D_POOL + D_SSM:D_PROJ])
        cext = [sc_ref[rows(i), :] for i in range(CONV_W - 1)] + [xr[rows(i)] for i in range(t)]
        xbc = []
        for i in range(t):
            acc = cext[i] * cw_ref[0:1, :]
            for k in range(1, CONV_W):
                acc = acc + cext[i + k] * cw_ref[k:k + 1, :]
            xbc.append(_silu(acc + cb_ref[...]))
        for i in range(CONV_W - 1):
            nconv_ref[rows(i), :] = cext[t + i]
        xs = [v[:, :D_SSM] for v in xbc]
        bm = [v[:, D_SSM:D_SSM + n_bc] for v in xbc]
        cm = [v[:, D_SSM + n_bc:] for v in xbc]
        for i in range(t):
            b_s[rows(i), :] = bm[i]
            c_s[rows(i), :] = cm[i]

        dt_all = jax.nn.softplus(_dot(xn, win_ref[:, D_PROJ:D_PROJ + LANES]) + dtb_ref[...])
        a_neg = -jnp.exp(alog_ref[...])
        dt = [dt_all[rows(i)] for i in range(t)]
        acs = []
        for i in range(t):
            da = dt[i] * a_neg
            acs.append(da if i == 0 else acs[-1] + da)
        dec = jnp.exp(acs[-1])
        for k in range(SSM_HEADS):
            dec_s[k] = _col_bcast(dec, k, LANES)

        def expand(v, exact):
            if exact:
                return jnp.dot(v, ex_ref[...], precision=lax.Precision.HIGHEST,
                               preferred_element_type=F32)
            return _dot(v.astype(BF16), ex_ref[...].astype(BF16))

        lane = lax.broadcasted_iota(jnp.int32, (nb, LANES), 1)
        for i in range(t):
            ydx = xs[i] * dsk_ref[...]
            for s in range(i + 1):
                sc = [jnp.sum(cm[i][:, g * SSM_STATE:(g + 1) * SSM_STATE]
                              * bm[s][:, g * SSM_STATE:(g + 1) * SSM_STATE], axis=-1, keepdims=True)
                      for g in range(SSM_GROUPS)]
                sc = jnp.where(lane < SSM_HPG, sc[0], sc[1])
                coef = sc * jnp.exp(acs[i] - acs[s]) * dt[s]
                ydx = ydx + expand(coef, False) * xs[s]
            ydx_s[rows(i), :] = ydx
            expa_s[rows(i), :] = expand(jnp.exp(acs[i]), True)
            xw_s[rows(i), :] = xs[i] * expand(jnp.exp(acs[-1] - acs[i]) * dt[i], True)

    base = pl.multiple_of(j * bb, bb)
    blk_rows = [pl.ds(i * nb + base, bb) for i in range(t)]

    def gather(ref):
        return jnp.concatenate([ref[r, :] for r in blk_rows], axis=0)

    def pad_rows(v):
        return jnp.concatenate([v, jnp.zeros((LANES - t * bb, v.shape[1]), v.dtype)], axis=0)

    c_blk = pad_rows(gather(c_s))
    b_blk = pad_rows(gather(b_s))
    xw_t = pad_rows(gather(xw_s)).T.astype(BF16)
    row_seq = lax.broadcasted_iota(jnp.int32, (LANES, SSM_STATE), 0) % bb
    col_seq = lax.broadcasted_iota(jnp.int32, (SSM_STATE, LANES), 1) % bb
    y_t = []
    for g in range(SSM_GROUPS):
        gs = slice(g * SSM_STATE, (g + 1) * SSM_STATE)
        c_t = c_blk[:, gs].T
        acc = jnp.zeros((gw, LANES), F32)
        for i in range(bb):
            h0 = h0_ref[i, g * SSM_HPG:(g + 1) * SSM_HPG].reshape(gw, SSM_STATE)
            acc = acc + _dot(h0.astype(BF16), jnp.where(col_seq == i, c_t, 0.0).astype(BF16))
            b_i = jnp.where(row_seq == i, b_blk[:, gs], 0.0).astype(BF16)
            st = _dot(xw_t[g * gw:(g + 1) * gw, :], b_i)
            for k in range(SSM_HPG):
                hd = g * SSM_HPG + k
                hout_ref[i, hd] = (h0_ref[i, hd] * dec_s[hd, pl.ds(base + i, 1), :]
                                   + st[k * SSM_HEADDIM:(k + 1) * SSM_HEADDIM])
        y_t.append(acc.T[:t * bb])
    y_blk = gather(ydx_s) + jnp.concatenate(y_t, axis=1) * gather(expa_s)
    for i in range(t):
        y_s[blk_rows[i], :] = y_blk[i * bb:(i + 1) * bb]

    @pl.when(j == pl.num_programs(0) - 1)
    def _():
        x2 = _gate_norm_out(y_s[...], z_s[...], pool_s[...], x_ref[...], sn_ref[...],
                            wo_ref[...].astype(BF16))
        x2_ref[...] = x2
        q_ref[...] = _dot(_rms(x2, gq_ref[...]).astype(BF16), wq_ref[...].astype(BF16))


def _mix_sample(x, sp, sc, h0, w_in_bf, w, *, nb, t, bb, start_pos):
    n, d = x.shape
    assert n == nb * t and nb % bb == 0 and t * bb <= LANES
    consts = [w["g_mix"], w_in_bf, w["conv_w"], w["conv_b"], w["dt_bias"], w["a_log"], w["d_skip"],
              w["w_pool"], w["pool_scale"], w["expand"], w["ssm_norm"], w["w_out"], w["g_xq"], w["w_xq"]]
    hblk = pl.BlockSpec((bb, SSM_HEADS, SSM_HEADDIM, SSM_STATE), lambda i: (i, 0, 0, 0))
    whole = lambda shape: pl.BlockSpec(shape, lambda i: (0,) * len(shape))
    outs = [
        jax.ShapeDtypeStruct((n, d), F32),
        jax.ShapeDtypeStruct((n, d), F32),
        jax.ShapeDtypeStruct((POOL_BUF * nb, D_POOL), F32),
        jax.ShapeDtypeStruct(((CONV_W - 1) * nb, CONV_DIM), F32),
        jax.ShapeDtypeStruct(h0.shape, F32),
    ]
    n_bc = SSM_GROUPS * SSM_STATE
    return pl.pallas_call(
        functools.partial(_mix_sample_kernel, nb=nb, t=t, bb=bb, start_pos=start_pos),
        grid=(nb // bb,),
        in_specs=[_const_spec(x.shape), _const_spec(sp.shape), _const_spec(sc.shape), hblk]
        + [_const_spec(a.shape) for a in consts],
        out_specs=[whole(o.shape) for o in outs[:4]] + [hblk],
        out_shape=outs,
        scratch_shapes=[
            pltpu.VMEM((n, D_POOL), F32),
            pltpu.VMEM((n, D_SSM), F32),
            pltpu.VMEM((n, D_SSM), F32),
            pltpu.VMEM((n, D_SSM), F32),
            pltpu.VMEM((n, D_SSM), F32),
            pltpu.VMEM((n, n_bc), F32),
            pltpu.VMEM((n, n_bc), F32),
            pltpu.VMEM((SSM_HEADS, nb, LANES), F32),
            pltpu.VMEM((n, D_SSM), F32),
        ],
        compiler_params=_cparams(("arbitrary",)),
        name="mix_sample",
    )(x, sp, sc, h0, *consts)


def _prep_weights(g_ffn1, w1_gate, w1_up, w1_down, g_mix, w_in, conv_w, conv_b, dt_bias, a_log,
                  d_skip, ssm_norm, w_pool, pool_scale, w_out, g_mem, w_mem_k, w_mem_v, g_xq,
                  w_xq, w_xo, g_ffn2, w2_gate, w2_up, w2_down, g_final):
    row = lambda v: v.reshape(1, -1).astype(F32)
    pad_heads = lambda v: jnp.pad(v.astype(F32), (0, LANES - SSM_HEADS)).reshape(1, LANES)
    w_pool_bd = jnp.zeros((D_POOL, D_POOL), F32)
    for g in range(len(POOL_WINDOWS)):
        w_pool_bd = w_pool_bd.at[g * POOL_GW:(g + 1) * POOL_GW, g * POOL_GW:(g + 1) * POOL_GW].set(w_pool[g])
    head_of_chan = jnp.arange(D_SSM) // SSM_HEADDIM
    expand = (jnp.arange(LANES)[:, None] == head_of_chan[None, :]).astype(F32)
    return dict(
        g_ffn1=row(g_ffn1), w1_gate=w1_gate, w1_up=w1_up, w1_down=w1_down,
        g_mix=row(g_mix), w_in=jnp.transpose(w_in),
        conv_w=conv_w.astype(F32), conv_b=row(conv_b), dt_bias=pad_heads(dt_bias),
        a_log=pad_heads(a_log), d_skip=row(jnp.repeat(d_skip, SSM_HEADDIM)), ssm_norm=row(ssm_norm),
        w_pool=w_pool_bd.astype(BF16), pool_scale=row(pool_scale), w_out=w_out,
        g_mem=row(g_mem), w_mem_k=w_mem_k, w_mem_v=w_mem_v,
        g_xq=row(g_xq), w_xq=w_xq, w_xo=w_xo,
        g_ffn2=row(g_ffn2), w2_gate=w2_gate, w2_up=w2_up, w2_down=w2_down,
        g_final=row(g_final), expand=expand,
    )


def _layer(x_prompt, x_sample, mem_prompt, mem_k, mem_v, state_pool, state_conv, state_ssm, start_pos, w):
    b, s, d = x_prompt.shape
    nb, t, _ = x_sample.shape
    n = nb * t
    tmaj = lambda a: jnp.swapaxes(a, 0, 1).reshape(-1, a.shape[-1])
    bmaj = lambda a, r: jnp.swapaxes(a.reshape(r, nb, -1), 0, 1)

    k, v, k_b, v_b = _memkv(mem_prompt, w["g_mem"], w["w_mem_k"], w["w_mem_v"])
    xp, xs = _ffn(x_prompt.reshape(b * s, d), tmaj(x_sample), w["g_ffn1"], w["w1_gate"], w["w1_up"],
                  w["w1_down"], w["g_final"], final=False, tm=FFN_ROWS)

    xp, pool_p, conv_p, ssm_p, w_in_bf = _mix_prompt(xp.reshape(b, s, d), w, tc=MIX_ROWS, nseq=MIX_SEQS)

    x2, q, new_pool, new_conv, ssm_s = _mix_sample(xs, tmaj(state_pool), tmaj(state_conv), state_ssm,
                                                   w_in_bf, w, nb=nb, t=t, bb=STATE_BLOCK,
                                                   start_pos=start_pos)

    xp, o, w_xo_bf = _attn(xp, k_b, v_b, w["g_xq"], w["w_xq"], w["w_xo"],
                           bmaj(q, t).reshape(nb, t * MEM_HEADS, MEM_HD), mem_k, mem_v, tq=ATTN_ROWS)

    yp, ys = _ffn(xp.reshape(b * s, d), bmaj(x2, t).reshape(n, d), w["g_ffn2"], w["w2_gate"], w["w2_up"],
                  w["w2_down"], w["g_final"], final=True, tm=FFN_ROWS,
                  attn_out=(o.reshape(n, d), w_xo_bf))
    return (yp.reshape(b, s, d), ys.reshape(nb, t, d), k, v, pool_p, conv_p, ssm_p,
            bmaj(new_pool, POOL_BUF), bmaj(new_conv, CONV_W - 1), ssm_s)


def kernel(x_prompt, x_sample, mem_prompt, cache_mem_k, cache_mem_v, state_pool, state_conv, state_ssm,
           g_ffn1, w1_gate, w1_up, w1_down, g_mix, w_in, conv_w, conv_b, dt_bias, a_log, d_skip,
           ssm_norm, w_pool, pool_scale, w_out, g_mem, w_mem_k, w_mem_v, g_xq, w_xq, w_xo,
           g_ffn2, w2_gate, w2_up, w2_down, g_final):
    assert g_ffn1.shape[0] == 1, "single-layer model"
    w = _prep_weights(g_ffn1[0], w1_gate[0], w1_up[0], w1_down[0], g_mix[0], w_in[0], conv_w[0],
                      conv_b[0], dt_bias[0], a_log[0], d_skip[0], ssm_norm[0], w_pool[0],
                      pool_scale[0], w_out[0], g_mem[0], w_mem_k[0], w_mem_v[0], g_xq[0], w_xq[0],
                      w_xo[0], g_ffn2[0], w2_gate[0], w2_up[0], w2_down[0], g_final)
    bp = x_prompt.shape[0]
    y_p, y_s, k_p, v_p, pool_p, conv_p, ssm_p, pool_s, conv_s, ssm_s = _layer(
        x_prompt, x_sample, mem_prompt, cache_mem_k[0], cache_mem_v[0], state_pool[0], state_conv[0],
        state_ssm[0], PAST_LEN, w)
    kv_shape = (1, bp, N_MEM, MEM_HEADS, MEM_HD)
    return (y_p, y_s, k_p.reshape(kv_shape), v_p.reshape(kv_shape), pool_p[None], conv_p[None],
            ssm_p[None], pool_s[None], conv_s[None], ssm_s[None])
```

```python
import functools

import jax
import jax.numpy as jnp
from jax import lax
from jax.experimental import pallas as pl
from jax.experimental.pallas import tpu as pltpu

F32 = jnp.float32
BF16 = jnp.bfloat16

D_POOL = 256
POOL_WINDOWS = (2, 4, 8, 16)
POOL_GW = 64
POOL_BUF = 15
D_SSM = 768
SSM_HEADDIM = 64
SSM_HEADS = 12
SSM_GROUPS = 2
SSM_HPG = 6
SSM_STATE = 128
CONV_W = 4
CONV_DIM = 1280
D_PROJ = D_POOL + D_SSM + CONV_DIM
N_MEM = 256
MEM_HEADS = 4
MEM_HD = 256
PAST_LEN = 16384
EPS = 1e-6
LANES = 128
CHUNK = 128
POOL_HIST = 32
CONV_HIST = 8
SCAN_PAD = CHUNK // 2

VMEM_LIMIT = 56 * 1024 * 1024
VMEM_LIMIT_FFN = 62 * 1024 * 1024
FFN_ROWS = 1024
FFN_COLS = 256
MIX_ROWS = 256
MIX_SEQS = 2
ATTN_ROWS = 512
STATE_BLOCK = 8


def _cparams(sem, vmem_limit=VMEM_LIMIT):
    return pltpu.CompilerParams(dimension_semantics=sem, vmem_limit_bytes=vmem_limit)


def _const_spec(shape):
    nd = len(shape)
    return pl.BlockSpec(shape, lambda *_: (0,) * nd, pipeline_mode=pl.Buffered(1))


def _rms(x, g):
    ms = jnp.mean(x * x, axis=-1, keepdims=True)
    return x * lax.rsqrt(ms + EPS) * g


def _silu(x):
    return x * jax.nn.sigmoid(x)


def _dot(a, b):
    return jnp.dot(a, b, preferred_element_type=F32)


def _dot_nt(a, b):
    return lax.dot_general(a, b, (((1,), (1,)), ((), ())), preferred_element_type=F32)


def _dot_tn(a, b):
    return lax.dot_general(a, b, (((0,), (0,)), ((), ())), preferred_element_type=F32)


def _ffn_kernel(xp_ref, xs_ref, g_ref, wg_ref, wu_ref, wd_ref, gf_ref, *rest, final, n_w, n_p, attn_out):
    if attn_out:
        ao_ref, wxo_ref, op_ref, os_ref, wg_s, wu_s, wd_s, xn_s = rest
    else:
        op_ref, os_ref, wg_s, wu_s, wd_s, xn_s = rest
    i = pl.program_id(0)

    def swiglu(xn, cols):
        gate = _dot(xn, wg_s[:, cols])
        up = _dot(xn, wu_s[:, cols])
        return _dot((_silu(gate) * up).astype(BF16), wd_s[cols, :])

    def finish(x, y):
        out = x + 0.5 * y
        return _rms(out, gf_ref[...]) if final else out

    @pl.when(i == 0)
    def _():
        xn_s[...] = _rms(xp_ref[...], g_ref[...]).astype(BF16)
        op_ref[...] = jnp.zeros_like(op_ref)

    for c in range(n_w):
        @pl.when(i == c)
        def _(c=c):
            cols = slice(c * FFN_COLS, (c + 1) * FFN_COLS)
            wg_s[:, cols] = wg_ref[...].astype(BF16)
            wu_s[:, cols] = wu_ref[...].astype(BF16)
            wd_s[cols, :] = wd_ref[...].astype(BF16)
            op_ref[...] += swiglu(xn_s[...], cols)

    @pl.when(i == n_w - 1)
    def _():
        op_ref[...] = finish(xp_ref[...], op_ref[...])

    def tile(x, o_ref):
        xn = _rms(x, g_ref[...]).astype(BF16)
        o_ref[...] = finish(x, swiglu(xn, slice(None)))

    @pl.when(jnp.logical_and(i >= n_w, i < n_w + n_p - 1))
    def _():
        tile(xp_ref[...], op_ref)

    @pl.when(i == n_w + n_p - 1)
    def _():
        x = xs_ref[...]
        if attn_out:
            x = x + _dot(ao_ref[...], wxo_ref[...])
        tile(x, os_ref)


def _ffn(xp, xs, g, wg, wu, wd, gf, *, final, tm, attn_out=None):
    tp, d = xp.shape
    dff = wg.shape[1]
    assert tp % tm == 0 and dff % FFN_COLS == 0
    n_p = tp // tm
    n_w = dff // FFN_COLS
    p_tile = pl.BlockSpec((tm, d), lambda i: (jnp.clip(i - (n_w - 1), 0, n_p - 1), 0))
    s_tile = pl.BlockSpec(xs.shape, lambda i: (0, 0))
    w_cols = pl.BlockSpec((d, FFN_COLS), lambda i: (0, jnp.minimum(i, n_w - 1)))
    w_rows = pl.BlockSpec((FFN_COLS, d), lambda i: (jnp.minimum(i, n_w - 1), 0))
    extra = list(attn_out) if attn_out else []
    return pl.pallas_call(
        functools.partial(_ffn_kernel, final=final, n_w=n_w, n_p=n_p, attn_out=bool(attn_out)),
        grid=(n_w + n_p,),
        in_specs=[p_tile, _const_spec(xs.shape), _const_spec(g.shape), w_cols, w_cols, w_rows,
                  _const_spec(gf.shape)] + [_const_spec(a.shape) for a in extra],
        out_specs=[p_tile, s_tile],
        out_shape=[jax.ShapeDtypeStruct(xp.shape, F32), jax.ShapeDtypeStruct(xs.shape, F32)],
        scratch_shapes=[pltpu.VMEM((d, dff), BF16), pltpu.VMEM((d, dff), BF16), pltpu.VMEM((dff, d), BF16),
                        pltpu.VMEM((tm, d), BF16)],
        compiler_params=_cparams(("arbitrary",), VMEM_LIMIT_FFN),
        name="ffn_final" if final else "ffn",
    )(xp, xs, g, wg, wu, wd, gf, *extra)


def _memkv_kernel(m_ref, g_ref, wk_ref, wv_ref, k_ref, v_ref, kb_ref, vb_ref, wk_s, wv_s):
    @pl.when(pl.program_id(0) == 0)
    def _():
        wk_s[...] = wk_ref[...].astype(BF16)
        wv_s[...] = wv_ref[...].astype(BF16)

    mn = _rms(m_ref[...], g_ref[...]).astype(BF16)
    k = _dot(mn, wk_s[...])
    v = _dot(mn, wv_s[...])
    for h in range(MEM_HEADS):
        k_ref[:, h, :] = k[:, h * MEM_HD:(h + 1) * MEM_HD]
        v_ref[:, h, :] = v[:, h * MEM_HD:(h + 1) * MEM_HD]
    kb_ref[...] = k.astype(BF16)
    vb_ref[...] = v.astype(BF16)


def _memkv(mem, g, wk, wv):
    b, n_mem, d = mem.shape
    row_blk = pl.BlockSpec((None, n_mem, d), lambda i: (i, 0, 0))
    head_blk = pl.BlockSpec((None, n_mem, MEM_HEADS, MEM_HD), lambda i: (i, 0, 0, 0))
    return pl.pallas_call(
        _memkv_kernel,
        grid=(b,),
        in_specs=[row_blk, _const_spec(g.shape), _const_spec(wk.shape), _const_spec(wv.shape)],
        out_specs=[head_blk, head_blk, row_blk, row_blk],
        out_shape=[jax.ShapeDtypeStruct((b, n_mem, MEM_HEADS, MEM_HD), F32)] * 2
        + [jax.ShapeDtypeStruct((b, n_mem, d), BF16)] * 2,
        scratch_shapes=[pltpu.VMEM(wk.shape, BF16), pltpu.VMEM(wv.shape, BF16)],
        compiler_params=_cparams(("arbitrary",)),
        name="memkv",
    )(mem, g, wk, wv)


def _lane_group_select(vals, width):
    lane = lax.broadcasted_iota(jnp.int32, vals[0].shape, 1)
    out = vals[-1]
    for g in range(len(vals) - 2, -1, -1):
        out = jnp.where(lane < (g + 1) * width, vals[g], out)
    return out


def _col_bcast(m, k, width):
    return jnp.broadcast_to(m[:, k:k + 1], (m.shape[0], width))


def _gate_norm_out(y, z, pool_out, x_res, ssm_norm, w_out):
    yn = _rms(y * _silu(z), ssm_norm)
    cat = jnp.concatenate([pool_out, yn], axis=-1).astype(BF16)
    return x_res + _dot(cat, w_out)


def _mix_prompt_kernel(x_ref, g_ref, win_ref, cw_ref, cb_ref,
                       dtb_ref, alog_ref, dsk_ref, sn_ref, wp_ref, ps_ref, wout_ref,
                       o_ref, opool_ref, oconv_ref, ossm_ref, owin_ref,
                       wu_s, wz_s, wx_s, wdt_s, wo_s,
                       ubuf, s2buf, s4buf, s8buf, cbuf, abuf, ht_ref, ybuf, *, tc, nseq):
    c = pl.program_id(1)
    nc = pl.num_programs(1)
    n_pool_slabs = D_POOL // LANES
    n_conv_slabs = CONV_DIM // LANES
    n_chunks = tc // CHUNK
    rp = POOL_HIST + tc
    seqs = range(nseq)

    def slab(s):
        return slice(s * LANES, (s + 1) * LANES)

    @pl.when(jnp.logical_and(pl.program_id(0) == 0, c == 0))
    def _():
        wu_s[...] = win_ref[0:D_POOL, :].T.astype(BF16)
        wz_s[...] = win_ref[D_POOL:D_POOL + D_SSM, :].T.astype(BF16)
        wx_s[...] = win_ref[D_POOL + D_SSM:D_PROJ, :].T.astype(BF16)
        lead = 16 - SSM_HEADS
        place = (lax.broadcasted_iota(jnp.int32, (16, LANES), 0) - lead
                 == lax.broadcasted_iota(jnp.int32, (16, LANES), 1))
        wdt_s[...] = _dot_tn(win_ref[D_PROJ - lead:D_PROJ + SSM_HEADS, :].astype(BF16),
                             jnp.where(place, 1.0, 0.0).astype(BF16)).astype(BF16)
        wo_s[...] = wout_ref[...].astype(BF16)
        owin_ref[:, 0:D_POOL] = wu_s[...]
        owin_ref[:, D_POOL:D_POOL + D_SSM] = wz_s[...]
        owin_ref[:, D_POOL + D_SSM:D_PROJ] = wx_s[...]
        owin_ref[:, D_PROJ:D_PROJ + LANES] = wdt_s[...]

    @pl.when(c == 0)
    def _():
        ubuf[:, :, 0:POOL_HIST, :] = jnp.zeros((nseq, n_pool_slabs, POOL_HIST, LANES), F32)
        cbuf[:, :, 0:CONV_HIST, :] = jnp.zeros((nseq, n_conv_slabs, CONV_HIST, LANES), F32)
        abuf[:, :, 0:SCAN_PAD, :] = jnp.zeros((nseq, n_chunks, SCAN_PAD, LANES), F32)
        ht_ref[...] = jnp.zeros_like(ht_ref)

    lo_half = lax.broadcasted_iota(jnp.int32, (tc, LANES), 1) < POOL_GW
    pos1 = c * tc + lax.broadcasted_iota(jnp.int32, (tc, LANES), 0) + 1
    a_neg = -jnp.exp(alog_ref[...])
    row = lax.broadcasted_iota(jnp.int32, (CHUNK, CHUNK), 0)
    col = lax.broadcasted_iota(jnp.int32, (CHUNK, CHUNK), 1)
    causal = row >= col
    lo = lax.broadcasted_iota(jnp.int32, (CHUNK, LANES), 1) < SSM_HEADDIM
    mask_lo = jnp.where(lo, 1.0, 0.0).astype(BF16)
    mask_hi = jnp.where(lo, 0.0, 1.0).astype(BF16)
    gw = SSM_HPG * SSM_HEADDIM
    n_x = D_SSM // LANES
    st8 = {}

    def head(i):
        x = x_ref[i]
        xn = _rms(x, g_ref[...]).astype(BF16)
        st8[i] = dict(x=x, xn=xn, xbc=[None] * n_conv_slabs)

    def conv_block(i, s0, ns):
        xn = st8[i]["xn"]
        raw = _dot(xn, wx_s[:, s0 * LANES:(s0 + ns) * LANES])
        for t in range(ns):
            s = s0 + t
            cbuf[i, s, CONV_HIST:CONV_HIST + tc, :] = raw[:, slab(t)]
            acc = cbuf[i, s, CONV_HIST:CONV_HIST + tc, :] * cw_ref[CONV_W - 1:CONV_W, slab(s)]
            for j in range(1, CONV_W):
                acc = acc + (cbuf[i, s, CONV_HIST - j:CONV_HIST - j + tc, :]
                             * cw_ref[CONV_W - 1 - j:CONV_W - j, slab(s)])
            st8[i]["xbc"][s] = _silu(acc + cb_ref[:, slab(s)])

    def dt_proj(i):
        st8[i]["dt"] = jax.nn.softplus(_dot(st8[i]["xn"], wdt_s[...]) + dtb_ref[...])

    def pool(i):
        u = _dot(st8[i]["xn"], wu_s[...])
        for s in range(n_pool_slabs):
            ubuf[i, s, POOL_HIST:rp, :] = u[:, slab(s)]
            s2buf[i, s, 8:rp, :] = ubuf[i, s, 8:rp, :] + ubuf[i, s, 7:rp - 1, :]
        s4buf[i, 16:rp, :] = s2buf[i, 1, 16:rp, :] + s2buf[i, 1, 14:rp - 2, :]
        s8buf[i, 24:rp, :] = s4buf[i, 24:rp, :] + s4buf[i, 20:rp - 4, :]
        win_sums = [
            jnp.where(lo_half, s2buf[i, 0, POOL_HIST:rp, :],
                      s2buf[i, 0, POOL_HIST:rp, :] + s2buf[i, 0, POOL_HIST - 2:rp - 2, :]),
            jnp.where(lo_half, s8buf[i, POOL_HIST:rp, :],
                      s8buf[i, POOL_HIST:rp, :] + s8buf[i, POOL_HIST - 8:rp - 8, :]),
        ]
        ps = []
        for s in range(n_pool_slabs):
            win = jnp.where(lo_half, POOL_WINDOWS[2 * s], POOL_WINDOWS[2 * s + 1])
            cnt = jnp.minimum(pos1, win).astype(F32)
            ps.append(win_sums[s] / cnt - u[:, slab(s)])
        st8[i]["pool_out"] = _dot(jnp.concatenate(ps, axis=1).astype(BF16), wp_ref[...]) * ps_ref[...]

    def ssd_chunk(i, j):
        xbc = st8[i]["xbc"]
        sl = slice(j * CHUNK, (j + 1) * CHUNK)
        dt_c = st8[i]["dt"][sl]
        acs = dt_c * a_neg
        sh = 1
        while sh < CHUNK:
            abuf[i, j, SCAN_PAD:SCAN_PAD + CHUNK, :] = acs
            acs = acs + abuf[i, j, SCAN_PAD - sh:SCAN_PAD - sh + CHUNK, :]
            sh *= 2
        acs_last = acs[CHUNK - 1:CHUNK, :]
        fdec = jnp.exp(acs_last - acs) * dt_c
        src_t = (acs - jnp.log(dt_c)).T
        ht = ht_ref[i]
        ht_b = ht.astype(BF16)
        cdec, st = [], []
        for g in range(SSM_GROUPS):
            c_g = xbc[n_x + SSM_GROUPS + g][sl].astype(BF16)
            b_g = xbc[n_x + g][sl].astype(BF16)
            scores = _dot_nt(c_g, b_g)
            y_off = _dot(c_g, ht_b[:, g * gw:(g + 1) * gw])
            xw = []
            for q in range(SSM_HPG // 2):
                k0 = g * SSM_HPG + 2 * q
                pair = k0 // 2
                a_cols = [_col_bcast(acs, k, CHUNK) for k in (k0, k0 + 1)]
                lhs = [(scores * jnp.exp(jnp.where(causal, a_cols[h] - src_t[k0 + h:k0 + h + 1, :], -jnp.inf))
                        ).astype(BF16) for h in range(2)]
                x_pair = xbc[pair][sl]
                x_b = x_pair.astype(BF16)
                rhs = jnp.concatenate([x_b * mask_lo, x_b * mask_hi], axis=0)
                e_pair = jnp.exp(jnp.where(lo, a_cols[0], a_cols[1]))
                ybuf[i, sl, slab(pair)] = (_dot(jnp.concatenate(lhs, axis=1), rhs)
                                           + y_off[:, slab(q)] * e_pair + x_pair * dsk_ref[:, slab(pair)])
                f_pair = jnp.where(lo, _col_bcast(fdec, k0, LANES), _col_bcast(fdec, k0 + 1, LANES))
                xw.append((x_pair * f_pair).astype(BF16))
                cdec.append(e_pair[CHUNK - 1:CHUNK, :])
            st.append(_dot_tn(b_g, jnp.concatenate(xw, axis=1)))
        ht_ref[i] = ht * jnp.concatenate(cdec, axis=1) + jnp.concatenate(st, axis=1)

    def z_proj(i):
        st8[i]["zs"] = _silu(_dot(st8[i]["xn"], wz_s[...]))

    def tail(i):
        d = st8[i]
        yn = _rms(ybuf[i] * d["zs"], sn_ref[...])
        cat = jnp.concatenate([d["pool_out"], yn], axis=-1).astype(BF16)
        o_ref[i] = d["x"] + _dot(cat, wo_s[...])

    def mid(i):
        conv_block(i, n_x, 2 * SSM_GROUPS)
        dt_proj(i)
        for s0 in range(0, n_x, 2):
            conv_block(i, s0, 2)

    for i in seqs:
        head(i)
    for i in seqs:
        mid(i)
    for i in seqs:
        pool(i)
    for j in range(n_chunks):
        for i in seqs:
            ssd_chunk(i, j)
        if j == 0:
            for i in seqs:
                z_proj(i)
    for i in seqs:
        tail(i)

    ubuf[:, :, 0:POOL_HIST, :] = ubuf[:, :, tc:tc + POOL_HIST, :]
    cbuf[:, :, 0:CONV_HIST, :] = cbuf[:, :, tc:tc + CONV_HIST, :]

    @pl.when(c == nc - 1)
    def _():
        for i in seqs:
            for s in range(n_pool_slabs):
                opool_ref[i, :, slab(s)] = ubuf[i, s, POOL_HIST - POOL_BUF:POOL_HIST, :]
            for s in range(n_conv_slabs):
                oconv_ref[i, :, slab(s)] = cbuf[i, s, CONV_HIST - (CONV_W - 1):CONV_HIST, :]
            ossm_ref[i] = ht_ref[i].T.reshape(SSM_HEADS, SSM_HEADDIM, SSM_STATE)


def _mix_prompt(x, w, *, tc, nseq):
    b, s, d = x.shape
    assert s % tc == 0 and tc % CHUNK == 0 and b % nseq == 0
    consts = [w["g_mix"], w["w_in"], w["conv_w"], w["conv_b"],
              w["dt_bias"], w["a_log"], w["d_skip"], w["ssm_norm"], w["w_pool"], w["pool_scale"],
              w["w_out"]]
    tile = pl.BlockSpec((nseq, tc, d), lambda i, j: (i, j, 0))
    return pl.pallas_call(
        functools.partial(_mix_prompt_kernel, tc=tc, nseq=nseq),
        grid=(b // nseq, s // tc),
        in_specs=[tile] + [_const_spec(a.shape) for a in consts],
        out_specs=[
            tile,
            pl.BlockSpec((nseq, POOL_BUF, D_POOL), lambda i, j: (i, 0, 0)),
            pl.BlockSpec((nseq, CONV_W - 1, CONV_DIM), lambda i, j: (i, 0, 0)),
            pl.BlockSpec((nseq, SSM_HEADS, SSM_HEADDIM, SSM_STATE), lambda i, j: (i, 0, 0, 0)),
            pl.BlockSpec((d, D_PROJ + LANES), lambda i, j: (0, 0)),
        ],
        out_shape=[
            jax.ShapeDtypeStruct((b, s, d), F32),
            jax.ShapeDtypeStruct((b, POOL_BUF, D_POOL), F32),
            jax.ShapeDtypeStruct((b, CONV_W - 1, CONV_DIM), F32),
            jax.ShapeDtypeStruct((b, SSM_HEADS, SSM_HEADDIM, SSM_STATE), F32),
            jax.ShapeDtypeStruct((d, D_PROJ + LANES), BF16),
        ],
        scratch_shapes=[
            pltpu.VMEM((d, D_POOL), BF16), pltpu.VMEM((d, D_SSM), BF16),
            pltpu.VMEM((d, CONV_DIM), BF16), pltpu.VMEM((d, LANES), BF16), pltpu.VMEM((d, d), BF16),
            pltpu.VMEM((nseq, D_POOL // LANES, POOL_HIST + tc, LANES), F32),
            pltpu.VMEM((nseq, D_POOL // LANES, POOL_HIST + tc, LANES), F32),
            pltpu.VMEM((nseq, POOL_HIST + tc, LANES), F32),
            pltpu.VMEM((nseq, POOL_HIST + tc, LANES), F32),
            pltpu.VMEM((nseq, CONV_DIM // LANES, CONV_HIST + tc, LANES), F32),
            pltpu.VMEM((nseq, tc // CHUNK, SCAN_PAD + CHUNK, LANES), F32),
            pltpu.VMEM((nseq, SSM_STATE, D_SSM), F32),
            pltpu.VMEM((nseq, tc, D_SSM), F32),
        ],
        compiler_params=_cparams(("arbitrary", "arbitrary")),
        name="mix_prompt",
    )(x, *consts)


def _attn_kernel(x_ref, g_ref, wq_ref, k_ref, v_ref, wo_ref, qs_ref, ks_ref, vs_ref,
                 o_ref, os_ref, owo_ref, wq_s, wo_s, *, bb):
    @pl.when(jnp.logical_and(pl.program_id(0) == 0, pl.program_id(1) == 0))
    def _():
        wq_s[...] = wq_ref[...].astype(BF16)
        wo_s[...] = wo_ref[...].astype(BF16)
        owo_ref[...] = wo_s[...]

    hs = [slice(h * MEM_HD, (h + 1) * MEM_HD) for h in range(MEM_HEADS)]
    scale = MEM_HD ** -0.5
    nq = qs_ref.shape[1]
    rows = N_MEM * MEM_HEADS

    def softmax(s):
        e = jnp.exp(s - jnp.max(s, axis=-1, keepdims=True))
        return (e / jnp.sum(e, axis=-1, keepdims=True)).astype(BF16)

    q_head = lax.broadcasted_iota(jnp.int32, (nq, rows), 0) % MEM_HEADS
    kv_head = lax.broadcasted_iota(jnp.int32, (nq, rows), 1) % MEM_HEADS
    same_head = q_head == kv_head

    q_all = qs_ref[...].reshape(bb * nq, MEM_HD)
    q_all = jnp.concatenate([q_all, jnp.zeros((LANES - bb * nq, MEM_HD), F32)], axis=0).astype(BF16)
    s_s = [_dot_nt(ks_ref[i].reshape(rows, MEM_HD).astype(BF16), q_all) for i in range(bb)]

    x = x_ref[...]
    q = _dot(_rms(x, g_ref[...]).astype(BF16), wq_s[...]).astype(BF16)

    p_s = [softmax(jnp.where(same_head, s_s[i].T[i * nq:(i + 1) * nq] * scale, -jnp.inf)) for i in range(bb)]

    s_p = [_dot_nt(q[:, hs[h]], k_ref[:, hs[h]]) * scale for h in range(MEM_HEADS)]

    for i in range(bb):
        os_ref[i] = _dot(p_s[i], vs_ref[i].reshape(rows, MEM_HD).astype(BF16)).astype(BF16)

    p_p = [softmax(s) for s in s_p]
    o = jnp.concatenate([_dot(p_p[h], v_ref[:, hs[h]]) for h in range(MEM_HEADS)], axis=-1).astype(BF16)
    o_ref[...] = x + _dot(o, wo_s[...])


def _attn(x, k, v, g, wq, wo, qs, ks, vs, *, tq):
    b, s, d = x.shape
    nb, nq, hd = qs.shape
    nj = s // tq
    assert nb % (b * nj) == 0
    bb = nb // (b * nj)
    assert bb * nq <= LANES
    tile = pl.BlockSpec((None, tq, d), lambda i, j: (i, j, 0))
    mem = pl.BlockSpec((None, N_MEM, d), lambda i, j: (i, 0, 0))
    s_q = pl.BlockSpec((bb, nq, hd), lambda i, j: (i * nj + j, 0, 0))
    s_mem = pl.BlockSpec((bb, N_MEM, MEM_HEADS, hd), lambda i, j: (i * nj + j, 0, 0, 0))
    return pl.pallas_call(
        functools.partial(_attn_kernel, bb=bb),
        grid=(b, nj),
        in_specs=[tile, _const_spec(g.shape), _const_spec(wq.shape), mem, mem, _const_spec(wo.shape),
                  s_q, s_mem, s_mem],
        out_specs=[tile, s_q, pl.BlockSpec(wo.shape, lambda i, j: (0, 0))],
        out_shape=[jax.ShapeDtypeStruct((b, s, d), F32), jax.ShapeDtypeStruct((nb, nq, hd), BF16),
                   jax.ShapeDtypeStruct(wo.shape, BF16)],
        scratch_shapes=[pltpu.VMEM(wq.shape, BF16), pltpu.VMEM(wo.shape, BF16)],
        compiler_params=_cparams(("arbitrary", "arbitrary")),
        name="attn",
    )(x, g, wq, k, v, wo, qs, ks, vs)


def _mix_sample_kernel(x_ref, sp_ref, sc_ref, h0_ref, g_ref, win_ref, cw_ref, cb_ref, dtb_ref, alog_ref,
                       dsk_ref, wp_ref, ps_ref, ex_ref, sn_ref, wo_ref, gq_ref, wq_ref,
                       x2_ref, q_ref, npool_ref, nconv_ref, hout_ref,
                       pool_s, z_s, ydx_s, expa_s, xw_s, b_s, c_s, dec_s, y_s, *, nb, t, bb, start_pos):
    j = pl.program_id(0)
    gw = SSM_HPG * SSM_HEADDIM
    n_bc = SSM_GROUPS * SSM_STATE

    def rows(i):
        return slice(i * nb, (i + 1) * nb)

    @pl.when(j == 0)
    def _():
        x = x_ref[...]
        xn = _rms(x, g_ref[...]).astype(BF16)
        z_s[...] = _dot(xn, win_ref[:, D_POOL:D_POOL + D_SSM])

        u = _dot(xn, win_ref[:, 0:D_POOL])
        ext = [sp_ref[rows(i), :] for i in range(POOL_BUF)] + [u[rows(i)] for i in range(t)]
        ps = []
        for i in range(t):
            sums, cnts = [], []
            for w in POOL_WINDOWS:
                s = ext[POOL_BUF + i]
                for k in range(1, w):
                    s = s + ext[POOL_BUF + i - k]
                sums.append(s)
                cnts.append(jnp.full((nb, D_POOL), float(min(start_pos + i + 1, w)), F32))
            mean = _lane_group_select(sums, POOL_GW) / _lane_group_select(cnts, POOL_GW)
            ps.append(mean - ext[POOL_BUF + i])
        p = jnp.concatenate(ps, axis=0).astype(BF16)
        pool_s[...] = _dot(p, wp_ref[...]) * ps_ref[...]
        for i in range(POOL_BUF):
            npool_ref[rows(i), :] = ext[t + i]

        xr = _dot(xn, win_ref[:, D_POOL + D_SSM:D_PROJ])
        cext = [sc_ref[rows(i), :] for i in range(CONV_W - 1)] + [xr[rows(i)] for i in range(t)]
        xbc = []
        for i in range(t):
            acc = cext[i] * cw_ref[0:1, :]
            for k in range(1, CONV_W):
                acc = acc + cext[i + k] * cw_ref[k:k + 1, :]
            xbc.append(_silu(acc + cb_ref[...]))
        for i in range(CONV_W - 1):
            nconv_ref[rows(i), :] = cext[t + i]
        xs = [v[:, :D_SSM] for v in xbc]
        bm = [v[:, D_SSM:D_SSM + n_bc] for v in xbc]
        cm = [v[:, D_SSM + n_bc:] for v in xbc]
        for i in range(t):
            b_s[rows(i), :] = bm[i]
            c_s[rows(i), :] = cm[i]

        dt_all = jax.nn.softplus(_dot(xn, win_ref[:, D_PROJ:D_PROJ + LANES]) + dtb_ref[...])
        a_neg = -jnp.exp(alog_ref[...])
        dt = [dt_all[rows(i)] for i in range(t)]
        acs = []
        for i in range(t):
            da = dt[i] * a_neg
            acs.append(da if i == 0 else acs[-1] + da)
        dec = jnp.exp(acs[-1])
        for k in range(SSM_HEADS):
            dec_s[k] = _col_bcast(dec, k, LANES)

        def expand(v, exact):
            if exact:
                return jnp.dot(v, ex_ref[...], precision=lax.Precision.HIGHEST,
                               preferred_element_type=F32)
            return _dot(v.astype(BF16), ex_ref[...].astype(BF16))

        lane = lax.broadcasted_iota(jnp.int32, (nb, LANES), 1)
        for i in range(t):
            ydx = xs[i] * dsk_ref[...]
            for s in range(i + 1):
                sc = [jnp.sum(cm[i][:, g * SSM_STATE:(g + 1) * SSM_STATE]
                              * bm[s][:, g * SSM_STATE:(g + 1) * SSM_STATE], axis=-1, keepdims=True)
                      for g in range(SSM_GROUPS)]
                sc = jnp.where(lane < SSM_HPG, sc[0], sc[1])
                coef = sc * jnp.exp(acs[i] - acs[s]) * dt[s]
                ydx = ydx + expand(coef, False) * xs[s]
            ydx_s[rows(i), :] = ydx
            expa_s[rows(i), :] = expand(jnp.exp(acs[i]), True)
            xw_s[rows(i), :] = xs[i] * expand(jnp.exp(acs[-1] - acs[i]) * dt[i], True)

    base = pl.multiple_of(j * bb, bb)
    blk_rows = [pl.ds(i * nb + base, bb) for i in range(t)]

    def gather(ref):
        return jnp.concatenate([ref[r, :] for r in blk_rows], axis=0)

    def pad_rows(v):
        return jnp.concatenate([v, jnp.zeros((LANES - t * bb, v.shape[1]), v.dtype)], axis=0)

    c_blk = pad_rows(gather(c_s))
    b_blk = pad_rows(gather(b_s))
    xw_t = pad_rows(gather(xw_s)).T.astype(BF16)
    row_seq = lax.broadcasted_iota(jnp.int32, (LANES, SSM_STATE), 0) % bb
    col_seq = lax.broadcasted_iota(jnp.int32, (SSM_STATE, LANES), 1) % bb
    y_t = []
    for g in range(SSM_GROUPS):
        gs = slice(g * SSM_STATE, (g + 1) * SSM_STATE)
        c_t = c_blk[:, gs].T
        acc = jnp.zeros((gw, LANES), F32)
        for i in range(bb):
            h0 = h0_ref[i, g * SSM_HPG:(g + 1) * SSM_HPG].reshape(gw, SSM_STATE)
            acc = acc + _dot(h0.astype(BF16), jnp.where(col_seq == i, c_t, 0.0).astype(BF16))
            b_i = jnp.where(row_seq == i, b_blk[:, gs], 0.0).astype(BF16)
            st = _dot(xw_t[g * gw:(g + 1) * gw, :], b_i)
            for k in range(SSM_HPG):
                hd = g * SSM_HPG + k
                hout_ref[i, hd] = (h0_ref[i, hd] * dec_s[hd, pl.ds(base + i, 1), :]
                                   + st[k * SSM_HEADDIM:(k + 1) * SSM_HEADDIM])
        y_t.append(acc.T[:t * bb])
    y_blk = gather(ydx_s) + jnp.concatenate(y_t, axis=1) * gather(expa_s)
    for i in range(t):
        y_s[blk_rows[i], :] = y_blk[i * bb:(i + 1) * bb]

    @pl.when(j == pl.num_programs(0) - 1)
    def _():
        x2 = _gate_norm_out(y_s[...], z_s[...], pool_s[...], x_ref[...], sn_ref[...],
                            wo_ref[...].astype(BF16))
        x2_ref[...] = x2
        q_ref[...] = _dot(_rms(x2, gq_ref[...]).astype(BF16), wq_ref[...].astype(BF16))


def _mix_sample(x, sp, sc, h0, w_in_bf, w, *, nb, t, bb, start_pos):
    n, d = x.shape
    assert n == nb * t and nb % bb == 0 and t * bb <= LANES
    consts = [w["g_mix"], w_in_bf, w["conv_w"], w["conv_b"], w["dt_bias"], w["a_log"], w["d_skip"],
              w["w_pool"], w["pool_scale"], w["expand"], w["ssm_norm"], w["w_out"], w["g_xq"], w["w_xq"]]
    hblk = pl.BlockSpec((bb, SSM_HEADS, SSM_HEADDIM, SSM_STATE), lambda i: (i, 0, 0, 0))
    whole = lambda shape: pl.BlockSpec(shape, lambda i: (0,) * len(shape))
    outs = [
        jax.ShapeDtypeStruct((n, d), F32),
        jax.ShapeDtypeStruct((n, d), F32),
        jax.ShapeDtypeStruct((POOL_BUF * nb, D_POOL), F32),
        jax.ShapeDtypeStruct(((CONV_W - 1) * nb, CONV_DIM), F32),
        jax.ShapeDtypeStruct(h0.shape, F32),
    ]
    n_bc = SSM_GROUPS * SSM_STATE
    return pl.pallas_call(
        functools.partial(_mix_sample_kernel, nb=nb, t=t, bb=bb, start_pos=start_pos),
        grid=(nb // bb,),
        in_specs=[_const_spec(x.shape), _const_spec(sp.shape), _const_spec(sc.shape), hblk]
        + [_const_spec(a.shape) for a in consts],
        out_specs=[whole(o.shape) for o in outs[:4]] + [hblk],
        out_shape=outs,
        scratch_shapes=[
            pltpu.VMEM((n, D_POOL), F32),
            pltpu.VMEM((n, D_SSM), F32),
            pltpu.VMEM((n, D_SSM), F32),
            pltpu.VMEM((n, D_SSM), F32),
            pltpu.VMEM((n, D_SSM), F32),
            pltpu.VMEM((n, n_bc), F32),
            pltpu.VMEM((n, n_bc), F32),
            pltpu.VMEM((SSM_HEADS, nb, LANES), F32),
            pltpu.VMEM((n, D_SSM), F32),
        ],
        compiler_params=_cparams(("arbitrary",)),
        name="mix_sample",
    )(x, sp, sc, h0, *consts)


def _prep_weights(g_ffn1, w1_gate, w1_up, w1_down, g_mix, w_in, conv_w, conv_b, dt_bias, a_log,
                  d_skip, ssm_norm, w_pool, pool_scale, w_out, g_mem, w_mem_k, w_mem_v, g_xq,
                  w_xq, w_xo, g_ffn2, w2_gate, w2_up, w2_down, g_final):
    row = lambda v: v.reshape(1, -1).astype(F32)
    pad_heads = lambda v: jnp.pad(v.astype(F32), (0, LANES - SSM_HEADS)).reshape(1, LANES)
    w_pool_bd = jnp.zeros((D_POOL, D_POOL), F32)
    for g in range(len(POOL_WINDOWS)):
        w_pool_bd = w_pool_bd.at[g * POOL_GW:(g + 1) * POOL_GW, g * POOL_GW:(g + 1) * POOL_GW].set(w_pool[g])
    head_of_chan = jnp.arange(D_SSM) // SSM_HEADDIM
    expand = (jnp.arange(LANES)[:, None] == head_of_chan[None, :]).astype(F32)
    return dict(
        g_ffn1=row(g_ffn1), w1_gate=w1_gate, w1_up=w1_up, w1_down=w1_down,
        g_mix=row(g_mix), w_in=jnp.transpose(w_in),
        conv_w=conv_w.astype(F32), conv_b=row(conv_b), dt_bias=pad_heads(dt_bias),
        a_log=pad_heads(a_log), d_skip=row(jnp.repeat(d_skip, SSM_HEADDIM)), ssm_norm=row(ssm_norm),
        w_pool=w_pool_bd.astype(BF16), pool_scale=row(pool_scale), w_out=w_out,
        g_mem=row(g_mem), w_mem_k=w_mem_k, w_mem_v=w_mem_v,
        g_xq=row(g_xq), w_xq=w_xq, w_xo=w_xo,
        g_ffn2=row(g_ffn2), w2_gate=w2_gate, w2_up=w2_up, w2_down=w2_down,
        g_final=row(g_final), expand=expand,
    )


def _layer(x_prompt, x_sample, mem_prompt, mem_k, mem_v, state_pool, state_conv, state_ssm, start_pos, w):
    b, s, d = x_prompt.shape
    nb, t, _ = x_sample.shape
    n = nb * t
    tmaj = lambda a: jnp.swapaxes(a, 0, 1).reshape(-1, a.shape[-1])
    bmaj = lambda a, r: jnp.swapaxes(a.reshape(r, nb, -1), 0, 1)

    k, v, k_b, v_b = _memkv(mem_prompt, w["g_mem"], w["w_mem_k"], w["w_mem_v"])
    xp, xs = _ffn(x_prompt.reshape(b * s, d), tmaj(x_sample), w["g_ffn1"], w["w1_gate"], w["w1_up"],
                  w["w1_down"], w["g_final"], final=False, tm=FFN_ROWS)

    xp, pool_p, conv_p, ssm_p, w_in_bf = _mix_prompt(xp.reshape(b, s, d), w, tc=MIX_ROWS, nseq=MIX_SEQS)

    x2, q, new_pool, new_conv, ssm_s = _mix_sample(xs, tmaj(state_pool), tmaj(state_conv), state_ssm,
                                                   w_in_bf, w, nb=nb, t=t, bb=STATE_BLOCK,
                                                   start_pos=start_pos)

    xp, o, w_xo_bf = _attn(xp, k_b, v_b, w["g_xq"], w["w_xq"], w["w_xo"],
                           bmaj(q, t).reshape(nb, t * MEM_HEADS, MEM_HD), mem_k, mem_v, tq=ATTN_ROWS)

    yp, ys = _ffn(xp.reshape(b * s, d), bmaj(x2, t).reshape(n, d), w["g_ffn2"], w["w2_gate"], w["w2_up"],
                  w["w2_down"], w["g_final"], final=True, tm=FFN_ROWS,
                  attn_out=(o.reshape(n, d), w_xo_bf))
    return (yp.reshape(b, s, d), ys.reshape(nb, t, d), k, v, pool_p, conv_p, ssm_p,
            bmaj(new_pool, POOL_BUF), bmaj(new_conv, CONV_W - 1), ssm_s)


def kernel(x_prompt, x_sample, mem_prompt, cache_mem_k, cache_mem_v, state_pool, state_conv, state_ssm,
           g_ffn1, w1_gate, w1_up, w1_down, g_mix, w_in, conv_w, conv_b, dt_bias, a_log, d_skip,
           ssm_norm, w_pool, pool_scale, w_out, g_mem, w_mem_k, w_mem_v, g_xq, w_xq, w_xo,
           g_ffn2, w2_gate, w2_up, w2_down, g_final):
    assert g_ffn1.shape[0] == 1, "single-layer model"
    w = _prep_weights(g_ffn1[0], w1_gate[0], w1_up[0], w1_down[0], g_mix[0], w_in[0], conv_w[0],
                      conv_b[0], dt_bias[0], a_log[0], d_skip[0], ssm_norm[0], w_pool[0],
                      pool_scale[0], w_out[0], g_mem[0], w_mem_k[0], w_mem_v[0], g_xq[0], w_xq[0],
                      w_xo[0], g_ffn2[0], w2_gate[0], w2_up[0], w2_down[0], g_final)
    bp = x_prompt.shape[0]
    y_p, y_s, k_p, v_p, pool_p, conv_p, ssm_p, pool_s, conv_s, ssm_s = _layer(
        x_prompt, x_sample, mem_prompt, cache_mem_k[0], cache_mem_v[0], state_pool[0], state_conv[0],
        state_ssm[0], PAST_LEN, w)
    kv_shape = (1, bp, N_MEM, MEM_HEADS, MEM_HD)
    return (y_p, y_s, k_p.reshape(kv_shape), v_p.reshape(kv_shape), pool_p[None], conv_p[None],
            ssm_p[None], pool_s[None], conv_s[None], ssm_s[None])
```

```python
import functools

import jax
import jax.numpy as jnp
from jax import lax
from jax.experimental import pallas as pl
from jax.experimental.pallas import tpu as pltpu

F32 = jnp.float32
BF16 = jnp.bfloat16

D_POOL = 256
POOL_WINDOWS = (2, 4, 8, 16)
POOL_GW = 64
POOL_BUF = 15
D_SSM = 768
SSM_HEADDIM = 64
SSM_HEADS = 12
SSM_GROUPS = 2
SSM_HPG = 6
SSM_STATE = 128
CONV_W = 4
CONV_DIM = 1280
D_PROJ = D_POOL + D_SSM + CONV_DIM
N_MEM = 256
MEM_HEADS = 4
MEM_HD = 256
PAST_LEN = 16384
EPS = 1e-6
LANES = 128
CHUNK = 128
POOL_HIST = 32
CONV_HIST = 8
SCAN_PAD = CHUNK // 2

VMEM_LIMIT = 56 * 1024 * 1024
VMEM_LIMIT_FFN = 60 * 1024 * 1024
FFN_ROWS = 1024
FFN_W_STEPS = 16
MIX_ROWS = 256
MIX_SEQS = 2
ATTN_ROWS = 512
STATE_BLOCK = 8


def _cparams(sem, vmem_limit=VMEM_LIMIT):
    return pltpu.CompilerParams(dimension_semantics=sem, vmem_limit_bytes=vmem_limit)


def _const_spec(shape):
    nd = len(shape)
    return pl.BlockSpec(shape, lambda *_: (0,) * nd, pipeline_mode=pl.Buffered(1))


def _tm_rows(ref):
    return jnp.concatenate([ref[:, i, :] for i in range(ref.shape[1])], axis=0)


def _store_tm_rows(ref, rows):
    nb = ref.shape[0]
    for i in range(ref.shape[1]):
        ref[:, i, :] = rows[i * nb:(i + 1) * nb]


def _rms(x, g):
    ms = jnp.mean(x * x, axis=-1, keepdims=True)
    return x * lax.rsqrt(ms + EPS) * g


def _silu(x):
    return x * jax.nn.sigmoid(x)


def _dot(a, b):
    return jnp.dot(a, b, preferred_element_type=F32)


def _dot_nt(a, b):
    return lax.dot_general(a, b, (((1,), (1,)), ((), ())), preferred_element_type=F32)


def _dot_tn(a, b):
    return lax.dot_general(a, b, (((0,), (0,)), ((), ())), preferred_element_type=F32)


def _ffn_kernel(xp_ref, xs_ref, g_ref, wg_ref, wu_ref, wd_ref, gf_ref, *rest, final, n_p, attn_out):
    if attn_out:
        ao_ref, wxo_ref, op_ref, os_ref, wg_s, wu_s, wd_s = rest
    else:
        op_ref, os_ref, wg_s, wu_s, wd_s = rest
    i = pl.program_id(0)
    rows_gu = wg_ref.shape[0]
    rows_d = wd_ref.shape[0]

    @pl.when(i < FFN_W_STEPS)
    def _():
        r = pl.multiple_of(i * rows_gu, rows_gu)
        wg_s[pl.ds(r, rows_gu), :] = wg_ref[...].astype(BF16)
        wu_s[pl.ds(r, rows_gu), :] = wu_ref[...].astype(BF16)
        r = pl.multiple_of(i * rows_d, rows_d)
        wd_s[pl.ds(r, rows_d), :] = wd_ref[...].astype(BF16)

    def tile(x):
        xn = _rms(x, g_ref[...]).astype(BF16)
        gate = _dot(xn, wg_s[...])
        up = _dot(xn, wu_s[...])
        h = (_silu(gate) * up).astype(BF16)
        out = x + 0.5 * _dot(h, wd_s[...])
        return _rms(out, gf_ref[...]) if final else out

    @pl.when(jnp.logical_and(i >= FFN_W_STEPS, i < FFN_W_STEPS + n_p))
    def _():
        op_ref[...] = tile(xp_ref[...])

    @pl.when(i == FFN_W_STEPS + n_p)
    def _():
        x = _tm_rows(xs_ref)
        if attn_out:
            nb, nq, hd = ao_ref.shape
            heads = wxo_ref.shape[0] // hd
            o = jnp.concatenate(
                [jnp.concatenate([ao_ref[:, t * heads + h, :] for h in range(heads)], axis=1)
                 for t in range(nq // heads)], axis=0)
            x = x + _dot(o.astype(BF16), wxo_ref[...])
        out = tile(x)
        if os_ref.ndim == 3:
            _store_tm_rows(os_ref, out)
        else:
            os_ref[...] = out


def _ffn(xp, xs, g, wg, wu, wd, gf, *, final, tm, attn_out=None):
    tp, d = xp.shape
    nb, t, _ = xs.shape
    os_shape = xs.shape if final else (nb * t, d)
    dff = wg.shape[1]
    assert tp % tm == 0 and d % FFN_W_STEPS == 0 and dff % FFN_W_STEPS == 0
    n_p = tp // tm
    p_tile = pl.BlockSpec((tm, d), lambda i: (jnp.clip(i - FFN_W_STEPS, 0, n_p - 1), 0))
    s_tile = pl.BlockSpec(os_shape, lambda i: (0,) * len(os_shape))
    w_chunk = lambda rows, cols: pl.BlockSpec((rows, cols), lambda i: (jnp.minimum(i, FFN_W_STEPS - 1), 0))
    extra = list(attn_out) if attn_out else []
    return pl.pallas_call(
        functools.partial(_ffn_kernel, final=final, n_p=n_p, attn_out=bool(attn_out)),
        grid=(FFN_W_STEPS + n_p + 1,),
        in_specs=[
            p_tile, _const_spec(xs.shape), _const_spec(g.shape),
            w_chunk(d // FFN_W_STEPS, dff), w_chunk(d // FFN_W_STEPS, dff), w_chunk(dff // FFN_W_STEPS, d),
            _const_spec(gf.shape),
        ] + [_const_spec(a.shape) for a in extra],
        out_specs=[p_tile, s_tile],
        out_shape=[jax.ShapeDtypeStruct(xp.shape, F32), jax.ShapeDtypeStruct(os_shape, F32)],
        scratch_shapes=[pltpu.VMEM((d, dff), BF16), pltpu.VMEM((d, dff), BF16), pltpu.VMEM((dff, d), BF16)],
        compiler_params=_cparams(("arbitrary",), VMEM_LIMIT_FFN),
        name="ffn_final" if final else "ffn",
    )(xp, xs, g, wg, wu, wd, gf, *extra)


def _memkv_kernel(m_ref, g_ref, wk_ref, wv_ref, k_ref, v_ref, kb_ref, vb_ref, wk_s, wv_s):
    @pl.when(pl.program_id(0) == 0)
    def _():
        wk_s[...] = wk_ref[...].astype(BF16)
        wv_s[...] = wv_ref[...].astype(BF16)

    mn = _rms(m_ref[...], g_ref[...]).astype(BF16)
    k = _dot(mn, wk_s[...])
    v = _dot(mn, wv_s[...])
    for h in range(MEM_HEADS):
        k_ref[:, h, :] = k[:, h * MEM_HD:(h + 1) * MEM_HD]
        v_ref[:, h, :] = v[:, h * MEM_HD:(h + 1) * MEM_HD]
    kb_ref[...] = k.astype(BF16)
    vb_ref[...] = v.astype(BF16)


def _memkv(mem, g, wk, wv):
    b, n_mem, d = mem.shape
    row_blk = pl.BlockSpec((None, n_mem, d), lambda i: (i, 0, 0))
    head_blk = pl.BlockSpec((None, n_mem, MEM_HEADS, MEM_HD), lambda i: (i, 0, 0, 0))
    return pl.pallas_call(
        _memkv_kernel,
        grid=(b,),
        in_specs=[row_blk, _const_spec(g.shape), _const_spec(wk.shape), _const_spec(wv.shape)],
        out_specs=[head_blk, head_blk, row_blk, row_blk],
        out_shape=[jax.ShapeDtypeStruct((b, n_mem, MEM_HEADS, MEM_HD), F32)] * 2
        + [jax.ShapeDtypeStruct((b, n_mem, d), BF16)] * 2,
        scratch_shapes=[pltpu.VMEM(wk.shape, BF16), pltpu.VMEM(wv.shape, BF16)],
        compiler_params=_cparams(("arbitrary",)),
        name="memkv",
    )(mem, g, wk, wv)


def _lane_group_select(vals, width):
    lane = lax.broadcasted_iota(jnp.int32, vals[0].shape, 1)
    out = vals[-1]
    for g in range(len(vals) - 2, -1, -1):
        out = jnp.where(lane < (g + 1) * width, vals[g], out)
    return out


def _col_bcast(m, k, width):
    return jnp.broadcast_to(m[:, k:k + 1], (m.shape[0], width))


def _gate_norm_out(y, z, pool_out, x_res, ssm_norm, w_out):
    yn = _rms(y * _silu(z), ssm_norm)
    cat = jnp.concatenate([pool_out, yn], axis=-1).astype(BF16)
    return x_res + _dot(cat, w_out)


def _mix_prompt_kernel(x_ref, g_ref, win_ref, cw_ref, cb_ref,
                       dtb_ref, alog_ref, dsk_ref, sn_ref, wp_ref, ps_ref, wout_ref,
                       o_ref, opool_ref, oconv_ref, ossm_ref, owin_ref,
                       wu_s, wz_s, wx_s, wdt_s, wo_s,
                       ubuf, s2buf, s4buf, s8buf, cbuf, abuf, ht_ref, ybuf, *, tc, nseq):
    c = pl.program_id(1)
    nc = pl.num_programs(1)
    n_pool_slabs = D_POOL // LANES
    n_conv_slabs = CONV_DIM // LANES
    n_chunks = tc // CHUNK
    rp = POOL_HIST + tc
    seqs = range(nseq)

    def slab(s):
        return slice(s * LANES, (s + 1) * LANES)

    @pl.when(jnp.logical_and(pl.program_id(0) == 0, c == 0))
    def _():
        wu_s[...] = win_ref[0:D_POOL, :].T.astype(BF16)
        wz_s[...] = win_ref[D_POOL:D_POOL + D_SSM, :].T.astype(BF16)
        wx_s[...] = win_ref[D_POOL + D_SSM:D_PROJ, :].T.astype(BF16)
        lead = 16 - SSM_HEADS
        place = (lax.broadcasted_iota(jnp.int32, (16, LANES), 0) - lead
                 == lax.broadcasted_iota(jnp.int32, (16, LANES), 1))
        wdt_s[...] = _dot_tn(win_ref[D_PROJ - lead:D_PROJ + SSM_HEADS, :].astype(BF16),
                             jnp.where(place, 1.0, 0.0).astype(BF16)).astype(BF16)
        wo_s[...] = wout_ref[...].astype(BF16)
        owin_ref[:, 0:D_POOL] = wu_s[...]
        owin_ref[:, D_POOL:D_POOL + D_SSM] = wz_s[...]
        owin_ref[:, D_POOL + D_SSM:D_PROJ] = wx_s[...]
        owin_ref[:, D_PROJ:D_PROJ + LANES] = wdt_s[...]

    @pl.when(c == 0)
    def _():
        ubuf[:, :, 0:POOL_HIST, :] = jnp.zeros((nseq, n_pool_slabs, POOL_HIST, LANES), F32)
        cbuf[:, :, 0:CONV_HIST, :] = jnp.zeros((nseq, n_conv_slabs, CONV_HIST, LANES), F32)
        abuf[:, :, 0:SCAN_PAD, :] = jnp.zeros((nseq, n_chunks, SCAN_PAD, LANES), F32)
        ht_ref[...] = jnp.zeros_like(ht_ref)

    lo_half = lax.broadcasted_iota(jnp.int32, (tc, LANES), 1) < POOL_GW
    pos1 = c * tc + lax.broadcasted_iota(jnp.int32, (tc, LANES), 0) + 1
    a_neg = -jnp.exp(alog_ref[...])
    row = lax.broadcasted_iota(jnp.int32, (CHUNK, CHUNK), 0)
    col = lax.broadcasted_iota(jnp.int32, (CHUNK, CHUNK), 1)
    causal = row >= col
    lo = lax.broadcasted_iota(jnp.int32, (CHUNK, LANES), 1) < SSM_HEADDIM
    mask_lo = jnp.where(lo, 1.0, 0.0).astype(BF16)
    mask_hi = jnp.where(lo, 0.0, 1.0).astype(BF16)
    gw = SSM_HPG * SSM_HEADDIM
    n_x = D_SSM // LANES
    st8 = {}

    def head(i):
        x = x_ref[i]
        xn = _rms(x, g_ref[...]).astype(BF16)
        st8[i] = dict(x=x, xn=xn, xbc=[None] * n_conv_slabs)

    def conv_block(i, s0, ns):
        xn = st8[i]["xn"]
        raw = _dot(xn, wx_s[:, s0 * LANES:(s0 + ns) * LANES])
        for t in range(ns):
            s = s0 + t
            cbuf[i, s, CONV_HIST:CONV_HIST + tc, :] = raw[:, slab(t)]
            acc = cbuf[i, s, CONV_HIST:CONV_HIST + tc, :] * cw_ref[CONV_W - 1:CONV_W, slab(s)]
            for j in range(1, CONV_W):
                acc = acc + (cbuf[i, s, CONV_HIST - j:CONV_HIST - j + tc, :]
                             * cw_ref[CONV_W - 1 - j:CONV_W - j, slab(s)])
            st8[i]["xbc"][s] = _silu(acc + cb_ref[:, slab(s)])

    def dt_proj(i):
        st8[i]["dt"] = jax.nn.softplus(_dot(st8[i]["xn"], wdt_s[...]) + dtb_ref[...])

    def pool(i):
        u = _dot(st8[i]["xn"], wu_s[...])
        for s in range(n_pool_slabs):
            ubuf[i, s, POOL_HIST:rp, :] = u[:, slab(s)]
            s2buf[i, s, 8:rp, :] = ubuf[i, s, 8:rp, :] + ubuf[i, s, 7:rp - 1, :]
        s4buf[i, 16:rp, :] = s2buf[i, 1, 16:rp, :] + s2buf[i, 1, 14:rp - 2, :]
        s8buf[i, 24:rp, :] = s4buf[i, 24:rp, :] + s4buf[i, 20:rp - 4, :]
        win_sums = [
            jnp.where(lo_half, s2buf[i, 0, POOL_HIST:rp, :],
                      s2buf[i, 0, POOL_HIST:rp, :] + s2buf[i, 0, POOL_HIST - 2:rp - 2, :]),
            jnp.where(lo_half, s8buf[i, POOL_HIST:rp, :],
                      s8buf[i, POOL_HIST:rp, :] + s8buf[i, POOL_HIST - 8:rp - 8, :]),
        ]
        ps = []
        for s in range(n_pool_slabs):
            win = jnp.where(lo_half, POOL_WINDOWS[2 * s], POOL_WINDOWS[2 * s + 1])
            cnt = jnp.minimum(pos1, win).astype(F32)
            ps.append(win_sums[s] / cnt - u[:, slab(s)])
        st8[i]["pool_out"] = _dot(jnp.concatenate(ps, axis=1).astype(BF16), wp_ref[...]) * ps_ref[...]

    def ssd_chunk(i, j):
        xbc = st8[i]["xbc"]
        sl = slice(j * CHUNK, (j + 1) * CHUNK)
        dt_c = st8[i]["dt"][sl]
        acs = dt_c * a_neg
        sh = 1
        while sh < CHUNK:
            abuf[i, j, SCAN_PAD:SCAN_PAD + CHUNK, :] = acs
            acs = acs + abuf[i, j, SCAN_PAD - sh:SCAN_PAD - sh + CHUNK, :]
            sh *= 2
        acs_last = acs[CHUNK - 1:CHUNK, :]
        fdec = jnp.exp(acs_last - acs) * dt_c
        src_t = (acs - jnp.log(dt_c)).T
        ht = ht_ref[i]
        ht_b = ht.astype(BF16)
        cdec, st = [], []
        for g in range(SSM_GROUPS):
            c_g = xbc[n_x + SSM_GROUPS + g][sl].astype(BF16)
            b_g = xbc[n_x + g][sl].astype(BF16)
            scores = _dot_nt(c_g, b_g)
            y_off = _dot(c_g, ht_b[:, g * gw:(g + 1) * gw])
            xw = []
            for q in range(SSM_HPG // 2):
                k0 = g * SSM_HPG + 2 * q
                pair = k0 // 2
                a_cols = [_col_bcast(acs, k, CHUNK) for k in (k0, k0 + 1)]
                lhs = [(scores * jnp.exp(jnp.where(causal, a_cols[h] - src_t[k0 + h:k0 + h + 1, :], -jnp.inf))
                        ).astype(BF16) for h in range(2)]
                x_pair = xbc[pair][sl]
                x_b = x_pair.astype(BF16)
                rhs = jnp.concatenate([x_b * mask_lo, x_b * mask_hi], axis=0)
                e_pair = jnp.exp(jnp.where(lo, a_cols[0], a_cols[1]))
                ybuf[i, sl, slab(pair)] = (_dot(jnp.concatenate(lhs, axis=1), rhs)
                                           + y_off[:, slab(q)] * e_pair + x_pair * dsk_ref[:, slab(pair)])
                f_pair = jnp.where(lo, _col_bcast(fdec, k0, LANES), _col_bcast(fdec, k0 + 1, LANES))
                xw.append((x_pair * f_pair).astype(BF16))
                cdec.append(e_pair[CHUNK - 1:CHUNK, :])
            st.append(_dot_tn(b_g, jnp.concatenate(xw, axis=1)))
        ht_ref[i] = ht * jnp.concatenate(cdec, axis=1) + jnp.concatenate(st, axis=1)

    def z_proj(i):
        st8[i]["zs"] = _silu(_dot(st8[i]["xn"], wz_s[...]))

    def tail(i):
        d = st8[i]
        yn = _rms(ybuf[i] * d["zs"], sn_ref[...])
        cat = jnp.concatenate([d["pool_out"], yn], axis=-1).astype(BF16)
        o_ref[i] = d["x"] + _dot(cat, wo_s[...])

    def mid(i):
        conv_block(i, n_x, 2 * SSM_GROUPS)
        dt_proj(i)
        for s0 in range(0, n_x, 2):
            conv_block(i, s0, 2)

    for i in seqs:
        head(i)
    for i in seqs:
        mid(i)
    for i in seqs:
        pool(i)
    for j in range(n_chunks):
        for i in seqs:
            ssd_chunk(i, j)
        if j == 0:
            for i in seqs:
                z_proj(i)
    for i in seqs:
        tail(i)

    ubuf[:, :, 0:POOL_HIST, :] = ubuf[:, :, tc:tc + POOL_HIST, :]
    cbuf[:, :, 0:CONV_HIST, :] = cbuf[:, :, tc:tc + CONV_HIST, :]

    @pl.when(c == nc - 1)
    def _():
        for i in seqs:
            for s in range(n_pool_slabs):
                opool_ref[i, :, slab(s)] = ubuf[i, s, POOL_HIST - POOL_BUF:POOL_HIST, :]
            for s in range(n_conv_slabs):
                oconv_ref[i, :, slab(s)] = cbuf[i, s, CONV_HIST - (CONV_W - 1):CONV_HIST, :]
            ossm_ref[i] = ht_ref[i].T.reshape(SSM_HEADS, SSM_HEADDIM, SSM_STATE)


def _mix_prompt(x, w, *, tc, nseq):
    b, s, d = x.shape
    assert s % tc == 0 and tc % CHUNK == 0 and b % nseq == 0
    consts = [w["g_mix"], w["w_in"], w["conv_w"], w["conv_b"],
              w["dt_bias"], w["a_log"], w["d_skip"], w["ssm_norm"], w["w_pool"], w["pool_scale"],
              w["w_out"]]
    tile = pl.BlockSpec((nseq, tc, d), lambda i, j: (i, j, 0))
    return pl.pallas_call(
        functools.partial(_mix_prompt_kernel, tc=tc, nseq=nseq),
        grid=(b // nseq, s // tc),
        in_specs=[tile] + [_const_spec(a.shape) for a in consts],
        out_specs=[
            tile,
            pl.BlockSpec((nseq, POOL_BUF, D_POOL), lambda i, j: (i, 0, 0)),
            pl.BlockSpec((nseq, CONV_W - 1, CONV_DIM), lambda i, j: (i, 0, 0)),
            pl.BlockSpec((nseq, SSM_HEADS, SSM_HEADDIM, SSM_STATE), lambda i, j: (i, 0, 0, 0)),
            pl.BlockSpec((d, D_PROJ + LANES), lambda i, j: (0, 0)),
        ],
        out_shape=[
            jax.ShapeDtypeStruct((b, s, d), F32),
            jax.ShapeDtypeStruct((b, POOL_BUF, D_POOL), F32),
            jax.ShapeDtypeStruct((b, CONV_W - 1, CONV_DIM), F32),
            jax.ShapeDtypeStruct((b, SSM_HEADS, SSM_HEADDIM, SSM_STATE), F32),
            jax.ShapeDtypeStruct((d, D_PROJ + LANES), BF16),
        ],
        scratch_shapes=[
            pltpu.VMEM((d, D_POOL), BF16), pltpu.VMEM((d, D_SSM), BF16),
            pltpu.VMEM((d, CONV_DIM), BF16), pltpu.VMEM((d, LANES), BF16), pltpu.VMEM((d, d), BF16),
            pltpu.VMEM((nseq, D_POOL // LANES, POOL_HIST + tc, LANES), F32),
            pltpu.VMEM((nseq, D_POOL // LANES, POOL_HIST + tc, LANES), F32),
            pltpu.VMEM((nseq, POOL_HIST + tc, LANES), F32),
            pltpu.VMEM((nseq, POOL_HIST + tc, LANES), F32),
            pltpu.VMEM((nseq, CONV_DIM // LANES, CONV_HIST + tc, LANES), F32),
            pltpu.VMEM((nseq, tc // CHUNK, SCAN_PAD + CHUNK, LANES), F32),
            pltpu.VMEM((nseq, SSM_STATE, D_SSM), F32),
            pltpu.VMEM((nseq, tc, D_SSM), F32),
        ],
        compiler_params=_cparams(("arbitrary", "arbitrary")),
        name="mix_prompt",
    )(x, *consts)


def _attn_kernel(x_ref, g_ref, wq_ref, k_ref, v_ref, wo_ref, qs_ref, ks_ref, vs_ref,
                 o_ref, os_ref, owo_ref, wq_s, wo_s, *, bb):
    @pl.when(jnp.logical_and(pl.program_id(0) == 0, pl.program_id(1) == 0))
    def _():
        wq_s[...] = wq_ref[...].astype(BF16)
        wo_s[...] = wo_ref[...].astype(BF16)
        owo_ref[...] = wo_s[...]

    hs = [slice(h * MEM_HD, (h + 1) * MEM_HD) for h in range(MEM_HEADS)]
    scale = MEM_HD ** -0.5
    nq = qs_ref.shape[1]
    rows = N_MEM * MEM_HEADS

    def softmax(s):
        e = jnp.exp(s - jnp.max(s, axis=-1, keepdims=True))
        return (e / jnp.sum(e, axis=-1, keepdims=True)).astype(BF16)

    q_head = lax.broadcasted_iota(jnp.int32, (nq, rows), 0) % MEM_HEADS
    kv_head = lax.broadcasted_iota(jnp.int32, (nq, rows), 1) % MEM_HEADS
    same_head = q_head == kv_head

    q_all = qs_ref[...].reshape(bb * nq, MEM_HD)
    q_all = jnp.concatenate([q_all, jnp.zeros((LANES - bb * nq, MEM_HD), F32)], axis=0).astype(BF16)
    s_s = [_dot_nt(ks_ref[i].reshape(rows, MEM_HD).astype(BF16), q_all) for i in range(bb)]

    x = x_ref[...]
    q = _dot(_rms(x, g_ref[...]).astype(BF16), wq_s[...]).astype(BF16)

    p_s = [softmax(jnp.where(same_head, s_s[i].T[i * nq:(i + 1) * nq] * scale, -jnp.inf)) for i in range(bb)]

    s_p = [_dot_nt(q[:, hs[h]], k_ref[:, hs[h]]) * scale for h in range(MEM_HEADS)]

    for i in range(bb):
        os_ref[i] = _dot(p_s[i], vs_ref[i].reshape(rows, MEM_HD).astype(BF16))

    p_p = [softmax(s) for s in s_p]
    o = jnp.concatenate([_dot(p_p[h], v_ref[:, hs[h]]) for h in range(MEM_HEADS)], axis=-1).astype(BF16)
    o_ref[...] = x + _dot(o, wo_s[...])


def _attn(x, k, v, g, wq, wo, qs, ks, vs, *, tq):
    b, s, d = x.shape
    nb, nq, hd = qs.shape
    nj = s // tq
    assert nb % (b * nj) == 0
    bb = nb // (b * nj)
    assert bb * nq <= LANES
    tile = pl.BlockSpec((None, tq, d), lambda i, j: (i, j, 0))
    mem = pl.BlockSpec((None, N_MEM, d), lambda i, j: (i, 0, 0))
    s_q = pl.BlockSpec((bb, nq, hd), lambda i, j: (i * nj + j, 0, 0))
    s_mem = pl.BlockSpec((bb, N_MEM, MEM_HEADS, hd), lambda i, j: (i * nj + j, 0, 0, 0))
    return pl.pallas_call(
        functools.partial(_attn_kernel, bb=bb),
        grid=(b, nj),
        in_specs=[tile, _const_spec(g.shape), _const_spec(wq.shape), mem, mem, _const_spec(wo.shape),
                  s_q, s_mem, s_mem],
        out_specs=[tile, s_q, pl.BlockSpec(wo.shape, lambda i, j: (0, 0))],
        out_shape=[jax.ShapeDtypeStruct((b, s, d), F32), jax.ShapeDtypeStruct((nb, nq, hd), F32),
                   jax.ShapeDtypeStruct(wo.shape, BF16)],
        scratch_shapes=[pltpu.VMEM(wq.shape, BF16), pltpu.VMEM(wo.shape, BF16)],
        compiler_params=_cparams(("arbitrary", "arbitrary")),
        name="attn",
    )(x, g, wq, k, v, wo, qs, ks, vs)


def _mix_sample_kernel(x_ref, sp_ref, sc_ref, h0_ref, g_ref, win_ref, cw_ref, cb_ref, dtb_ref, alog_ref,
                       dsk_ref, wp_ref, ps_ref, ex_ref, sn_ref, wo_ref, gq_ref, wq_ref,
                       x2_ref, q_ref, npool_ref, nconv_ref, hout_ref,
                       pool_s, z_s, ydx_s, expa_s, xw_s, b_s, c_s, dec_s, y_s, *, nb, t, bb, start_pos):
    j = pl.program_id(0)
    gw = SSM_HPG * SSM_HEADDIM
    n_bc = SSM_GROUPS * SSM_STATE

    def rows(i):
        return slice(i * nb, (i + 1) * nb)

    @pl.when(j == 0)
    def _():
        x = x_ref[...]
        xn = _rms(x, g_ref[...]).astype(BF16)
        z_s[...] = _dot(xn, win_ref[:, D_POOL:D_POOL + D_SSM])

        u = _dot(xn, win_ref[:, 0:D_POOL])
        ext = [sp_ref[rows(i), :] for i in range(POOL_BUF)] + [u[rows(i)] for i in range(t)]
        ps = []
        for i in range(t):
            sums, cnts = [], []
            for w in POOL_WINDOWS:
                s = ext[POOL_BUF + i]
                for k in range(1, w):
                    s = s + ext[POOL_BUF + i - k]
                sums.append(s)
                cnts.append(jnp.full((nb, D_POOL), float(min(start_pos + i + 1, w)), F32))
            mean = _lane_group_select(sums, POOL_GW) / _lane_group_select(cnts, POOL_GW)
            ps.append(mean - ext[POOL_BUF + i])
        p = jnp.concatenate(ps, axis=0).astype(BF16)
        pool_s[...] = _dot(p, wp_ref[...]) * ps_ref[...]
        for i in range(POOL_BUF):
            npool_ref[rows(i), :] = ext[t + i]

        xr = _dot(xn, win_ref[:, D_POOL + D_SSM:D_PROJ])
        cext = [sc_ref[rows(i), :] for i in range(CONV_W - 1)] + [xr[rows(i)] for i in range(t)]
        xbc = []
        for i in range(t):
            acc = cext[i] * cw_ref[0:1, :]
            for k in range(1, CONV_W):
                acc = acc + cext[i + k] * cw_ref[k:k + 1, :]
            xbc.append(_silu(acc + cb_ref[...]))
        for i in range(CONV_W - 1):
            nconv_ref[rows(i), :] = cext[t + i]
        xs = [v[:, :D_SSM] for v in xbc]
        bm = [v[:, D_SSM:D_SSM + n_bc] for v in xbc]
        cm = [v[:, D_SSM + n_bc:] for v in xbc]
        for i in range(t):
            b_s[rows(i), :] = bm[i]
            c_s[rows(i), :] = cm[i]

        dt_all = jax.nn.softplus(_dot(xn, win_ref[:, D_PROJ:D_PROJ + LANES]) + dtb_ref[...])
        a_neg = -jnp.exp(alog_ref[...])
        dt = [dt_all[rows(i)] for i in range(t)]
        acs = []
        for i in range(t):
            da = dt[i] * a_neg
            acs.append(da if i == 0 else acs[-1] + da)
        dec = jnp.exp(acs[-1])
        for k in range(SSM_HEADS):
            dec_s[k] = _col_bcast(dec, k, LANES)

        def expand(v, exact):
            if exact:
                return jnp.dot(v, ex_ref[...], precision=lax.Precision.HIGHEST,
                               preferred_element_type=F32)
            return _dot(v.astype(BF16), ex_ref[...].astype(BF16))

        lane = lax.broadcasted_iota(jnp.int32, (nb, LANES), 1)
        for i in range(t):
            ydx = xs[i] * dsk_ref[...]
            for s in range(i + 1):
                sc = [jnp.sum(cm[i][:, g * SSM_STATE:(g + 1) * SSM_STATE]
                              * bm[s][:, g * SSM_STATE:(g + 1) * SSM_STATE], axis=-1, keepdims=True)
                      for g in range(SSM_GROUPS)]
                sc = jnp.where(lane < SSM_HPG, sc[0], sc[1])
                coef = sc * jnp.exp(acs[i] - acs[s]) * dt[s]
                ydx = ydx + expand(coef, False) * xs[s]
            ydx_s[rows(i), :] = ydx
            expa_s[rows(i), :] = expand(jnp.exp(acs[i]), True)
            xw_s[rows(i), :] = xs[i] * expand(jnp.exp(acs[-1] - acs[i]) * dt[i], True)

    base = pl.multiple_of(j * bb, bb)
    blk_rows = [pl.ds(i * nb + base, bb) for i in range(t)]

    def gather(ref):
        return jnp.concatenate([ref[r, :] for r in blk_rows], axis=0)

    def pad_rows(v):
        return jnp.concatenate([v, jnp.zeros((LANES - t * bb, v.shape[1]), v.dtype)], axis=0)

    c_blk = pad_rows(gather(c_s))
    b_blk = pad_rows(gather(b_s))
    xw_t = pad_rows(gather(xw_s)).T.astype(BF16)
    row_seq = lax.broadcasted_iota(jnp.int32, (LANES, SSM_STATE), 0) % bb
    col_seq = lax.broadcasted_iota(jnp.int32, (SSM_STATE, LANES), 1) % bb
    y_t = []
    for g in range(SSM_GROUPS):
        gs = slice(g * SSM_STATE, (g + 1) * SSM_STATE)
        c_t = c_blk[:, gs].T
        acc = jnp.zeros((gw, LANES), F32)
        for i in range(bb):
            h0 = h0_ref[i, g * SSM_HPG:(g + 1) * SSM_HPG].reshape(gw, SSM_STATE)
            acc = acc + _dot(h0.astype(BF16), jnp.where(col_seq == i, c_t, 0.0).astype(BF16))
            b_i = jnp.where(row_seq == i, b_blk[:, gs], 0.0).astype(BF16)
            st = _dot(xw_t[g * gw:(g + 1) * gw, :], b_i)
            for k in range(SSM_HPG):
                hd = g * SSM_HPG + k
                hout_ref[i, hd] = (h0_ref[i, hd] * dec_s[hd, pl.ds(base + i, 1), :]
                                   + st[k * SSM_HEADDIM:(k + 1) * SSM_HEADDIM])
        y_t.append(acc.T[:t * bb])
    y_blk = gather(ydx_s) + jnp.concatenate(y_t, axis=1) * gather(expa_s)
    for i in range(t):
        y_s[blk_rows[i], :] = y_blk[i * bb:(i + 1) * bb]

    @pl.when(j == pl.num_programs(0) - 1)
    def _():
        x2 = _gate_norm_out(y_s[...], z_s[...], pool_s[...], x_ref[...], sn_ref[...],
                            wo_ref[...].astype(BF16))
        _store_tm_rows(x2_ref, x2)
        q = _dot(_rms(x2, gq_ref[...]).astype(BF16), wq_ref[...].astype(BF16))
        for i in range(t):
            for h in range(MEM_HEADS):
                q_ref[:, i * MEM_HEADS + h, :] = q[rows(i), h * MEM_HD:(h + 1) * MEM_HD]


def _mix_sample(x, sp, sc, h0, w_in_bf, w, *, nb, t, bb, start_pos):
    n, d = x.shape
    assert n == nb * t and nb % bb == 0 and t * bb <= LANES
    consts = [w["g_mix"], w_in_bf, w["conv_w"], w["conv_b"], w["dt_bias"], w["a_log"], w["d_skip"],
              w["w_pool"], w["pool_scale"], w["expand"], w["ssm_norm"], w["w_out"], w["g_xq"], w["w_xq"]]
    hblk = pl.BlockSpec((bb, SSM_HEADS, SSM_HEADDIM, SSM_STATE), lambda i: (i, 0, 0, 0))
    whole = _const_spec
    outs = [
        jax.ShapeDtypeStruct((nb, t, d), F32),
        jax.ShapeDtypeStruct((nb, t * MEM_HEADS, MEM_HD), F32),
        jax.ShapeDtypeStruct((POOL_BUF * nb, D_POOL), F32),
        jax.ShapeDtypeStruct(((CONV_W - 1) * nb, CONV_DIM), F32),
        jax.ShapeDtypeStruct(h0.shape, F32),
    ]
    n_bc = SSM_GROUPS * SSM_STATE
    return pl.pallas_call(
        functools.partial(_mix_sample_kernel, nb=nb, t=t, bb=bb, start_pos=start_pos),
        grid=(nb // bb,),
        in_specs=[_const_spec(x.shape), _const_spec(sp.shape), _const_spec(sc.shape), hblk]
        + [_const_spec(a.shape) for a in consts],
        out_specs=[whole(o.shape) for o in outs[:4]] + [hblk],
        out_shape=outs,
        scratch_shapes=[
            pltpu.VMEM((n, D_POOL), F32),
            pltpu.VMEM((n, D_SSM), F32),
            pltpu.VMEM((n, D_SSM), F32),
            pltpu.VMEM((n, D_SSM), F32),
            pltpu.VMEM((n, D_SSM), F32),
            pltpu.VMEM((n, n_bc), F32),
            pltpu.VMEM((n, n_bc), F32),
            pltpu.VMEM((SSM_HEADS, nb, LANES), F32),
            pltpu.VMEM((n, D_SSM), F32),
        ],
        compiler_params=_cparams(("arbitrary",)),
        name="mix_sample",
    )(x, sp, sc, h0, *consts)


def _prep_weights(g_ffn1, w1_gate, w1_up, w1_down, g_mix, w_in, conv_w, conv_b, dt_bias, a_log,
                  d_skip, ssm_norm, w_pool, pool_scale, w_out, g_mem, w_mem_k, w_mem_v, g_xq,
                  w_xq, w_xo, g_ffn2, w2_gate, w2_up, w2_down, g_final):
    row = lambda v: v.reshape(1, -1).astype(F32)
    pad_heads = lambda v: jnp.pad(v.astype(F32), (0, LANES - SSM_HEADS)).reshape(1, LANES)
    w_pool_bd = jnp.zeros((D_POOL, D_POOL), F32)
    for g in range(len(POOL_WINDOWS)):
        w_pool_bd = w_pool_bd.at[g * POOL_GW:(g + 1) * POOL_GW, g * POOL_GW:(g + 1) * POOL_GW].set(w_pool[g])
    head_of_chan = jnp.arange(D_SSM) // SSM_HEADDIM
    expand = (jnp.arange(LANES)[:, None] == head_of_chan[None, :]).astype(F32)
    return dict(
        g_ffn1=row(g_ffn1), w1_gate=w1_gate, w1_up=w1_up, w1_down=w1_down,
        g_mix=row(g_mix), w_in=jnp.transpose(w_in),
        conv_w=conv_w.astype(F32), conv_b=row(conv_b), dt_bias=pad_heads(dt_bias),
        a_log=pad_heads(a_log), d_skip=row(jnp.repeat(d_skip, SSM_HEADDIM)), ssm_norm=row(ssm_norm),
        w_pool=w_pool_bd.astype(BF16), pool_scale=row(pool_scale), w_out=w_out,
        g_mem=row(g_mem), w_mem_k=w_mem_k, w_mem_v=w_mem_v,
        g_xq=row(g_xq), w_xq=w_xq, w_xo=w_xo,
        g_ffn2=row(g_ffn2), w2_gate=w2_gate, w2_up=w2_up, w2_down=w2_down,
        g_final=row(g_final), expand=expand,
    )


def _layer(x_prompt, x_sample, mem_prompt, mem_k, mem_v, state_pool, state_conv, state_ssm, start_pos, w):
    b, s, d = x_prompt.shape
    nb, t, _ = x_sample.shape
    tmaj = lambda a: jnp.swapaxes(a, 0, 1).reshape(-1, a.shape[-1])
    bmaj = lambda a, r: jnp.swapaxes(a.reshape(r, nb, -1), 0, 1)

    k, v, k_b, v_b = _memkv(mem_prompt, w["g_mem"], w["w_mem_k"], w["w_mem_v"])
    xp, xs = _ffn(x_prompt.reshape(b * s, d), x_sample, w["g_ffn1"], w["w1_gate"], w["w1_up"],
                  w["w1_down"], w["g_final"], final=False, tm=FFN_ROWS)

    xp, pool_p, conv_p, ssm_p, w_in_bf = _mix_prompt(xp.reshape(b, s, d), w, tc=MIX_ROWS, nseq=MIX_SEQS)

    x2, q, pool_s, conv_s, ssm_s = _mix_sample(xs, tmaj(state_pool), tmaj(state_conv), state_ssm, w_in_bf, w,
                                               nb=nb, t=t, bb=STATE_BLOCK, start_pos=start_pos)

    xp, o, w_xo_bf = _attn(xp, k_b, v_b, w["g_xq"], w["w_xq"], w["w_xo"], q, mem_k, mem_v, tq=ATTN_ROWS)

    yp, ys = _ffn(xp.reshape(b * s, d), x2, w["g_ffn2"], w["w2_gate"], w["w2_up"], w["w2_down"],
                  w["g_final"], final=True, tm=FFN_ROWS, attn_out=(o, w_xo_bf))
    return (yp.reshape(b, s, d), ys, k, v, pool_p, conv_p, ssm_p,
            bmaj(pool_s, POOL_BUF), bmaj(conv_s, CONV_W - 1), ssm_s)


def kernel(x_prompt, x_sample, mem_prompt, cache_mem_k, cache_mem_v, state_pool, state_conv, state_ssm,
           g_ffn1, w1_gate, w1_up, w1_down, g_mix, w_in, conv_w, conv_b, dt_bias, a_log, d_skip,
           ssm_norm, w_pool, pool_scale, w_out, g_mem, w_mem_k, w_mem_v, g_xq, w_xq, w_xo,
           g_ffn2, w2_gate, w2_up, w2_down, g_final):
    assert g_ffn1.shape[0] == 1, "single-layer model"
    w = _prep_weights(g_ffn1[0], w1_gate[0], w1_up[0], w1_down[0], g_mix[0], w_in[0], conv_w[0],
                      conv_b[0], dt_bias[0], a_log[0], d_skip[0], ssm_norm[0], w_pool[0],
                      pool_scale[0], w_out[0], g_mem[0], w_mem_k[0], w_mem_v[0], g_xq[0], w_xq[0],
                      w_xo[0], g_ffn2[0], w2_gate[0], w2_up[0], w2_down[0], g_final)
    bp = x_prompt.shape[0]
    y_p, y_s, k_p, v_p, pool_p, conv_p, ssm_p, pool_s, conv_s, ssm_s = _layer(
        x_prompt, x_sample, mem_prompt, cache_mem_k[0], cache_mem_v[0], state_pool[0], state_conv[0],
        state_ssm[0], PAST_LEN, w)
    kv_shape = (1, bp, N_MEM, MEM_HEADS, MEM_HD)
    return (y_p, y_s, k_p.reshape(kv_shape), v_p.reshape(kv_shape), pool_p[None], conv_p[None],
            ssm_p[None], pool_s[None], conv_s[None], ssm_s[None])
```

```python
import functools

import jax
import jax.numpy as jnp
from jax import lax
from jax.experimental import pallas as pl
from jax.experimental.pallas import tpu as pltpu

F32 = jnp.float32
BF16 = jnp.bfloat16

D_POOL = 256
POOL_WINDOWS = (2, 4, 8, 16)
POOL_GW = 64
POOL_BUF = 15
D_SSM = 768
SSM_HEADDIM = 64
SSM_HEADS = 12
SSM_GROUPS = 2
SSM_HPG = 6
SSM_STATE = 128
CONV_W = 4
CONV_DIM = 1280
D_PROJ = D_POOL + D_SSM + CONV_DIM
N_MEM = 256
MEM_HEADS = 4
MEM_HD = 256
PAST_LEN = 16384
EPS = 1e-6
LANES = 128
CHUNK = 128
POOL_HIST = 32
CONV_HIST = 8
SCAN_PAD = CHUNK // 2

VMEM_LIMIT = 56 * 1024 * 1024
VMEM_LIMIT_FFN = 60 * 1024 * 1024
FFN_ROWS = 1024
FFN1_ROWS = 512
FFN_W_STEPS = 16
MEM_W_STEPS = 4
MIX_ROWS = 256
MIX_SEQS = 2
ATTN_ROWS = 512
STATE_BLOCK = 8


def _cparams(sem, vmem_limit=VMEM_LIMIT):
    return pltpu.CompilerParams(dimension_semantics=sem, vmem_limit_bytes=vmem_limit)


def _const_spec(shape):
    nd = len(shape)
    return pl.BlockSpec(shape, lambda *_: (0,) * nd, pipeline_mode=pl.Buffered(1))


def _tm_rows(ref):
    return jnp.concatenate([ref[:, i, :] for i in range(ref.shape[1])], axis=0)


def _store_tm_rows(ref, rows):
    nb = ref.shape[0]
    for i in range(ref.shape[1]):
        ref[:, i, :] = rows[i * nb:(i + 1) * nb]


def _rms(x, g):
    ms = jnp.mean(x * x, axis=-1, keepdims=True)
    return x * lax.rsqrt(ms + EPS) * g


def _silu(x):
    return x * jax.nn.sigmoid(x)


def _dot(a, b):
    return jnp.dot(a, b, preferred_element_type=F32)


def _dot_nt(a, b):
    return lax.dot_general(a, b, (((1,), (1,)), ((), ())), preferred_element_type=F32)


def _dot_tn(a, b):
    return lax.dot_general(a, b, (((0,), (0,)), ((), ())), preferred_element_type=F32)


def _ffn_kernel(xp_ref, xs_ref, g_ref, wg_ref, wu_ref, wd_ref, gf_ref, *rest, final, n_p, attn_out, n_mem_seq):
    rest = list(rest)
    if attn_out:
        ao_ref, wxo_ref = rest[:2]
        rest = rest[2:]
    if n_mem_seq:
        m_ref, gm_ref, wk_ref, wv_ref = rest[:4]
        op_ref, os_ref, k_ref, v_ref, kb_ref, vb_ref, wg_s, wu_s, wd_s, wk_s, wv_s = rest[4:]
    else:
        op_ref, os_ref, wg_s, wu_s, wd_s = rest
    i = pl.program_id(0)
    rows_gu = wg_ref.shape[0]
    rows_d = wd_ref.shape[0]

    @pl.when(i < FFN_W_STEPS)
    def _():
        r = pl.multiple_of(i * rows_gu, rows_gu)
        wg_s[pl.ds(r, rows_gu), :] = wg_ref[...].astype(BF16)
        wu_s[pl.ds(r, rows_gu), :] = wu_ref[...].astype(BF16)
        r = pl.multiple_of(i * rows_d, rows_d)
        wd_s[pl.ds(r, rows_d), :] = wd_ref[...].astype(BF16)

    if n_mem_seq:
        rows_m = wk_ref.shape[0]

        @pl.when(i < MEM_W_STEPS)
        def _():
            r = pl.multiple_of(i * rows_m, rows_m)
            wk_s[pl.ds(r, rows_m), :] = wk_ref[...].astype(BF16)
            wv_s[pl.ds(r, rows_m), :] = wv_ref[...].astype(BF16)

        @pl.when(jnp.logical_and(i >= MEM_W_STEPS, i < MEM_W_STEPS + n_mem_seq))
        def _():
            mn = _rms(m_ref[...], gm_ref[...]).astype(BF16)
            k = _dot(mn, wk_s[...])
            v = _dot(mn, wv_s[...])
            for h in range(MEM_HEADS):
                k_ref[:, h, :] = k[:, h * MEM_HD:(h + 1) * MEM_HD]
                v_ref[:, h, :] = v[:, h * MEM_HD:(h + 1) * MEM_HD]
            kb_ref[...] = k.astype(BF16)
            vb_ref[...] = v.astype(BF16)

    def tile(x):
        xn = _rms(x, g_ref[...]).astype(BF16)
        gate = _dot(xn, wg_s[...])
        up = _dot(xn, wu_s[...])
        h = (_silu(gate) * up).astype(BF16)
        out = x + 0.5 * _dot(h, wd_s[...])
        return _rms(out, gf_ref[...]) if final else out

    @pl.when(jnp.logical_and(i >= FFN_W_STEPS, i < FFN_W_STEPS + n_p))
    def _():
        op_ref[...] = tile(xp_ref[...])

    @pl.when(i == FFN_W_STEPS + n_p)
    def _():
        x = _tm_rows(xs_ref)
        if attn_out:
            nb, nq, hd = ao_ref.shape
            heads = wxo_ref.shape[0] // hd
            o = jnp.concatenate(
                [jnp.concatenate([ao_ref[:, t * heads + h, :] for h in range(heads)], axis=1)
                 for t in range(nq // heads)], axis=0)
            x = x + _dot(o.astype(BF16), wxo_ref[...])
        out = tile(x)
        if os_ref.ndim == 3:
            _store_tm_rows(os_ref, out)
        else:
            os_ref[...] = out


def _ffn(xp, xs, g, wg, wu, wd, gf, *, final, tm, attn_out=None, memkv=None):
    tp, d = xp.shape
    nb, t, _ = xs.shape
    os_shape = xs.shape if final else (nb * t, d)
    dff = wg.shape[1]
    assert tp % tm == 0 and d % FFN_W_STEPS == 0 and dff % FFN_W_STEPS == 0
    n_p = tp // tm
    p_tile = pl.BlockSpec((tm, d), lambda i: (jnp.clip(i - FFN_W_STEPS, 0, n_p - 1), 0))
    s_tile = pl.BlockSpec(os_shape, lambda i: (0,) * len(os_shape))
    w_chunk = lambda rows, cols: pl.BlockSpec((rows, cols), lambda i: (jnp.minimum(i, FFN_W_STEPS - 1), 0))
    ins = [xp, xs, g, wg, wu, wd, gf]
    in_specs = [
        p_tile, _const_spec(xs.shape), _const_spec(g.shape),
        w_chunk(d // FFN_W_STEPS, dff), w_chunk(d // FFN_W_STEPS, dff), w_chunk(dff // FFN_W_STEPS, d),
        _const_spec(gf.shape),
    ]
    out_specs = [p_tile, s_tile]
    out_shape = [jax.ShapeDtypeStruct(xp.shape, F32), jax.ShapeDtypeStruct(os_shape, F32)]
    scratch = [pltpu.VMEM((d, dff), BF16), pltpu.VMEM((d, dff), BF16), pltpu.VMEM((dff, d), BF16)]
    if attn_out:
        ins += list(attn_out)
        in_specs += [_const_spec(a.shape) for a in attn_out]
    n_mem_seq = 0
    if memkv:
        mem, gm, wk, wv = memkv
        n_mem_seq, n_mem, _ = mem.shape
        assert MEM_W_STEPS + n_mem_seq <= FFN_W_STEPS and d % MEM_W_STEPS == 0
        seq = lambda i: jnp.clip(i - MEM_W_STEPS, 0, n_mem_seq - 1)
        row_blk = pl.BlockSpec((None, n_mem, d), lambda i: (seq(i), 0, 0))
        head_blk = pl.BlockSpec((None, n_mem, MEM_HEADS, MEM_HD), lambda i: (seq(i), 0, 0, 0))
        wm_chunk = pl.BlockSpec((d // MEM_W_STEPS, d), lambda i: (jnp.minimum(i, MEM_W_STEPS - 1), 0))
        ins += [mem, gm, wk, wv]
        in_specs += [row_blk, _const_spec(gm.shape), wm_chunk, wm_chunk]
        out_specs += [head_blk, head_blk, row_blk, row_blk]
        out_shape += ([jax.ShapeDtypeStruct((n_mem_seq, n_mem, MEM_HEADS, MEM_HD), F32)] * 2
                      + [jax.ShapeDtypeStruct((n_mem_seq, n_mem, d), BF16)] * 2)
        scratch += [pltpu.VMEM((d, d), BF16), pltpu.VMEM((d, d), BF16)]
    return pl.pallas_call(
        functools.partial(_ffn_kernel, final=final, n_p=n_p, attn_out=bool(attn_out), n_mem_seq=n_mem_seq),
        grid=(FFN_W_STEPS + n_p + 1,),
        in_specs=in_specs,
        out_specs=out_specs,
        out_shape=out_shape,
        scratch_shapes=scratch,
        compiler_params=_cparams(("arbitrary",), VMEM_LIMIT_FFN),
        name="ffn_final" if final else "ffn",
    )(*ins)


def _lane_group_select(vals, width):
    lane = lax.broadcasted_iota(jnp.int32, vals[0].shape, 1)
    out = vals[-1]
    for g in range(len(vals) - 2, -1, -1):
        out = jnp.where(lane < (g + 1) * width, vals[g], out)
    return out


def _col_bcast(m, k, width):
    return jnp.broadcast_to(m[:, k:k + 1], (m.shape[0], width))


def _gate_norm_out(y, z, pool_out, x_res, ssm_norm, w_out):
    yn = _rms(y * _silu(z), ssm_norm)
    cat = jnp.concatenate([pool_out, yn], axis=-1).astype(BF16)
    return x_res + _dot(cat, w_out)


def _mix_prompt_kernel(x_ref, g_ref, win_ref, cw_ref, cb_ref,
                       dtb_ref, alog_ref, dsk_ref, sn_ref, wp_ref, ps_ref, wout_ref,
                       o_ref, opool_ref, oconv_ref, ossm_ref, owin_ref,
                       wu_s, wz_s, wx_s, wdt_s, wo_s,
                       ubuf, s2buf, s4buf, s8buf, cbuf, abuf, ht_ref, ybuf, *, tc, nseq):
    c = pl.program_id(1)
    nc = pl.num_programs(1)
    n_pool_slabs = D_POOL // LANES
    n_conv_slabs = CONV_DIM // LANES
    n_chunks = tc // CHUNK
    rp = POOL_HIST + tc
    seqs = range(nseq)

    def slab(s):
        return slice(s * LANES, (s + 1) * LANES)

    @pl.when(jnp.logical_and(pl.program_id(0) == 0, c == 0))
    def _():
        wu_s[...] = win_ref[0:D_POOL, :].T.astype(BF16)
        wz_s[...] = win_ref[D_POOL:D_POOL + D_SSM, :].T.astype(BF16)
        wx_s[...] = win_ref[D_POOL + D_SSM:D_PROJ, :].T.astype(BF16)
        lead = 16 - SSM_HEADS
        place = (lax.broadcasted_iota(jnp.int32, (16, LANES), 0) - lead
                 == lax.broadcasted_iota(jnp.int32, (16, LANES), 1))
        wdt_s[...] = _dot_tn(win_ref[D_PROJ - lead:D_PROJ + SSM_HEADS, :].astype(BF16),
                             jnp.where(place, 1.0, 0.0).astype(BF16)).astype(BF16)
        wo_s[...] = wout_ref[...].astype(BF16)
        owin_ref[:, 0:D_POOL] = wu_s[...]
        owin_ref[:, D_POOL:D_POOL + D_SSM] = wz_s[...]
        owin_ref[:, D_POOL + D_SSM:D_PROJ] = wx_s[...]
        owin_ref[:, D_PROJ:D_PROJ + LANES] = wdt_s[...]

    @pl.when(c == 0)
    def _():
        ubuf[:, :, 0:POOL_HIST, :] = jnp.zeros((nseq, n_pool_slabs, POOL_HIST, LANES), F32)
        cbuf[:, :, 0:CONV_HIST, :] = jnp.zeros((nseq, n_conv_slabs, CONV_HIST, LANES), F32)
        abuf[:, :, 0:SCAN_PAD, :] = jnp.zeros((nseq, n_chunks, SCAN_PAD, LANES), F32)
        ht_ref[...] = jnp.zeros_like(ht_ref)

    lo_half = lax.broadcasted_iota(jnp.int32, (tc, LANES), 1) < POOL_GW
    pos1 = c * tc + lax.broadcasted_iota(jnp.int32, (tc, LANES), 0) + 1
    a_neg = -jnp.exp(alog_ref[...])
    row = lax.broadcasted_iota(jnp.int32, (CHUNK, CHUNK), 0)
    col = lax.broadcasted_iota(jnp.int32, (CHUNK, CHUNK), 1)
    causal = row >= col
    lo = lax.broadcasted_iota(jnp.int32, (CHUNK, LANES), 1) < SSM_HEADDIM
    mask_lo = jnp.where(lo, 1.0, 0.0).astype(BF16)
    mask_hi = jnp.where(lo, 0.0, 1.0).astype(BF16)
    gw = SSM_HPG * SSM_HEADDIM
    n_x = D_SSM // LANES
    st8 = {}

    def head(i):
        x = x_ref[i]
        xn = _rms(x, g_ref[...]).astype(BF16)
        st8[i] = dict(x=x, xn=xn, xbc=[None] * n_conv_slabs)

    def conv_block(i, s0, ns):
        xn = st8[i]["xn"]
        raw = _dot(xn, wx_s[:, s0 * LANES:(s0 + ns) * LANES])
        for t in range(ns):
            s = s0 + t
            cbuf[i, s, CONV_HIST:CONV_HIST + tc, :] = raw[:, slab(t)]
            acc = cbuf[i, s, CONV_HIST:CONV_HIST + tc, :] * cw_ref[CONV_W - 1:CONV_W, slab(s)]
            for j in range(1, CONV_W):
                acc = acc + (cbuf[i, s, CONV_HIST - j:CONV_HIST - j + tc, :]
                             * cw_ref[CONV_W - 1 - j:CONV_W - j, slab(s)])
            st8[i]["xbc"][s] = _silu(acc + cb_ref[:, slab(s)])

    def dt_proj(i):
        st8[i]["dt"] = jax.nn.softplus(_dot(st8[i]["xn"], wdt_s[...]) + dtb_ref[...])

    def pool(i):
        u = _dot(st8[i]["xn"], wu_s[...])
        for s in range(n_pool_slabs):
            ubuf[i, s, POOL_HIST:rp, :] = u[:, slab(s)]
            s2buf[i, s, 8:rp, :] = ubuf[i, s, 8:rp, :] + ubuf[i, s, 7:rp - 1, :]
        s4buf[i, 16:rp, :] = s2buf[i, 1, 16:rp, :] + s2buf[i, 1, 14:rp - 2, :]
        s8buf[i, 24:rp, :] = s4buf[i, 24:rp, :] + s4buf[i, 20:rp - 4, :]
        win_sums = [
            jnp.where(lo_half, s2buf[i, 0, POOL_HIST:rp, :],
                      s2buf[i, 0, POOL_HIST:rp, :] + s2buf[i, 0, POOL_HIST - 2:rp - 2, :]),
            jnp.where(lo_half, s8buf[i, POOL_HIST:rp, :],
                      s8buf[i, POOL_HIST:rp, :] + s8buf[i, POOL_HIST - 8:rp - 8, :]),
        ]
        ps = []
        for s in range(n_pool_slabs):
            win = jnp.where(lo_half, POOL_WINDOWS[2 * s], POOL_WINDOWS[2 * s + 1])
            cnt = jnp.minimum(pos1, win).astype(F32)
            ps.append(win_sums[s] / cnt - u[:, slab(s)])
        st8[i]["pool_out"] = _dot(jnp.concatenate(ps, axis=1).astype(BF16), wp_ref[...]) * ps_ref[...]

    def ssd_chunk(i, j):
        xbc = st8[i]["xbc"]
        sl = slice(j * CHUNK, (j + 1) * CHUNK)
        dt_c = st8[i]["dt"][sl]
        acs = dt_c * a_neg
        sh = 1
        while sh < CHUNK:
            abuf[i, j, SCAN_PAD:SCAN_PAD + CHUNK, :] = acs
            acs = acs + abuf[i, j, SCAN_PAD - sh:SCAN_PAD - sh + CHUNK, :]
            sh *= 2
        acs_last = acs[CHUNK - 1:CHUNK, :]
        fdec = jnp.exp(acs_last - acs) * dt_c
        src_t = (acs - jnp.log(dt_c)).T
        ht = ht_ref[i]
        ht_b = ht.astype(BF16)
        cdec, st = [], []
        for g in range(SSM_GROUPS):
            c_g = xbc[n_x + SSM_GROUPS + g][sl].astype(BF16)
            b_g = xbc[n_x + g][sl].astype(BF16)
            scores = _dot_nt(c_g, b_g)
            y_off = _dot(c_g, ht_b[:, g * gw:(g + 1) * gw])
            xw = []
            for q in range(SSM_HPG // 2):
                k0 = g * SSM_HPG + 2 * q
                pair = k0 // 2
                a_cols = [_col_bcast(acs, k, CHUNK) for k in (k0, k0 + 1)]
                lhs = [(scores * jnp.exp(jnp.where(causal, a_cols[h] - src_t[k0 + h:k0 + h + 1, :], -jnp.inf))
                        ).astype(BF16) for h in range(2)]
                x_pair = xbc[pair][sl]
                x_b = x_pair.astype(BF16)
                rhs = jnp.concatenate([x_b * mask_lo, x_b * mask_hi], axis=0)
                e_pair = jnp.exp(jnp.where(lo, a_cols[0], a_cols[1]))
                ybuf[i, sl, slab(pair)] = (_dot(jnp.concatenate(lhs, axis=1), rhs)
                                           + y_off[:, slab(q)] * e_pair + x_pair * dsk_ref[:, slab(pair)])
                f_pair = jnp.where(lo, _col_bcast(fdec, k0, LANES), _col_bcast(fdec, k0 + 1, LANES))
                xw.append((x_pair * f_pair).astype(BF16))
                cdec.append(e_pair[CHUNK - 1:CHUNK, :])
            st.append(_dot_tn(b_g, jnp.concatenate(xw, axis=1)))
        ht_ref[i] = ht * jnp.concatenate(cdec, axis=1) + jnp.concatenate(st, axis=1)

    def z_proj(i):
        st8[i]["zs"] = _silu(_dot(st8[i]["xn"], wz_s[...]))

    def tail(i):
        d = st8[i]
        yn = _rms(ybuf[i] * d["zs"], sn_ref[...])
        cat = jnp.concatenate([d["pool_out"], yn], axis=-1).astype(BF16)
        o_ref[i] = d["x"] + _dot(cat, wo_s[...])

    def mid(i):
        conv_block(i, n_x, 2 * SSM_GROUPS)
        dt_proj(i)
        for s0 in range(0, n_x, 2):
            conv_block(i, s0, 2)

    for i in seqs:
        head(i)
    for i in seqs:
        mid(i)
    for i in seqs:
        pool(i)
    for j in range(n_chunks):
        for i in seqs:
            ssd_chunk(i, j)
        if j == 0:
            for i in seqs:
                z_proj(i)
    for i in seqs:
        tail(i)

    ubuf[:, :, 0:POOL_HIST, :] = ubuf[:, :, tc:tc + POOL_HIST, :]
    cbuf[:, :, 0:CONV_HIST, :] = cbuf[:, :, tc:tc + CONV_HIST, :]

    @pl.when(c == nc - 1)
    def _():
        for i in seqs:
            for s in range(n_pool_slabs):
                opool_ref[i, :, slab(s)] = ubuf[i, s, POOL_HIST - POOL_BUF:POOL_HIST, :]
            for s in range(n_conv_slabs):
                oconv_ref[i, :, slab(s)] = cbuf[i, s, CONV_HIST - (CONV_W - 1):CONV_HIST, :]
            ossm_ref[i] = ht_ref[i].T.reshape(SSM_HEADS, SSM_HEADDIM, SSM_STATE)


def _mix_prompt(x, w, *, tc, nseq):
    b, s, d = x.shape
    assert s % tc == 0 and tc % CHUNK == 0 and b % nseq == 0
    consts = [w["g_mix"], w["w_in"], w["conv_w"], w["conv_b"],
              w["dt_bias"], w["a_log"], w["d_skip"], w["ssm_norm"], w["w_pool"], w["pool_scale"],
              w["w_out"]]
    tile = pl.BlockSpec((nseq, tc, d), lambda i, j: (i, j, 0))
    return pl.pallas_call(
        functools.partial(_mix_prompt_kernel, tc=tc, nseq=nseq),
        grid=(b // nseq, s // tc),
        in_specs=[tile] + [_const_spec(a.shape) for a in consts],
        out_specs=[
            tile,
            pl.BlockSpec((nseq, POOL_BUF, D_POOL), lambda i, j: (i, 0, 0)),
            pl.BlockSpec((nseq, CONV_W - 1, CONV_DIM), lambda i, j: (i, 0, 0)),
            pl.BlockSpec((nseq, SSM_HEADS, SSM_HEADDIM, SSM_STATE), lambda i, j: (i, 0, 0, 0)),
            pl.BlockSpec((d, D_PROJ + LANES), lambda i, j: (0, 0)),
        ],
        out_shape=[
            jax.ShapeDtypeStruct((b, s, d), F32),
            jax.ShapeDtypeStruct((b, POOL_BUF, D_POOL), F32),
            jax.ShapeDtypeStruct((b, CONV_W - 1, CONV_DIM), F32),
            jax.ShapeDtypeStruct((b, SSM_HEADS, SSM_HEADDIM, SSM_STATE), F32),
            jax.ShapeDtypeStruct((d, D_PROJ + LANES), BF16),
        ],
        scratch_shapes=[
            pltpu.VMEM((d, D_POOL), BF16), pltpu.VMEM((d, D_SSM), BF16),
            pltpu.VMEM((d, CONV_DIM), BF16), pltpu.VMEM((d, LANES), BF16), pltpu.VMEM((d, d), BF16),
            pltpu.VMEM((nseq, D_POOL // LANES, POOL_HIST + tc, LANES), F32),
            pltpu.VMEM((nseq, D_POOL // LANES, POOL_HIST + tc, LANES), F32),
            pltpu.VMEM((nseq, POOL_HIST + tc, LANES), F32),
            pltpu.VMEM((nseq, POOL_HIST + tc, LANES), F32),
            pltpu.VMEM((nseq, CONV_DIM // LANES, CONV_HIST + tc, LANES), F32),
            pltpu.VMEM((nseq, tc // CHUNK, SCAN_PAD + CHUNK, LANES), F32),
            pltpu.VMEM((nseq, SSM_STATE, D_SSM), F32),
            pltpu.VMEM((nseq, tc, D_SSM), F32),
        ],
        compiler_params=_cparams(("arbitrary", "arbitrary")),
        name="mix_prompt",
    )(x, *consts)


def _attn_kernel(x_ref, g_ref, wq_ref, k_ref, v_ref, wo_ref, qs_ref, ks_ref, vs_ref,
                 o_ref, os_ref, owo_ref, wq_s, wo_s, *, bb):
    @pl.when(jnp.logical_and(pl.program_id(0) == 0, pl.program_id(1) == 0))
    def _():
        wq_s[...] = wq_ref[...].astype(BF16)
        wo_s[...] = wo_ref[...].astype(BF16)
        owo_ref[...] = wo_s[...]

    hs = [slice(h * MEM_HD, (h + 1) * MEM_HD) for h in range(MEM_HEADS)]
    scale = MEM_HD ** -0.5
    nq = qs_ref.shape[1]
    rows = N_MEM * MEM_HEADS

    def softmax(s):
        e = jnp.exp(s - jnp.max(s, axis=-1, keepdims=True))
        return (e / jnp.sum(e, axis=-1, keepdims=True)).astype(BF16)

    q_head = lax.broadcasted_iota(jnp.int32, (nq, rows), 0) % MEM_HEADS
    kv_head = lax.broadcasted_iota(jnp.int32, (nq, rows), 1) % MEM_HEADS
    same_head = q_head == kv_head

    q_all = qs_ref[...].reshape(bb * nq, MEM_HD)
    q_all = jnp.concatenate([q_all, jnp.zeros((LANES - bb * nq, MEM_HD), F32)], axis=0).astype(BF16)
    s_s = [_dot_nt(ks_ref[i].reshape(rows, MEM_HD).astype(BF16), q_all) for i in range(bb)]

    x = x_ref[...]
    q = _dot(_rms(x, g_ref[...]).astype(BF16), wq_s[...]).astype(BF16)

    p_s = [softmax(jnp.where(same_head, s_s[i].T[i * nq:(i + 1) * nq] * scale, -jnp.inf)) for i in range(bb)]

    s_p = [_dot_nt(q[:, hs[h]], k_ref[:, hs[h]]) * scale for h in range(MEM_HEADS)]

    for i in range(bb):
        os_ref[i] = _dot(p_s[i], vs_ref[i].reshape(rows, MEM_HD).astype(BF16))

    p_p = [softmax(s) for s in s_p]
    o = jnp.concatenate([_dot(p_p[h], v_ref[:, hs[h]]) for h in range(MEM_HEADS)], axis=-1).astype(BF16)
    o_ref[...] = x + _dot(o, wo_s[...])


def _attn(x, k, v, g, wq, wo, qs, ks, vs, *, tq):
    b, s, d = x.shape
    nb, nq, hd = qs.shape
    nj = s // tq
    assert nb % (b * nj) == 0
    bb = nb // (b * nj)
    assert bb * nq <= LANES
    tile = pl.BlockSpec((None, tq, d), lambda i, j: (i, j, 0))
    mem = pl.BlockSpec((None, N_MEM, d), lambda i, j: (i, 0, 0))
    s_q = pl.BlockSpec((bb, nq, hd), lambda i, j: (i * nj + j, 0, 0))
    s_mem = pl.BlockSpec((bb, N_MEM, MEM_HEADS, hd), lambda i, j: (i * nj + j, 0, 0, 0))
    return pl.pallas_call(
        functools.partial(_attn_kernel, bb=bb),
        grid=(b, nj),
        in_specs=[tile, _const_spec(g.shape), _const_spec(wq.shape), mem, mem, _const_spec(wo.shape),
                  s_q, s_mem, s_mem],
        out_specs=[tile, s_q, pl.BlockSpec(wo.shape, lambda i, j: (0, 0))],
        out_shape=[jax.ShapeDtypeStruct((b, s, d), F32), jax.ShapeDtypeStruct((nb, nq, hd), F32),
                   jax.ShapeDtypeStruct(wo.shape, BF16)],
        scratch_shapes=[pltpu.VMEM(wq.shape, BF16), pltpu.VMEM(wo.shape, BF16)],
        compiler_params=_cparams(("arbitrary", "arbitrary")),
        name="attn",
    )(x, g, wq, k, v, wo, qs, ks, vs)


def _mix_sample_kernel(x_ref, sp_ref, sc_ref, h0_ref, g_ref, win_ref, cw_ref, cb_ref, dtb_ref, alog_ref,
                       dsk_ref, wp_ref, ps_ref, ex_ref, sn_ref, wo_ref, gq_ref, wq_ref,
                       x2_ref, q_ref, npool_ref, nconv_ref, hout_ref,
                       pool_s, z_s, ydx_s, expa_s, xw_s, b_s, c_s, dec_s, y_s, *, nb, t, bb, start_pos):
    j = pl.program_id(0)
    gw = SSM_HPG * SSM_HEADDIM
    n_bc = SSM_GROUPS * SSM_STATE

    def rows(i):
        return slice(i * nb, (i + 1) * nb)

    @pl.when(j == 0)
    def _():
        x = x_ref[...]
        xn = _rms(x, g_ref[...]).astype(BF16)
        z_s[...] = _dot(xn, win_ref[:, D_POOL:D_POOL + D_SSM])

        u = _dot(xn, win_ref[:, 0:D_POOL])
        ext = [sp_ref[rows(i), :] for i in range(POOL_BUF)] + [u[rows(i)] for i in range(t)]
        ps = []
        for i in range(t):
            sums, cnts = [], []
            for w in POOL_WINDOWS:
                s = ext[POOL_BUF + i]
                for k in range(1, w):
                    s = s + ext[POOL_BUF + i - k]
                sums.append(s)
                cnts.append(jnp.full((nb, D_POOL), float(min(start_pos + i + 1, w)), F32))
            mean = _lane_group_select(sums, POOL_GW) / _lane_group_select(cnts, POOL_GW)
            ps.append(mean - ext[POOL_BUF + i])
        p = jnp.concatenate(ps, axis=0).astype(BF16)
        pool_s[...] = _dot(p, wp_ref[...]) * ps_ref[...]
        for i in range(POOL_BUF):
            npool_ref[rows(i), :] = ext[t + i]

        xr = _dot(xn, win_ref[:, D_POOL + D_SSM:D_PROJ])
        cext = [sc_ref[rows(i), :] for i in range(CONV_W - 1)] + [xr[rows(i)] for i in range(t)]
        xbc = []
        for i in range(t):
            acc = cext[i] * cw_ref[0:1, :]
            for k in range(1, CONV_W):
                acc = acc + cext[i + k] * cw_ref[k:k + 1, :]
            xbc.append(_silu(acc + cb_ref[...]))
        for i in range(CONV_W - 1):
            nconv_ref[rows(i), :] = cext[t + i]
        xs = [v[:, :D_SSM] for v in xbc]
        bm = [v[:, D_SSM:D_SSM + n_bc] for v in xbc]
        cm = [v[:, D_SSM + n_bc:] for v in xbc]
        for i in range(t):
            b_s[rows(i), :] = bm[i]
            c_s[rows(i), :] = cm[i]

        dt_all = jax.nn.softplus(_dot(xn, win_ref[:, D_PROJ:D_PROJ + LANES]) + dtb_ref[...])
        a_neg = -jnp.exp(alog_ref[...])
        dt = [dt_all[rows(i)] for i in range(t)]
        acs = []
        for i in range(t):
            da = dt[i] * a_neg
            acs.append(da if i == 0 else acs[-1] + da)
        dec = jnp.exp(acs[-1])
        for k in range(SSM_HEADS):
            dec_s[k] = _col_bcast(dec, k, LANES)

        def expand(v, exact):
            if exact:
                return jnp.dot(v, ex_ref[...], precision=lax.Precision.HIGHEST,
                               preferred_element_type=F32)
            return _dot(v.astype(BF16), ex_ref[...].astype(BF16))

        lane = lax.broadcasted_iota(jnp.int32, (nb, LANES), 1)
        for i in range(t):
            ydx = xs[i] * dsk_ref[...]
            for s in range(i + 1):
                sc = [jnp.sum(cm[i][:, g * SSM_STATE:(g + 1) * SSM_STATE]
                              * bm[s][:, g * SSM_STATE:(g + 1) * SSM_STATE], axis=-1, keepdims=True)
                      for g in range(SSM_GROUPS)]
                sc = jnp.where(lane < SSM_HPG, sc[0], sc[1])
                coef = sc * jnp.exp(acs[i] - acs[s]) * dt[s]
                ydx = ydx + expand(coef, False) * xs[s]
            ydx_s[rows(i), :] = ydx
            expa_s[rows(i), :] = expand(jnp.exp(acs[i]), True)
            xw_s[rows(i), :] = xs[i] * expand(jnp.exp(acs[-1] - acs[i]) * dt[i], True)

    base = pl.multiple_of(j * bb, bb)
    blk_rows = [pl.ds(i * nb + base, bb) for i in range(t)]

    def gather(ref):
        return jnp.concatenate([ref[r, :] for r in blk_rows], axis=0)

    def pad_rows(v):
        return jnp.concatenate([v, jnp.zeros((LANES - t * bb, v.shape[1]), v.dtype)], axis=0)

    c_blk = pad_rows(gather(c_s))
    b_blk = pad_rows(gather(b_s))
    xw_t = pad_rows(gather(xw_s)).T.astype(BF16)
    row_seq = lax.broadcasted_iota(jnp.int32, (LANES, SSM_STATE), 0) % bb
    col_seq = lax.broadcasted_iota(jnp.int32, (SSM_STATE, LANES), 1) % bb
    y_t = []
    for g in range(SSM_GROUPS):
        gs = slice(g * SSM_STATE, (g + 1) * SSM_STATE)
        c_t = c_blk[:, gs].T
        acc = jnp.zeros((gw, LANES), F32)
        for i in range(bb):
            h0 = h0_ref[i, g * SSM_HPG:(g + 1) * SSM_HPG].reshape(gw, SSM_STATE)
            acc = acc + _dot(h0.astype(BF16), jnp.where(col_seq == i, c_t, 0.0).astype(BF16))
            b_i = jnp.where(row_seq == i, b_blk[:, gs], 0.0).astype(BF16)
            st = _dot(xw_t[g * gw:(g + 1) * gw, :], b_i)
            for k in range(SSM_HPG):
                hd = g * SSM_HPG + k
                hout_ref[i, hd] = (h0_ref[i, hd] * dec_s[hd, pl.ds(base + i, 1), :]
                                   + st[k * SSM_HEADDIM:(k + 1) * SSM_HEADDIM])
        y_t.append(acc.T[:t * bb])
    y_blk = gather(ydx_s) + jnp.concatenate(y_t, axis=1) * gather(expa_s)
    for i in range(t):
        y_s[blk_rows[i], :] = y_blk[i * bb:(i + 1) * bb]

    @pl.when(j == pl.num_programs(0) - 1)
    def _():
        x2 = _gate_norm_out(y_s[...], z_s[...], pool_s[...], x_ref[...], sn_ref[...],
                            wo_ref[...].astype(BF16))
        _store_tm_rows(x2_ref, x2)
        q = _dot(_rms(x2, gq_ref[...]).astype(BF16), wq_ref[...].astype(BF16))
        for i in range(t):
            for h in range(MEM_HEADS):
                q_ref[:, i * MEM_HEADS + h, :] = q[rows(i), h * MEM_HD:(h + 1) * MEM_HD]


def _mix_sample(x, sp, sc, h0, w_in_bf, w, *, nb, t, bb, start_pos):
    n, d = x.shape
    assert n == nb * t and nb % bb == 0 and t * bb <= LANES
    consts = [w["g_mix"], w_in_bf, w["conv_w"], w["conv_b"], w["dt_bias"], w["a_log"], w["d_skip"],
              w["w_pool"], w["pool_scale"], w["expand"], w["ssm_norm"], w["w_out"], w["g_xq"], w["w_xq"]]
    hblk = pl.BlockSpec((bb, SSM_HEADS, SSM_HEADDIM, SSM_STATE), lambda i: (i, 0, 0, 0))
    whole = _const_spec
    outs = [
        jax.ShapeDtypeStruct((nb, t, d), F32),
        jax.ShapeDtypeStruct((nb, t * MEM_HEADS, MEM_HD), F32),
        jax.ShapeDtypeStruct((POOL_BUF * nb, D_POOL), F32),
        jax.ShapeDtypeStruct(((CONV_W - 1) * nb, CONV_DIM), F32),
        jax.ShapeDtypeStruct(h0.shape, F32),
    ]
    n_bc = SSM_GROUPS * SSM_STATE
    return pl.pallas_call(
        functools.partial(_mix_sample_kernel, nb=nb, t=t, bb=bb, start_pos=start_pos),
        grid=(nb // bb,),
        in_specs=[_const_spec(x.shape), _const_spec(sp.shape), _const_spec(sc.shape), hblk]
        + [_const_spec(a.shape) for a in consts],
        out_specs=[whole(o.shape) for o in outs[:4]] + [hblk],
        out_shape=outs,
        scratch_shapes=[
            pltpu.VMEM((n, D_POOL), F32),
            pltpu.VMEM((n, D_SSM), F32),
            pltpu.VMEM((n, D_SSM), F32),
            pltpu.VMEM((n, D_SSM), F32),
            pltpu.VMEM((n, D_SSM), F32),
            pltpu.VMEM((n, n_bc), F32),
            pltpu.VMEM((n, n_bc), F32),
            pltpu.VMEM((SSM_HEADS, nb, LANES), F32),
            pltpu.VMEM((n, D_SSM), F32),
        ],
        compiler_params=_cparams(("arbitrary",)),
        name="mix_sample",
    )(x, sp, sc, h0, *consts)


def _prep_weights(g_ffn1, w1_gate, w1_up, w1_down, g_mix, w_in, conv_w, conv_b, dt_bias, a_log,
                  d_skip, ssm_norm, w_pool, pool_scale, w_out, g_mem, w_mem_k, w_mem_v, g_xq,
                  w_xq, w_xo, g_ffn2, w2_gate, w2_up, w2_down, g_final):
    row = lambda v: v.reshape(1, -1).astype(F32)
    pad_heads = lambda v: jnp.pad(v.astype(F32), (0, LANES - SSM_HEADS)).reshape(1, LANES)
    w_pool_bd = jnp.zeros((D_POOL, D_POOL), F32)
    for g in range(len(POOL_WINDOWS)):
        w_pool_bd = w_pool_bd.at[g * POOL_GW:(g + 1) * POOL_GW, g * POOL_GW:(g + 1) * POOL_GW].set(w_pool[g])
    head_of_chan = jnp.arange(D_SSM) // SSM_HEADDIM
    expand = (jnp.arange(LANES)[:, None] == head_of_chan[None, :]).astype(F32)
    return dict(
        g_ffn1=row(g_ffn1), w1_gate=w1_gate, w1_up=w1_up, w1_down=w1_down,
        g_mix=row(g_mix), w_in=jnp.transpose(w_in),
        conv_w=conv_w.astype(F32), conv_b=row(conv_b), dt_bias=pad_heads(dt_bias),
        a_log=pad_heads(a_log), d_skip=row(jnp.repeat(d_skip, SSM_HEADDIM)), ssm_norm=row(ssm_norm),
        w_pool=w_pool_bd.astype(BF16), pool_scale=row(pool_scale), w_out=w_out,
        g_mem=row(g_mem), w_mem_k=w_mem_k, w_mem_v=w_mem_v,
        g_xq=row(g_xq), w_xq=w_xq, w_xo=w_xo,
        g_ffn2=row(g_ffn2), w2_gate=w2_gate, w2_up=w2_up, w2_down=w2_down,
        g_final=row(g_final), expand=expand,
    )


def _layer(x_prompt, x_sample, mem_prompt, mem_k, mem_v, state_pool, state_conv, state_ssm, start_pos, w):
    b, s, d = x_prompt.shape
    nb, t, _ = x_sample.shape
    tmaj = lambda a: jnp.swapaxes(a, 0, 1).reshape(-1, a.shape[-1])
    bmaj = lambda a, r: jnp.swapaxes(a.reshape(r, nb, -1), 0, 1)

    xp, xs, k, v, k_b, v_b = _ffn(x_prompt.reshape(b * s, d), x_sample, w["g_ffn1"], w["w1_gate"], w["w1_up"],
                                  w["w1_down"], w["g_final"], final=False, tm=FFN1_ROWS,
                                  memkv=(mem_prompt, w["g_mem"], w["w_mem_k"], w["w_mem_v"]))

    xp, pool_p, conv_p, ssm_p, w_in_bf = _mix_prompt(xp.reshape(b, s, d), w, tc=MIX_ROWS, nseq=MIX_SEQS)

    x2, q, pool_s, conv_s, ssm_s = _mix_sample(xs, tmaj(state_pool), tmaj(state_conv), state_ssm, w_in_bf, w,
                                               nb=nb, t=t, bb=STATE_BLOCK, start_pos=start_pos)

    xp, o, w_xo_bf = _attn(xp, k_b, v_b, w["g_xq"], w["w_xq"], w["w_xo"], q, mem_k, mem_v, tq=ATTN_ROWS)

    yp, ys = _ffn(xp.reshape(b * s, d), x2, w["g_ffn2"], w["w2_gate"], w["w2_up"], w["w2_down"],
                  w["g_final"], final=True, tm=FFN_ROWS, attn_out=(o, w_xo_bf))
    return (yp.reshape(b, s, d), ys, k, v, pool_p, conv_p, ssm_p,
            bmaj(pool_s, POOL_BUF), bmaj(conv_s, CONV_W - 1), ssm_s)


def kernel(x_prompt, x_sample, mem_prompt, cache_mem_k, cache_mem_v, state_pool, state_conv, state_ssm,
           g_ffn1, w1_gate, w1_up, w1_down, g_mix, w_in, conv_w, conv_b, dt_bias, a_log, d_skip,
           ssm_norm, w_pool, pool_scale, w_out, g_mem, w_mem_k, w_mem_v, g_xq, w_xq, w_xo,
           g_ffn2, w2_gate, w2_up, w2_down, g_final):
    assert g_ffn1.shape[0] == 1, "single-layer model"
    w = _prep_weights(g_ffn1[0], w1_gate[0], w1_up[0], w1_down[0], g_mix[0], w_in[0], conv_w[0],
                      conv_b[0], dt_bias[0], a_log[0], d_skip[0], ssm_norm[0], w_pool[0],
                      pool_scale[0], w_out[0], g_mem[0], w_mem_k[0], w_mem_v[0], g_xq[0], w_xq[0],
                      w_xo[0], g_ffn2[0], w2_gate[0], w2_up[0], w2_down[0], g_final)
    bp = x_prompt.shape[0]
    y_p, y_s, k_p, v_p, pool_p, conv_p, ssm_p, pool_s, conv_s, ssm_s = _layer(
        x_prompt, x_sample, mem_prompt, cache_mem_k[0], cache_mem_v[0], state_pool[0], state_conv[0],
        state_ssm[0], PAST_LEN, w)
    kv_shape = (1, bp, N_MEM, MEM_HEADS, MEM_HD)
    return (y_p, y_s, k_p.reshape(kv_shape), v_p.reshape(kv_shape), pool_p[None], conv_p[None],
            ssm_p[None], pool_s[None], conv_s[None], ssm_s[None])
```

```python
import functools

import jax
import jax.numpy as jnp
from jax import lax
from jax.experimental import pallas as pl
from jax.experimental.pallas import tpu as pltpu

F32 = jnp.float32
BF16 = jnp.bfloat16

D_POOL = 256
POOL_WINDOWS = (2, 4, 8, 16)
POOL_GW = 64
POOL_BUF = 15
D_SSM = 768
SSM_HEADDIM = 64
SSM_HEADS = 12
SSM_GROUPS = 2
SSM_HPG = 6
SSM_STATE = 128
CONV_W = 4
CONV_DIM = 1280
D_PROJ = D_POOL + D_SSM + CONV_DIM
N_MEM = 256
MEM_HEADS = 4
MEM_HD = 256
PAST_LEN = 16384
EPS = 1e-6
LANES = 128
CHUNK = 128
POOL_HIST = 32
CONV_HIST = 8
SCAN_PAD = CHUNK // 2

VMEM_LIMIT = 56 * 1024 * 1024
VMEM_LIMIT_FFN = 60 * 1024 * 1024
FFN_ROWS = 1024
FFN_W_STEPS = 16
MIX_ROWS = 256
MIX_SEQS = 2
ATTN_ROWS = 512
STATE_BLOCK = 8


def _cparams(sem, vmem_limit=VMEM_LIMIT):
    return pltpu.CompilerParams(dimension_semantics=sem, vmem_limit_bytes=vmem_limit)


def _const_spec(shape):
    nd = len(shape)
    return pl.BlockSpec(shape, lambda *_: (0,) * nd, pipeline_mode=pl.Buffered(1))


def _tm_rows(ref):
    return jnp.concatenate([ref[:, i, :] for i in range(ref.shape[1])], axis=0)


def _store_tm_rows(ref, rows):
    nb = ref.shape[0]
    for i in range(ref.shape[1]):
        ref[:, i, :] = rows[i * nb:(i + 1) * nb]


def _rms(x, g):
    ms = jnp.mean(x * x, axis=-1, keepdims=True)
    return x * lax.rsqrt(ms + EPS) * g


def _silu(x):
    return x * jax.nn.sigmoid(x)


def _dot(a, b):
    return jnp.dot(a, b, preferred_element_type=F32)


def _dot_nt(a, b):
    return lax.dot_general(a, b, (((1,), (1,)), ((), ())), preferred_element_type=F32)


def _dot_tn(a, b):
    return lax.dot_general(a, b, (((0,), (0,)), ((), ())), preferred_element_type=F32)


def _ffn_kernel(xp_ref, xs_ref, g_ref, wg_ref, wu_ref, wd_ref, gf_ref, *rest, final, n_p, attn_out):
    if attn_out:
        ao_ref, wxo_ref, op_ref, os_ref, wg_s, wu_s, wd_s = rest
    else:
        op_ref, os_ref, wg_s, wu_s, wd_s = rest
    i = pl.program_id(0)
    rows_gu = wg_ref.shape[0]
    rows_d = wd_ref.shape[0]

    @pl.when(i < FFN_W_STEPS)
    def _():
        r = pl.multiple_of(i * rows_gu, rows_gu)
        wg_s[pl.ds(r, rows_gu), :] = wg_ref[...].astype(BF16)
        wu_s[pl.ds(r, rows_gu), :] = wu_ref[...].astype(BF16)
        r = pl.multiple_of(i * rows_d, rows_d)
        wd_s[pl.ds(r, rows_d), :] = wd_ref[...].astype(BF16)

    def tile(x):
        xn = _rms(x, g_ref[...]).astype(BF16)
        gate = _dot(xn, wg_s[...])
        up = _dot(xn, wu_s[...])
        h = (_silu(gate) * up).astype(BF16)
        out = x + 0.5 * _dot(h, wd_s[...])
        return _rms(out, gf_ref[...]) if final else out

    @pl.when(jnp.logical_and(i >= FFN_W_STEPS, i < FFN_W_STEPS + n_p))
    def _():
        op_ref[...] = tile(xp_ref[...])

    @pl.when(i == FFN_W_STEPS + n_p)
    def _():
        x = _tm_rows(xs_ref)
        if attn_out:
            nb, nq, hd = ao_ref.shape
            heads = wxo_ref.shape[0] // hd
            o = jnp.concatenate(
                [jnp.concatenate([ao_ref[:, t * heads + h, :] for h in range(heads)], axis=1)
                 for t in range(nq // heads)], axis=0)
            x = x + _dot(o.astype(BF16), wxo_ref[...])
        out = tile(x)
        if os_ref.ndim == 3:
            _store_tm_rows(os_ref, out)
        else:
            os_ref[...] = out


def _ffn(xp, xs, g, wg, wu, wd, gf, *, final, tm, attn_out=None):
    tp, d = xp.shape
    nb, t, _ = xs.shape
    os_shape = xs.shape if final else (nb * t, d)
    dff = wg.shape[1]
    assert tp % tm == 0 and d % FFN_W_STEPS == 0 and dff % FFN_W_STEPS == 0
    n_p = tp // tm
    p_tile = pl.BlockSpec((tm, d), lambda i: (jnp.clip(i - FFN_W_STEPS, 0, n_p - 1), 0))
    s_tile = pl.BlockSpec(os_shape, lambda i: (0,) * len(os_shape))
    w_chunk = lambda rows, cols: pl.BlockSpec((rows, cols), lambda i: (jnp.minimum(i, FFN_W_STEPS - 1), 0))
    extra = list(attn_out) if attn_out else []
    return pl.pallas_call(
        functools.partial(_ffn_kernel, final=final, n_p=n_p, attn_out=bool(attn_out)),
        grid=(FFN_W_STEPS + n_p + 1,),
        in_specs=[
            p_tile, _const_spec(xs.shape), _const_spec(g.shape),
            w_chunk(d // FFN_W_STEPS, dff), w_chunk(d // FFN_W_STEPS, dff), w_chunk(dff // FFN_W_STEPS, d),
            _const_spec(gf.shape),
        ] + [_const_spec(a.shape) for a in extra],
        out_specs=[p_tile, s_tile],
        out_shape=[jax.ShapeDtypeStruct(xp.shape, F32), jax.ShapeDtypeStruct(os_shape, F32)],
        scratch_shapes=[pltpu.VMEM((d, dff), BF16), pltpu.VMEM((d, dff), BF16), pltpu.VMEM((dff, d), BF16)],
        compiler_params=_cparams(("arbitrary",), VMEM_LIMIT_FFN),
        name="ffn_final" if final else "ffn",
    )(xp, xs, g, wg, wu, wd, gf, *extra)


def _memkv_kernel(m_ref, g_ref, wk_ref, wv_ref, k_ref, v_ref, kb_ref, vb_ref, wk_s, wv_s):
    @pl.when(pl.program_id(0) == 0)
    def _():
        wk_s[...] = wk_ref[...].astype(BF16)
        wv_s[...] = wv_ref[...].astype(BF16)

    mn = _rms(m_ref[...], g_ref[...]).astype(BF16)
    k = _dot(mn, wk_s[...])
    v = _dot(mn, wv_s[...])
    for h in range(MEM_HEADS):
        k_ref[:, h, :] = k[:, h * MEM_HD:(h + 1) * MEM_HD]
        v_ref[:, h, :] = v[:, h * MEM_HD:(h + 1) * MEM_HD]
    kb_ref[...] = k.astype(BF16)
    vb_ref[...] = v.astype(BF16)


def _memkv(mem, g, wk, wv):
    b, n_mem, d = mem.shape
    row_blk = pl.BlockSpec((None, n_mem, d), lambda i: (i, 0, 0))
    head_blk = pl.BlockSpec((None, n_mem, MEM_HEADS, MEM_HD), lambda i: (i, 0, 0, 0))
    return pl.pallas_call(
        _memkv_kernel,
        grid=(b,),
        in_specs=[row_blk, _const_spec(g.shape), _const_spec(wk.shape), _const_spec(wv.shape)],
        out_specs=[head_blk, head_blk, row_blk, row_blk],
        out_shape=[jax.ShapeDtypeStruct((b, n_mem, MEM_HEADS, MEM_HD), F32)] * 2
        + [jax.ShapeDtypeStruct((b, n_mem, d), BF16)] * 2,
        scratch_shapes=[pltpu.VMEM(wk.shape, BF16), pltpu.VMEM(wv.shape, BF16)],
        compiler_params=_cparams(("arbitrary",)),
        name="memkv",
    )(mem, g, wk, wv)


def _lane_group_select(vals, width):
    lane = lax.broadcasted_iota(jnp.int32, vals[0].shape, 1)
    out = vals[-1]
    for g in range(len(vals) - 2, -1, -1):
        out = jnp.where(lane < (g + 1) * width, vals[g], out)
    return out


def _col_bcast(m, k, width):
    return jnp.broadcast_to(m[:, k:k + 1], (m.shape[0], width))


def _gate_norm_out(y, z, pool_out, x_res, ssm_norm, w_out):
    yn = _rms(y * _silu(z), ssm_norm)
    cat = jnp.concatenate([pool_out, yn], axis=-1).astype(BF16)
    return x_res + _dot(cat, w_out)


def _mix_prompt_kernel(x_ref, g_ref, win_ref, cw_ref, cb_ref,
                       dtb_ref, alog_ref, dsk_ref, sn_ref, wp_ref, ps_ref, wout_ref,
                       o_ref, opool_ref, oconv_ref, ossm_ref, owin_ref,
                       wu_s, wz_s, wx_s, wdt_s, wo_s,
                       ubuf, s2buf, s4buf, s8buf, cbuf, abuf, ht_ref, ybuf, *, tc, nseq):
    c = pl.program_id(1)
    nc = pl.num_programs(1)
    n_pool_slabs = D_POOL // LANES
    n_conv_slabs = CONV_DIM // LANES
    n_chunks = tc // CHUNK
    rp = POOL_HIST + tc
    seqs = range(nseq)

    def slab(s):
        return slice(s * LANES, (s + 1) * LANES)

    @pl.when(jnp.logical_and(pl.program_id(0) == 0, c == 0))
    def _():
        wu_s[...] = win_ref[0:D_POOL, :].T.astype(BF16)
        wz_s[...] = win_ref[D_POOL:D_POOL + D_SSM, :].T.astype(BF16)
        wx_s[...] = win_ref[D_POOL + D_SSM:D_PROJ, :].T.astype(BF16)
        lead = 16 - SSM_HEADS
        place = (lax.broadcasted_iota(jnp.int32, (16, LANES), 0) - lead
                 == lax.broadcasted_iota(jnp.int32, (16, LANES), 1))
        wdt_s[...] = _dot_tn(win_ref[D_PROJ - lead:D_PROJ + SSM_HEADS, :].astype(BF16),
                             jnp.where(place, 1.0, 0.0).astype(BF16)).astype(BF16)
        wo_s[...] = wout_ref[...].astype(BF16)
        owin_ref[:, 0:D_POOL] = wu_s[...]
        owin_ref[:, D_POOL:D_POOL + D_SSM] = wz_s[...]
        owin_ref[:, D_POOL + D_SSM:D_PROJ] = wx_s[...]
        owin_ref[:, D_PROJ:D_PROJ + LANES] = wdt_s[...]

    @pl.when(c == 0)
    def _():
        ubuf[:, :, 0:POOL_HIST, :] = jnp.zeros((nseq, n_pool_slabs, POOL_HIST, LANES), F32)
        cbuf[:, :, 0:CONV_HIST, :] = jnp.zeros((nseq, n_conv_slabs, CONV_HIST, LANES), F32)
        abuf[:, :, 0:SCAN_PAD, :] = jnp.zeros((nseq, n_chunks, SCAN_PAD, LANES), F32)
        ht_ref[...] = jnp.zeros_like(ht_ref)

    lo_half = lax.broadcasted_iota(jnp.int32, (tc, LANES), 1) < POOL_GW
    pos1 = c * tc + lax.broadcasted_iota(jnp.int32, (tc, LANES), 0) + 1
    a_neg = -jnp.exp(alog_ref[...])
    row = lax.broadcasted_iota(jnp.int32, (CHUNK, CHUNK), 0)
    col = lax.broadcasted_iota(jnp.int32, (CHUNK, CHUNK), 1)
    causal = row >= col
    lo = lax.broadcasted_iota(jnp.int32, (CHUNK, LANES), 1) < SSM_HEADDIM
    mask_lo = jnp.where(lo, 1.0, 0.0).astype(BF16)
    mask_hi = jnp.where(lo, 0.0, 1.0).astype(BF16)
    gw = SSM_HPG * SSM_HEADDIM
    n_x = D_SSM // LANES
    st8 = {}

    def head(i):
        x = x_ref[i]
        xn = _rms(x, g_ref[...]).astype(BF16)
        st8[i] = dict(x=x, xn=xn, xbc=[None] * n_conv_slabs)

    def conv_block(i, s0, ns):
        xn = st8[i]["xn"]
        raw = _dot(xn, wx_s[:, s0 * LANES:(s0 + ns) * LANES])
        for t in range(ns):
            s = s0 + t
            cbuf[i, s, CONV_HIST:CONV_HIST + tc, :] = raw[:, slab(t)]
            acc = cbuf[i, s, CONV_HIST:CONV_HIST + tc, :] * cw_ref[CONV_W - 1:CONV_W, slab(s)]
            for j in range(1, CONV_W):
                acc = acc + (cbuf[i, s, CONV_HIST - j:CONV_HIST - j + tc, :]
                             * cw_ref[CONV_W - 1 - j:CONV_W - j, slab(s)])
            st8[i]["xbc"][s] = _silu(acc + cb_ref[:, slab(s)])

    def dt_proj(i):
        st8[i]["dt"] = jax.nn.softplus(_dot(st8[i]["xn"], wdt_s[...]) + dtb_ref[...])

    def pool(i):
        u = _dot(st8[i]["xn"], wu_s[...])
        for s in range(n_pool_slabs):
            ubuf[i, s, POOL_HIST:rp, :] = u[:, slab(s)]
            s2buf[i, s, 8:rp, :] = ubuf[i, s, 8:rp, :] + ubuf[i, s, 7:rp - 1, :]
        s4buf[i, 16:rp, :] = s2buf[i, 1, 16:rp, :] + s2buf[i, 1, 14:rp - 2, :]
        s8buf[i, 24:rp, :] = s4buf[i, 24:rp, :] + s4buf[i, 20:rp - 4, :]
        win_sums = [
            jnp.where(lo_half, s2buf[i, 0, POOL_HIST:rp, :],
                      s2buf[i, 0, POOL_HIST:rp, :] + s2buf[i, 0, POOL_HIST - 2:rp - 2, :]),
            jnp.where(lo_half, s8buf[i, POOL_HIST:rp, :],
                      s8buf[i, POOL_HIST:rp, :] + s8buf[i, POOL_HIST - 8:rp - 8, :]),
        ]
        ps = []
        for s in range(n_pool_slabs):
            win = jnp.where(lo_half, POOL_WINDOWS[2 * s], POOL_WINDOWS[2 * s + 1])
            cnt = jnp.minimum(pos1, win).astype(F32)
            ps.append(win_sums[s] / cnt - u[:, slab(s)])
        st8[i]["pool_out"] = _dot(jnp.concatenate(ps, axis=1).astype(BF16), wp_ref[...]) * ps_ref[...]

    def ssd_chunk(i, j):
        xbc = st8[i]["xbc"]
        sl = slice(j * CHUNK, (j + 1) * CHUNK)
        dt_c = st8[i]["dt"][sl]
        acs = dt_c * a_neg
        sh = 1
        while sh < CHUNK:
            abuf[i, j, SCAN_PAD:SCAN_PAD + CHUNK, :] = acs
            acs = acs + abuf[i, j, SCAN_PAD - sh:SCAN_PAD - sh + CHUNK, :]
            sh *= 2
        acs_last = acs[CHUNK - 1:CHUNK, :]
        fdec = jnp.exp(acs_last - acs) * dt_c
        src_t = (acs - jnp.log(dt_c)).T
        ht = ht_ref[i]
        ht_b = ht.astype(BF16)
        cdec, st = [], []
        for g in range(SSM_GROUPS):
            c_g = xbc[n_x + SSM_GROUPS + g][sl].astype(BF16)
            b_g = xbc[n_x + g][sl].astype(BF16)
            scores = _dot_nt(c_g, b_g)
            y_off = _dot(c_g, ht_b[:, g * gw:(g + 1) * gw])
            xw = []
            for q in range(SSM_HPG // 2):
                k0 = g * SSM_HPG + 2 * q
                pair = k0 // 2
                a_cols = [_col_bcast(acs, k, CHUNK) for k in (k0, k0 + 1)]
                lhs = [(scores * jnp.exp(jnp.where(causal, a_cols[h] - src_t[k0 + h:k0 + h + 1, :], -jnp.inf))
                        ).astype(BF16) for h in range(2)]
                x_pair = xbc[pair][sl]
                x_b = x_pair.astype(BF16)
                rhs = jnp.concatenate([x_b * mask_lo, x_b * mask_hi], axis=0)
                e_pair = jnp.exp(jnp.where(lo, a_cols[0], a_cols[1]))
                ybuf[i, sl, slab(pair)] = (_dot(jnp.concatenate(lhs, axis=1), rhs)
                                           + y_off[:, slab(q)] * e_pair + x_pair * dsk_ref[:, slab(pair)])
                f_pair = jnp.where(lo, _col_bcast(fdec, k0, LANES), _col_bcast(fdec, k0 + 1, LANES))
                xw.append((x_pair * f_pair).astype(BF16))
                cdec.append(e_pair[CHUNK - 1:CHUNK, :])
            st.append(_dot_tn(b_g, jnp.concatenate(xw, axis=1)))
        ht_ref[i] = ht * jnp.concatenate(cdec, axis=1) + jnp.concatenate(st, axis=1)

    def z_proj(i):
        st8[i]["zs"] = _silu(_dot(st8[i]["xn"], wz_s[...]))

    def tail(i):
        d = st8[i]
        yn = _rms(ybuf[i] * d["zs"], sn_ref[...])
        cat = jnp.concatenate([d["pool_out"], yn], axis=-1).astype(BF16)
        o_ref[i] = d["x"] + _dot(cat, wo_s[...])

    def mid(i):
        conv_block(i, n_x, 2 * SSM_GROUPS)
        dt_proj(i)
        for s0 in range(0, n_x, 2):
            conv_block(i, s0, 2)

    for i in seqs:
        head(i)
    for i in seqs:
        mid(i)
    for i in seqs:
        pool(i)
    for j in range(n_chunks):
        for i in seqs:
            ssd_chunk(i, j)
        if j == 0:
            for i in seqs:
                z_proj(i)
    for i in seqs:
        tail(i)

    ubuf[:, :, 0:POOL_HIST, :] = ubuf[:, :, tc:tc + POOL_HIST, :]
    cbuf[:, :, 0:CONV_HIST, :] = cbuf[:, :, tc:tc + CONV_HIST, :]

    @pl.when(c == nc - 1)
    def _():
        for i in seqs:
            for s in range(n_pool_slabs):
                opool_ref[i, :, slab(s)] = ubuf[i, s, POOL_HIST - POOL_BUF:POOL_HIST, :]
            for s in range(n_conv_slabs):
                oconv_ref[i, :, slab(s)] = cbuf[i, s, CONV_HIST - (CONV_W - 1):CONV_HIST, :]
            ossm_ref[i] = ht_ref[i].T.reshape(SSM_HEADS, SSM_HEADDIM, SSM_STATE)


def _mix_prompt(x, w, *, tc, nseq):
    b, s, d = x.shape
    assert s % tc == 0 and tc % CHUNK == 0 and b % nseq == 0
    consts = [w["g_mix"], w["w_in"], w["conv_w"], w["conv_b"],
              w["dt_bias"], w["a_log"], w["d_skip"], w["ssm_norm"], w["w_pool"], w["pool_scale"],
              w["w_out"]]
    tile = pl.BlockSpec((nseq, tc, d), lambda i, j: (i, j, 0))
    return pl.pallas_call(
        functools.partial(_mix_prompt_kernel, tc=tc, nseq=nseq),
        grid=(b // nseq, s // tc),
        in_specs=[tile] + [_const_spec(a.shape) for a in consts],
        out_specs=[
            tile,
            pl.BlockSpec((nseq, POOL_BUF, D_POOL), lambda i, j: (i, 0, 0)),
            pl.BlockSpec((nseq, CONV_W - 1, CONV_DIM), lambda i, j: (i, 0, 0)),
            pl.BlockSpec((nseq, SSM_HEADS, SSM_HEADDIM, SSM_STATE), lambda i, j: (i, 0, 0, 0)),
            pl.BlockSpec((d, D_PROJ + LANES), lambda i, j: (0, 0)),
        ],
        out_shape=[
            jax.ShapeDtypeStruct((b, s, d), F32),
            jax.ShapeDtypeStruct((b, POOL_BUF, D_POOL), F32),
            jax.ShapeDtypeStruct((b, CONV_W - 1, CONV_DIM), F32),
            jax.ShapeDtypeStruct((b, SSM_HEADS, SSM_HEADDIM, SSM_STATE), F32),
            jax.ShapeDtypeStruct((d, D_PROJ + LANES), BF16),
        ],
        scratch_shapes=[
            pltpu.VMEM((d, D_POOL), BF16), pltpu.VMEM((d, D_SSM), BF16),
            pltpu.VMEM((d, CONV_DIM), BF16), pltpu.VMEM((d, LANES), BF16), pltpu.VMEM((d, d), BF16),
            pltpu.VMEM((nseq, D_POOL // LANES, POOL_HIST + tc, LANES), F32),
            pltpu.VMEM((nseq, D_POOL // LANES, POOL_HIST + tc, LANES), F32),
            pltpu.VMEM((nseq, POOL_HIST + tc, LANES), F32),
            pltpu.VMEM((nseq, POOL_HIST + tc, LANES), F32),
            pltpu.VMEM((nseq, CONV_DIM // LANES, CONV_HIST + tc, LANES), F32),
            pltpu.VMEM((nseq, tc // CHUNK, SCAN_PAD + CHUNK, LANES), F32),
            pltpu.VMEM((nseq, SSM_STATE, D_SSM), F32),
            pltpu.VMEM((nseq, tc, D_SSM), F32),
        ],
        compiler_params=_cparams(("arbitrary", "arbitrary")),
        name="mix_prompt",
    )(x, *consts)


def _attn_kernel(x_ref, g_ref, wq_ref, k_ref, v_ref, wo_ref, qs_ref, ks_ref, vs_ref, fg_ref, fu_ref, fd_ref,
                 o_ref, os_ref, owo_ref, ofg_ref, ofu_ref, ofd_ref, wq_s, wo_s, *, bb):
    @pl.when(jnp.logical_and(pl.program_id(0) == 0, pl.program_id(1) == 0))
    def _():
        wq_s[...] = wq_ref[...].astype(BF16)
        wo_s[...] = wo_ref[...].astype(BF16)
        owo_ref[...] = wo_s[...]

    ofg_ref[...] = fg_ref[...].astype(BF16)
    ofu_ref[...] = fu_ref[...].astype(BF16)
    ofd_ref[...] = fd_ref[...].astype(BF16)

    hs = [slice(h * MEM_HD, (h + 1) * MEM_HD) for h in range(MEM_HEADS)]
    scale = MEM_HD ** -0.5
    nq = qs_ref.shape[1]
    rows = N_MEM * MEM_HEADS

    def softmax(s):
        e = jnp.exp(s - jnp.max(s, axis=-1, keepdims=True))
        return (e / jnp.sum(e, axis=-1, keepdims=True)).astype(BF16)

    q_head = lax.broadcasted_iota(jnp.int32, (nq, rows), 0) % MEM_HEADS
    kv_head = lax.broadcasted_iota(jnp.int32, (nq, rows), 1) % MEM_HEADS
    same_head = q_head == kv_head

    q_all = qs_ref[...].reshape(bb * nq, MEM_HD)
    q_all = jnp.concatenate([q_all, jnp.zeros((LANES - bb * nq, MEM_HD), F32)], axis=0).astype(BF16)
    s_s = [_dot_nt(ks_ref[i].reshape(rows, MEM_HD).astype(BF16), q_all) for i in range(bb)]

    x = x_ref[...]
    q = _dot(_rms(x, g_ref[...]).astype(BF16), wq_s[...]).astype(BF16)

    p_s = [softmax(jnp.where(same_head, s_s[i].T[i * nq:(i + 1) * nq] * scale, -jnp.inf)) for i in range(bb)]

    s_p = [_dot_nt(q[:, hs[h]], k_ref[:, hs[h]]) * scale for h in range(MEM_HEADS)]

    for i in range(bb):
        os_ref[i] = _dot(p_s[i], vs_ref[i].reshape(rows, MEM_HD).astype(BF16))

    p_p = [softmax(s) for s in s_p]
    o = jnp.concatenate([_dot(p_p[h], v_ref[:, hs[h]]) for h in range(MEM_HEADS)], axis=-1).astype(BF16)
    o_ref[...] = x + _dot(o, wo_s[...])


def _attn(x, k, v, g, wq, wo, qs, ks, vs, ffn_w, *, tq):
    b, s, d = x.shape
    nb, nq, hd = qs.shape
    nj = s // tq
    assert nb % (b * nj) == 0
    bb = nb // (b * nj)
    assert bb * nq <= LANES
    tile = pl.BlockSpec((None, tq, d), lambda i, j: (i, j, 0))
    mem = pl.BlockSpec((None, N_MEM, d), lambda i, j: (i, 0, 0))
    s_q = pl.BlockSpec((bb, nq, hd), lambda i, j: (i * nj + j, 0, 0))
    s_mem = pl.BlockSpec((bb, N_MEM, MEM_HEADS, hd), lambda i, j: (i * nj + j, 0, 0, 0))
    fg, fu, fd = ffn_w
    steps = b * nj
    assert fg.shape[0] % (16 * steps) == 0 and fd.shape[0] % (16 * steps // 2) == 0
    gu_chunk = pl.BlockSpec((fg.shape[0] // steps, fg.shape[1]), lambda i, j: (i * nj + j, 0))
    d_chunk = pl.BlockSpec((fd.shape[0] // (steps // 2), fd.shape[1]), lambda i, j: ((i * nj + j) // 2, 0))
    return pl.pallas_call(
        functools.partial(_attn_kernel, bb=bb),
        grid=(b, nj),
        in_specs=[tile, _const_spec(g.shape), _const_spec(wq.shape), mem, mem, _const_spec(wo.shape),
                  s_q, s_mem, s_mem, gu_chunk, gu_chunk, d_chunk],
        out_specs=[tile, s_q, pl.BlockSpec(wo.shape, lambda i, j: (0, 0)), gu_chunk, gu_chunk, d_chunk],
        out_shape=[jax.ShapeDtypeStruct((b, s, d), F32), jax.ShapeDtypeStruct((nb, nq, hd), F32),
                   jax.ShapeDtypeStruct(wo.shape, BF16), jax.ShapeDtypeStruct(fg.shape, BF16),
                   jax.ShapeDtypeStruct(fu.shape, BF16), jax.ShapeDtypeStruct(fd.shape, BF16)],
        scratch_shapes=[pltpu.VMEM(wq.shape, BF16), pltpu.VMEM(wo.shape, BF16)],
        compiler_params=_cparams(("arbitrary", "arbitrary")),
        name="attn",
    )(x, g, wq, k, v, wo, qs, ks, vs, fg, fu, fd)


def _mix_sample_kernel(x_ref, sp_ref, sc_ref, h0_ref, g_ref, win_ref, cw_ref, cb_ref, dtb_ref, alog_ref,
                       dsk_ref, wp_ref, ps_ref, ex_ref, sn_ref, wo_ref, gq_ref, wq_ref,
                       x2_ref, q_ref, npool_ref, nconv_ref, hout_ref,
                       pool_s, z_s, ydx_s, expa_s, xw_s, b_s, c_s, dec_s, y_s, *, nb, t, bb, start_pos):
    j = pl.program_id(0)
    gw = SSM_HPG * SSM_HEADDIM
    n_bc = SSM_GROUPS * SSM_STATE

    def rows(i):
        return slice(i * nb, (i + 1) * nb)

    @pl.when(j == 0)
    def _():
        x = x_ref[...]
        xn = _rms(x, g_ref[...]).astype(BF16)
        z_s[...] = _dot(xn, win_ref[:, D_POOL:D_POOL + D_SSM])

        u = _dot(xn, win_ref[:, 0:D_POOL])
        ext = [sp_ref[rows(i), :] for i in range(POOL_BUF)] + [u[rows(i)] for i in range(t)]
        ps = []
        for i in range(t):
            sums, cnts = [], []
            for w in POOL_WINDOWS:
                s = ext[POOL_BUF + i]
                for k in range(1, w):
                    s = s + ext[POOL_BUF + i - k]
                sums.append(s)
                cnts.append(jnp.full((nb, D_POOL), float(min(start_pos + i + 1, w)), F32))
            mean = _lane_group_select(sums, POOL_GW) / _lane_group_select(cnts, POOL_GW)
            ps.append(mean - ext[POOL_BUF + i])
        p = jnp.concatenate(ps, axis=0).astype(BF16)
        pool_s[...] = _dot(p, wp_ref[...]) * ps_ref[...]
        for i in range(POOL_BUF):
            npool_ref[rows(i), :] = ext[t + i]

        xr = _dot(xn, win_ref[:, D_POOL + D_SSM:D_PROJ])
        cext = [sc_ref[rows(i), :] for i in range(CONV_W - 1)] + [xr[rows(i)] for i in range(t)]
        xbc = []
        for i in range(t):
            acc = cext[i] * cw_ref[0:1, :]
            for k in range(1, CONV_W):
                acc = acc + cext[i + k] * cw_ref[k:k + 1, :]
            xbc.append(_silu(acc + cb_ref[...]))
        for i in range(CONV_W - 1):
            nconv_ref[rows(i), :] = cext[t + i]
        xs = [v[:, :D_SSM] for v in xbc]
        bm = [v[:, D_SSM:D_SSM + n_bc] for v in xbc]
        cm = [v[:, D_SSM + n_bc:] for v in xbc]
        for i in range(t):
            b_s[rows(i), :] = bm[i]
            c_s[rows(i), :] = cm[i]

        dt_all = jax.nn.softplus(_dot(xn, win_ref[:, D_PROJ:D_PROJ + LANES]) + dtb_ref[...])
        a_neg = -jnp.exp(alog_ref[...])
        dt = [dt_all[rows(i)] for i in range(t)]
        acs = []
        for i in range(t):
            da = dt[i] * a_neg
            acs.append(da if i == 0 else acs[-1] + da)
        dec = jnp.exp(acs[-1])
        for k in range(SSM_HEADS):
            dec_s[k] = _col_bcast(dec, k, LANES)

        def expand(v, exact):
            if exact:
                return jnp.dot(v, ex_ref[...], precision=lax.Precision.HIGHEST,
                               preferred_element_type=F32)
            return _dot(v.astype(BF16), ex_ref[...].astype(BF16))

        lane = lax.broadcasted_iota(jnp.int32, (nb, LANES), 1)
        for i in range(t):
            ydx = xs[i] * dsk_ref[...]
            for s in range(i + 1):
                sc = [jnp.sum(cm[i][:, g * SSM_STATE:(g + 1) * SSM_STATE]
                              * bm[s][:, g * SSM_STATE:(g + 1) * SSM_STATE], axis=-1, keepdims=True)
                      for g in range(SSM_GROUPS)]
                sc = jnp.where(lane < SSM_HPG, sc[0], sc[1])
                coef = sc * jnp.exp(acs[i] - acs[s]) * dt[s]
                ydx = ydx + expand(coef, False) * xs[s]
            ydx_s[rows(i), :] = ydx
            expa_s[rows(i), :] = expand(jnp.exp(acs[i]), True)
            xw_s[rows(i), :] = xs[i] * expand(jnp.exp(acs[-1] - acs[i]) * dt[i], True)

    base = pl.multiple_of(j * bb, bb)
    blk_rows = [pl.ds(i * nb + base, bb) for i in range(t)]

    def gather(ref):
        return jnp.concatenate([ref[r, :] for r in blk_rows], axis=0)

    def pad_rows(v):
        return jnp.concatenate([v, jnp.zeros((LANES - t * bb, v.shape[1]), v.dtype)], axis=0)

    c_blk = pad_rows(gather(c_s))
    b_blk = pad_rows(gather(b_s))
    xw_t = pad_rows(gather(xw_s)).T.astype(BF16)
    row_seq = lax.broadcasted_iota(jnp.int32, (LANES, SSM_STATE), 0) % bb
    col_seq = lax.broadcasted_iota(jnp.int32, (SSM_STATE, LANES), 1) % bb
    y_t = []
    for g in range(SSM_GROUPS):
        gs = slice(g * SSM_STATE, (g + 1) * SSM_STATE)
        c_t = c_blk[:, gs].T
        acc = jnp.zeros((gw, LANES), F32)
        for i in range(bb):
            h0 = h0_ref[i, g * SSM_HPG:(g + 1) * SSM_HPG].reshape(gw, SSM_STATE)
            acc = acc + _dot(h0.astype(BF16), jnp.where(col_seq == i, c_t, 0.0).astype(BF16))
            b_i = jnp.where(row_seq == i, b_blk[:, gs], 0.0).astype(BF16)
            st = _dot(xw_t[g * gw:(g + 1) * gw, :], b_i)
            for k in range(SSM_HPG):
                hd = g * SSM_HPG + k
                hout_ref[i, hd] = (h0_ref[i, hd] * dec_s[hd, pl.ds(base + i, 1), :]
                                   + st[k * SSM_HEADDIM:(k + 1) * SSM_HEADDIM])
        y_t.append(acc.T[:t * bb])
    y_blk = gather(ydx_s) + jnp.concatenate(y_t, axis=1) * gather(expa_s)
    for i in range(t):
        y_s[blk_rows[i], :] = y_blk[i * bb:(i + 1) * bb]

    @pl.when(j == pl.num_programs(0) - 1)
    def _():
        x2 = _gate_norm_out(y_s[...], z_s[...], pool_s[...], x_ref[...], sn_ref[...],
                            wo_ref[...].astype(BF16))
        _store_tm_rows(x2_ref, x2)
        q = _dot(_rms(x2, gq_ref[...]).astype(BF16), wq_ref[...].astype(BF16))
        for i in range(t):
            for h in range(MEM_HEADS):
                q_ref[:, i * MEM_HEADS + h, :] = q[rows(i), h * MEM_HD:(h + 1) * MEM_HD]


def _mix_sample(x, sp, sc, h0, w_in_bf, w, *, nb, t, bb, start_pos):
    n, d = x.shape
    assert n == nb * t and nb % bb == 0 and t * bb <= LANES
    consts = [w["g_mix"], w_in_bf, w["conv_w"], w["conv_b"], w["dt_bias"], w["a_log"], w["d_skip"],
              w["w_pool"], w["pool_scale"], w["expand"], w["ssm_norm"], w["w_out"], w["g_xq"], w["w_xq"]]
    hblk = pl.BlockSpec((bb, SSM_HEADS, SSM_HEADDIM, SSM_STATE), lambda i: (i, 0, 0, 0))
    whole = _const_spec
    outs = [
        jax.ShapeDtypeStruct((nb, t, d), F32),
        jax.ShapeDtypeStruct((nb, t * MEM_HEADS, MEM_HD), F32),
        jax.ShapeDtypeStruct((POOL_BUF * nb, D_POOL), F32),
        jax.ShapeDtypeStruct(((CONV_W - 1) * nb, CONV_DIM), F32),
        jax.ShapeDtypeStruct(h0.shape, F32),
    ]
    n_bc = SSM_GROUPS * SSM_STATE
    return pl.pallas_call(
        functools.partial(_mix_sample_kernel, nb=nb, t=t, bb=bb, start_pos=start_pos),
        grid=(nb // bb,),
        in_specs=[_const_spec(x.shape), _const_spec(sp.shape), _const_spec(sc.shape), hblk]
        + [_const_spec(a.shape) for a in consts],
        out_specs=[whole(o.shape) for o in outs[:4]] + [hblk],
        out_shape=outs,
        scratch_shapes=[
            pltpu.VMEM((n, D_POOL), F32),
            pltpu.VMEM((n, D_SSM), F32),
            pltpu.VMEM((n, D_SSM), F32),
            pltpu.VMEM((n, D_SSM), F32),
            pltpu.VMEM((n, D_SSM), F32),
            pltpu.VMEM((n, n_bc), F32),
            pltpu.VMEM((n, n_bc), F32),
            pltpu.VMEM((SSM_HEADS, nb, LANES), F32),
            pltpu.VMEM((n, D_SSM), F32),
        ],
        compiler_params=_cparams(("arbitrary",)),
        name="mix_sample",
    )(x, sp, sc, h0, *consts)


def _prep_weights(g_ffn1, w1_gate, w1_up, w1_down, g_mix, w_in, conv_w, conv_b, dt_bias, a_log,
                  d_skip, ssm_norm, w_pool, pool_scale, w_out, g_mem, w_mem_k, w_mem_v, g_xq,
                  w_xq, w_xo, g_ffn2, w2_gate, w2_up, w2_down, g_final):
    row = lambda v: v.reshape(1, -1).astype(F32)
    pad_heads = lambda v: jnp.pad(v.astype(F32), (0, LANES - SSM_HEADS)).reshape(1, LANES)
    w_pool_bd = jnp.zeros((D_POOL, D_POOL), F32)
    for g in range(len(POOL_WINDOWS)):
        w_pool_bd = w_pool_bd.at[g * POOL_GW:(g + 1) * POOL_GW, g * POOL_GW:(g + 1) * POOL_GW].set(w_pool[g])
    head_of_chan = jnp.arange(D_SSM) // SSM_HEADDIM
    expand = (jnp.arange(LANES)[:, None] == head_of_chan[None, :]).astype(F32)
    return dict(
        g_ffn1=row(g_ffn1), w1_gate=w1_gate, w1_up=w1_up, w1_down=w1_down,
        g_mix=row(g_mix), w_in=jnp.transpose(w_in),
        conv_w=conv_w.astype(F32), conv_b=row(conv_b), dt_bias=pad_heads(dt_bias),
        a_log=pad_heads(a_log), d_skip=row(jnp.repeat(d_skip, SSM_HEADDIM)), ssm_norm=row(ssm_norm),
        w_pool=w_pool_bd.astype(BF16), pool_scale=row(pool_scale), w_out=w_out,
        g_mem=row(g_mem), w_mem_k=w_mem_k, w_mem_v=w_mem_v,
        g_xq=row(g_xq), w_xq=w_xq, w_xo=w_xo,
        g_ffn2=row(g_ffn2), w2_gate=w2_gate, w2_up=w2_up, w2_down=w2_down,
        g_final=row(g_final), expand=expand,
    )


def _layer(x_prompt, x_sample, mem_prompt, mem_k, mem_v, state_pool, state_conv, state_ssm, start_pos, w):
    b, s, d = x_prompt.shape
    nb, t, _ = x_sample.shape
    tmaj = lambda a: jnp.swapaxes(a, 0, 1).reshape(-1, a.shape[-1])
    bmaj = lambda a, r: jnp.swapaxes(a.reshape(r, nb, -1), 0, 1)

    k, v, k_b, v_b = _memkv(mem_prompt, w["g_mem"], w["w_mem_k"], w["w_mem_v"])
    xp, xs = _ffn(x_prompt.reshape(b * s, d), x_sample, w["g_ffn1"], w["w1_gate"], w["w1_up"],
                  w["w1_down"], w["g_final"], final=False, tm=FFN_ROWS)

    xp, pool_p, conv_p, ssm_p, w_in_bf = _mix_prompt(xp.reshape(b, s, d), w, tc=MIX_ROWS, nseq=MIX_SEQS)

    x2, q, pool_s, conv_s, ssm_s = _mix_sample(xs, tmaj(state_pool), tmaj(state_conv), state_ssm, w_in_bf, w,
                                               nb=nb, t=t, bb=STATE_BLOCK, start_pos=start_pos)

    xp, o, w_xo_bf, w2g, w2u, w2d = _attn(xp, k_b, v_b, w["g_xq"], w["w_xq"], w["w_xo"], q, mem_k, mem_v,
                                          (w["w2_gate"], w["w2_up"], w["w2_down"]), tq=ATTN_ROWS)

    yp, ys = _ffn(xp.reshape(b * s, d), x2, w["g_ffn2"], w2g, w2u, w2d,
                  w["g_final"], final=True, tm=FFN_ROWS, attn_out=(o, w_xo_bf))
    return (yp.reshape(b, s, d), ys, k, v, pool_p, conv_p, ssm_p,
            bmaj(pool_s, POOL_BUF), bmaj(conv_s, CONV_W - 1), ssm_s)


def kernel(x_prompt, x_sample, mem_prompt, cache_mem_k, cache_mem_v, state_pool, state_conv, state_ssm,
           g_ffn1, w1_gate, w1_up, w1_down, g_mix, w_in, conv_w, conv_b, dt_bias, a_log, d_skip,
           ssm_norm, w_pool, pool_scale, w_out, g_mem, w_mem_k, w_mem_v, g_xq, w_xq, w_xo,
           g_ffn2, w2_gate, w2_up, w2_down, g_final):
    assert g_ffn1.shape[0] == 1, "single-layer model"
    w = _prep_weights(g_ffn1[0], w1_gate[0], w1_up[0], w1_down[0], g_mix[0], w_in[0], conv_w[0],
                      conv_b[0], dt_bias[0], a_log[0], d_skip[0], ssm_norm[0], w_pool[0],
                      pool_scale[0], w_out[0], g_mem[0], w_mem_k[0], w_mem_v[0], g_xq[0], w_xq[0],
                      w_xo[0], g_ffn2[0], w2_gate[0], w2_up[0], w2_down[0], g_final)
    bp = x_prompt.shape[0]
    y_p, y_s, k_p, v_p, pool_p, conv_p, ssm_p, pool_s, conv_s, ssm_s = _layer(
        x_prompt, x_sample, mem_prompt, cache_mem_k[0], cache_mem_v[0], state_pool[0], state_conv[0],
        state_ssm[0], PAST_LEN, w)
    kv_shape = (1, bp, N_MEM, MEM_HEADS, MEM_HD)
    return (y_p, y_s, k_p.reshape(kv_shape), v_p.reshape(kv_shape), pool_p[None], conv_p[None],
            ssm_p[None], pool_s[None], conv_s[None], ssm_s[None])
```

```python
import functools

import jax
import jax.numpy as jnp
from jax import lax
from jax.experimental import pallas as pl
from jax.experimental.pallas import tpu as pltpu

F32 = jnp.float32
BF16 = jnp.bfloat16

D_POOL = 256
POOL_WINDOWS = (2, 4, 8, 16)
POOL_GW = 64
POOL_BUF = 15
D_SSM = 768
SSM_HEADDIM = 64
SSM_HEADS = 12
SSM_GROUPS = 2
SSM_HPG = 6
SSM_STATE = 128
CONV_W = 4
CONV_DIM = 1280
D_PROJ = D_POOL + D_SSM + CONV_DIM
N_MEM = 256
MEM_HEADS = 4
MEM_HD = 256
PAST_LEN = 16384
EPS = 1e-6
LANES = 128
CHUNK = 128
POOL_HIST = 32
CONV_HIST = 8
SCAN_PAD = CHUNK // 2

VMEM_LIMIT = 56 * 1024 * 1024
VMEM_LIMIT_FFN = 60 * 1024 * 1024
FFN_ROWS = 1024
FFN_W_STEPS = 16
MIX_ROWS = 256
MIX_SEQS = 2
ATTN_ROWS = 512
STATE_BLOCK = 8
STATE_SLOTS = 3


def _cparams(sem, vmem_limit=VMEM_LIMIT):
    return pltpu.CompilerParams(dimension_semantics=sem, vmem_limit_bytes=vmem_limit)


def _const_spec(shape):
    nd = len(shape)
    return pl.BlockSpec(shape, lambda *_: (0,) * nd, pipeline_mode=pl.Buffered(1))


def _tm_rows(ref):
    return jnp.concatenate([ref[:, i, :] for i in range(ref.shape[1])], axis=0)


def _store_tm_rows(ref, rows):
    nb = ref.shape[0]
    for i in range(ref.shape[1]):
        ref[:, i, :] = rows[i * nb:(i + 1) * nb]


def _rms(x, g):
    ms = jnp.mean(x * x, axis=-1, keepdims=True)
    return x * lax.rsqrt(ms + EPS) * g


def _silu(x):
    return x * jax.nn.sigmoid(x)


def _dot(a, b):
    return jnp.dot(a, b, preferred_element_type=F32)


def _dot_nt(a, b):
    return lax.dot_general(a, b, (((1,), (1,)), ((), ())), preferred_element_type=F32)


def _dot_tn(a, b):
    return lax.dot_general(a, b, (((0,), (0,)), ((), ())), preferred_element_type=F32)


def _ffn_kernel(xp_ref, xs_ref, g_ref, wg_ref, wu_ref, wd_ref, gf_ref, *rest, final, n_p, attn_out):
    if attn_out:
        ao_ref, wxo_ref, op_ref, os_ref, wg_s, wu_s, wd_s = rest
    else:
        op_ref, os_ref, wg_s, wu_s, wd_s = rest
    i = pl.program_id(0)
    rows_gu = wg_ref.shape[0]
    rows_d = wd_ref.shape[0]

    @pl.when(i < FFN_W_STEPS)
    def _():
        r = pl.multiple_of(i * rows_gu, rows_gu)
        wg_s[pl.ds(r, rows_gu), :] = wg_ref[...].astype(BF16)
        wu_s[pl.ds(r, rows_gu), :] = wu_ref[...].astype(BF16)
        r = pl.multiple_of(i * rows_d, rows_d)
        wd_s[pl.ds(r, rows_d), :] = wd_ref[...].astype(BF16)

    def tile(x):
        xn = _rms(x, g_ref[...]).astype(BF16)
        gate = _dot(xn, wg_s[...])
        up = _dot(xn, wu_s[...])
        h = (_silu(gate) * up).astype(BF16)
        out = x + 0.5 * _dot(h, wd_s[...])
        return _rms(out, gf_ref[...]) if final else out

    @pl.when(jnp.logical_and(i >= FFN_W_STEPS, i < FFN_W_STEPS + n_p))
    def _():
        op_ref[...] = tile(xp_ref[...])

    @pl.when(i == FFN_W_STEPS + n_p)
    def _():
        x = _tm_rows(xs_ref)
        if attn_out:
            nb, nq, hd = ao_ref.shape
            heads = wxo_ref.shape[0] // hd
            o = jnp.concatenate(
                [jnp.concatenate([ao_ref[:, t * heads + h, :] for h in range(heads)], axis=1)
                 for t in range(nq // heads)], axis=0)
            x = x + _dot(o.astype(BF16), wxo_ref[...])
        out = tile(x)
        if os_ref.ndim == 3:
            _store_tm_rows(os_ref, out)
        else:
            os_ref[...] = out


def _ffn(xp, xs, g, wg, wu, wd, gf, *, final, tm, attn_out=None):
    tp, d = xp.shape
    nb, t, _ = xs.shape
    os_shape = xs.shape if final else (nb * t, d)
    dff = wg.shape[1]
    assert tp % tm == 0 and d % FFN_W_STEPS == 0 and dff % FFN_W_STEPS == 0
    n_p = tp // tm
    p_tile = pl.BlockSpec((tm, d), lambda i: (jnp.clip(i - FFN_W_STEPS, 0, n_p - 1), 0))
    s_tile = pl.BlockSpec(os_shape, lambda i: (0,) * len(os_shape))
    w_chunk = lambda rows, cols: pl.BlockSpec((rows, cols), lambda i: (jnp.minimum(i, FFN_W_STEPS - 1), 0))
    extra = list(attn_out) if attn_out else []
    return pl.pallas_call(
        functools.partial(_ffn_kernel, final=final, n_p=n_p, attn_out=bool(attn_out)),
        grid=(FFN_W_STEPS + n_p + 1,),
        in_specs=[
            p_tile, _const_spec(xs.shape), _const_spec(g.shape),
            w_chunk(d // FFN_W_STEPS, dff), w_chunk(d // FFN_W_STEPS, dff), w_chunk(dff // FFN_W_STEPS, d),
            _const_spec(gf.shape),
        ] + [_const_spec(a.shape) for a in extra],
        out_specs=[p_tile, s_tile],
        out_shape=[jax.ShapeDtypeStruct(xp.shape, F32), jax.ShapeDtypeStruct(os_shape, F32)],
        scratch_shapes=[pltpu.VMEM((d, dff), BF16), pltpu.VMEM((d, dff), BF16), pltpu.VMEM((dff, d), BF16)],
        compiler_params=_cparams(("arbitrary",), VMEM_LIMIT_FFN),
        name="ffn_final" if final else "ffn",
    )(xp, xs, g, wg, wu, wd, gf, *extra)


def _memkv_kernel(m_ref, g_ref, wk_ref, wv_ref, k_ref, v_ref, kb_ref, vb_ref, wk_s, wv_s):
    @pl.when(pl.program_id(0) == 0)
    def _():
        wk_s[...] = wk_ref[...].astype(BF16)
        wv_s[...] = wv_ref[...].astype(BF16)

    mn = _rms(m_ref[...], g_ref[...]).astype(BF16)
    k = _dot(mn, wk_s[...])
    v = _dot(mn, wv_s[...])
    for h in range(MEM_HEADS):
        k_ref[:, h, :] = k[:, h * MEM_HD:(h + 1) * MEM_HD]
        v_ref[:, h, :] = v[:, h * MEM_HD:(h + 1) * MEM_HD]
    kb_ref[...] = k.astype(BF16)
    vb_ref[...] = v.astype(BF16)


def _memkv(mem, g, wk, wv):
    b, n_mem, d = mem.shape
    row_blk = pl.BlockSpec((None, n_mem, d), lambda i: (i, 0, 0))
    head_blk = pl.BlockSpec((None, n_mem, MEM_HEADS, MEM_HD), lambda i: (i, 0, 0, 0))
    return pl.pallas_call(
        _memkv_kernel,
        grid=(b,),
        in_specs=[row_blk, _const_spec(g.shape), _const_spec(wk.shape), _const_spec(wv.shape)],
        out_specs=[head_blk, head_blk, row_blk, row_blk],
        out_shape=[jax.ShapeDtypeStruct((b, n_mem, MEM_HEADS, MEM_HD), F32)] * 2
        + [jax.ShapeDtypeStruct((b, n_mem, d), BF16)] * 2,
        scratch_shapes=[pltpu.VMEM(wk.shape, BF16), pltpu.VMEM(wv.shape, BF16)],
        compiler_params=_cparams(("arbitrary",)),
        name="memkv",
    )(mem, g, wk, wv)


def _lane_group_select(vals, width):
    lane = lax.broadcasted_iota(jnp.int32, vals[0].shape, 1)
    out = vals[-1]
    for g in range(len(vals) - 2, -1, -1):
        out = jnp.where(lane < (g + 1) * width, vals[g], out)
    return out


def _col_bcast(m, k, width):
    return jnp.broadcast_to(m[:, k:k + 1], (m.shape[0], width))


def _gate_norm_out(y, z, pool_out, x_res, ssm_norm, w_out):
    yn = _rms(y * _silu(z), ssm_norm)
    cat = jnp.concatenate([pool_out, yn], axis=-1).astype(BF16)
    return x_res + _dot(cat, w_out)


def _mix_prompt_kernel(x_ref, g_ref, win_ref, cw_ref, cb_ref,
                       dtb_ref, alog_ref, dsk_ref, sn_ref, wp_ref, ps_ref, wout_ref,
                       o_ref, opool_ref, oconv_ref, ossm_ref, owin_ref,
                       wu_s, wz_s, wx_s, wdt_s, wo_s,
                       ubuf, s2buf, s4buf, s8buf, cbuf, abuf, ht_ref, ybuf, *, tc, nseq):
    c = pl.program_id(1)
    nc = pl.num_programs(1)
    n_pool_slabs = D_POOL // LANES
    n_conv_slabs = CONV_DIM // LANES
    n_chunks = tc // CHUNK
    rp = POOL_HIST + tc
    seqs = range(nseq)

    def slab(s):
        return slice(s * LANES, (s + 1) * LANES)

    @pl.when(jnp.logical_and(pl.program_id(0) == 0, c == 0))
    def _():
        wu_s[...] = win_ref[0:D_POOL, :].T.astype(BF16)
        wz_s[...] = win_ref[D_POOL:D_POOL + D_SSM, :].T.astype(BF16)
        wx_s[...] = win_ref[D_POOL + D_SSM:D_PROJ, :].T.astype(BF16)
        lead = 16 - SSM_HEADS
        place = (lax.broadcasted_iota(jnp.int32, (16, LANES), 0) - lead
                 == lax.broadcasted_iota(jnp.int32, (16, LANES), 1))
        wdt_s[...] = _dot_tn(win_ref[D_PROJ - lead:D_PROJ + SSM_HEADS, :].astype(BF16),
                             jnp.where(place, 1.0, 0.0).astype(BF16)).astype(BF16)
        wo_s[...] = wout_ref[...].astype(BF16)
        owin_ref[:, 0:D_POOL] = wu_s[...]
        owin_ref[:, D_POOL:D_POOL + D_SSM] = wz_s[...]
        owin_ref[:, D_POOL + D_SSM:D_PROJ] = wx_s[...]
        owin_ref[:, D_PROJ:D_PROJ + LANES] = wdt_s[...]

    @pl.when(c == 0)
    def _():
        ubuf[:, :, 0:POOL_HIST, :] = jnp.zeros((nseq, n_pool_slabs, POOL_HIST, LANES), F32)
        cbuf[:, :, 0:CONV_HIST, :] = jnp.zeros((nseq, n_conv_slabs, CONV_HIST, LANES), F32)
        abuf[:, :, 0:SCAN_PAD, :] = jnp.zeros((nseq, n_chunks, SCAN_PAD, LANES), F32)
        ht_ref[...] = jnp.zeros_like(ht_ref)

    lo_half = lax.broadcasted_iota(jnp.int32, (tc, LANES), 1) < POOL_GW
    pos1 = c * tc + lax.broadcasted_iota(jnp.int32, (tc, LANES), 0) + 1
    a_neg = -jnp.exp(alog_ref[...])
    row = lax.broadcasted_iota(jnp.int32, (CHUNK, CHUNK), 0)
    col = lax.broadcasted_iota(jnp.int32, (CHUNK, CHUNK), 1)
    causal = row >= col
    lo = lax.broadcasted_iota(jnp.int32, (CHUNK, LANES), 1) < SSM_HEADDIM
    mask_lo = jnp.where(lo, 1.0, 0.0).astype(BF16)
    mask_hi = jnp.where(lo, 0.0, 1.0).astype(BF16)
    gw = SSM_HPG * SSM_HEADDIM
    n_x = D_SSM // LANES
    st8 = {}

    def head(i):
        x = x_ref[i]
        xn = _rms(x, g_ref[...]).astype(BF16)
        st8[i] = dict(x=x, xn=xn, xbc=[None] * n_conv_slabs)

    def conv_block(i, s0, ns):
        xn = st8[i]["xn"]
        raw = _dot(xn, wx_s[:, s0 * LANES:(s0 + ns) * LANES])
        for t in range(ns):
            s = s0 + t
            cbuf[i, s, CONV_HIST:CONV_HIST + tc, :] = raw[:, slab(t)]
            acc = cbuf[i, s, CONV_HIST:CONV_HIST + tc, :] * cw_ref[CONV_W - 1:CONV_W, slab(s)]
            for j in range(1, CONV_W):
                acc = acc + (cbuf[i, s, CONV_HIST - j:CONV_HIST - j + tc, :]
                             * cw_ref[CONV_W - 1 - j:CONV_W - j, slab(s)])
            st8[i]["xbc"][s] = _silu(acc + cb_ref[:, slab(s)])

    def dt_proj(i):
        st8[i]["dt"] = jax.nn.softplus(_dot(st8[i]["xn"], wdt_s[...]) + dtb_ref[...])

    def pool(i):
        u = _dot(st8[i]["xn"], wu_s[...])
        for s in range(n_pool_slabs):
            ubuf[i, s, POOL_HIST:rp, :] = u[:, slab(s)]
            s2buf[i, s, 8:rp, :] = ubuf[i, s, 8:rp, :] + ubuf[i, s, 7:rp - 1, :]
        s4buf[i, 16:rp, :] = s2buf[i, 1, 16:rp, :] + s2buf[i, 1, 14:rp - 2, :]
        s8buf[i, 24:rp, :] = s4buf[i, 24:rp, :] + s4buf[i, 20:rp - 4, :]
        win_sums = [
            jnp.where(lo_half, s2buf[i, 0, POOL_HIST:rp, :],
                      s2buf[i, 0, POOL_HIST:rp, :] + s2buf[i, 0, POOL_HIST - 2:rp - 2, :]),
            jnp.where(lo_half, s8buf[i, POOL_HIST:rp, :],
                      s8buf[i, POOL_HIST:rp, :] + s8buf[i, POOL_HIST - 8:rp - 8, :]),
        ]
        ps = []
        for s in range(n_pool_slabs):
            win = jnp.where(lo_half, POOL_WINDOWS[2 * s], POOL_WINDOWS[2 * s + 1])
            cnt = jnp.minimum(pos1, win).astype(F32)
            ps.append(win_sums[s] / cnt - u[:, slab(s)])
        st8[i]["pool_out"] = _dot(jnp.concatenate(ps, axis=1).astype(BF16), wp_ref[...]) * ps_ref[...]

    def ssd_chunk(i, j):
        xbc = st8[i]["xbc"]
        sl = slice(j * CHUNK, (j + 1) * CHUNK)
        dt_c = st8[i]["dt"][sl]
        acs = dt_c * a_neg
        sh = 1
        while sh < CHUNK:
            abuf[i, j, SCAN_PAD:SCAN_PAD + CHUNK, :] = acs
            acs = acs + abuf[i, j, SCAN_PAD - sh:SCAN_PAD - sh + CHUNK, :]
            sh *= 2
        acs_last = acs[CHUNK - 1:CHUNK, :]
        fdec = jnp.exp(acs_last - acs) * dt_c
        src_t = (acs - jnp.log(dt_c)).T
        ht = ht_ref[i]
        ht_b = ht.astype(BF16)
        cdec, st = [], []
        for g in range(SSM_GROUPS):
            c_g = xbc[n_x + SSM_GROUPS + g][sl].astype(BF16)
            b_g = xbc[n_x + g][sl].astype(BF16)
            scores = _dot_nt(c_g, b_g)
            y_off = _dot(c_g, ht_b[:, g * gw:(g + 1) * gw])
            xw = []
            for q in range(SSM_HPG // 2):
                k0 = g * SSM_HPG + 2 * q
                pair = k0 // 2
                a_cols = [_col_bcast(acs, k, CHUNK) for k in (k0, k0 + 1)]
                lhs = [(scores * jnp.exp(jnp.where(causal, a_cols[h] - src_t[k0 + h:k0 + h + 1, :], -jnp.inf))
                        ).astype(BF16) for h in range(2)]
                x_pair = xbc[pair][sl]
                x_b = x_pair.astype(BF16)
                rhs = jnp.concatenate([x_b * mask_lo, x_b * mask_hi], axis=0)
                e_pair = jnp.exp(jnp.where(lo, a_cols[0], a_cols[1]))
                ybuf[i, sl, slab(pair)] = (_dot(jnp.concatenate(lhs, axis=1), rhs)
                                           + y_off[:, slab(q)] * e_pair + x_pair * dsk_ref[:, slab(pair)])
                f_pair = jnp.where(lo, _col_bcast(fdec, k0, LANES), _col_bcast(fdec, k0 + 1, LANES))
                xw.append((x_pair * f_pair).astype(BF16))
                cdec.append(e_pair[CHUNK - 1:CHUNK, :])
            st.append(_dot_tn(b_g, jnp.concatenate(xw, axis=1)))
        ht_ref[i] = ht * jnp.concatenate(cdec, axis=1) + jnp.concatenate(st, axis=1)

    def z_proj(i):
        st8[i]["zs"] = _silu(_dot(st8[i]["xn"], wz_s[...]))

    def tail(i):
        d = st8[i]
        yn = _rms(ybuf[i] * d["zs"], sn_ref[...])
        cat = jnp.concatenate([d["pool_out"], yn], axis=-1).astype(BF16)
        o_ref[i] = d["x"] + _dot(cat, wo_s[...])

    def mid(i):
        conv_block(i, n_x, 2 * SSM_GROUPS)
        dt_proj(i)
        for s0 in range(0, n_x, 2):
            conv_block(i, s0, 2)

    for i in seqs:
        head(i)
    for i in seqs:
        mid(i)
    for i in seqs:
        pool(i)
    for j in range(n_chunks):
        for i in seqs:
            ssd_chunk(i, j)
        if j == 0:
            for i in seqs:
                z_proj(i)
    for i in seqs:
        tail(i)

    ubuf[:, :, 0:POOL_HIST, :] = ubuf[:, :, tc:tc + POOL_HIST, :]
    cbuf[:, :, 0:CONV_HIST, :] = cbuf[:, :, tc:tc + CONV_HIST, :]

    @pl.when(c == nc - 1)
    def _():
        for i in seqs:
            for s in range(n_pool_slabs):
                opool_ref[i, :, slab(s)] = ubuf[i, s, POOL_HIST - POOL_BUF:POOL_HIST, :]
            for s in range(n_conv_slabs):
                oconv_ref[i, :, slab(s)] = cbuf[i, s, CONV_HIST - (CONV_W - 1):CONV_HIST, :]
            ossm_ref[i] = ht_ref[i].T.reshape(SSM_HEADS, SSM_HEADDIM, SSM_STATE)


def _mix_prompt(x, w, *, tc, nseq):
    b, s, d = x.shape
    assert s % tc == 0 and tc % CHUNK == 0 and b % nseq == 0
    consts = [w["g_mix"], w["w_in"], w["conv_w"], w["conv_b"],
              w["dt_bias"], w["a_log"], w["d_skip"], w["ssm_norm"], w["w_pool"], w["pool_scale"],
              w["w_out"]]
    tile = pl.BlockSpec((nseq, tc, d), lambda i, j: (i, j, 0))
    return pl.pallas_call(
        functools.partial(_mix_prompt_kernel, tc=tc, nseq=nseq),
        grid=(b // nseq, s // tc),
        in_specs=[tile] + [_const_spec(a.shape) for a in consts],
        out_specs=[
            tile,
            pl.BlockSpec((nseq, POOL_BUF, D_POOL), lambda i, j: (i, 0, 0)),
            pl.BlockSpec((nseq, CONV_W - 1, CONV_DIM), lambda i, j: (i, 0, 0)),
            pl.BlockSpec((nseq, SSM_HEADS, SSM_HEADDIM, SSM_STATE), lambda i, j: (i, 0, 0, 0)),
            pl.BlockSpec((d, D_PROJ + LANES), lambda i, j: (0, 0)),
        ],
        out_shape=[
            jax.ShapeDtypeStruct((b, s, d), F32),
            jax.ShapeDtypeStruct((b, POOL_BUF, D_POOL), F32),
            jax.ShapeDtypeStruct((b, CONV_W - 1, CONV_DIM), F32),
            jax.ShapeDtypeStruct((b, SSM_HEADS, SSM_HEADDIM, SSM_STATE), F32),
            jax.ShapeDtypeStruct((d, D_PROJ + LANES), BF16),
        ],
        scratch_shapes=[
            pltpu.VMEM((d, D_POOL), BF16), pltpu.VMEM((d, D_SSM), BF16),
            pltpu.VMEM((d, CONV_DIM), BF16), pltpu.VMEM((d, LANES), BF16), pltpu.VMEM((d, d), BF16),
            pltpu.VMEM((nseq, D_POOL // LANES, POOL_HIST + tc, LANES), F32),
            pltpu.VMEM((nseq, D_POOL // LANES, POOL_HIST + tc, LANES), F32),
            pltpu.VMEM((nseq, POOL_HIST + tc, LANES), F32),
            pltpu.VMEM((nseq, POOL_HIST + tc, LANES), F32),
            pltpu.VMEM((nseq, CONV_DIM // LANES, CONV_HIST + tc, LANES), F32),
            pltpu.VMEM((nseq, tc // CHUNK, SCAN_PAD + CHUNK, LANES), F32),
            pltpu.VMEM((nseq, SSM_STATE, D_SSM), F32),
            pltpu.VMEM((nseq, tc, D_SSM), F32),
        ],
        compiler_params=_cparams(("arbitrary", "arbitrary")),
        name="mix_prompt",
    )(x, *consts)


def _attn_kernel(x_ref, g_ref, wq_ref, k_ref, v_ref, wo_ref, qs_ref, ks_ref, vs_ref,
                 o_ref, os_ref, owo_ref, wq_s, wo_s, *, bb):
    @pl.when(jnp.logical_and(pl.program_id(0) == 0, pl.program_id(1) == 0))
    def _():
        wq_s[...] = wq_ref[...].astype(BF16)
        wo_s[...] = wo_ref[...].astype(BF16)
        owo_ref[...] = wo_s[...]

    hs = [slice(h * MEM_HD, (h + 1) * MEM_HD) for h in range(MEM_HEADS)]
    scale = MEM_HD ** -0.5
    nq = qs_ref.shape[1]
    rows = N_MEM * MEM_HEADS

    def softmax(s):
        e = jnp.exp(s - jnp.max(s, axis=-1, keepdims=True))
        return (e / jnp.sum(e, axis=-1, keepdims=True)).astype(BF16)

    q_head = lax.broadcasted_iota(jnp.int32, (nq, rows), 0) % MEM_HEADS
    kv_head = lax.broadcasted_iota(jnp.int32, (nq, rows), 1) % MEM_HEADS
    same_head = q_head == kv_head

    q_all = qs_ref[...].reshape(bb * nq, MEM_HD)
    q_all = jnp.concatenate([q_all, jnp.zeros((LANES - bb * nq, MEM_HD), F32)], axis=0).astype(BF16)
    s_s = [_dot_nt(ks_ref[i].reshape(rows, MEM_HD).astype(BF16), q_all) for i in range(bb)]

    x = x_ref[...]
    q = _dot(_rms(x, g_ref[...]).astype(BF16), wq_s[...]).astype(BF16)

    p_s = [softmax(jnp.where(same_head, s_s[i].T[i * nq:(i + 1) * nq] * scale, -jnp.inf)) for i in range(bb)]

    s_p = [_dot_nt(q[:, hs[h]], k_ref[:, hs[h]]) * scale for h in range(MEM_HEADS)]

    for i in range(bb):
        os_ref[i] = _dot(p_s[i], vs_ref[i].reshape(rows, MEM_HD).astype(BF16))

    p_p = [softmax(s) for s in s_p]
    o = jnp.concatenate([_dot(p_p[h], v_ref[:, hs[h]]) for h in range(MEM_HEADS)], axis=-1).astype(BF16)
    o_ref[...] = x + _dot(o, wo_s[...])


def _attn(x, k, v, g, wq, wo, qs, ks, vs, *, tq):
    b, s, d = x.shape
    nb, nq, hd = qs.shape
    nj = s // tq
    assert nb % (b * nj) == 0
    bb = nb // (b * nj)
    assert bb * nq <= LANES
    tile = pl.BlockSpec((None, tq, d), lambda i, j: (i, j, 0))
    mem = pl.BlockSpec((None, N_MEM, d), lambda i, j: (i, 0, 0))
    s_q = pl.BlockSpec((bb, nq, hd), lambda i, j: (i * nj + j, 0, 0))
    s_mem = pl.BlockSpec((bb, N_MEM, MEM_HEADS, hd), lambda i, j: (i * nj + j, 0, 0, 0))
    return pl.pallas_call(
        functools.partial(_attn_kernel, bb=bb),
        grid=(b, nj),
        in_specs=[tile, _const_spec(g.shape), _const_spec(wq.shape), mem, mem, _const_spec(wo.shape),
                  s_q, s_mem, s_mem],
        out_specs=[tile, s_q, pl.BlockSpec(wo.shape, lambda i, j: (0, 0))],
        out_shape=[jax.ShapeDtypeStruct((b, s, d), F32), jax.ShapeDtypeStruct((nb, nq, hd), F32),
                   jax.ShapeDtypeStruct(wo.shape, BF16)],
        scratch_shapes=[pltpu.VMEM(wq.shape, BF16), pltpu.VMEM(wo.shape, BF16)],
        compiler_params=_cparams(("arbitrary", "arbitrary")),
        name="attn",
    )(x, g, wq, k, v, wo, qs, ks, vs)


def _mix_sample_kernel(x_ref, sp_ref, sc_ref, h0_ref, g_ref, win_ref, cw_ref, cb_ref, dtb_ref, alog_ref,
                       dsk_ref, wp_ref, ps_ref, ex_ref, sn_ref, wo_ref, gq_ref, wq_ref,
                       x2_ref, q_ref, npool_ref, nconv_ref, hout_ref,
                       pool_s, z_s, ydx_s, expa_s, xw_s, b_s, c_s, dec_s, y_s,
                       hbuf, hsem, wo_f, wq_f, wsem, *, nb, t, bb, start_pos):
    j = pl.program_id(0)
    n_blocks = pl.num_programs(0)
    gw = SSM_HPG * SSM_HEADDIM
    n_bc = SSM_GROUPS * SSM_STATE

    def rows(i):
        return slice(i * nb, (i + 1) * nb)

    def state_copy(blk, slot):
        return pltpu.make_async_copy(h0_ref.at[pl.ds(blk * bb, bb)], hbuf.at[slot], hsem.at[slot])

    def weight_copies():
        return (pltpu.make_async_copy(wo_ref, wo_f, wsem.at[0]),
                pltpu.make_async_copy(wq_ref, wq_f, wsem.at[1]))

    @pl.when(j == 0)
    def _():
        state_copy(0, 0).start()
        for c in weight_copies():
            c.start()

    @pl.when(jnp.logical_and(j == 0, n_blocks > 1))
    def _():
        state_copy(1, 1).start()

    @pl.when(j + 2 < n_blocks)
    def _():
        state_copy(j + 2, (j + 2) % STATE_SLOTS).start()

    @pl.when(j == 0)
    def _():
        x = x_ref[...]
        xn = _rms(x, g_ref[...]).astype(BF16)
        z_s[...] = _dot(xn, win_ref[:, D_POOL:D_POOL + D_SSM])

        u = _dot(xn, win_ref[:, 0:D_POOL])
        ext = [sp_ref[rows(i), :] for i in range(POOL_BUF)] + [u[rows(i)] for i in range(t)]
        ps = []
        for i in range(t):
            sums, cnts = [], []
            for w in POOL_WINDOWS:
                s = ext[POOL_BUF + i]
                for k in range(1, w):
                    s = s + ext[POOL_BUF + i - k]
                sums.append(s)
                cnts.append(jnp.full((nb, D_POOL), float(min(start_pos + i + 1, w)), F32))
            mean = _lane_group_select(sums, POOL_GW) / _lane_group_select(cnts, POOL_GW)
            ps.append(mean - ext[POOL_BUF + i])
        p = jnp.concatenate(ps, axis=0).astype(BF16)
        pool_s[...] = _dot(p, wp_ref[...]) * ps_ref[...]
        for i in range(POOL_BUF):
            npool_ref[rows(i), :] = ext[t + i]

        xr = _dot(xn, win_ref[:, D_POOL + D_SSM:D_PROJ])
        cext = [sc_ref[rows(i), :] for i in range(CONV_W - 1)] + [xr[rows(i)] for i in range(t)]
        xbc = []
        for i in range(t):
            acc = cext[i] * cw_ref[0:1, :]
            for k in range(1, CONV_W):
                acc = acc + cext[i + k] * cw_ref[k:k + 1, :]
            xbc.append(_silu(acc + cb_ref[...]))
        for i in range(CONV_W - 1):
            nconv_ref[rows(i), :] = cext[t + i]
        xs = [v[:, :D_SSM] for v in xbc]
        bm = [v[:, D_SSM:D_SSM + n_bc] for v in xbc]
        cm = [v[:, D_SSM + n_bc:] for v in xbc]
        for i in range(t):
            b_s[rows(i), :] = bm[i]
            c_s[rows(i), :] = cm[i]

        dt_all = jax.nn.softplus(_dot(xn, win_ref[:, D_PROJ:D_PROJ + LANES]) + dtb_ref[...])
        a_neg = -jnp.exp(alog_ref[...])
        dt = [dt_all[rows(i)] for i in range(t)]
        acs = []
        for i in range(t):
            da = dt[i] * a_neg
            acs.append(da if i == 0 else acs[-1] + da)
        dec = jnp.exp(acs[-1])
        for k in range(SSM_HEADS):
            dec_s[k] = _col_bcast(dec, k, LANES)

        def expand(v, exact):
            if exact:
                return jnp.dot(v, ex_ref[...], precision=lax.Precision.HIGHEST,
                               preferred_element_type=F32)
            return _dot(v.astype(BF16), ex_ref[...].astype(BF16))

        lane = lax.broadcasted_iota(jnp.int32, (nb, LANES), 1)
        for i in range(t):
            ydx = xs[i] * dsk_ref[...]
            for s in range(i + 1):
                sc = [jnp.sum(cm[i][:, g * SSM_STATE:(g + 1) * SSM_STATE]
                              * bm[s][:, g * SSM_STATE:(g + 1) * SSM_STATE], axis=-1, keepdims=True)
                      for g in range(SSM_GROUPS)]
                sc = jnp.where(lane < SSM_HPG, sc[0], sc[1])
                coef = sc * jnp.exp(acs[i] - acs[s]) * dt[s]
                ydx = ydx + expand(coef, False) * xs[s]
            ydx_s[rows(i), :] = ydx
            expa_s[rows(i), :] = expand(jnp.exp(acs[i]), True)
            xw_s[rows(i), :] = xs[i] * expand(jnp.exp(acs[-1] - acs[i]) * dt[i], True)

    base = pl.multiple_of(j * bb, bb)
    blk_rows = [pl.ds(i * nb + base, bb) for i in range(t)]

    def gather(ref):
        return jnp.concatenate([ref[r, :] for r in blk_rows], axis=0)

    def pad_rows(v):
        return jnp.concatenate([v, jnp.zeros((LANES - t * bb, v.shape[1]), v.dtype)], axis=0)

    c_blk = pad_rows(gather(c_s))
    b_blk = pad_rows(gather(b_s))
    xw_t = pad_rows(gather(xw_s)).T.astype(BF16)
    row_seq = lax.broadcasted_iota(jnp.int32, (LANES, SSM_STATE), 0) % bb
    col_seq = lax.broadcasted_iota(jnp.int32, (SSM_STATE, LANES), 1) % bb
    slot = j % STATE_SLOTS
    state_copy(j, slot).wait()
    h_blk = hbuf.at[slot]
    y_t = []
    for g in range(SSM_GROUPS):
        gs = slice(g * SSM_STATE, (g + 1) * SSM_STATE)
        c_t = c_blk[:, gs].T
        acc = jnp.zeros((gw, LANES), F32)
        for i in range(bb):
            h0 = h_blk[i, g * SSM_HPG:(g + 1) * SSM_HPG].reshape(gw, SSM_STATE)
            acc = acc + _dot(h0.astype(BF16), jnp.where(col_seq == i, c_t, 0.0).astype(BF16))
            b_i = jnp.where(row_seq == i, b_blk[:, gs], 0.0).astype(BF16)
            st = _dot(xw_t[g * gw:(g + 1) * gw, :], b_i)
            for k in range(SSM_HPG):
                hd = g * SSM_HPG + k
                hout_ref[i, hd] = (h_blk[i, hd] * dec_s[hd, pl.ds(base + i, 1), :]
                                   + st[k * SSM_HEADDIM:(k + 1) * SSM_HEADDIM])
        y_t.append(acc.T[:t * bb])
    y_blk = gather(ydx_s) + jnp.concatenate(y_t, axis=1) * gather(expa_s)
    for i in range(t):
        y_s[blk_rows[i], :] = y_blk[i * bb:(i + 1) * bb]

    @pl.when(j == n_blocks - 1)
    def _():
        for c in weight_copies():
            c.wait()
        x2 = _gate_norm_out(y_s[...], z_s[...], pool_s[...], x_ref[...], sn_ref[...],
                            wo_f[...].astype(BF16))
        _store_tm_rows(x2_ref, x2)
        q = _dot(_rms(x2, gq_ref[...]).astype(BF16), wq_f[...].astype(BF16))
        for i in range(t):
            for h in range(MEM_HEADS):
                q_ref[:, i * MEM_HEADS + h, :] = q[rows(i), h * MEM_HD:(h + 1) * MEM_HD]


def _mix_sample(x, sp, sc, h0, w_in_bf, w, *, nb, t, bb, start_pos):
    n, d = x.shape
    assert n == nb * t and nb % bb == 0 and t * bb <= LANES
    consts = [w["g_mix"], w_in_bf, w["conv_w"], w["conv_b"], w["dt_bias"], w["a_log"], w["d_skip"],
              w["w_pool"], w["pool_scale"], w["expand"], w["ssm_norm"], w["w_out"], w["g_xq"], w["w_xq"]]
    in_hbm = pl.BlockSpec(memory_space=pl.ANY)
    const_specs = [in_hbm if a is w["w_out"] or a is w["w_xq"] else _const_spec(a.shape) for a in consts]
    hblk = pl.BlockSpec((bb, SSM_HEADS, SSM_HEADDIM, SSM_STATE), lambda i: (i, 0, 0, 0))
    whole = _const_spec
    outs = [
        jax.ShapeDtypeStruct((nb, t, d), F32),
        jax.ShapeDtypeStruct((nb, t * MEM_HEADS, MEM_HD), F32),
        jax.ShapeDtypeStruct((POOL_BUF * nb, D_POOL), F32),
        jax.ShapeDtypeStruct(((CONV_W - 1) * nb, CONV_DIM), F32),
        jax.ShapeDtypeStruct(h0.shape, F32),
    ]
    n_bc = SSM_GROUPS * SSM_STATE
    return pl.pallas_call(
        functools.partial(_mix_sample_kernel, nb=nb, t=t, bb=bb, start_pos=start_pos),
        grid=(nb // bb,),
        in_specs=[_const_spec(x.shape), _const_spec(sp.shape), _const_spec(sc.shape), in_hbm] + const_specs,
        out_specs=[whole(o.shape) for o in outs[:4]] + [hblk],
        out_shape=outs,
        scratch_shapes=[
            pltpu.VMEM((n, D_POOL), F32),
            pltpu.VMEM((n, D_SSM), F32),
            pltpu.VMEM((n, D_SSM), F32),
            pltpu.VMEM((n, D_SSM), F32),
            pltpu.VMEM((n, D_SSM), F32),
            pltpu.VMEM((n, n_bc), F32),
            pltpu.VMEM((n, n_bc), F32),
            pltpu.VMEM((SSM_HEADS, nb, LANES), F32),
            pltpu.VMEM((n, D_SSM), F32),
            pltpu.VMEM((STATE_SLOTS, bb, SSM_HEADS, SSM_HEADDIM, SSM_STATE), F32),
            pltpu.SemaphoreType.DMA((STATE_SLOTS,)),
            pltpu.VMEM(w["w_out"].shape, F32), pltpu.VMEM(w["w_xq"].shape, F32),
            pltpu.SemaphoreType.DMA((2,)),
        ],
        compiler_params=_cparams(("arbitrary",), VMEM_LIMIT_FFN),
        name="mix_sample",
    )(x, sp, sc, h0, *consts)


def _prep_weights(g_ffn1, w1_gate, w1_up, w1_down, g_mix, w_in, conv_w, conv_b, dt_bias, a_log,
                  d_skip, ssm_norm, w_pool, pool_scale, w_out, g_mem, w_mem_k, w_mem_v, g_xq,
                  w_xq, w_xo, g_ffn2, w2_gate, w2_up, w2_down, g_final):
    row = lambda v: v.reshape(1, -1).astype(F32)
    pad_heads = lambda v: jnp.pad(v.astype(F32), (0, LANES - SSM_HEADS)).reshape(1, LANES)
    w_pool_bd = jnp.zeros((D_POOL, D_POOL), F32)
    for g in range(len(POOL_WINDOWS)):
        w_pool_bd = w_pool_bd.at[g * POOL_GW:(g + 1) * POOL_GW, g * POOL_GW:(g + 1) * POOL_GW].set(w_pool[g])
    head_of_chan = jnp.arange(D_SSM) // SSM_HEADDIM
    expand = (jnp.arange(LANES)[:, None] == head_of_chan[None, :]).astype(F32)
    return dict(
        g_ffn1=row(g_ffn1), w1_gate=w1_gate, w1_up=w1_up, w1_down=w1_down,
        g_mix=row(g_mix), w_in=jnp.transpose(w_in),
        conv_w=conv_w.astype(F32), conv_b=row(conv_b), dt_bias=pad_heads(dt_bias),
        a_log=pad_heads(a_log), d_skip=row(jnp.repeat(d_skip, SSM_HEADDIM)), ssm_norm=row(ssm_norm),
        w_pool=w_pool_bd.astype(BF16), pool_scale=row(pool_scale), w_out=w_out,
        g_mem=row(g_mem), w_mem_k=w_mem_k, w_mem_v=w_mem_v,
        g_xq=row(g_xq), w_xq=w_xq, w_xo=w_xo,
        g_ffn2=row(g_ffn2), w2_gate=w2_gate, w2_up=w2_up, w2_down=w2_down,
        g_final=row(g_final), expand=expand,
    )


def _layer(x_prompt, x_sample, mem_prompt, mem_k, mem_v, state_pool, state_conv, state_ssm, start_pos, w):
    b, s, d = x_prompt.shape
    nb, t, _ = x_sample.shape
    tmaj = lambda a: jnp.swapaxes(a, 0, 1).reshape(-1, a.shape[-1])
    bmaj = lambda a, r: jnp.swapaxes(a.reshape(r, nb, -1), 0, 1)

    k, v, k_b, v_b = _memkv(mem_prompt, w["g_mem"], w["w_mem_k"], w["w_mem_v"])
    xp, xs = _ffn(x_prompt.reshape(b * s, d), x_sample, w["g_ffn1"], w["w1_gate"], w["w1_up"],
                  w["w1_down"], w["g_final"], final=False, tm=FFN_ROWS)

    xp, pool_p, conv_p, ssm_p, w_in_bf = _mix_prompt(xp.reshape(b, s, d), w, tc=MIX_ROWS, nseq=MIX_SEQS)

    x2, q, pool_s, conv_s, ssm_s = _mix_sample(xs, tmaj(state_pool), tmaj(state_conv), state_ssm, w_in_bf, w,
                                               nb=nb, t=t, bb=STATE_BLOCK, start_pos=start_pos)

    xp, o, w_xo_bf = _attn(xp, k_b, v_b, w["g_xq"], w["w_xq"], w["w_xo"], q, mem_k, mem_v, tq=ATTN_ROWS)

    yp, ys = _ffn(xp.reshape(b * s, d), x2, w["g_ffn2"], w["w2_gate"], w["w2_up"], w["w2_down"],
                  w["g_final"], final=True, tm=FFN_ROWS, attn_out=(o, w_xo_bf))
    return (yp.reshape(b, s, d), ys, k, v, pool_p, conv_p, ssm_p,
            bmaj(pool_s, POOL_BUF), bmaj(conv_s, CONV_W - 1), ssm_s)


def kernel(x_prompt, x_sample, mem_prompt, cache_mem_k, cache_mem_v, state_pool, state_conv, state_ssm,
           g_ffn1, w1_gate, w1_up, w1_down, g_mix, w_in, conv_w, conv_b, dt_bias, a_log, d_skip,
           ssm_norm, w_pool, pool_scale, w_out, g_mem, w_mem_k, w_mem_v, g_xq, w_xq, w_xo,
           g_ffn2, w2_gate, w2_up, w2_down, g_final):
    assert g_ffn1.shape[0] == 1, "single-layer model"
    w = _prep_weights(g_ffn1[0], w1_gate[0], w1_up[0], w1_down[0], g_mix[0], w_in[0], conv_w[0],
                      conv_b[0], dt_bias[0], a_log[0], d_skip[0], ssm_norm[0], w_pool[0],
                      pool_scale[0], w_out[0], g_mem[0], w_mem_k[0], w_mem_v[0], g_xq[0], w_xq[0],
                      w_xo[0], g_ffn2[0], w2_gate[0], w2_up[0], w2_down[0], g_final)
    bp = x_prompt.shape[0]
    y_p, y_s, k_p, v_p, pool_p, conv_p, ssm_p, pool_s, conv_s, ssm_s = _layer(
        x_prompt, x_sample, mem_prompt, cache_mem_k[0], cache_mem_v[0], state_pool[0], state_conv[0],
        state_ssm[0], PAST_LEN, w)
    kv_shape = (1, bp, N_MEM, MEM_HEADS, MEM_HD)
    return (y_p, y_s, k_p.reshape(kv_shape), v_p.reshape(kv_shape), pool_p[None], conv_p[None],
            ssm_p[None], pool_s[None], conv_s[None], ssm_s[None])
```

```python
import functools

import jax
import jax.numpy as jnp
from jax import lax
from jax.experimental import pallas as pl
from jax.experimental.pallas import tpu as pltpu

F32 = jnp.float32
BF16 = jnp.bfloat16

D_POOL = 256
POOL_WINDOWS = (2, 4, 8, 16)
POOL_GW = 64
POOL_BUF = 15
D_SSM = 768
SSM_HEADDIM = 64
SSM_HEADS = 12
SSM_GROUPS = 2
SSM_HPG = 6
SSM_STATE = 128
CONV_W = 4
CONV_DIM = 1280
D_PROJ = D_POOL + D_SSM + CONV_DIM
N_MEM = 256
MEM_HEADS = 4
MEM_HD = 256
PAST_LEN = 16384
EPS = 1e-6
LANES = 128
CHUNK = 128
POOL_HIST = 32
CONV_HIST = 8
SCAN_PAD = CHUNK // 2

VMEM_LIMIT = 56 * 1024 * 1024
VMEM_LIMIT_FFN = 60 * 1024 * 1024
FFN_ROWS = 1024
FFN_W_STEPS = 16
MIX_ROWS = 256
MIX_SEQS = 2
ATTN_ROWS = 512
STATE_BLOCK = 8
STATE_SLOTS = 3
CACHE_SLOTS = 3


def _cparams(sem, vmem_limit=VMEM_LIMIT):
    return pltpu.CompilerParams(dimension_semantics=sem, vmem_limit_bytes=vmem_limit)


def _const_spec(shape):
    nd = len(shape)
    return pl.BlockSpec(shape, lambda *_: (0,) * nd, pipeline_mode=pl.Buffered(1))


def _tm_rows(ref):
    return jnp.concatenate([ref[:, i, :] for i in range(ref.shape[1])], axis=0)


def _store_tm_rows(ref, rows):
    nb = ref.shape[0]
    for i in range(ref.shape[1]):
        ref[:, i, :] = rows[i * nb:(i + 1) * nb]


def _rms(x, g):
    ms = jnp.mean(x * x, axis=-1, keepdims=True)
    return x * lax.rsqrt(ms + EPS) * g


def _silu(x):
    return x * jax.nn.sigmoid(x)


def _dot(a, b):
    return jnp.dot(a, b, preferred_element_type=F32)


def _dot_nt(a, b):
    return lax.dot_general(a, b, (((1,), (1,)), ((), ())), preferred_element_type=F32)


def _dot_tn(a, b):
    return lax.dot_general(a, b, (((0,), (0,)), ((), ())), preferred_element_type=F32)


def _ffn_kernel(xp_ref, xs_ref, g_ref, wg_ref, wu_ref, wd_ref, gf_ref, *rest, final, n_p, attn_out):
    if attn_out:
        ao_ref, wxo_ref, op_ref, os_ref, wg_s, wu_s, wd_s = rest
    else:
        op_ref, os_ref, wg_s, wu_s, wd_s = rest
    i = pl.program_id(0)
    rows_gu = wg_ref.shape[0]
    rows_d = wd_ref.shape[0]

    @pl.when(i < FFN_W_STEPS)
    def _():
        r = pl.multiple_of(i * rows_gu, rows_gu)
        wg_s[pl.ds(r, rows_gu), :] = wg_ref[...].astype(BF16)
        wu_s[pl.ds(r, rows_gu), :] = wu_ref[...].astype(BF16)
        r = pl.multiple_of(i * rows_d, rows_d)
        wd_s[pl.ds(r, rows_d), :] = wd_ref[...].astype(BF16)

    def tile(x):
        xn = _rms(x, g_ref[...]).astype(BF16)
        gate = _dot(xn, wg_s[...])
        up = _dot(xn, wu_s[...])
        h = (_silu(gate) * up).astype(BF16)
        out = x + 0.5 * _dot(h, wd_s[...])
        return _rms(out, gf_ref[...]) if final else out

    @pl.when(jnp.logical_and(i >= FFN_W_STEPS, i < FFN_W_STEPS + n_p))
    def _():
        op_ref[...] = tile(xp_ref[...])

    @pl.when(i == FFN_W_STEPS + n_p)
    def _():
        x = _tm_rows(xs_ref)
        if attn_out:
            nb, nq, hd = ao_ref.shape
            heads = wxo_ref.shape[0] // hd
            o = jnp.concatenate(
                [jnp.concatenate([ao_ref[:, t * heads + h, :] for h in range(heads)], axis=1)
                 for t in range(nq // heads)], axis=0)
            x = x + _dot(o.astype(BF16), wxo_ref[...])
        out = tile(x)
        if os_ref.ndim == 3:
            _store_tm_rows(os_ref, out)
        else:
            os_ref[...] = out


def _ffn(xp, xs, g, wg, wu, wd, gf, *, final, tm, attn_out=None):
    tp, d = xp.shape
    nb, t, _ = xs.shape
    os_shape = xs.shape if final else (nb * t, d)
    dff = wg.shape[1]
    assert tp % tm == 0 and d % FFN_W_STEPS == 0 and dff % FFN_W_STEPS == 0
    n_p = tp // tm
    p_tile = pl.BlockSpec((tm, d), lambda i: (jnp.clip(i - FFN_W_STEPS, 0, n_p - 1), 0))
    s_tile = pl.BlockSpec(os_shape, lambda i: (0,) * len(os_shape))
    w_chunk = lambda rows, cols: pl.BlockSpec((rows, cols), lambda i: (jnp.minimum(i, FFN_W_STEPS - 1), 0))
    extra = list(attn_out) if attn_out else []
    return pl.pallas_call(
        functools.partial(_ffn_kernel, final=final, n_p=n_p, attn_out=bool(attn_out)),
        grid=(FFN_W_STEPS + n_p + 1,),
        in_specs=[
            p_tile, _const_spec(xs.shape), _const_spec(g.shape),
            w_chunk(d // FFN_W_STEPS, dff), w_chunk(d // FFN_W_STEPS, dff), w_chunk(dff // FFN_W_STEPS, d),
            _const_spec(gf.shape),
        ] + [_const_spec(a.shape) for a in extra],
        out_specs=[p_tile, s_tile],
        out_shape=[jax.ShapeDtypeStruct(xp.shape, F32), jax.ShapeDtypeStruct(os_shape, F32)],
        scratch_shapes=[pltpu.VMEM((d, dff), BF16), pltpu.VMEM((d, dff), BF16), pltpu.VMEM((dff, d), BF16)],
        compiler_params=_cparams(("arbitrary",), VMEM_LIMIT_FFN),
        name="ffn_final" if final else "ffn",
    )(xp, xs, g, wg, wu, wd, gf, *extra)


def _memkv_kernel(m_ref, g_ref, wk_ref, wv_ref, k_ref, v_ref, kb_ref, vb_ref, wk_s, wv_s):
    @pl.when(pl.program_id(0) == 0)
    def _():
        wk_s[...] = wk_ref[...].astype(BF16)
        wv_s[...] = wv_ref[...].astype(BF16)

    mn = _rms(m_ref[...], g_ref[...]).astype(BF16)
    k = _dot(mn, wk_s[...])
    v = _dot(mn, wv_s[...])
    for h in range(MEM_HEADS):
        k_ref[:, h, :] = k[:, h * MEM_HD:(h + 1) * MEM_HD]
        v_ref[:, h, :] = v[:, h * MEM_HD:(h + 1) * MEM_HD]
    kb_ref[...] = k.astype(BF16)
    vb_ref[...] = v.astype(BF16)


def _memkv(mem, g, wk, wv):
    b, n_mem, d = mem.shape
    row_blk = pl.BlockSpec((None, n_mem, d), lambda i: (i, 0, 0))
    head_blk = pl.BlockSpec((None, n_mem, MEM_HEADS, MEM_HD), lambda i: (i, 0, 0, 0))
    return pl.pallas_call(
        _memkv_kernel,
        grid=(b,),
        in_specs=[row_blk, _const_spec(g.shape), _const_spec(wk.shape), _const_spec(wv.shape)],
        out_specs=[head_blk, head_blk, row_blk, row_blk],
        out_shape=[jax.ShapeDtypeStruct((b, n_mem, MEM_HEADS, MEM_HD), F32)] * 2
        + [jax.ShapeDtypeStruct((b, n_mem, d), BF16)] * 2,
        scratch_shapes=[pltpu.VMEM(wk.shape, BF16), pltpu.VMEM(wv.shape, BF16)],
        compiler_params=_cparams(("arbitrary",)),
        name="memkv",
    )(mem, g, wk, wv)


def _lane_group_select(vals, width):
    lane = lax.broadcasted_iota(jnp.int32, vals[0].shape, 1)
    out = vals[-1]
    for g in range(len(vals) - 2, -1, -1):
        out = jnp.where(lane < (g + 1) * width, vals[g], out)
    return out


def _col_bcast(m, k, width):
    return jnp.broadcast_to(m[:, k:k + 1], (m.shape[0], width))


def _gate_norm_out(y, z, pool_out, x_res, ssm_norm, w_out):
    yn = _rms(y * _silu(z), ssm_norm)
    cat = jnp.concatenate([pool_out, yn], axis=-1).astype(BF16)
    return x_res + _dot(cat, w_out)


def _mix_prompt_kernel(x_ref, g_ref, win_ref, cw_ref, cb_ref,
                       dtb_ref, alog_ref, dsk_ref, sn_ref, wp_ref, ps_ref, wout_ref,
                       o_ref, opool_ref, oconv_ref, ossm_ref, owin_ref,
                       wu_s, wz_s, wx_s, wdt_s, wo_s,
                       ubuf, s2buf, s4buf, s8buf, cbuf, abuf, ht_ref, ybuf, *, tc, nseq):
    c = pl.program_id(1)
    nc = pl.num_programs(1)
    n_pool_slabs = D_POOL // LANES
    n_conv_slabs = CONV_DIM // LANES
    n_chunks = tc // CHUNK
    rp = POOL_HIST + tc
    seqs = range(nseq)

    def slab(s):
        return slice(s * LANES, (s + 1) * LANES)

    @pl.when(jnp.logical_and(pl.program_id(0) == 0, c == 0))
    def _():
        wu_s[...] = win_ref[0:D_POOL, :].T.astype(BF16)
        wz_s[...] = win_ref[D_POOL:D_POOL + D_SSM, :].T.astype(BF16)
        wx_s[...] = win_ref[D_POOL + D_SSM:D_PROJ, :].T.astype(BF16)
        lead = 16 - SSM_HEADS
        place = (lax.broadcasted_iota(jnp.int32, (16, LANES), 0) - lead
                 == lax.broadcasted_iota(jnp.int32, (16, LANES), 1))
        wdt_s[...] = _dot_tn(win_ref[D_PROJ - lead:D_PROJ + SSM_HEADS, :].astype(BF16),
                             jnp.where(place, 1.0, 0.0).astype(BF16)).astype(BF16)
        wo_s[...] = wout_ref[...].astype(BF16)
        owin_ref[:, 0:D_POOL] = wu_s[...]
        owin_ref[:, D_POOL:D_POOL + D_SSM] = wz_s[...]
        owin_ref[:, D_POOL + D_SSM:D_PROJ] = wx_s[...]
        owin_ref[:, D_PROJ:D_PROJ + LANES] = wdt_s[...]

    @pl.when(c == 0)
    def _():
        ubuf[:, :, 0:POOL_HIST, :] = jnp.zeros((nseq, n_pool_slabs, POOL_HIST, LANES), F32)
        cbuf[:, :, 0:CONV_HIST, :] = jnp.zeros((nseq, n_conv_slabs, CONV_HIST, LANES), F32)
        abuf[:, :, 0:SCAN_PAD, :] = jnp.zeros((nseq, n_chunks, SCAN_PAD, LANES), F32)
        ht_ref[...] = jnp.zeros_like(ht_ref)

    lo_half = lax.broadcasted_iota(jnp.int32, (tc, LANES), 1) < POOL_GW
    pos1 = c * tc + lax.broadcasted_iota(jnp.int32, (tc, LANES), 0) + 1
    a_neg = -jnp.exp(alog_ref[...])
    row = lax.broadcasted_iota(jnp.int32, (CHUNK, CHUNK), 0)
    col = lax.broadcasted_iota(jnp.int32, (CHUNK, CHUNK), 1)
    causal = row >= col
    lo = lax.broadcasted_iota(jnp.int32, (CHUNK, LANES), 1) < SSM_HEADDIM
    mask_lo = jnp.where(lo, 1.0, 0.0).astype(BF16)
    mask_hi = jnp.where(lo, 0.0, 1.0).astype(BF16)
    gw = SSM_HPG * SSM_HEADDIM
    n_x = D_SSM // LANES
    st8 = {}

    def head(i):
        x = x_ref[i]
        xn = _rms(x, g_ref[...]).astype(BF16)
        st8[i] = dict(x=x, xn=xn, xbc=[None] * n_conv_slabs)

    def conv_block(i, s0, ns):
        xn = st8[i]["xn"]
        raw = _dot(xn, wx_s[:, s0 * LANES:(s0 + ns) * LANES])
        for t in range(ns):
            s = s0 + t
            cbuf[i, s, CONV_HIST:CONV_HIST + tc, :] = raw[:, slab(t)]
            acc = cbuf[i, s, CONV_HIST:CONV_HIST + tc, :] * cw_ref[CONV_W - 1:CONV_W, slab(s)]
            for j in range(1, CONV_W):
                acc = acc + (cbuf[i, s, CONV_HIST - j:CONV_HIST - j + tc, :]
                             * cw_ref[CONV_W - 1 - j:CONV_W - j, slab(s)])
            st8[i]["xbc"][s] = _silu(acc + cb_ref[:, slab(s)])

    def dt_proj(i):
        st8[i]["dt"] = jax.nn.softplus(_dot(st8[i]["xn"], wdt_s[...]) + dtb_ref[...])

    def pool(i):
        u = _dot(st8[i]["xn"], wu_s[...])
        for s in range(n_pool_slabs):
            ubuf[i, s, POOL_HIST:rp, :] = u[:, slab(s)]
            s2buf[i, s, 8:rp, :] = ubuf[i, s, 8:rp, :] + ubuf[i, s, 7:rp - 1, :]
        s4buf[i, 16:rp, :] = s2buf[i, 1, 16:rp, :] + s2buf[i, 1, 14:rp - 2, :]
        s8buf[i, 24:rp, :] = s4buf[i, 24:rp, :] + s4buf[i, 20:rp - 4, :]
        win_sums = [
            jnp.where(lo_half, s2buf[i, 0, POOL_HIST:rp, :],
                      s2buf[i, 0, POOL_HIST:rp, :] + s2buf[i, 0, POOL_HIST - 2:rp - 2, :]),
            jnp.where(lo_half, s8buf[i, POOL_HIST:rp, :],
                      s8buf[i, POOL_HIST:rp, :] + s8buf[i, POOL_HIST - 8:rp - 8, :]),
        ]
        ps = []
        for s in range(n_pool_slabs):
            win = jnp.where(lo_half, POOL_WINDOWS[2 * s], POOL_WINDOWS[2 * s + 1])
            cnt = jnp.minimum(pos1, win).astype(F32)
            ps.append(win_sums[s] / cnt - u[:, slab(s)])
        st8[i]["pool_out"] = _dot(jnp.concatenate(ps, axis=1).astype(BF16), wp_ref[...]) * ps_ref[...]

    def ssd_chunk(i, j):
        xbc = st8[i]["xbc"]
        sl = slice(j * CHUNK, (j + 1) * CHUNK)
        dt_c = st8[i]["dt"][sl]
        acs = dt_c * a_neg
        sh = 1
        while sh < CHUNK:
            abuf[i, j, SCAN_PAD:SCAN_PAD + CHUNK, :] = acs
            acs = acs + abuf[i, j, SCAN_PAD - sh:SCAN_PAD - sh + CHUNK, :]
            sh *= 2
        acs_last = acs[CHUNK - 1:CHUNK, :]
        fdec = jnp.exp(acs_last - acs) * dt_c
        src_t = (acs - jnp.log(dt_c)).T
        ht = ht_ref[i]
        ht_b = ht.astype(BF16)
        cdec, st = [], []
        for g in range(SSM_GROUPS):
            c_g = xbc[n_x + SSM_GROUPS + g][sl].astype(BF16)
            b_g = xbc[n_x + g][sl].astype(BF16)
            scores = _dot_nt(c_g, b_g)
            y_off = _dot(c_g, ht_b[:, g * gw:(g + 1) * gw])
            xw = []
            for q in range(SSM_HPG // 2):
                k0 = g * SSM_HPG + 2 * q
                pair = k0 // 2
                a_cols = [_col_bcast(acs, k, CHUNK) for k in (k0, k0 + 1)]
                lhs = [(scores * jnp.exp(jnp.where(causal, a_cols[h] - src_t[k0 + h:k0 + h + 1, :], -jnp.inf))
                        ).astype(BF16) for h in range(2)]
                x_pair = xbc[pair][sl]
                x_b = x_pair.astype(BF16)
                rhs = jnp.concatenate([x_b * mask_lo, x_b * mask_hi], axis=0)
                e_pair = jnp.exp(jnp.where(lo, a_cols[0], a_cols[1]))
                ybuf[i, sl, slab(pair)] = (_dot(jnp.concatenate(lhs, axis=1), rhs)
                                           + y_off[:, slab(q)] * e_pair + x_pair * dsk_ref[:, slab(pair)])
                f_pair = jnp.where(lo, _col_bcast(fdec, k0, LANES), _col_bcast(fdec, k0 + 1, LANES))
                xw.append((x_pair * f_pair).astype(BF16))
                cdec.append(e_pair[CHUNK - 1:CHUNK, :])
            st.append(_dot_tn(b_g, jnp.concatenate(xw, axis=1)))
        ht_ref[i] = ht * jnp.concatenate(cdec, axis=1) + jnp.concatenate(st, axis=1)

    def z_proj(i):
        st8[i]["zs"] = _silu(_dot(st8[i]["xn"], wz_s[...]))

    def tail(i):
        d = st8[i]
        yn = _rms(ybuf[i] * d["zs"], sn_ref[...])
        cat = jnp.concatenate([d["pool_out"], yn], axis=-1).astype(BF16)
        o_ref[i] = d["x"] + _dot(cat, wo_s[...])

    def mid(i):
        conv_block(i, n_x, 2 * SSM_GROUPS)
        dt_proj(i)
        for s0 in range(0, n_x, 2):
            conv_block(i, s0, 2)

    for i in seqs:
        head(i)
    for i in seqs:
        mid(i)
    for i in seqs:
        pool(i)
    for j in range(n_chunks):
        for i in seqs:
            ssd_chunk(i, j)
        if j == 0:
            for i in seqs:
                z_proj(i)
    for i in seqs:
        tail(i)

    ubuf[:, :, 0:POOL_HIST, :] = ubuf[:, :, tc:tc + POOL_HIST, :]
    cbuf[:, :, 0:CONV_HIST, :] = cbuf[:, :, tc:tc + CONV_HIST, :]

    @pl.when(c == nc - 1)
    def _():
        for i in seqs:
            for s in range(n_pool_slabs):
                opool_ref[i, :, slab(s)] = ubuf[i, s, POOL_HIST - POOL_BUF:POOL_HIST, :]
            for s in range(n_conv_slabs):
                oconv_ref[i, :, slab(s)] = cbuf[i, s, CONV_HIST - (CONV_W - 1):CONV_HIST, :]
            ossm_ref[i] = ht_ref[i].T.reshape(SSM_HEADS, SSM_HEADDIM, SSM_STATE)


def _mix_prompt(x, w, *, tc, nseq):
    b, s, d = x.shape
    assert s % tc == 0 and tc % CHUNK == 0 and b % nseq == 0
    consts = [w["g_mix"], w["w_in"], w["conv_w"], w["conv_b"],
              w["dt_bias"], w["a_log"], w["d_skip"], w["ssm_norm"], w["w_pool"], w["pool_scale"],
              w["w_out"]]
    tile = pl.BlockSpec((nseq, tc, d), lambda i, j: (i, j, 0))
    return pl.pallas_call(
        functools.partial(_mix_prompt_kernel, tc=tc, nseq=nseq),
        grid=(b // nseq, s // tc),
        in_specs=[tile] + [_const_spec(a.shape) for a in consts],
        out_specs=[
            tile,
            pl.BlockSpec((nseq, POOL_BUF, D_POOL), lambda i, j: (i, 0, 0)),
            pl.BlockSpec((nseq, CONV_W - 1, CONV_DIM), lambda i, j: (i, 0, 0)),
            pl.BlockSpec((nseq, SSM_HEADS, SSM_HEADDIM, SSM_STATE), lambda i, j: (i, 0, 0, 0)),
            pl.BlockSpec((d, D_PROJ + LANES), lambda i, j: (0, 0)),
        ],
        out_shape=[
            jax.ShapeDtypeStruct((b, s, d), F32),
            jax.ShapeDtypeStruct((b, POOL_BUF, D_POOL), F32),
            jax.ShapeDtypeStruct((b, CONV_W - 1, CONV_DIM), F32),
            jax.ShapeDtypeStruct((b, SSM_HEADS, SSM_HEADDIM, SSM_STATE), F32),
            jax.ShapeDtypeStruct((d, D_PROJ + LANES), BF16),
        ],
        scratch_shapes=[
            pltpu.VMEM((d, D_POOL), BF16), pltpu.VMEM((d, D_SSM), BF16),
            pltpu.VMEM((d, CONV_DIM), BF16), pltpu.VMEM((d, LANES), BF16), pltpu.VMEM((d, d), BF16),
            pltpu.VMEM((nseq, D_POOL // LANES, POOL_HIST + tc, LANES), F32),
            pltpu.VMEM((nseq, D_POOL // LANES, POOL_HIST + tc, LANES), F32),
            pltpu.VMEM((nseq, POOL_HIST + tc, LANES), F32),
            pltpu.VMEM((nseq, POOL_HIST + tc, LANES), F32),
            pltpu.VMEM((nseq, CONV_DIM // LANES, CONV_HIST + tc, LANES), F32),
            pltpu.VMEM((nseq, tc // CHUNK, SCAN_PAD + CHUNK, LANES), F32),
            pltpu.VMEM((nseq, SSM_STATE, D_SSM), F32),
            pltpu.VMEM((nseq, tc, D_SSM), F32),
        ],
        compiler_params=_cparams(("arbitrary", "arbitrary")),
        name="mix_prompt",
    )(x, *consts)


def _attn_kernel(x_ref, g_ref, wq_ref, k_ref, v_ref, wo_ref, qs_ref, ks_hbm, vs_hbm,
                 o_ref, os_ref, owo_ref, wq_s, wo_s, kbuf, vbuf, ksem, vsem, *, bb):
    step = pl.program_id(0) * pl.num_programs(1) + pl.program_id(1)
    n_steps = pl.num_programs(0) * pl.num_programs(1)

    def cache_copies(blk, slot):
        return (pltpu.make_async_copy(ks_hbm.at[pl.ds(blk * bb, bb)], kbuf.at[slot], ksem.at[slot]),
                pltpu.make_async_copy(vs_hbm.at[pl.ds(blk * bb, bb)], vbuf.at[slot], vsem.at[slot]))

    @pl.when(step == 0)
    def _():
        wq_s[...] = wq_ref[...].astype(BF16)
        wo_s[...] = wo_ref[...].astype(BF16)
        owo_ref[...] = wo_s[...]
        for c in cache_copies(0, 0):
            c.start()

    @pl.when(jnp.logical_and(step == 0, n_steps > 1))
    def _():
        for c in cache_copies(1, 1):
            c.start()

    @pl.when(step + 2 < n_steps)
    def _():
        for c in cache_copies(step + 2, (step + 2) % CACHE_SLOTS):
            c.start()

    slot = step % CACHE_SLOTS
    for c in cache_copies(step, slot):
        c.wait()
    ks_ref = kbuf.at[slot]
    vs_ref = vbuf.at[slot]

    hs = [slice(h * MEM_HD, (h + 1) * MEM_HD) for h in range(MEM_HEADS)]
    scale = MEM_HD ** -0.5
    nq = qs_ref.shape[1]
    rows = N_MEM * MEM_HEADS

    def softmax(s):
        e = jnp.exp(s - jnp.max(s, axis=-1, keepdims=True))
        return (e / jnp.sum(e, axis=-1, keepdims=True)).astype(BF16)

    q_head = lax.broadcasted_iota(jnp.int32, (nq, rows), 0) % MEM_HEADS
    kv_head = lax.broadcasted_iota(jnp.int32, (nq, rows), 1) % MEM_HEADS
    same_head = q_head == kv_head

    q_all = qs_ref[...].reshape(bb * nq, MEM_HD)
    q_all = jnp.concatenate([q_all, jnp.zeros((LANES - bb * nq, MEM_HD), F32)], axis=0).astype(BF16)
    s_s = [_dot_nt(ks_ref[i].reshape(rows, MEM_HD).astype(BF16), q_all) for i in range(bb)]

    x = x_ref[...]
    q = _dot(_rms(x, g_ref[...]).astype(BF16), wq_s[...]).astype(BF16)

    p_s = [softmax(jnp.where(same_head, s_s[i].T[i * nq:(i + 1) * nq] * scale, -jnp.inf)) for i in range(bb)]

    s_p = [_dot_nt(q[:, hs[h]], k_ref[:, hs[h]]) * scale for h in range(MEM_HEADS)]

    for i in range(bb):
        os_ref[i] = _dot(p_s[i], vs_ref[i].reshape(rows, MEM_HD).astype(BF16))

    p_p = [softmax(s) for s in s_p]
    o = jnp.concatenate([_dot(p_p[h], v_ref[:, hs[h]]) for h in range(MEM_HEADS)], axis=-1).astype(BF16)
    o_ref[...] = x + _dot(o, wo_s[...])


def _attn(x, k, v, g, wq, wo, qs, ks, vs, *, tq):
    b, s, d = x.shape
    nb, nq, hd = qs.shape
    nj = s // tq
    assert nb % (b * nj) == 0
    bb = nb // (b * nj)
    assert bb * nq <= LANES
    tile = pl.BlockSpec((None, tq, d), lambda i, j: (i, j, 0))
    mem = pl.BlockSpec((None, N_MEM, d), lambda i, j: (i, 0, 0))
    s_q = pl.BlockSpec((bb, nq, hd), lambda i, j: (i * nj + j, 0, 0))
    return pl.pallas_call(
        functools.partial(_attn_kernel, bb=bb),
        grid=(b, nj),
        in_specs=[tile, _const_spec(g.shape), _const_spec(wq.shape), mem, mem, _const_spec(wo.shape),
                  s_q, pl.BlockSpec(memory_space=pl.ANY), pl.BlockSpec(memory_space=pl.ANY)],
        out_specs=[tile, s_q, pl.BlockSpec(wo.shape, lambda i, j: (0, 0))],
        out_shape=[jax.ShapeDtypeStruct((b, s, d), F32), jax.ShapeDtypeStruct((nb, nq, hd), F32),
                   jax.ShapeDtypeStruct(wo.shape, BF16)],
        scratch_shapes=[pltpu.VMEM(wq.shape, BF16), pltpu.VMEM(wo.shape, BF16),
                        pltpu.VMEM((CACHE_SLOTS, bb, N_MEM, MEM_HEADS, hd), F32),
                        pltpu.VMEM((CACHE_SLOTS, bb, N_MEM, MEM_HEADS, hd), F32),
                        pltpu.SemaphoreType.DMA((CACHE_SLOTS,)), pltpu.SemaphoreType.DMA((CACHE_SLOTS,))],
        compiler_params=_cparams(("arbitrary", "arbitrary")),
        name="attn",
    )(x, g, wq, k, v, wo, qs, ks, vs)


def _mix_sample_kernel(x_ref, sp_ref, sc_ref, h0_ref, g_ref, win_ref, cw_ref, cb_ref, dtb_ref, alog_ref,
                       dsk_ref, wp_ref, ps_ref, ex_ref, sn_ref, wo_ref, gq_ref, wq_ref,
                       x2_ref, q_ref, npool_ref, nconv_ref, hout_ref,
                       pool_s, z_s, ydx_s, expa_s, xw_s, b_s, c_s, dec_s, y_s,
                       hbuf, hsem, wo_f, wq_f, wsem, *, nb, t, bb, start_pos):
    j = pl.program_id(0)
    n_blocks = pl.num_programs(0)
    gw = SSM_HPG * SSM_HEADDIM
    n_bc = SSM_GROUPS * SSM_STATE

    def rows(i):
        return slice(i * nb, (i + 1) * nb)

    def state_copy(blk, slot):
        return pltpu.make_async_copy(h0_ref.at[pl.ds(blk * bb, bb)], hbuf.at[slot], hsem.at[slot])

    def weight_copies():
        return (pltpu.make_async_copy(wo_ref, wo_f, wsem.at[0]),
                pltpu.make_async_copy(wq_ref, wq_f, wsem.at[1]))

    @pl.when(j == 0)
    def _():
        state_copy(0, 0).start()
        for c in weight_copies():
            c.start()

    @pl.when(jnp.logical_and(j == 0, n_blocks > 1))
    def _():
        state_copy(1, 1).start()

    @pl.when(j + 2 < n_blocks)
    def _():
        state_copy(j + 2, (j + 2) % STATE_SLOTS).start()

    @pl.when(j == 0)
    def _():
        x = x_ref[...]
        xn = _rms(x, g_ref[...]).astype(BF16)
        z_s[...] = _dot(xn, win_ref[:, D_POOL:D_POOL + D_SSM])

        u = _dot(xn, win_ref[:, 0:D_POOL])
        ext = [sp_ref[rows(i), :] for i in range(POOL_BUF)] + [u[rows(i)] for i in range(t)]
        ps = []
        for i in range(t):
            sums, cnts = [], []
            for w in POOL_WINDOWS:
                s = ext[POOL_BUF + i]
                for k in range(1, w):
                    s = s + ext[POOL_BUF + i - k]
                sums.append(s)
                cnts.append(jnp.full((nb, D_POOL), float(min(start_pos + i + 1, w)), F32))
            mean = _lane_group_select(sums, POOL_GW) / _lane_group_select(cnts, POOL_GW)
            ps.append(mean - ext[POOL_BUF + i])
        p = jnp.concatenate(ps, axis=0).astype(BF16)
        pool_s[...] = _dot(p, wp_ref[...]) * ps_ref[...]
        for i in range(POOL_BUF):
            npool_ref[rows(i), :] = ext[t + i]

        xr = _dot(xn, win_ref[:, D_POOL + D_SSM:D_PROJ])
        cext = [sc_ref[rows(i), :] for i in range(CONV_W - 1)] + [xr[rows(i)] for i in range(t)]
        xbc = []
        for i in range(t):
            acc = cext[i] * cw_ref[0:1, :]
            for k in range(1, CONV_W):
                acc = acc + cext[i + k] * cw_ref[k:k + 1, :]
            xbc.append(_silu(acc + cb_ref[...]))
        for i in range(CONV_W - 1):
            nconv_ref[rows(i), :] = cext[t + i]
        xs = [v[:, :D_SSM] for v in xbc]
        bm = [v[:, D_SSM:D_SSM + n_bc] for v in xbc]
        cm = [v[:, D_SSM + n_bc:] for v in xbc]
        for i in range(t):
            b_s[rows(i), :] = bm[i]
            c_s[rows(i), :] = cm[i]

        dt_all = jax.nn.softplus(_dot(xn, win_ref[:, D_PROJ:D_PROJ + LANES]) + dtb_ref[...])
        a_neg = -jnp.exp(alog_ref[...])
        dt = [dt_all[rows(i)] for i in range(t)]
        acs = []
        for i in range(t):
            da = dt[i] * a_neg
            acs.append(da if i == 0 else acs[-1] + da)
        dec = jnp.exp(acs[-1])
        for k in range(SSM_HEADS):
            dec_s[k] = _col_bcast(dec, k, LANES)

        def expand(v, exact):
            if exact:
                return jnp.dot(v, ex_ref[...], precision=lax.Precision.HIGHEST,
                               preferred_element_type=F32)
            return _dot(v.astype(BF16), ex_ref[...].astype(BF16))

        lane = lax.broadcasted_iota(jnp.int32, (nb, LANES), 1)
        for i in range(t):
            ydx = xs[i] * dsk_ref[...]
            for s in range(i + 1):
                sc = [jnp.sum(cm[i][:, g * SSM_STATE:(g + 1) * SSM_STATE]
                              * bm[s][:, g * SSM_STATE:(g + 1) * SSM_STATE], axis=-1, keepdims=True)
                      for g in range(SSM_GROUPS)]
                sc = jnp.where(lane < SSM_HPG, sc[0], sc[1])
                coef = sc * jnp.exp(acs[i] - acs[s]) * dt[s]
                ydx = ydx + expand(coef, False) * xs[s]
            ydx_s[rows(i), :] = ydx
            expa_s[rows(i), :] = expand(jnp.exp(acs[i]), True)
            xw_s[rows(i), :] = xs[i] * expand(jnp.exp(acs[-1] - acs[i]) * dt[i], True)

    base = pl.multiple_of(j * bb, bb)
    blk_rows = [pl.ds(i * nb + base, bb) for i in range(t)]

    def gather(ref):
        return jnp.concatenate([ref[r, :] for r in blk_rows], axis=0)

    def pad_rows(v):
        return jnp.concatenate([v, jnp.zeros((LANES - t * bb, v.shape[1]), v.dtype)], axis=0)

    c_blk = pad_rows(gather(c_s))
    b_blk = pad_rows(gather(b_s))
    xw_t = pad_rows(gather(xw_s)).T.astype(BF16)
    row_seq = lax.broadcasted_iota(jnp.int32, (LANES, SSM_STATE), 0) % bb
    col_seq = lax.broadcasted_iota(jnp.int32, (SSM_STATE, LANES), 1) % bb
    slot = j % STATE_SLOTS
    state_copy(j, slot).wait()
    h_blk = hbuf.at[slot]
    y_t = []
    for g in range(SSM_GROUPS):
        gs = slice(g * SSM_STATE, (g + 1) * SSM_STATE)
        c_t = c_blk[:, gs].T
        acc = jnp.zeros((gw, LANES), F32)
        for i in range(bb):
            h0 = h_blk[i, g * SSM_HPG:(g + 1) * SSM_HPG].reshape(gw, SSM_STATE)
            acc = acc + _dot(h0.astype(BF16), jnp.where(col_seq == i, c_t, 0.0).astype(BF16))
            b_i = jnp.where(row_seq == i, b_blk[:, gs], 0.0).astype(BF16)
            st = _dot(xw_t[g * gw:(g + 1) * gw, :], b_i)
            for k in range(SSM_HPG):
                hd = g * SSM_HPG + k
                hout_ref[i, hd] = (h_blk[i, hd] * dec_s[hd, pl.ds(base + i, 1), :]
                                   + st[k * SSM_HEADDIM:(k + 1) * SSM_HEADDIM])
        y_t.append(acc.T[:t * bb])
    y_blk = gather(ydx_s) + jnp.concatenate(y_t, axis=1) * gather(expa_s)
    for i in range(t):
        y_s[blk_rows[i], :] = y_blk[i * bb:(i + 1) * bb]

    @pl.when(j == n_blocks - 1)
    def _():
        for c in weight_copies():
            c.wait()
        x2 = _gate_norm_out(y_s[...], z_s[...], pool_s[...], x_ref[...], sn_ref[...],
                            wo_f[...].astype(BF16))
        _store_tm_rows(x2_ref, x2)
        q = _dot(_rms(x2, gq_ref[...]).astype(BF16), wq_f[...].astype(BF16))
        for i in range(t):
            for h in range(MEM_HEADS):
                q_ref[:, i * MEM_HEADS + h, :] = q[rows(i), h * MEM_HD:(h + 1) * MEM_HD]


def _mix_sample(x, sp, sc, h0, w_in_bf, w, *, nb, t, bb, start_pos):
    n, d = x.shape
    assert n == nb * t and nb % bb == 0 and t * bb <= LANES
    consts = [w["g_mix"], w_in_bf, w["conv_w"], w["conv_b"], w["dt_bias"], w["a_log"], w["d_skip"],
              w["w_pool"], w["pool_scale"], w["expand"], w["ssm_norm"], w["w_out"], w["g_xq"], w["w_xq"]]
    in_hbm = pl.BlockSpec(memory_space=pl.ANY)
    const_specs = [in_hbm if a is w["w_out"] or a is w["w_xq"] else _const_spec(a.shape) for a in consts]
    hblk = pl.BlockSpec((bb, SSM_HEADS, SSM_HEADDIM, SSM_STATE), lambda i: (i, 0, 0, 0))
    whole = _const_spec
    outs = [
        jax.ShapeDtypeStruct((nb, t, d), F32),
        jax.ShapeDtypeStruct((nb, t * MEM_HEADS, MEM_HD), F32),
        jax.ShapeDtypeStruct((POOL_BUF * nb, D_POOL), F32),
        jax.ShapeDtypeStruct(((CONV_W - 1) * nb, CONV_DIM), F32),
        jax.ShapeDtypeStruct(h0.shape, F32),
    ]
    n_bc = SSM_GROUPS * SSM_STATE
    return pl.pallas_call(
        functools.partial(_mix_sample_kernel, nb=nb, t=t, bb=bb, start_pos=start_pos),
        grid=(nb // bb,),
        in_specs=[_const_spec(x.shape), _const_spec(sp.shape), _const_spec(sc.shape), in_hbm] + const_specs,
        out_specs=[whole(o.shape) for o in outs[:4]] + [hblk],
        out_shape=outs,
        scratch_shapes=[
            pltpu.VMEM((n, D_POOL), F32),
            pltpu.VMEM((n, D_SSM), F32),
            pltpu.VMEM((n, D_SSM), F32),
            pltpu.VMEM((n, D_SSM), F32),
            pltpu.VMEM((n, D_SSM), F32),
            pltpu.VMEM((n, n_bc), F32),
            pltpu.VMEM((n, n_bc), F32),
            pltpu.VMEM((SSM_HEADS, nb, LANES), F32),
            pltpu.VMEM((n, D_SSM), F32),
            pltpu.VMEM((STATE_SLOTS, bb, SSM_HEADS, SSM_HEADDIM, SSM_STATE), F32),
            pltpu.SemaphoreType.DMA((STATE_SLOTS,)),
            pltpu.VMEM(w["w_out"].shape, F32), pltpu.VMEM(w["w_xq"].shape, F32),
            pltpu.SemaphoreType.DMA((2,)),
        ],
        compiler_params=_cparams(("arbitrary",), VMEM_LIMIT_FFN),
        name="mix_sample",
    )(x, sp, sc, h0, *consts)


def _prep_weights(g_ffn1, w1_gate, w1_up, w1_down, g_mix, w_in, conv_w, conv_b, dt_bias, a_log,
                  d_skip, ssm_norm, w_pool, pool_scale, w_out, g_mem, w_mem_k, w_mem_v, g_xq,
                  w_xq, w_xo, g_ffn2, w2_gate, w2_up, w2_down, g_final):
    row = lambda v: v.reshape(1, -1).astype(F32)
    pad_heads = lambda v: jnp.pad(v.astype(F32), (0, LANES - SSM_HEADS)).reshape(1, LANES)
    w_pool_bd = jnp.zeros((D_POOL, D_POOL), F32)
    for g in range(len(POOL_WINDOWS)):
        w_pool_bd = w_pool_bd.at[g * POOL_GW:(g + 1) * POOL_GW, g * POOL_GW:(g + 1) * POOL_GW].set(w_pool[g])
    head_of_chan = jnp.arange(D_SSM) // SSM_HEADDIM
    expand = (jnp.arange(LANES)[:, None] == head_of_chan[None, :]).astype(F32)
    return dict(
        g_ffn1=row(g_ffn1), w1_gate=w1_gate, w1_up=w1_up, w1_down=w1_down,
        g_mix=row(g_mix), w_in=jnp.transpose(w_in),
        conv_w=conv_w.astype(F32), conv_b=row(conv_b), dt_bias=pad_heads(dt_bias),
        a_log=pad_heads(a_log), d_skip=row(jnp.repeat(d_skip, SSM_HEADDIM)), ssm_norm=row(ssm_norm),
        w_pool=w_pool_bd.astype(BF16), pool_scale=row(pool_scale), w_out=w_out,
        g_mem=row(g_mem), w_mem_k=w_mem_k, w_mem_v=w_mem_v,
        g_xq=row(g_xq), w_xq=w_xq, w_xo=w_xo,
        g_ffn2=row(g_ffn2), w2_gate=w2_gate, w2_up=w2_up, w2_down=w2_down,
        g_final=row(g_final), expand=expand,
    )


def _layer(x_prompt, x_sample, mem_prompt, mem_k, mem_v, state_pool, state_conv, state_ssm, start_pos, w):
    b, s, d = x_prompt.shape
    nb, t, _ = x_sample.shape
    tmaj = lambda a: jnp.swapaxes(a, 0, 1).reshape(-1, a.shape[-1])
    bmaj = lambda a, r: jnp.swapaxes(a.reshape(r, nb, -1), 0, 1)

    k, v, k_b, v_b = _memkv(mem_prompt, w["g_mem"], w["w_mem_k"], w["w_mem_v"])
    xp, xs = _ffn(x_prompt.reshape(b * s, d), x_sample, w["g_ffn1"], w["w1_gate"], w["w1_up"],
                  w["w1_down"], w["g_final"], final=False, tm=FFN_ROWS)

    xp, pool_p, conv_p, ssm_p, w_in_bf = _mix_prompt(xp.reshape(b, s, d), w, tc=MIX_ROWS, nseq=MIX_SEQS)

    x2, q, pool_s, conv_s, ssm_s = _mix_sample(xs, tmaj(state_pool), tmaj(state_conv), state_ssm, w_in_bf, w,
                                               nb=nb, t=t, bb=STATE_BLOCK, start_pos=start_pos)

    xp, o, w_xo_bf = _attn(xp, k_b, v_b, w["g_xq"], w["w_xq"], w["w_xo"], q, mem_k, mem_v, tq=ATTN_ROWS)

    yp, ys = _ffn(xp.reshape(b * s, d), x2, w["g_ffn2"], w["w2_gate"], w["w2_up"], w["w2_down"],
                  w["g_final"], final=True, tm=FFN_ROWS, attn_out=(o, w_xo_bf))
    return (yp.reshape(b, s, d), ys, k, v, pool_p, conv_p, ssm_p,
            bmaj(pool_s, POOL_BUF), bmaj(conv_s, CONV_W - 1), ssm_s)


def kernel(x_prompt, x_sample, mem_prompt, cache_mem_k, cache_mem_v, state_pool, state_conv, state_ssm,
           g_ffn1, w1_gate, w1_up, w1_down, g_mix, w_in, conv_w, conv_b, dt_bias, a_log, d_skip,
           ssm_norm, w_pool, pool_scale, w_out, g_mem, w_mem_k, w_mem_v, g_xq, w_xq, w_xo,
           g_ffn2, w2_gate, w2_up, w2_down, g_final):
    assert g_ffn1.shape[0] == 1, "single-layer model"
    w = _prep_weights(g_ffn1[0], w1_gate[0], w1_up[0], w1_down[0], g_mix[0], w_in[0], conv_w[0],
                      conv_b[0], dt_bias[0], a_log[0], d_skip[0], ssm_norm[0], w_pool[0],
                      pool_scale[0], w_out[0], g_mem[0], w_mem_k[0], w_mem_v[0], g_xq[0], w_xq[0],
                      w_xo[0], g_ffn2[0], w2_gate[0], w2_up[0], w2_down[0], g_final)
    bp = x_prompt.shape[0]
    y_p, y_s, k_p, v_p, pool_p, conv_p, ssm_p, pool_s, conv_s, ssm_s = _layer(
        x_prompt, x_sample, mem_prompt, cache_mem_k[0], cache_mem_v[0], state_pool[0], state_conv[0],
        state_ssm[0], PAST_LEN, w)
    kv_shape = (1, bp, N_MEM, MEM_HEADS, MEM_HD)
    return (y_p, y_s, k_p.reshape(kv_shape), v_p.reshape(kv_shape), pool_p[None], conv_p[None],
            ssm_p[None], pool_s[None], conv_s[None], ssm_s[None])
```

```python
import functools

import jax
import jax.numpy as jnp
from jax import lax
from jax.experimental import pallas as pl
from jax.experimental.pallas import tpu as pltpu

F32 = jnp.float32
BF16 = jnp.bfloat16

D_POOL = 256
POOL_WINDOWS = (2, 4, 8, 16)
POOL_GW = 64
POOL_BUF = 15
D_SSM = 768
SSM_HEADDIM = 64
SSM_HEADS = 12
SSM_GROUPS = 2
SSM_HPG = 6
SSM_STATE = 128
CONV_W = 4
CONV_DIM = 1280
D_PROJ = D_POOL + D_SSM + CONV_DIM
N_MEM = 256
MEM_HEADS = 4
MEM_HD = 256
PAST_LEN = 16384
EPS = 1e-6
LANES = 128
CHUNK = 128
POOL_HIST = 32
CONV_HIST = 8
SCAN_PAD = CHUNK // 2

VMEM_LIMIT = 56 * 1024 * 1024
VMEM_LIMIT_FFN = 60 * 1024 * 1024
FFN_ROWS = 1024
FFN_W_STEPS = 16
MIX_ROWS = 256
MIX_SEQS = 2
ATTN_ROWS = 512
STATE_BLOCK = 8
STATE_SLOTS = 3
CACHE_SLOTS = 3


def _cparams(sem, vmem_limit=VMEM_LIMIT):
    return pltpu.CompilerParams(dimension_semantics=sem, vmem_limit_bytes=vmem_limit)


def _const_spec(shape):
    nd = len(shape)
    return pl.BlockSpec(shape, lambda *_: (0,) * nd, pipeline_mode=pl.Buffered(1))


def _tm_rows(ref):
    return jnp.concatenate([ref[:, i, :] for i in range(ref.shape[1])], axis=0)


def _store_tm_rows(ref, rows):
    nb = ref.shape[0]
    for i in range(ref.shape[1]):
        ref[:, i, :] = rows[i * nb:(i + 1) * nb]


def _rms(x, g):
    ms = jnp.mean(x * x, axis=-1, keepdims=True)
    return x * lax.rsqrt(ms + EPS) * g


def _silu(x):
    return x * jax.nn.sigmoid(x)


def _dot(a, b):
    return jnp.dot(a, b, preferred_element_type=F32)


def _dot_nt(a, b):
    return lax.dot_general(a, b, (((1,), (1,)), ((), ())), preferred_element_type=F32)


def _dot_tn(a, b):
    return lax.dot_general(a, b, (((0,), (0,)), ((), ())), preferred_element_type=F32)


def _ffn_kernel(xp_ref, xs_ref, g_ref, wg_ref, wu_ref, wd_ref, gf_ref, *rest, final, n_p, attn_out):
    if attn_out:
        ao_ref, wxo_ref, op_ref, os_ref, wg_s, wu_s, wd_s = rest
    else:
        op_ref, os_ref, wg_s, wu_s, wd_s = rest
    i = pl.program_id(0)
    rows_gu = wg_ref.shape[0]
    rows_d = wd_ref.shape[0]

    @pl.when(i < FFN_W_STEPS)
    def _():
        r = pl.multiple_of(i * rows_gu, rows_gu)
        wg_s[pl.ds(r, rows_gu), :] = wg_ref[...].astype(BF16)
        wu_s[pl.ds(r, rows_gu), :] = wu_ref[...].astype(BF16)
        r = pl.multiple_of(i * rows_d, rows_d)
        wd_s[pl.ds(r, rows_d), :] = wd_ref[...].astype(BF16)

    def tile(x):
        xn = _rms(x, g_ref[...]).astype(BF16)
        gate = _dot(xn, wg_s[...])
        up = _dot(xn, wu_s[...])
        h = (_silu(gate) * up).astype(BF16)
        out = x + 0.5 * _dot(h, wd_s[...])
        return _rms(out, gf_ref[...]) if final else out

    @pl.when(jnp.logical_and(i >= FFN_W_STEPS, i < FFN_W_STEPS + n_p))
    def _():
        op_ref[...] = tile(xp_ref[...])

    @pl.when(i == FFN_W_STEPS + n_p)
    def _():
        x = _tm_rows(xs_ref)
        if attn_out:
            nb, nq, hd = ao_ref.shape
            heads = wxo_ref.shape[0] // hd
            o = jnp.concatenate(
                [jnp.concatenate([ao_ref[:, t * heads + h, :] for h in range(heads)], axis=1)
                 for t in range(nq // heads)], axis=0)
            x = x + _dot(o.astype(BF16), wxo_ref[...])
        out = tile(x)
        if os_ref.ndim == 3:
            _store_tm_rows(os_ref, out)
        else:
            os_ref[...] = out


def _ffn(xp, xs, g, wg, wu, wd, gf, *, final, tm, attn_out=None):
    tp, d = xp.shape
    nb, t, _ = xs.shape
    os_shape = xs.shape if final else (nb * t, d)
    dff = wg.shape[1]
    assert tp % tm == 0 and d % FFN_W_STEPS == 0 and dff % FFN_W_STEPS == 0
    n_p = tp // tm
    p_tile = pl.BlockSpec((tm, d), lambda i: (jnp.clip(i - FFN_W_STEPS, 0, n_p - 1), 0))
    s_tile = pl.BlockSpec(os_shape, lambda i: (0,) * len(os_shape))
    w_chunk = lambda rows, cols: pl.BlockSpec((rows, cols), lambda i: (jnp.minimum(i, FFN_W_STEPS - 1), 0))
    extra = list(attn_out) if attn_out else []
    return pl.pallas_call(
        functools.partial(_ffn_kernel, final=final, n_p=n_p, attn_out=bool(attn_out)),
        grid=(FFN_W_STEPS + n_p + 1,),
        in_specs=[
            p_tile, _const_spec(xs.shape), _const_spec(g.shape),
            w_chunk(d // FFN_W_STEPS, dff), w_chunk(d // FFN_W_STEPS, dff), w_chunk(dff // FFN_W_STEPS, d),
            _const_spec(gf.shape),
        ] + [_const_spec(a.shape) for a in extra],
        out_specs=[p_tile, s_tile],
        out_shape=[jax.ShapeDtypeStruct(xp.shape, F32), jax.ShapeDtypeStruct(os_shape, F32)],
        scratch_shapes=[pltpu.VMEM((d, dff), BF16), pltpu.VMEM((d, dff), BF16), pltpu.VMEM((dff, d), BF16)],
        compiler_params=_cparams(("arbitrary",), VMEM_LIMIT_FFN),
        name="ffn_final" if final else "ffn",
    )(xp, xs, g, wg, wu, wd, gf, *extra)


def _memkv_kernel(m_ref, g_ref, wk_ref, wv_ref, k_ref, v_ref, kb_ref, vb_ref, wk_s, wv_s):
    @pl.when(pl.program_id(0) == 0)
    def _():
        wk_s[...] = wk_ref[...].astype(BF16)
        wv_s[...] = wv_ref[...].astype(BF16)

    mn = _rms(m_ref[...], g_ref[...]).astype(BF16)
    k = _dot(mn, wk_s[...])
    v = _dot(mn, wv_s[...])
    for h in range(MEM_HEADS):
        k_ref[:, h, :] = k[:, h * MEM_HD:(h + 1) * MEM_HD]
        v_ref[:, h, :] = v[:, h * MEM_HD:(h + 1) * MEM_HD]
    kb_ref[...] = k.astype(BF16)
    vb_ref[...] = v.astype(BF16)


def _memkv(mem, g, wk, wv):
    b, n_mem, d = mem.shape
    row_blk = pl.BlockSpec((None, n_mem, d), lambda i: (i, 0, 0))
    head_blk = pl.BlockSpec((None, n_mem, MEM_HEADS, MEM_HD), lambda i: (i, 0, 0, 0))
    return pl.pallas_call(
        _memkv_kernel,
        grid=(b,),
        in_specs=[row_blk, _const_spec(g.shape), _const_spec(wk.shape), _const_spec(wv.shape)],
        out_specs=[head_blk, head_blk, row_blk, row_blk],
        out_shape=[jax.ShapeDtypeStruct((b, n_mem, MEM_HEADS, MEM_HD), F32)] * 2
        + [jax.ShapeDtypeStruct((b, n_mem, d), BF16)] * 2,
        scratch_shapes=[pltpu.VMEM(wk.shape, BF16), pltpu.VMEM(wv.shape, BF16)],
        compiler_params=_cparams(("arbitrary",)),
        name="memkv",
    )(mem, g, wk, wv)


def _lane_group_select(vals, width):
    lane = lax.broadcasted_iota(jnp.int32, vals[0].shape, 1)
    out = vals[-1]
    for g in range(len(vals) - 2, -1, -1):
        out = jnp.where(lane < (g + 1) * width, vals[g], out)
    return out


def _col_bcast(m, k, width):
    return jnp.broadcast_to(m[:, k:k + 1], (m.shape[0], width))


def _gate_norm_out(y, z, pool_out, x_res, ssm_norm, w_out):
    yn = _rms(y * _silu(z), ssm_norm)
    cat = jnp.concatenate([pool_out, yn], axis=-1).astype(BF16)
    return x_res + _dot(cat, w_out)


def _mix_prompt_kernel(x_ref, g_ref, win_ref, cw_ref, cb_ref,
                       dtb_ref, alog_ref, dsk_ref, sn_ref, wp_ref, ps_ref, wout_ref,
                       o_ref, opool_ref, oconv_ref, ossm_ref, owin_ref,
                       wu_s, wz_s, wx_s, wdt_s, wo_s,
                       ubuf, s2buf, s4buf, s8buf, cbuf, abuf, ht_ref, ybuf, *, tc, nseq):
    c = pl.program_id(1)
    nc = pl.num_programs(1)
    n_pool_slabs = D_POOL // LANES
    n_conv_slabs = CONV_DIM // LANES
    n_chunks = tc // CHUNK
    rp = POOL_HIST + tc
    seqs = range(nseq)

    def slab(s):
        return slice(s * LANES, (s + 1) * LANES)

    @pl.when(jnp.logical_and(pl.program_id(0) == 0, c == 0))
    def _():
        wu_s[...] = win_ref[0:D_POOL, :].T.astype(BF16)
        wz_s[...] = win_ref[D_POOL:D_POOL + D_SSM, :].T.astype(BF16)
        wx_s[...] = win_ref[D_POOL + D_SSM:D_PROJ, :].T.astype(BF16)
        lead = 16 - SSM_HEADS
        place = (lax.broadcasted_iota(jnp.int32, (16, LANES), 0) - lead
                 == lax.broadcasted_iota(jnp.int32, (16, LANES), 1))
        wdt_s[...] = _dot_tn(win_ref[D_PROJ - lead:D_PROJ + SSM_HEADS, :].astype(BF16),
                             jnp.where(place, 1.0, 0.0).astype(BF16)).astype(BF16)
        wo_s[...] = wout_ref[...].astype(BF16)
        owin_ref[:, 0:D_POOL] = wu_s[...]
        owin_ref[:, D_POOL:D_POOL + D_SSM] = wz_s[...]
        owin_ref[:, D_POOL + D_SSM:D_PROJ] = wx_s[...]
        owin_ref[:, D_PROJ:D_PROJ + LANES] = wdt_s[...]

    @pl.when(c == 0)
    def _():
        ubuf[:, :, 0:POOL_HIST, :] = jnp.zeros((nseq, n_pool_slabs, POOL_HIST, LANES), F32)
        cbuf[:, :, 0:CONV_HIST, :] = jnp.zeros((nseq, n_conv_slabs, CONV_HIST, LANES), F32)
        abuf[:, :, 0:SCAN_PAD, :] = jnp.zeros((nseq, n_chunks, SCAN_PAD, LANES), F32)
        ht_ref[...] = jnp.zeros_like(ht_ref)

    lo_half = lax.broadcasted_iota(jnp.int32, (tc, LANES), 1) < POOL_GW
    pos1 = c * tc + lax.broadcasted_iota(jnp.int32, (tc, LANES), 0) + 1
    a_neg = -jnp.exp(alog_ref[...])
    row = lax.broadcasted_iota(jnp.int32, (CHUNK, CHUNK), 0)
    col = lax.broadcasted_iota(jnp.int32, (CHUNK, CHUNK), 1)
    causal = row >= col
    lo = lax.broadcasted_iota(jnp.int32, (CHUNK, LANES), 1) < SSM_HEADDIM
    mask_lo = jnp.where(lo, 1.0, 0.0).astype(BF16)
    mask_hi = jnp.where(lo, 0.0, 1.0).astype(BF16)
    gw = SSM_HPG * SSM_HEADDIM
    n_x = D_SSM // LANES
    st8 = {}

    def head(i):
        x = x_ref[i]
        xn = _rms(x, g_ref[...]).astype(BF16)
        st8[i] = dict(x=x, xn=xn, xbc=[None] * n_conv_slabs)

    def conv_block(i, s0, ns):
        xn = st8[i]["xn"]
        raw = _dot(xn, wx_s[:, s0 * LANES:(s0 + ns) * LANES])
        for t in range(ns):
            s = s0 + t
            cbuf[i, s, CONV_HIST:CONV_HIST + tc, :] = raw[:, slab(t)]
            acc = cbuf[i, s, CONV_HIST:CONV_HIST + tc, :] * cw_ref[CONV_W - 1:CONV_W, slab(s)]
            for j in range(1, CONV_W):
                acc = acc + (cbuf[i, s, CONV_HIST - j:CONV_HIST - j + tc, :]
                             * cw_ref[CONV_W - 1 - j:CONV_W - j, slab(s)])
            st8[i]["xbc"][s] = _silu(acc + cb_ref[:, slab(s)])

    def dt_proj(i):
        st8[i]["dt"] = jax.nn.softplus(_dot(st8[i]["xn"], wdt_s[...]) + dtb_ref[...])

    def pool(i):
        u = _dot(st8[i]["xn"], wu_s[...])
        for s in range(n_pool_slabs):
            ubuf[i, s, POOL_HIST:rp, :] = u[:, slab(s)]
            s2buf[i, s, 8:rp, :] = ubuf[i, s, 8:rp, :] + ubuf[i, s, 7:rp - 1, :]
        s4buf[i, 16:rp, :] = s2buf[i, 1, 16:rp, :] + s2buf[i, 1, 14:rp - 2, :]
        s8buf[i, 24:rp, :] = s4buf[i, 24:rp, :] + s4buf[i, 20:rp - 4, :]
        win_sums = [
            jnp.where(lo_half, s2buf[i, 0, POOL_HIST:rp, :],
                      s2buf[i, 0, POOL_HIST:rp, :] + s2buf[i, 0, POOL_HIST - 2:rp - 2, :]),
            jnp.where(lo_half, s8buf[i, POOL_HIST:rp, :],
                      s8buf[i, POOL_HIST:rp, :] + s8buf[i, POOL_HIST - 8:rp - 8, :]),
        ]
        ps = []
        for s in range(n_pool_slabs):
            win = jnp.where(lo_half, POOL_WINDOWS[2 * s], POOL_WINDOWS[2 * s + 1])
            cnt = jnp.minimum(pos1, win).astype(F32)
            ps.append(win_sums[s] / cnt - u[:, slab(s)])
        st8[i]["pool_out"] = _dot(jnp.concatenate(ps, axis=1).astype(BF16), wp_ref[...]) * ps_ref[...]

    def ssd_chunk(i, j):
        xbc = st8[i]["xbc"]
        sl = slice(j * CHUNK, (j + 1) * CHUNK)
        dt_c = st8[i]["dt"][sl]
        acs = dt_c * a_neg
        sh = 1
        while sh < CHUNK:
            abuf[i, j, SCAN_PAD:SCAN_PAD + CHUNK, :] = acs
            acs = acs + abuf[i, j, SCAN_PAD - sh:SCAN_PAD - sh + CHUNK, :]
            sh *= 2
        acs_last = acs[CHUNK - 1:CHUNK, :]
        fdec = jnp.exp(acs_last - acs) * dt_c
        src_t = (acs - jnp.log(dt_c)).T
        ht = ht_ref[i]
        ht_b = ht.astype(BF16)
        cdec, st = [], []
        for g in range(SSM_GROUPS):
            c_g = xbc[n_x + SSM_GROUPS + g][sl].astype(BF16)
            b_g = xbc[n_x + g][sl].astype(BF16)
            scores = _dot_nt(c_g, b_g)
            y_off = _dot(c_g, ht_b[:, g * gw:(g + 1) * gw])
            xw = []
            for q in range(SSM_HPG // 2):
                k0 = g * SSM_HPG + 2 * q
                pair = k0 // 2
                a_cols = [_col_bcast(acs, k, CHUNK) for k in (k0, k0 + 1)]
                lhs = [(scores * jnp.exp(jnp.where(causal, a_cols[h] - src_t[k0 + h:k0 + h + 1, :], -jnp.inf))
                        ).astype(BF16) for h in range(2)]
                x_pair = xbc[pair][sl]
                x_b = x_pair.astype(BF16)
                rhs = jnp.concatenate([x_b * mask_lo, x_b * mask_hi], axis=0)
                e_pair = jnp.exp(jnp.where(lo, a_cols[0], a_cols[1]))
                ybuf[i, sl, slab(pair)] = (_dot(jnp.concatenate(lhs, axis=1), rhs)
                                           + y_off[:, slab(q)] * e_pair + x_pair * dsk_ref[:, slab(pair)])
                f_pair = jnp.where(lo, _col_bcast(fdec, k0, LANES), _col_bcast(fdec, k0 + 1, LANES))
                xw.append((x_pair * f_pair).astype(BF16))
                cdec.append(e_pair[CHUNK - 1:CHUNK, :])
            st.append(_dot_tn(b_g, jnp.concatenate(xw, axis=1)))
        ht_ref[i] = ht * jnp.concatenate(cdec, axis=1) + jnp.concatenate(st, axis=1)

    def z_proj(i):
        st8[i]["zs"] = _silu(_dot(st8[i]["xn"], wz_s[...]))

    def tail(i):
        d = st8[i]
        yn = _rms(ybuf[i] * d["zs"], sn_ref[...])
        cat = jnp.concatenate([d["pool_out"], yn], axis=-1).astype(BF16)
        o_ref[i] = d["x"] + _dot(cat, wo_s[...])

    def mid(i):
        conv_block(i, n_x, 2 * SSM_GROUPS)
        dt_proj(i)
        for s0 in range(0, n_x, 2):
            conv_block(i, s0, 2)

    for i in seqs:
        head(i)
    for i in seqs:
        mid(i)
    for i in seqs:
        pool(i)
    for j in range(n_chunks):
        for i in seqs:
            ssd_chunk(i, j)
        if j == 0:
            for i in seqs:
                z_proj(i)
    for i in seqs:
        tail(i)

    ubuf[:, :, 0:POOL_HIST, :] = ubuf[:, :, tc:tc + POOL_HIST, :]
    cbuf[:, :, 0:CONV_HIST, :] = cbuf[:, :, tc:tc + CONV_HIST, :]

    @pl.when(c == nc - 1)
    def _():
        for i in seqs:
            for s in range(n_pool_slabs):
                opool_ref[i, :, slab(s)] = ubuf[i, s, POOL_HIST - POOL_BUF:POOL_HIST, :]
            for s in range(n_conv_slabs):
                oconv_ref[i, :, slab(s)] = cbuf[i, s, CONV_HIST - (CONV_W - 1):CONV_HIST, :]
            ossm_ref[i] = ht_ref[i].T.reshape(SSM_HEADS, SSM_HEADDIM, SSM_STATE)


def _mix_prompt(x, w, *, tc, nseq):
    b, s, d = x.shape
    assert s % tc == 0 and tc % CHUNK == 0 and b % nseq == 0
    consts = [w["g_mix"], w["w_in"], w["conv_w"], w["conv_b"],
              w["dt_bias"], w["a_log"], w["d_skip"], w["ssm_norm"], w["w_pool"], w["pool_scale"],
              w["w_out"]]
    tile = pl.BlockSpec((nseq, tc, d), lambda i, j: (i, j, 0))
    return pl.pallas_call(
        functools.partial(_mix_prompt_kernel, tc=tc, nseq=nseq),
        grid=(b // nseq, s // tc),
        in_specs=[tile] + [_const_spec(a.shape) for a in consts],
        out_specs=[
            tile,
            pl.BlockSpec((nseq, POOL_BUF, D_POOL), lambda i, j: (i, 0, 0)),
            pl.BlockSpec((nseq, CONV_W - 1, CONV_DIM), lambda i, j: (i, 0, 0)),
            pl.BlockSpec((nseq, SSM_HEADS, SSM_HEADDIM, SSM_STATE), lambda i, j: (i, 0, 0, 0)),
            pl.BlockSpec((d, D_PROJ + LANES), lambda i, j: (0, 0)),
        ],
        out_shape=[
            jax.ShapeDtypeStruct((b, s, d), F32),
            jax.ShapeDtypeStruct((b, POOL_BUF, D_POOL), F32),
            jax.ShapeDtypeStruct((b, CONV_W - 1, CONV_DIM), F32),
            jax.ShapeDtypeStruct((b, SSM_HEADS, SSM_HEADDIM, SSM_STATE), F32),
            jax.ShapeDtypeStruct((d, D_PROJ + LANES), BF16),
        ],
        scratch_shapes=[
            pltpu.VMEM((d, D_POOL), BF16), pltpu.VMEM((d, D_SSM), BF16),
            pltpu.VMEM((d, CONV_DIM), BF16), pltpu.VMEM((d, LANES), BF16), pltpu.VMEM((d, d), BF16),
            pltpu.VMEM((nseq, D_POOL // LANES, POOL_HIST + tc, LANES), F32),
            pltpu.VMEM((nseq, D_POOL // LANES, POOL_HIST + tc, LANES), F32),
            pltpu.VMEM((nseq, POOL_HIST + tc, LANES), F32),
            pltpu.VMEM((nseq, POOL_HIST + tc, LANES), F32),
            pltpu.VMEM((nseq, CONV_DIM // LANES, CONV_HIST + tc, LANES), F32),
            pltpu.VMEM((nseq, tc // CHUNK, SCAN_PAD + CHUNK, LANES), F32),
            pltpu.VMEM((nseq, SSM_STATE, D_SSM), F32),
            pltpu.VMEM((nseq, tc, D_SSM), F32),
        ],
        compiler_params=_cparams(("arbitrary", "arbitrary")),
        name="mix_prompt",
    )(x, *consts)


def _attn_kernel(x_ref, g_ref, wq_ref, k_ref, v_ref, wo_ref, qs_ref, ks_hbm, vs_hbm,
                 o_ref, os_ref, owo_ref, wq_s, wo_s, kbuf, vbuf, ksem, vsem, *, bb):
    step = pl.program_id(0) * pl.num_programs(1) + pl.program_id(1)
    n_steps = pl.num_programs(0) * pl.num_programs(1)

    def cache_copies(blk, slot):
        return (pltpu.make_async_copy(ks_hbm.at[pl.ds(blk * bb, bb)], kbuf.at[slot], ksem.at[slot]),
                pltpu.make_async_copy(vs_hbm.at[pl.ds(blk * bb, bb)], vbuf.at[slot], vsem.at[slot]))

    @pl.when(step == 0)
    def _():
        for c in cache_copies(0, 0):
            c.start()

    @pl.when(jnp.logical_and(step == 0, n_steps > 1))
    def _():
        for c in cache_copies(1, 1):
            c.start()

    @pl.when(step == 0)
    def _():
        wq_s[...] = wq_ref[...].astype(BF16)
        wo_s[...] = wo_ref[...].astype(BF16)
        owo_ref[...] = wo_s[...]

    @pl.when(step + 2 < n_steps)
    def _():
        for c in cache_copies(step + 2, (step + 2) % CACHE_SLOTS):
            c.start()

    slot = step % CACHE_SLOTS
    for c in cache_copies(step, slot):
        c.wait()
    ks_ref = kbuf.at[slot]
    vs_ref = vbuf.at[slot]

    hs = [slice(h * MEM_HD, (h + 1) * MEM_HD) for h in range(MEM_HEADS)]
    scale = MEM_HD ** -0.5
    nq = qs_ref.shape[1]
    rows = N_MEM * MEM_HEADS

    def softmax(s):
        e = jnp.exp(s - jnp.max(s, axis=-1, keepdims=True))
        return (e / jnp.sum(e, axis=-1, keepdims=True)).astype(BF16)

    q_head = lax.broadcasted_iota(jnp.int32, (nq, rows), 0) % MEM_HEADS
    kv_head = lax.broadcasted_iota(jnp.int32, (nq, rows), 1) % MEM_HEADS
    same_head = q_head == kv_head

    q_all = qs_ref[...].reshape(bb * nq, MEM_HD)
    q_all = jnp.concatenate([q_all, jnp.zeros((LANES - bb * nq, MEM_HD), F32)], axis=0).astype(BF16)
    s_s = [_dot_nt(ks_ref[i].reshape(rows, MEM_HD).astype(BF16), q_all) for i in range(bb)]

    x = x_ref[...]
    q = _dot(_rms(x, g_ref[...]).astype(BF16), wq_s[...]).astype(BF16)

    p_s = [softmax(jnp.where(same_head, s_s[i].T[i * nq:(i + 1) * nq] * scale, -jnp.inf)) for i in range(bb)]

    s_p = [_dot_nt(q[:, hs[h]], k_ref[:, hs[h]]) * scale for h in range(MEM_HEADS)]

    for i in range(bb):
        os_ref[i] = _dot(p_s[i], vs_ref[i].reshape(rows, MEM_HD).astype(BF16))

    p_p = [softmax(s) for s in s_p]
    o = jnp.concatenate([_dot(p_p[h], v_ref[:, hs[h]]) for h in range(MEM_HEADS)], axis=-1).astype(BF16)
    o_ref[...] = x + _dot(o, wo_s[...])


def _attn(x, k, v, g, wq, wo, qs, ks, vs, *, tq):
    b, s, d = x.shape
    nb, nq, hd = qs.shape
    nj = s // tq
    assert nb % (b * nj) == 0
    bb = nb // (b * nj)
    assert bb * nq <= LANES
    tile = pl.BlockSpec((None, tq, d), lambda i, j: (i, j, 0))
    mem = pl.BlockSpec((None, N_MEM, d), lambda i, j: (i, 0, 0))
    s_q = pl.BlockSpec((bb, nq, hd), lambda i, j: (i * nj + j, 0, 0))
    return pl.pallas_call(
        functools.partial(_attn_kernel, bb=bb),
        grid=(b, nj),
        in_specs=[tile, _const_spec(g.shape), _const_spec(wq.shape), mem, mem, _const_spec(wo.shape),
                  s_q, pl.BlockSpec(memory_space=pl.ANY), pl.BlockSpec(memory_space=pl.ANY)],
        out_specs=[tile, s_q, pl.BlockSpec(wo.shape, lambda i, j: (0, 0))],
        out_shape=[jax.ShapeDtypeStruct((b, s, d), F32), jax.ShapeDtypeStruct((nb, nq, hd), F32),
                   jax.ShapeDtypeStruct(wo.shape, BF16)],
        scratch_shapes=[pltpu.VMEM(wq.shape, BF16), pltpu.VMEM(wo.shape, BF16),
                        pltpu.VMEM((CACHE_SLOTS, bb, N_MEM, MEM_HEADS, hd), F32),
                        pltpu.VMEM((CACHE_SLOTS, bb, N_MEM, MEM_HEADS, hd), F32),
                        pltpu.SemaphoreType.DMA((CACHE_SLOTS,)), pltpu.SemaphoreType.DMA((CACHE_SLOTS,))],
        compiler_params=_cparams(("arbitrary", "arbitrary")),
        name="attn",
    )(x, g, wq, k, v, wo, qs, ks, vs)


def _mix_sample_kernel(x_ref, sp_ref, sc_ref, h0_ref, g_ref, win_ref, cw_ref, cb_ref, dtb_ref, alog_ref,
                       dsk_ref, wp_ref, ps_ref, ex_ref, sn_ref, wo_ref, gq_ref, wq_ref,
                       x2_ref, q_ref, npool_ref, nconv_ref, hout_ref,
                       pool_s, z_s, ydx_s, expa_s, xw_s, b_s, c_s, dec_s, y_s,
                       hbuf, hsem, wo_f, wq_f, wsem, *, nb, t, bb, start_pos):
    j = pl.program_id(0)
    n_blocks = pl.num_programs(0)
    gw = SSM_HPG * SSM_HEADDIM
    n_bc = SSM_GROUPS * SSM_STATE

    def rows(i):
        return slice(i * nb, (i + 1) * nb)

    def state_copy(blk, slot):
        return pltpu.make_async_copy(h0_ref.at[pl.ds(blk * bb, bb)], hbuf.at[slot], hsem.at[slot])

    def weight_copies():
        return (pltpu.make_async_copy(wo_ref, wo_f, wsem.at[0]),
                pltpu.make_async_copy(wq_ref, wq_f, wsem.at[1]))

    @pl.when(j == 0)
    def _():
        state_copy(0, 0).start()
        for c in weight_copies():
            c.start()

    @pl.when(jnp.logical_and(j == 0, n_blocks > 1))
    def _():
        state_copy(1, 1).start()

    @pl.when(j + 2 < n_blocks)
    def _():
        state_copy(j + 2, (j + 2) % STATE_SLOTS).start()

    @pl.when(j == 0)
    def _():
        x = x_ref[...]
        xn = _rms(x, g_ref[...]).astype(BF16)
        z_s[...] = _dot(xn, win_ref[:, D_POOL:D_POOL + D_SSM])

        u = _dot(xn, win_ref[:, 0:D_POOL])
        ext = [sp_ref[rows(i), :] for i in range(POOL_BUF)] + [u[rows(i)] for i in range(t)]
        ps = []
        for i in range(t):
            sums, cnts = [], []
            for w in POOL_WINDOWS:
                s = ext[POOL_BUF + i]
                for k in range(1, w):
                    s = s + ext[POOL_BUF + i - k]
                sums.append(s)
                cnts.append(jnp.full((nb, D_POOL), float(min(start_pos + i + 1, w)), F32))
            mean = _lane_group_select(sums, POOL_GW) / _lane_group_select(cnts, POOL_GW)
            ps.append(mean - ext[POOL_BUF + i])
        p = jnp.concatenate(ps, axis=0).astype(BF16)
        pool_s[...] = _dot(p, wp_ref[...]) * ps_ref[...]
        for i in range(POOL_BUF):
            npool_ref[rows(i), :] = ext[t + i]

        xr = _dot(xn, win_ref[:, D_POOL + D_SSM:D_PROJ])
        cext = [sc_ref[rows(i), :] for i in range(CONV_W - 1)] + [xr[rows(i)] for i in range(t)]
        xbc = []
        for i in range(t):
            acc = cext[i] * cw_ref[0:1, :]
            for k in range(1, CONV_W):
                acc = acc + cext[i + k] * cw_ref[k:k + 1, :]
            xbc.append(_silu(acc + cb_ref[...]))
        for i in range(CONV_W - 1):
            nconv_ref[rows(i), :] = cext[t + i]
        xs = [v[:, :D_SSM] for v in xbc]
        bm = [v[:, D_SSM:D_SSM + n_bc] for v in xbc]
        cm = [v[:, D_SSM + n_bc:] for v in xbc]
        for i in range(t):
            b_s[rows(i), :] = bm[i]
            c_s[rows(i), :] = cm[i]

        dt_all = jax.nn.softplus(_dot(xn, win_ref[:, D_PROJ:D_PROJ + LANES]) + dtb_ref[...])
        a_neg = -jnp.exp(alog_ref[...])
        dt = [dt_all[rows(i)] for i in range(t)]
        acs = []
        for i in range(t):
            da = dt[i] * a_neg
            acs.append(da if i == 0 else acs[-1] + da)
        dec = jnp.exp(acs[-1])
        for k in range(SSM_HEADS):
            dec_s[k] = _col_bcast(dec, k, LANES)

        def expand(v, exact):
            if exact:
                return jnp.dot(v, ex_ref[...], precision=lax.Precision.HIGHEST,
                               preferred_element_type=F32)
            return _dot(v.astype(BF16), ex_ref[...].astype(BF16))

        lane = lax.broadcasted_iota(jnp.int32, (nb, LANES), 1)
        for i in range(t):
            ydx = xs[i] * dsk_ref[...]
            for s in range(i + 1):
                sc = [jnp.sum(cm[i][:, g * SSM_STATE:(g + 1) * SSM_STATE]
                              * bm[s][:, g * SSM_STATE:(g + 1) * SSM_STATE], axis=-1, keepdims=True)
                      for g in range(SSM_GROUPS)]
                sc = jnp.where(lane < SSM_HPG, sc[0], sc[1])
                coef = sc * jnp.exp(acs[i] - acs[s]) * dt[s]
                ydx = ydx + expand(coef, False) * xs[s]
            ydx_s[rows(i), :] = ydx
            expa_s[rows(i), :] = expand(jnp.exp(acs[i]), True)
            xw_s[rows(i), :] = xs[i] * expand(jnp.exp(acs[-1] - acs[i]) * dt[i], True)

    base = pl.multiple_of(j * bb, bb)
    blk_rows = [pl.ds(i * nb + base, bb) for i in range(t)]

    def gather(ref):
        return jnp.concatenate([ref[r, :] for r in blk_rows], axis=0)

    def pad_rows(v):
        return jnp.concatenate([v, jnp.zeros((LANES - t * bb, v.shape[1]), v.dtype)], axis=0)

    c_blk = pad_rows(gather(c_s))
    b_blk = pad_rows(gather(b_s))
    xw_t = pad_rows(gather(xw_s)).T.astype(BF16)
    row_seq = lax.broadcasted_iota(jnp.int32, (LANES, SSM_STATE), 0) % bb
    col_seq = lax.broadcasted_iota(jnp.int32, (SSM_STATE, LANES), 1) % bb
    slot = j % STATE_SLOTS
    state_copy(j, slot).wait()
    h_blk = hbuf.at[slot]
    y_t = []
    for g in range(SSM_GROUPS):
        gs = slice(g * SSM_STATE, (g + 1) * SSM_STATE)
        c_t = c_blk[:, gs].T
        acc = jnp.zeros((gw, LANES), F32)
        for i in range(bb):
            h0 = h_blk[i, g * SSM_HPG:(g + 1) * SSM_HPG].reshape(gw, SSM_STATE)
            acc = acc + _dot(h0.astype(BF16), jnp.where(col_seq == i, c_t, 0.0).astype(BF16))
            b_i = jnp.where(row_seq == i, b_blk[:, gs], 0.0).astype(BF16)
            st = _dot(xw_t[g * gw:(g + 1) * gw, :], b_i)
            for k in range(SSM_HPG):
                hd = g * SSM_HPG + k
                hout_ref[i, hd] = (h_blk[i, hd] * dec_s[hd, pl.ds(base + i, 1), :]
                                   + st[k * SSM_HEADDIM:(k + 1) * SSM_HEADDIM])
        y_t.append(acc.T[:t * bb])
    y_blk = gather(ydx_s) + jnp.concatenate(y_t, axis=1) * gather(expa_s)
    for i in range(t):
        y_s[blk_rows[i], :] = y_blk[i * bb:(i + 1) * bb]

    @pl.when(j == n_blocks - 1)
    def _():
        for c in weight_copies():
            c.wait()
        x2 = _gate_norm_out(y_s[...], z_s[...], pool_s[...], x_ref[...], sn_ref[...],
                            wo_f[...].astype(BF16))
        _store_tm_rows(x2_ref, x2)
        q = _dot(_rms(x2, gq_ref[...]).astype(BF16), wq_f[...].astype(BF16))
        for i in range(t):
            for h in range(MEM_HEADS):
                q_ref[:, i * MEM_HEADS + h, :] = q[rows(i), h * MEM_HD:(h + 1) * MEM_HD]


def _mix_sample(x, sp, sc, h0, w_in_bf, w, *, nb, t, bb, start_pos):
    n, d = x.shape
    assert n == nb * t and nb % bb == 0 and t * bb <= LANES
    consts = [w["g_mix"], w_in_bf, w["conv_w"], w["conv_b"], w["dt_bias"], w["a_log"], w["d_skip"],
              w["w_pool"], w["pool_scale"], w["expand"], w["ssm_norm"], w["w_out"], w["g_xq"], w["w_xq"]]
    in_hbm = pl.BlockSpec(memory_space=pl.ANY)
    const_specs = [in_hbm if a is w["w_out"] or a is w["w_xq"] else _const_spec(a.shape) for a in consts]
    hblk = pl.BlockSpec((bb, SSM_HEADS, SSM_HEADDIM, SSM_STATE), lambda i: (i, 0, 0, 0))
    whole = _const_spec
    outs = [
        jax.ShapeDtypeStruct((nb, t, d), F32),
        jax.ShapeDtypeStruct((nb, t * MEM_HEADS, MEM_HD), F32),
        jax.ShapeDtypeStruct((POOL_BUF * nb, D_POOL), F32),
        jax.ShapeDtypeStruct(((CONV_W - 1) * nb, CONV_DIM), F32),
        jax.ShapeDtypeStruct(h0.shape, F32),
    ]
    n_bc = SSM_GROUPS * SSM_STATE
    return pl.pallas_call(
        functools.partial(_mix_sample_kernel, nb=nb, t=t, bb=bb, start_pos=start_pos),
        grid=(nb // bb,),
        in_specs=[_const_spec(x.shape), _const_spec(sp.shape), _const_spec(sc.shape), in_hbm] + const_specs,
        out_specs=[whole(o.shape) for o in outs[:4]] + [hblk],
        out_shape=outs,
        scratch_shapes=[
            pltpu.VMEM((n, D_POOL), F32),
            pltpu.VMEM((n, D_SSM), F32),
            pltpu.VMEM((n, D_SSM), F32),
            pltpu.VMEM((n, D_SSM), F32),
            pltpu.VMEM((n, D_SSM), F32),
            pltpu.VMEM((n, n_bc), F32),
            pltpu.VMEM((n, n_bc), F32),
            pltpu.VMEM((SSM_HEADS, nb, LANES), F32),
            pltpu.VMEM((n, D_SSM), F32),
            pltpu.VMEM((STATE_SLOTS, bb, SSM_HEADS, SSM_HEADDIM, SSM_STATE), F32),
            pltpu.SemaphoreType.DMA((STATE_SLOTS,)),
            pltpu.VMEM(w["w_out"].shape, F32), pltpu.VMEM(w["w_xq"].shape, F32),
            pltpu.SemaphoreType.DMA((2,)),
        ],
        compiler_params=_cparams(("arbitrary",), VMEM_LIMIT_FFN),
        name="mix_sample",
    )(x, sp, sc, h0, *consts)


def _prep_weights(g_ffn1, w1_gate, w1_up, w1_down, g_mix, w_in, conv_w, conv_b, dt_bias, a_log,
                  d_skip, ssm_norm, w_pool, pool_scale, w_out, g_mem, w_mem_k, w_mem_v, g_xq,
                  w_xq, w_xo, g_ffn2, w2_gate, w2_up, w2_down, g_final):
    row = lambda v: v.reshape(1, -1).astype(F32)
    pad_heads = lambda v: jnp.pad(v.astype(F32), (0, LANES - SSM_HEADS)).reshape(1, LANES)
    w_pool_bd = jnp.zeros((D_POOL, D_POOL), F32)
    for g in range(len(POOL_WINDOWS)):
        w_pool_bd = w_pool_bd.at[g * POOL_GW:(g + 1) * POOL_GW, g * POOL_GW:(g + 1) * POOL_GW].set(w_pool[g])
    head_of_chan = jnp.arange(D_SSM) // SSM_HEADDIM
    expand = (jnp.arange(LANES)[:, None] == head_of_chan[None, :]).astype(F32)
    return dict(
        g_ffn1=row(g_ffn1), w1_gate=w1_gate, w1_up=w1_up, w1_down=w1_down,
        g_mix=row(g_mix), w_in=jnp.transpose(w_in),
        conv_w=conv_w.astype(F32), conv_b=row(conv_b), dt_bias=pad_heads(dt_bias),
        a_log=pad_heads(a_log), d_skip=row(jnp.repeat(d_skip, SSM_HEADDIM)), ssm_norm=row(ssm_norm),
        w_pool=w_pool_bd.astype(BF16), pool_scale=row(pool_scale), w_out=w_out,
        g_mem=row(g_mem), w_mem_k=w_mem_k, w_mem_v=w_mem_v,
        g_xq=row(g_xq), w_xq=w_xq, w_xo=w_xo,
        g_ffn2=row(g_ffn2), w2_gate=w2_gate, w2_up=w2_up, w2_down=w2_down,
        g_final=row(g_final), expand=expand,
    )


def _layer(x_prompt, x_sample, mem_prompt, mem_k, mem_v, state_pool, state_conv, state_ssm, start_pos, w):
    b, s, d = x_prompt.shape
    nb, t, _ = x_sample.shape
    tmaj = lambda a: jnp.swapaxes(a, 0, 1).reshape(-1, a.shape[-1])
    bmaj = lambda a, r: jnp.swapaxes(a.reshape(r, nb, -1), 0, 1)

    k, v, k_b, v_b = _memkv(mem_prompt, w["g_mem"], w["w_mem_k"], w["w_mem_v"])
    xp, xs = _ffn(x_prompt.reshape(b * s, d), x_sample, w["g_ffn1"], w["w1_gate"], w["w1_up"],
                  w["w1_down"], w["g_final"], final=False, tm=FFN_ROWS)

    xp, pool_p, conv_p, ssm_p, w_in_bf = _mix_prompt(xp.reshape(b, s, d), w, tc=MIX_ROWS, nseq=MIX_SEQS)

    x2, q, pool_s, conv_s, ssm_s = _mix_sample(xs, tmaj(state_pool), tmaj(state_conv), state_ssm, w_in_bf, w,
                                               nb=nb, t=t, bb=STATE_BLOCK, start_pos=start_pos)

    xp, o, w_xo_bf = _attn(xp, k_b, v_b, w["g_xq"], w["w_xq"], w["w_xo"], q, mem_k, mem_v, tq=ATTN_ROWS)

    yp, ys = _ffn(xp.reshape(b * s, d), x2, w["g_ffn2"], w["w2_gate"], w["w2_up"], w["w2_down"],
                  w["g_final"], final=True, tm=FFN_ROWS, attn_out=(o, w_xo_bf))
    return (yp.reshape(b, s, d), ys, k, v, pool_p, conv_p, ssm_p,
            bmaj(pool_s, POOL_BUF), bmaj(conv_s, CONV_W - 1), ssm_s)


def kernel(x_prompt, x_sample, mem_prompt, cache_mem_k, cache_mem_v, state_pool, state_conv, state_ssm,
           g_ffn1, w1_gate, w1_up, w1_down, g_mix, w_in, conv_w, conv_b, dt_bias, a_log, d_skip,
           ssm_norm, w_pool, pool_scale, w_out, g_mem, w_mem_k, w_mem_v, g_xq, w_xq, w_xo,
           g_ffn2, w2_gate, w2_up, w2_down, g_final):
    assert g_ffn1.shape[0] == 1, "single-layer model"
    w = _prep_weights(g_ffn1[0], w1_gate[0], w1_up[0], w1_down[0], g_mix[0], w_in[0], conv_w[0],
                      conv_b[0], dt_bias[0], a_log[0], d_skip[0], ssm_norm[0], w_pool[0],
                      pool_scale[0], w_out[0], g_mem[0], w_mem_k[0], w_mem_v[0], g_xq[0], w_xq[0],
                      w_xo[0], g_ffn2[0], w2_gate[0], w2_up[0], w2_down[0], g_final)
    bp = x_prompt.shape[0]
    y_p, y_s, k_p, v_p, pool_p, conv_p, ssm_p, pool_s, conv_s, ssm_s = _layer(
        x_prompt, x_sample, mem_prompt, cache_mem_k[0], cache_mem_v[0], state_pool[0], state_conv[0],
        state_ssm[0], PAST_LEN, w)
    kv_shape = (1, bp, N_MEM, MEM_HEADS, MEM_HD)
    return (y_p, y_s, k_p.reshape(kv_shape), v_p.reshape(kv_shape), pool_p[None], conv_p[None],
            ssm_p[None], pool_s[None], conv_s[None], ssm_s[None])
```

```python
import functools

import jax
import jax.numpy as jnp
from jax import lax
from jax.experimental import pallas as pl
from jax.experimental.pallas import tpu as pltpu

F32 = jnp.float32
BF16 = jnp.bfloat16

D_POOL = 256
POOL_WINDOWS = (2, 4, 8, 16)
POOL_GW = 64
POOL_BUF = 15
D_SSM = 768
SSM_HEADDIM = 64
SSM_HEADS = 12
SSM_GROUPS = 2
SSM_HPG = 6
SSM_STATE = 128
CONV_W = 4
CONV_DIM = 1280
D_PROJ = D_POOL + D_SSM + CONV_DIM
N_MEM = 256
MEM_HEADS = 4
MEM_HD = 256
PAST_LEN = 16384
EPS = 1e-6
LANES = 128
CHUNK = 128
POOL_HIST = 32
CONV_HIST = 8
SCAN_PAD = CHUNK // 2

VMEM_LIMIT = 56 * 1024 * 1024
VMEM_LIMIT_FFN = 60 * 1024 * 1024
FFN_ROWS = 1024
FFN_W_STEPS = 16
MIX_ROWS = 256
MIX_SEQS = 2
ATTN_ROWS = 512
STATE_BLOCK = 8
STATE_SLOTS = 3
CACHE_SLOTS = 3
MEMKV_W_CHUNKS = 4


def _cparams(sem, vmem_limit=VMEM_LIMIT):
    return pltpu.CompilerParams(dimension_semantics=sem, vmem_limit_bytes=vmem_limit)


def _const_spec(shape):
    nd = len(shape)
    return pl.BlockSpec(shape, lambda *_: (0,) * nd, pipeline_mode=pl.Buffered(1))


def _tm_rows(ref):
    return jnp.concatenate([ref[:, i, :] for i in range(ref.shape[1])], axis=0)


def _store_tm_rows(ref, rows):
    nb = ref.shape[0]
    for i in range(ref.shape[1]):
        ref[:, i, :] = rows[i * nb:(i + 1) * nb]


def _rms(x, g):
    ms = jnp.mean(x * x, axis=-1, keepdims=True)
    return x * lax.rsqrt(ms + EPS) * g


def _silu(x):
    return x * jax.nn.sigmoid(x)


def _dot(a, b):
    return jnp.dot(a, b, preferred_element_type=F32)


def _dot_nt(a, b):
    return lax.dot_general(a, b, (((1,), (1,)), ((), ())), preferred_element_type=F32)


def _dot_tn(a, b):
    return lax.dot_general(a, b, (((0,), (0,)), ((), ())), preferred_element_type=F32)


def _ffn_kernel(xp_ref, xs_ref, g_ref, wg_ref, wu_ref, wd_ref, gf_ref, *rest, final, n_p, attn_out):
    if attn_out:
        ao_ref, wxo_ref, op_ref, os_ref, wg_s, wu_s, wd_s = rest
    else:
        op_ref, os_ref, wg_s, wu_s, wd_s = rest
    i = pl.program_id(0)
    rows_gu = wg_ref.shape[0]
    rows_d = wd_ref.shape[0]

    @pl.when(i < FFN_W_STEPS)
    def _():
        r = pl.multiple_of(i * rows_gu, rows_gu)
        wg_s[pl.ds(r, rows_gu), :] = wg_ref[...].astype(BF16)
        wu_s[pl.ds(r, rows_gu), :] = wu_ref[...].astype(BF16)
        r = pl.multiple_of(i * rows_d, rows_d)
        wd_s[pl.ds(r, rows_d), :] = wd_ref[...].astype(BF16)

    def tile(x):
        xn = _rms(x, g_ref[...]).astype(BF16)
        gate = _dot(xn, wg_s[...])
        up = _dot(xn, wu_s[...])
        h = (_silu(gate) * up).astype(BF16)
        out = x + 0.5 * _dot(h, wd_s[...])
        return _rms(out, gf_ref[...]) if final else out

    @pl.when(jnp.logical_and(i >= FFN_W_STEPS, i < FFN_W_STEPS + n_p))
    def _():
        op_ref[...] = tile(xp_ref[...])

    @pl.when(i == FFN_W_STEPS + n_p)
    def _():
        x = _tm_rows(xs_ref)
        if attn_out:
            nb, nq, hd = ao_ref.shape
            heads = wxo_ref.shape[0] // hd
            o = jnp.concatenate(
                [jnp.concatenate([ao_ref[:, t * heads + h, :] for h in range(heads)], axis=1)
                 for t in range(nq // heads)], axis=0)
            x = x + _dot(o.astype(BF16), wxo_ref[...])
        out = tile(x)
        if os_ref.ndim == 3:
            _store_tm_rows(os_ref, out)
        else:
            os_ref[...] = out


def _ffn(xp, xs, g, wg, wu, wd, gf, *, final, tm, attn_out=None):
    tp, d = xp.shape
    nb, t, _ = xs.shape
    os_shape = xs.shape if final else (nb * t, d)
    dff = wg.shape[1]
    assert tp % tm == 0 and d % FFN_W_STEPS == 0 and dff % FFN_W_STEPS == 0
    n_p = tp // tm
    p_tile = pl.BlockSpec((tm, d), lambda i: (jnp.clip(i - FFN_W_STEPS, 0, n_p - 1), 0))
    s_tile = pl.BlockSpec(os_shape, lambda i: (0,) * len(os_shape))
    w_chunk = lambda rows, cols: pl.BlockSpec((rows, cols), lambda i: (jnp.minimum(i, FFN_W_STEPS - 1), 0))
    extra = list(attn_out) if attn_out else []
    return pl.pallas_call(
        functools.partial(_ffn_kernel, final=final, n_p=n_p, attn_out=bool(attn_out)),
        grid=(FFN_W_STEPS + n_p + 1,),
        in_specs=[
            p_tile, _const_spec(xs.shape), _const_spec(g.shape),
            w_chunk(d // FFN_W_STEPS, dff), w_chunk(d // FFN_W_STEPS, dff), w_chunk(dff // FFN_W_STEPS, d),
            _const_spec(gf.shape),
        ] + [_const_spec(a.shape) for a in extra],
        out_specs=[p_tile, s_tile],
        out_shape=[jax.ShapeDtypeStruct(xp.shape, F32), jax.ShapeDtypeStruct(os_shape, F32)],
        scratch_shapes=[pltpu.VMEM((d, dff), BF16), pltpu.VMEM((d, dff), BF16), pltpu.VMEM((dff, d), BF16)],
        compiler_params=_cparams(("arbitrary",), VMEM_LIMIT_FFN),
        name="ffn_final" if final else "ffn",
    )(xp, xs, g, wg, wu, wd, gf, *extra)


def _memkv_kernel(m_ref, g_ref, wk_hbm, wv_hbm, k_ref, v_ref, kb_ref, vb_ref,
                  wk_s, wv_s, wk_f, wv_f, wsem):
    @pl.when(pl.program_id(0) == 0)
    def _():
        rc = wk_f.shape[0] // MEMKV_W_CHUNKS
        parts = [(src, stage, dst, pl.ds(c * rc, rc))
                 for src, stage, dst in ((wk_hbm, wk_f, wk_s), (wv_hbm, wv_f, wv_s))
                 for c in range(MEMKV_W_CHUNKS)]
        copies = [pltpu.make_async_copy(src.at[r], stage.at[r], wsem.at[n])
                  for n, (src, stage, _, r) in enumerate(parts)]
        copies[0].start()
        copies[1].start()
        for n, (_, stage, dst, r) in enumerate(parts):
            copies[n].wait()
            if n + 2 < len(copies):
                copies[n + 2].start()
            dst[r, :] = stage[r, :].astype(BF16)

    mn = _rms(m_ref[...], g_ref[...]).astype(BF16)
    k = _dot(mn, wk_s[...])
    v = _dot(mn, wv_s[...])
    for h in range(MEM_HEADS):
        k_ref[:, h, :] = k[:, h * MEM_HD:(h + 1) * MEM_HD]
        v_ref[:, h, :] = v[:, h * MEM_HD:(h + 1) * MEM_HD]
    kb_ref[...] = k.astype(BF16)
    vb_ref[...] = v.astype(BF16)


def _memkv(mem, g, wk, wv):
    b, n_mem, d = mem.shape
    row_blk = pl.BlockSpec((None, n_mem, d), lambda i: (i, 0, 0))
    head_blk = pl.BlockSpec((None, n_mem, MEM_HEADS, MEM_HD), lambda i: (i, 0, 0, 0))
    return pl.pallas_call(
        _memkv_kernel,
        grid=(b,),
        in_specs=[row_blk, _const_spec(g.shape),
                  pl.BlockSpec(memory_space=pl.ANY), pl.BlockSpec(memory_space=pl.ANY)],
        out_specs=[head_blk, head_blk, row_blk, row_blk],
        out_shape=[jax.ShapeDtypeStruct((b, n_mem, MEM_HEADS, MEM_HD), F32)] * 2
        + [jax.ShapeDtypeStruct((b, n_mem, d), BF16)] * 2,
        scratch_shapes=[pltpu.VMEM(wk.shape, BF16), pltpu.VMEM(wv.shape, BF16),
                        pltpu.VMEM(wk.shape, F32), pltpu.VMEM(wv.shape, F32),
                        pltpu.SemaphoreType.DMA((2 * MEMKV_W_CHUNKS,))],
        compiler_params=_cparams(("arbitrary",)),
        name="memkv",
    )(mem, g, wk, wv)


def _lane_group_select(vals, width):
    lane = lax.broadcasted_iota(jnp.int32, vals[0].shape, 1)
    out = vals[-1]
    for g in range(len(vals) - 2, -1, -1):
        out = jnp.where(lane < (g + 1) * width, vals[g], out)
    return out


def _col_bcast(m, k, width):
    return jnp.broadcast_to(m[:, k:k + 1], (m.shape[0], width))


def _gate_norm_out(y, z, pool_out, x_res, ssm_norm, w_out):
    yn = _rms(y * _silu(z), ssm_norm)
    cat = jnp.concatenate([pool_out, yn], axis=-1).astype(BF16)
    return x_res + _dot(cat, w_out)


def _mix_prompt_kernel(x_ref, g_ref, win_ref, cw_ref, cb_ref,
                       dtb_ref, alog_ref, dsk_ref, sn_ref, wp_ref, ps_ref, wout_ref,
                       o_ref, opool_ref, oconv_ref, ossm_ref, owin_ref,
                       wu_s, wz_s, wx_s, wdt_s, wo_s,
                       ubuf, s2buf, s4buf, s8buf, cbuf, abuf, ht_ref, ybuf, *, tc, nseq):
    c = pl.program_id(1)
    nc = pl.num_programs(1)
    n_pool_slabs = D_POOL // LANES
    n_conv_slabs = CONV_DIM // LANES
    n_chunks = tc // CHUNK
    rp = POOL_HIST + tc
    seqs = range(nseq)

    def slab(s):
        return slice(s * LANES, (s + 1) * LANES)

    @pl.when(jnp.logical_and(pl.program_id(0) == 0, c == 0))
    def _():
        wu_s[...] = win_ref[0:D_POOL, :].T.astype(BF16)
        wz_s[...] = win_ref[D_POOL:D_POOL + D_SSM, :].T.astype(BF16)
        wx_s[...] = win_ref[D_POOL + D_SSM:D_PROJ, :].T.astype(BF16)
        lead = 16 - SSM_HEADS
        place = (lax.broadcasted_iota(jnp.int32, (16, LANES), 0) - lead
                 == lax.broadcasted_iota(jnp.int32, (16, LANES), 1))
        wdt_s[...] = _dot_tn(win_ref[D_PROJ - lead:D_PROJ + SSM_HEADS, :].astype(BF16),
                             jnp.where(place, 1.0, 0.0).astype(BF16)).astype(BF16)
        wo_s[...] = wout_ref[...].astype(BF16)
        owin_ref[:, 0:D_POOL] = wu_s[...]
        owin_ref[:, D_POOL:D_POOL + D_SSM] = wz_s[...]
        owin_ref[:, D_POOL + D_SSM:D_PROJ] = wx_s[...]
        owin_ref[:, D_PROJ:D_PROJ + LANES] = wdt_s[...]

    @pl.when(c == 0)
    def _():
        ubuf[:, :, 0:POOL_HIST, :] = jnp.zeros((nseq, n_pool_slabs, POOL_HIST, LANES), F32)
        cbuf[:, :, 0:CONV_HIST, :] = jnp.zeros((nseq, n_conv_slabs, CONV_HIST, LANES), F32)
        abuf[:, :, 0:SCAN_PAD, :] = jnp.zeros((nseq, n_chunks, SCAN_PAD, LANES), F32)
        ht_ref[...] = jnp.zeros_like(ht_ref)

    lo_half = lax.broadcasted_iota(jnp.int32, (tc, LANES), 1) < POOL_GW
    pos1 = c * tc + lax.broadcasted_iota(jnp.int32, (tc, LANES), 0) + 1
    a_neg = -jnp.exp(alog_ref[...])
    row = lax.broadcasted_iota(jnp.int32, (CHUNK, CHUNK), 0)
    col = lax.broadcasted_iota(jnp.int32, (CHUNK, CHUNK), 1)
    causal = row >= col
    lo = lax.broadcasted_iota(jnp.int32, (CHUNK, LANES), 1) < SSM_HEADDIM
    mask_lo = jnp.where(lo, 1.0, 0.0).astype(BF16)
    mask_hi = jnp.where(lo, 0.0, 1.0).astype(BF16)
    gw = SSM_HPG * SSM_HEADDIM
    n_x = D_SSM // LANES
    st8 = {}

    def head(i):
        x = x_ref[i]
        xn = _rms(x, g_ref[...]).astype(BF16)
        st8[i] = dict(x=x, xn=xn, xbc=[None] * n_conv_slabs)

    def conv_block(i, s0, ns):
        xn = st8[i]["xn"]
        raw = _dot(xn, wx_s[:, s0 * LANES:(s0 + ns) * LANES])
        for t in range(ns):
            s = s0 + t
            cbuf[i, s, CONV_HIST:CONV_HIST + tc, :] = raw[:, slab(t)]
            acc = cbuf[i, s, CONV_HIST:CONV_HIST + tc, :] * cw_ref[CONV_W - 1:CONV_W, slab(s)]
            for j in range(1, CONV_W):
                acc = acc + (cbuf[i, s, CONV_HIST - j:CONV_HIST - j + tc, :]
                             * cw_ref[CONV_W - 1 - j:CONV_W - j, slab(s)])
            st8[i]["xbc"][s] = _silu(acc + cb_ref[:, slab(s)])

    def dt_proj(i):
        st8[i]["dt"] = jax.nn.softplus(_dot(st8[i]["xn"], wdt_s[...]) + dtb_ref[...])

    def pool(i):
        u = _dot(st8[i]["xn"], wu_s[...])
        for s in range(n_pool_slabs):
            ubuf[i, s, POOL_HIST:rp, :] = u[:, slab(s)]
            s2buf[i, s, 8:rp, :] = ubuf[i, s, 8:rp, :] + ubuf[i, s, 7:rp - 1, :]
        s4buf[i, 16:rp, :] = s2buf[i, 1, 16:rp, :] + s2buf[i, 1, 14:rp - 2, :]
        s8buf[i, 24:rp, :] = s4buf[i, 24:rp, :] + s4buf[i, 20:rp - 4, :]
        win_sums = [
            jnp.where(lo_half, s2buf[i, 0, POOL_HIST:rp, :],
                      s2buf[i, 0, POOL_HIST:rp, :] + s2buf[i, 0, POOL_HIST - 2:rp - 2, :]),
            jnp.where(lo_half, s8buf[i, POOL_HIST:rp, :],
                      s8buf[i, POOL_HIST:rp, :] + s8buf[i, POOL_HIST - 8:rp - 8, :]),
        ]
        ps = []
        for s in range(n_pool_slabs):
            win = jnp.where(lo_half, POOL_WINDOWS[2 * s], POOL_WINDOWS[2 * s + 1])
            cnt = jnp.minimum(pos1, win).astype(F32)
            ps.append(win_sums[s] / cnt - u[:, slab(s)])
        st8[i]["pool_out"] = _dot(jnp.concatenate(ps, axis=1).astype(BF16), wp_ref[...]) * ps_ref[...]

    def ssd_chunk(i, j):
        xbc = st8[i]["xbc"]
        sl = slice(j * CHUNK, (j + 1) * CHUNK)
        dt_c = st8[i]["dt"][sl]
        acs = dt_c * a_neg
        sh = 1
        while sh < CHUNK:
            abuf[i, j, SCAN_PAD:SCAN_PAD + CHUNK, :] = acs
            acs = acs + abuf[i, j, SCAN_PAD - sh:SCAN_PAD - sh + CHUNK, :]
            sh *= 2
        acs_last = acs[CHUNK - 1:CHUNK, :]
        fdec = jnp.exp(acs_last - acs) * dt_c
        src_t = (acs - jnp.log(dt_c)).T
        ht = ht_ref[i]
        ht_b = ht.astype(BF16)
        cdec, st = [], []
        for g in range(SSM_GROUPS):
            c_g = xbc[n_x + SSM_GROUPS + g][sl].astype(BF16)
            b_g = xbc[n_x + g][sl].astype(BF16)
            scores = _dot_nt(c_g, b_g)
            y_off = _dot(c_g, ht_b[:, g * gw:(g + 1) * gw])
            xw = []
            for q in range(SSM_HPG // 2):
                k0 = g * SSM_HPG + 2 * q
                pair = k0 // 2
                a_cols = [_col_bcast(acs, k, CHUNK) for k in (k0, k0 + 1)]
                lhs = [(scores * jnp.exp(jnp.where(causal, a_cols[h] - src_t[k0 + h:k0 + h + 1, :], -jnp.inf))
                        ).astype(BF16) for h in range(2)]
                x_pair = xbc[pair][sl]
                x_b = x_pair.astype(BF16)
                rhs = jnp.concatenate([x_b * mask_lo, x_b * mask_hi], axis=0)
                e_pair = jnp.exp(jnp.where(lo, a_cols[0], a_cols[1]))
                ybuf[i, sl, slab(pair)] = (_dot(jnp.concatenate(lhs, axis=1), rhs)
                                           + y_off[:, slab(q)] * e_pair + x_pair * dsk_ref[:, slab(pair)])
                f_pair = jnp.where(lo, _col_bcast(fdec, k0, LANES), _col_bcast(fdec, k0 + 1, LANES))
                xw.append((x_pair * f_pair).astype(BF16))
                cdec.append(e_pair[CHUNK - 1:CHUNK, :])
            st.append(_dot_tn(b_g, jnp.concatenate(xw, axis=1)))
        ht_ref[i] = ht * jnp.concatenate(cdec, axis=1) + jnp.concatenate(st, axis=1)

    def z_proj(i):
        st8[i]["zs"] = _silu(_dot(st8[i]["xn"], wz_s[...]))

    def tail(i):
        d = st8[i]
        yn = _rms(ybuf[i] * d["zs"], sn_ref[...])
        cat = jnp.concatenate([d["pool_out"], yn], axis=-1).astype(BF16)
        o_ref[i] = d["x"] + _dot(cat, wo_s[...])

    def mid(i):
        conv_block(i, n_x, 2 * SSM_GROUPS)
        dt_proj(i)
        for s0 in range(0, n_x, 2):
            conv_block(i, s0, 2)

    for i in seqs:
        head(i)
    for i in seqs:
        mid(i)
    for i in seqs:
        pool(i)
    for j in range(n_chunks):
        for i in seqs:
            ssd_chunk(i, j)
        if j == 0:
            for i in seqs:
                z_proj(i)
    for i in seqs:
        tail(i)

    ubuf[:, :, 0:POOL_HIST, :] = ubuf[:, :, tc:tc + POOL_HIST, :]
    cbuf[:, :, 0:CONV_HIST, :] = cbuf[:, :, tc:tc + CONV_HIST, :]

    @pl.when(c == nc - 1)
    def _():
        for i in seqs:
            for s in range(n_pool_slabs):
                opool_ref[i, :, slab(s)] = ubuf[i, s, POOL_HIST - POOL_BUF:POOL_HIST, :]
            for s in range(n_conv_slabs):
                oconv_ref[i, :, slab(s)] = cbuf[i, s, CONV_HIST - (CONV_W - 1):CONV_HIST, :]
            ossm_ref[i] = ht_ref[i].T.reshape(SSM_HEADS, SSM_HEADDIM, SSM_STATE)


def _mix_prompt(x, w, *, tc, nseq):
    b, s, d = x.shape
    assert s % tc == 0 and tc % CHUNK == 0 and b % nseq == 0
    consts = [w["g_mix"], w["w_in"], w["conv_w"], w["conv_b"],
              w["dt_bias"], w["a_log"], w["d_skip"], w["ssm_norm"], w["w_pool"], w["pool_scale"],
              w["w_out"]]
    tile = pl.BlockSpec((nseq, tc, d), lambda i, j: (i, j, 0))
    return pl.pallas_call(
        functools.partial(_mix_prompt_kernel, tc=tc, nseq=nseq),
        grid=(b // nseq, s // tc),
        in_specs=[tile] + [_const_spec(a.shape) for a in consts],
        out_specs=[
            tile,
            pl.BlockSpec((nseq, POOL_BUF, D_POOL), lambda i, j: (i, 0, 0)),
            pl.BlockSpec((nseq, CONV_W - 1, CONV_DIM), lambda i, j: (i, 0, 0)),
            pl.BlockSpec((nseq, SSM_HEADS, SSM_HEADDIM, SSM_STATE), lambda i, j: (i, 0, 0, 0)),
            pl.BlockSpec((d, D_PROJ + LANES), lambda i, j: (0, 0)),
        ],
        out_shape=[
            jax.ShapeDtypeStruct((b, s, d), F32),
            jax.ShapeDtypeStruct((b, POOL_BUF, D_POOL), F32),
            jax.ShapeDtypeStruct((b, CONV_W - 1, CONV_DIM), F32),
            jax.ShapeDtypeStruct((b, SSM_HEADS, SSM_HEADDIM, SSM_STATE), F32),
            jax.ShapeDtypeStruct((d, D_PROJ + LANES), BF16),
        ],
        scratch_shapes=[
            pltpu.VMEM((d, D_POOL), BF16), pltpu.VMEM((d, D_SSM), BF16),
            pltpu.VMEM((d, CONV_DIM), BF16), pltpu.VMEM((d, LANES), BF16), pltpu.VMEM((d, d), BF16),
            pltpu.VMEM((nseq, D_POOL // LANES, POOL_HIST + tc, LANES), F32),
            pltpu.VMEM((nseq, D_POOL // LANES, POOL_HIST + tc, LANES), F32),
            pltpu.VMEM((nseq, POOL_HIST + tc, LANES), F32),
            pltpu.VMEM((nseq, POOL_HIST + tc, LANES), F32),
            pltpu.VMEM((nseq, CONV_DIM // LANES, CONV_HIST + tc, LANES), F32),
            pltpu.VMEM((nseq, tc // CHUNK, SCAN_PAD + CHUNK, LANES), F32),
            pltpu.VMEM((nseq, SSM_STATE, D_SSM), F32),
            pltpu.VMEM((nseq, tc, D_SSM), F32),
        ],
        compiler_params=_cparams(("arbitrary", "arbitrary")),
        name="mix_prompt",
    )(x, *consts)


def _attn_kernel(x_ref, g_ref, wq_ref, k_ref, v_ref, wo_ref, qs_ref, ks_hbm, vs_hbm,
                 o_ref, os_ref, owo_ref, wq_s, wo_s, kbuf, vbuf, ksem, vsem, *, bb):
    step = pl.program_id(0) * pl.num_programs(1) + pl.program_id(1)
    n_steps = pl.num_programs(0) * pl.num_programs(1)

    def cache_copies(blk, slot):
        return (pltpu.make_async_copy(ks_hbm.at[pl.ds(blk * bb, bb)], kbuf.at[slot], ksem.at[slot]),
                pltpu.make_async_copy(vs_hbm.at[pl.ds(blk * bb, bb)], vbuf.at[slot], vsem.at[slot]))

    @pl.when(step == 0)
    def _():
        wq_s[...] = wq_ref[...].astype(BF16)
        wo_s[...] = wo_ref[...].astype(BF16)
        owo_ref[...] = wo_s[...]
        for c in cache_copies(0, 0):
            c.start()

    @pl.when(jnp.logical_and(step == 0, n_steps > 1))
    def _():
        for c in cache_copies(1, 1):
            c.start()

    @pl.when(step + 2 < n_steps)
    def _():
        for c in cache_copies(step + 2, (step + 2) % CACHE_SLOTS):
            c.start()

    slot = step % CACHE_SLOTS
    for c in cache_copies(step, slot):
        c.wait()
    ks_ref = kbuf.at[slot]
    vs_ref = vbuf.at[slot]

    hs = [slice(h * MEM_HD, (h + 1) * MEM_HD) for h in range(MEM_HEADS)]
    scale = MEM_HD ** -0.5
    nq = qs_ref.shape[1]
    rows = N_MEM * MEM_HEADS

    def softmax(s):
        e = jnp.exp(s - jnp.max(s, axis=-1, keepdims=True))
        return (e / jnp.sum(e, axis=-1, keepdims=True)).astype(BF16)

    q_head = lax.broadcasted_iota(jnp.int32, (nq, rows), 0) % MEM_HEADS
    kv_head = lax.broadcasted_iota(jnp.int32, (nq, rows), 1) % MEM_HEADS
    same_head = q_head == kv_head

    q_all = qs_ref[...].reshape(bb * nq, MEM_HD)
    q_all = jnp.concatenate([q_all, jnp.zeros((LANES - bb * nq, MEM_HD), F32)], axis=0).astype(BF16)
    s_s = [_dot_nt(ks_ref[i].reshape(rows, MEM_HD).astype(BF16), q_all) for i in range(bb)]

    x = x_ref[...]
    q = _dot(_rms(x, g_ref[...]).astype(BF16), wq_s[...]).astype(BF16)

    p_s = [softmax(jnp.where(same_head, s_s[i].T[i * nq:(i + 1) * nq] * scale, -jnp.inf)) for i in range(bb)]

    s_p = [_dot_nt(q[:, hs[h]], k_ref[:, hs[h]]) * scale for h in range(MEM_HEADS)]

    for i in range(bb):
        os_ref[i] = _dot(p_s[i], vs_ref[i].reshape(rows, MEM_HD).astype(BF16))

    p_p = [softmax(s) for s in s_p]
    o = jnp.concatenate([_dot(p_p[h], v_ref[:, hs[h]]) for h in range(MEM_HEADS)], axis=-1).astype(BF16)
    o_ref[...] = x + _dot(o, wo_s[...])


def _attn(x, k, v, g, wq, wo, qs, ks, vs, *, tq):
    b, s, d = x.shape
    nb, nq, hd = qs.shape
    nj = s // tq
    assert nb % (b * nj) == 0
    bb = nb // (b * nj)
    assert bb * nq <= LANES
    tile = pl.BlockSpec((None, tq, d), lambda i, j: (i, j, 0))
    mem = pl.BlockSpec((None, N_MEM, d), lambda i, j: (i, 0, 0))
    s_q = pl.BlockSpec((bb, nq, hd), lambda i, j: (i * nj + j, 0, 0))
    return pl.pallas_call(
        functools.partial(_attn_kernel, bb=bb),
        grid=(b, nj),
        in_specs=[tile, _const_spec(g.shape), _const_spec(wq.shape), mem, mem, _const_spec(wo.shape),
                  s_q, pl.BlockSpec(memory_space=pl.ANY), pl.BlockSpec(memory_space=pl.ANY)],
        out_specs=[tile, s_q, pl.BlockSpec(wo.shape, lambda i, j: (0, 0))],
        out_shape=[jax.ShapeDtypeStruct((b, s, d), F32), jax.ShapeDtypeStruct((nb, nq, hd), F32),
                   jax.ShapeDtypeStruct(wo.shape, BF16)],
        scratch_shapes=[pltpu.VMEM(wq.shape, BF16), pltpu.VMEM(wo.shape, BF16),
                        pltpu.VMEM((CACHE_SLOTS, bb, N_MEM, MEM_HEADS, hd), F32),
                        pltpu.VMEM((CACHE_SLOTS, bb, N_MEM, MEM_HEADS, hd), F32),
                        pltpu.SemaphoreType.DMA((CACHE_SLOTS,)), pltpu.SemaphoreType.DMA((CACHE_SLOTS,))],
        compiler_params=_cparams(("arbitrary", "arbitrary")),
        name="attn",
    )(x, g, wq, k, v, wo, qs, ks, vs)


def _mix_sample_kernel(x_ref, sp_ref, sc_ref, h0_ref, g_ref, win_ref, cw_ref, cb_ref, dtb_ref, alog_ref,
                       dsk_ref, wp_ref, ps_ref, ex_ref, sn_ref, wo_ref, gq_ref, wq_ref,
                       x2_ref, q_ref, npool_ref, nconv_ref, hout_ref,
                       pool_s, z_s, ydx_s, expa_s, xw_s, b_s, c_s, dec_s, y_s,
                       hbuf, hsem, wo_f, wq_f, wsem, *, nb, t, bb, start_pos):
    j = pl.program_id(0)
    n_blocks = pl.num_programs(0)
    gw = SSM_HPG * SSM_HEADDIM
    n_bc = SSM_GROUPS * SSM_STATE

    def rows(i):
        return slice(i * nb, (i + 1) * nb)

    def state_copy(blk, slot):
        return pltpu.make_async_copy(h0_ref.at[pl.ds(blk * bb, bb)], hbuf.at[slot], hsem.at[slot])

    def weight_copies():
        return (pltpu.make_async_copy(wo_ref, wo_f, wsem.at[0]),
                pltpu.make_async_copy(wq_ref, wq_f, wsem.at[1]))

    @pl.when(j == 0)
    def _():
        state_copy(0, 0).start()
        for c in weight_copies():
            c.start()

    @pl.when(jnp.logical_and(j == 0, n_blocks > 1))
    def _():
        state_copy(1, 1).start()

    @pl.when(j + 2 < n_blocks)
    def _():
        state_copy(j + 2, (j + 2) % STATE_SLOTS).start()

    @pl.when(j == 0)
    def _():
        x = x_ref[...]
        xn = _rms(x, g_ref[...]).astype(BF16)
        z_s[...] = _dot(xn, win_ref[:, D_POOL:D_POOL + D_SSM])

        u = _dot(xn, win_ref[:, 0:D_POOL])
        ext = [sp_ref[rows(i), :] for i in range(POOL_BUF)] + [u[rows(i)] for i in range(t)]
        ps = []
        for i in range(t):
            sums, cnts = [], []
            for w in POOL_WINDOWS:
                s = ext[POOL_BUF + i]
                for k in range(1, w):
                    s = s + ext[POOL_BUF + i - k]
                sums.append(s)
                cnts.append(jnp.full((nb, D_POOL), float(min(start_pos + i + 1, w)), F32))
            mean = _lane_group_select(sums, POOL_GW) / _lane_group_select(cnts, POOL_GW)
            ps.append(mean - ext[POOL_BUF + i])
        p = jnp.concatenate(ps, axis=0).astype(BF16)
        pool_s[...] = _dot(p, wp_ref[...]) * ps_ref[...]
        for i in range(POOL_BUF):
            npool_ref[rows(i), :] = ext[t + i]

        xr = _dot(xn, win_ref[:, D_POOL + D_SSM:D_PROJ])
        cext = [sc_ref[rows(i), :] for i in range(CONV_W - 1)] + [xr[rows(i)] for i in range(t)]
        xbc = []
        for i in range(t):
            acc = cext[i] * cw_ref[0:1, :]
            for k in range(1, CONV_W):
                acc = acc + cext[i + k] * cw_ref[k:k + 1, :]
            xbc.append(_silu(acc + cb_ref[...]))
        for i in range(CONV_W - 1):
            nconv_ref[rows(i), :] = cext[t + i]
        xs = [v[:, :D_SSM] for v in xbc]
        bm = [v[:, D_SSM:D_SSM + n_bc] for v in xbc]
        cm = [v[:, D_SSM + n_bc:] for v in xbc]
        for i in range(t):
            b_s[rows(i), :] = bm[i]
            c_s[rows(i), :] = cm[i]

        dt_all = jax.nn.softplus(_dot(xn, win_ref[:, D_PROJ:D_PROJ + LANES]) + dtb_ref[...])
        a_neg = -jnp.exp(alog_ref[...])
        dt = [dt_all[rows(i)] for i in range(t)]
        acs = []
        for i in range(t):
            da = dt[i] * a_neg
            acs.append(da if i == 0 else acs[-1] + da)
        dec = jnp.exp(acs[-1])
        for k in range(SSM_HEADS):
            dec_s[k] = _col_bcast(dec, k, LANES)

        def expand(v, exact):
            if exact:
                return jnp.dot(v, ex_ref[...], precision=lax.Precision.HIGHEST,
                               preferred_element_type=F32)
            return _dot(v.astype(BF16), ex_ref[...].astype(BF16))

        lane = lax.broadcasted_iota(jnp.int32, (nb, LANES), 1)
        for i in range(t):
            ydx = xs[i] * dsk_ref[...]
            for s in range(i + 1):
                sc = [jnp.sum(cm[i][:, g * SSM_STATE:(g + 1) * SSM_STATE]
                              * bm[s][:, g * SSM_STATE:(g + 1) * SSM_STATE], axis=-1, keepdims=True)
                      for g in range(SSM_GROUPS)]
                sc = jnp.where(lane < SSM_HPG, sc[0], sc[1])
                coef = sc * jnp.exp(acs[i] - acs[s]) * dt[s]
                ydx = ydx + expand(coef, False) * xs[s]
            ydx_s[rows(i), :] = ydx
            expa_s[rows(i), :] = expand(jnp.exp(acs[i]), True)
            xw_s[rows(i), :] = xs[i] * expand(jnp.exp(acs[-1] - acs[i]) * dt[i], True)

    base = pl.multiple_of(j * bb, bb)
    blk_rows = [pl.ds(i * nb + base, bb) for i in range(t)]

    def gather(ref):
        return jnp.concatenate([ref[r, :] for r in blk_rows], axis=0)

    def pad_rows(v):
        return jnp.concatenate([v, jnp.zeros((LANES - t * bb, v.shape[1]), v.dtype)], axis=0)

    c_blk = pad_rows(gather(c_s))
    b_blk = pad_rows(gather(b_s))
    xw_t = pad_rows(gather(xw_s)).T.astype(BF16)
    row_seq = lax.broadcasted_iota(jnp.int32, (LANES, SSM_STATE), 0) % bb
    col_seq = lax.broadcasted_iota(jnp.int32, (SSM_STATE, LANES), 1) % bb
    slot = j % STATE_SLOTS
    state_copy(j, slot).wait()
    h_blk = hbuf.at[slot]
    y_t = []
    for g in range(SSM_GROUPS):
        gs = slice(g * SSM_STATE, (g + 1) * SSM_STATE)
        c_t = c_blk[:, gs].T
        acc = jnp.zeros((gw, LANES), F32)
        for i in range(bb):
            h0 = h_blk[i, g * SSM_HPG:(g + 1) * SSM_HPG].reshape(gw, SSM_STATE)
            acc = acc + _dot(h0.astype(BF16), jnp.where(col_seq == i, c_t, 0.0).astype(BF16))
            b_i = jnp.where(row_seq == i, b_blk[:, gs], 0.0).astype(BF16)
            st = _dot(xw_t[g * gw:(g + 1) * gw, :], b_i)
            for k in range(SSM_HPG):
                hd = g * SSM_HPG + k
                hout_ref[i, hd] = (h_blk[i, hd] * dec_s[hd, pl.ds(base + i, 1), :]
                                   + st[k * SSM_HEADDIM:(k + 1) * SSM_HEADDIM])
        y_t.append(acc.T[:t * bb])
    y_blk = gather(ydx_s) + jnp.concatenate(y_t, axis=1) * gather(expa_s)
    for i in range(t):
        y_s[blk_rows[i], :] = y_blk[i * bb:(i + 1) * bb]

    @pl.when(j == n_blocks - 1)
    def _():
        for c in weight_copies():
            c.wait()
        x2 = _gate_norm_out(y_s[...], z_s[...], pool_s[...], x_ref[...], sn_ref[...],
                            wo_f[...].astype(BF16))
        _store_tm_rows(x2_ref, x2)
        q = _dot(_rms(x2, gq_ref[...]).astype(BF16), wq_f[...].astype(BF16))
        for i in range(t):
            for h in range(MEM_HEADS):
                q_ref[:, i * MEM_HEADS + h, :] = q[rows(i), h * MEM_HD:(h + 1) * MEM_HD]


def _mix_sample(x, sp, sc, h0, w_in_bf, w, *, nb, t, bb, start_pos):
    n, d = x.shape
    assert n == nb * t and nb % bb == 0 and t * bb <= LANES
    consts = [w["g_mix"], w_in_bf, w["conv_w"], w["conv_b"], w["dt_bias"], w["a_log"], w["d_skip"],
              w["w_pool"], w["pool_scale"], w["expand"], w["ssm_norm"], w["w_out"], w["g_xq"], w["w_xq"]]
    in_hbm = pl.BlockSpec(memory_space=pl.ANY)
    const_specs = [in_hbm if a is w["w_out"] or a is w["w_xq"] else _const_spec(a.shape) for a in consts]
    hblk = pl.BlockSpec((bb, SSM_HEADS, SSM_HEADDIM, SSM_STATE), lambda i: (i, 0, 0, 0))
    whole = _const_spec
    outs = [
        jax.ShapeDtypeStruct((nb, t, d), F32),
        jax.ShapeDtypeStruct((nb, t * MEM_HEADS, MEM_HD), F32),
        jax.ShapeDtypeStruct((POOL_BUF * nb, D_POOL), F32),
        jax.ShapeDtypeStruct(((CONV_W - 1) * nb, CONV_DIM), F32),
        jax.ShapeDtypeStruct(h0.shape, F32),
    ]
    n_bc = SSM_GROUPS * SSM_STATE
    return pl.pallas_call(
        functools.partial(_mix_sample_kernel, nb=nb, t=t, bb=bb, start_pos=start_pos),
        grid=(nb // bb,),
        in_specs=[_const_spec(x.shape), _const_spec(sp.shape), _const_spec(sc.shape), in_hbm] + const_specs,
        out_specs=[whole(o.shape) for o in outs[:4]] + [hblk],
        out_shape=outs,
        scratch_shapes=[
            pltpu.VMEM((n, D_POOL), F32),
            pltpu.VMEM((n, D_SSM), F32),
            pltpu.VMEM((n, D_SSM), F32),
            pltpu.VMEM((n, D_SSM), F32),
            pltpu.VMEM((n, D_SSM), F32),
            pltpu.VMEM((n, n_bc), F32),
            pltpu.VMEM((n, n_bc), F32),
            pltpu.VMEM((SSM_HEADS, nb, LANES), F32),
            pltpu.VMEM((n, D_SSM), F32),
            pltpu.VMEM((STATE_SLOTS, bb, SSM_HEADS, SSM_HEADDIM, SSM_STATE), F32),
            pltpu.SemaphoreType.DMA((STATE_SLOTS,)),
            pltpu.VMEM(w["w_out"].shape, F32), pltpu.VMEM(w["w_xq"].shape, F32),
            pltpu.SemaphoreType.DMA((2,)),
        ],
        compiler_params=_cparams(("arbitrary",), VMEM_LIMIT_FFN),
        name="mix_sample",
    )(x, sp, sc, h0, *consts)


def _prep_weights(g_ffn1, w1_gate, w1_up, w1_down, g_mix, w_in, conv_w, conv_b, dt_bias, a_log,
                  d_skip, ssm_norm, w_pool, pool_scale, w_out, g_mem, w_mem_k, w_mem_v, g_xq,
                  w_xq, w_xo, g_ffn2, w2_gate, w2_up, w2_down, g_final):
    row = lambda v: v.reshape(1, -1).astype(F32)
    pad_heads = lambda v: jnp.pad(v.astype(F32), (0, LANES - SSM_HEADS)).reshape(1, LANES)
    w_pool_bd = jnp.zeros((D_POOL, D_POOL), F32)
    for g in range(len(POOL_WINDOWS)):
        w_pool_bd = w_pool_bd.at[g * POOL_GW:(g + 1) * POOL_GW, g * POOL_GW:(g + 1) * POOL_GW].set(w_pool[g])
    head_of_chan = jnp.arange(D_SSM) // SSM_HEADDIM
    expand = (jnp.arange(LANES)[:, None] == head_of_chan[None, :]).astype(F32)
    return dict(
        g_ffn1=row(g_ffn1), w1_gate=w1_gate, w1_up=w1_up, w1_down=w1_down,
        g_mix=row(g_mix), w_in=jnp.transpose(w_in),
        conv_w=conv_w.astype(F32), conv_b=row(conv_b), dt_bias=pad_heads(dt_bias),
        a_log=pad_heads(a_log), d_skip=row(jnp.repeat(d_skip, SSM_HEADDIM)), ssm_norm=row(ssm_norm),
        w_pool=w_pool_bd.astype(BF16), pool_scale=row(pool_scale), w_out=w_out,
        g_mem=row(g_mem), w_mem_k=w_mem_k, w_mem_v=w_mem_v,
        g_xq=row(g_xq), w_xq=w_xq, w_xo=w_xo,
        g_ffn2=row(g_ffn2), w2_gate=w2_gate, w2_up=w2_up, w2_down=w2_down,
        g_final=row(g_final), expand=expand,
    )


def _layer(x_prompt, x_sample, mem_prompt, mem_k, mem_v, state_pool, state_conv, state_ssm, start_pos, w):
    b, s, d = x_prompt.shape
    nb, t, _ = x_sample.shape
    tmaj = lambda a: jnp.swapaxes(a, 0, 1).reshape(-1, a.shape[-1])
    bmaj = lambda a, r: jnp.swapaxes(a.reshape(r, nb, -1), 0, 1)

    k, v, k_b, v_b = _memkv(mem_prompt, w["g_mem"], w["w_mem_k"], w["w_mem_v"])
    xp, xs = _ffn(x_prompt.reshape(b * s, d), x_sample, w["g_ffn1"], w["w1_gate"], w["w1_up"],
                  w["w1_down"], w["g_final"], final=False, tm=FFN_ROWS)

    xp, pool_p, conv_p, ssm_p, w_in_bf = _mix_prompt(xp.reshape(b, s, d), w, tc=MIX_ROWS, nseq=MIX_SEQS)

    x2, q, pool_s, conv_s, ssm_s = _mix_sample(xs, tmaj(state_pool), tmaj(state_conv), state_ssm, w_in_bf, w,
                                               nb=nb, t=t, bb=STATE_BLOCK, start_pos=start_pos)

    xp, o, w_xo_bf = _attn(xp, k_b, v_b, w["g_xq"], w["w_xq"], w["w_xo"], q, mem_k, mem_v, tq=ATTN_ROWS)

    yp, ys = _ffn(xp.reshape(b * s, d), x2, w["g_ffn2"], w["w2_gate"], w["w2_up"], w["w2_down"],
                  w["g_final"], final=True, tm=FFN_ROWS, attn_out=(o, w_xo_bf))
    return (yp.reshape(b, s, d), ys, k, v, pool_p, conv_p, ssm_p,
            bmaj(pool_s, POOL_BUF), bmaj(conv_s, CONV_W - 1), ssm_s)


def kernel(x_prompt, x_sample, mem_prompt, cache_mem_k, cache_mem_v, state_pool, state_conv, state_ssm,
           g_ffn1, w1_gate, w1_up, w1_down, g_mix, w_in, conv_w, conv_b, dt_bias, a_log, d_skip,
           ssm_norm, w_pool, pool_scale, w_out, g_mem, w_mem_k, w_mem_v, g_xq, w_xq, w_xo,
           g_ffn2, w2_gate, w2_up, w2_down, g_final):
    assert g_ffn1.shape[0] == 1, "single-layer model"
    w = _prep_weights(g_ffn1[0], w1_gate[0], w1_up[0], w1_down[0], g_mix[0], w_in[0], conv_w[0],
                      conv_b[0], dt_bias[0], a_log[0], d_skip[0], ssm_norm[0], w_pool[0],
                      pool_scale[0], w_out[0], g_mem[0], w_mem_k[0], w_mem_v[0], g_xq[0], w_xq[0],
                      w_xo[0], g_ffn2[0], w2_gate[0], w2_up[0], w2_down[0], g_final)
    bp = x_prompt.shape[0]
    y_p, y_s, k_p, v_p, pool_p, conv_p, ssm_p, pool_s, conv_s, ssm_s = _layer(
        x_prompt, x_sample, mem_prompt, cache_mem_k[0], cache_mem_v[0], state_pool[0], state_conv[0],
        state_ssm[0], PAST_LEN, w)
    kv_shape = (1, bp, N_MEM, MEM_HEADS, MEM_HD)
    return (y_p, y_s, k_p.reshape(kv_shape), v_p.reshape(kv_shape), pool_p[None], conv_p[None],
            ssm_p[None], pool_s[None], conv_s[None], ssm_s[None])
```
